```python
import math
import jax, jax.numpy as jnp
from jax import lax
import numpy as np

D_MODEL = 2048
BATCH = 4
SEQ = 4096
DEPTH = 2

GRID_W = 64
CTX_LEN = 256
N_EVEN = (DEPTH + 1) // 2
N_ODD = DEPTH // 2
HEAD_DIM = 128
NA_HEADS = D_MODEL // (2 * HEAD_DIM)
NA_WIN_H = 8
NA_WIN_W = 16
DIFF_HEAD_DIM = 128
DIFF_HEADS = D_MODEL // (4 * DIFF_HEAD_DIM)
NA_WIDTH = NA_HEADS * HEAD_DIM
DIFF_WIDTH = DIFF_HEADS * 2 * DIFF_HEAD_DIM
EVEN_PROJ = 3 * NA_WIDTH + 3 * DIFF_WIDTH
MLA_HEADS = D_MODEL // 128
MLA_Q_RANK = 512
MLA_KV_RANK = 512
MLA_NOPE = 128
MLA_ROPE = 64
MLA_V = 128
MLA_DOWN = MLA_Q_RANK + MLA_KV_RANK + MLA_ROPE
D_FF = 128 * ((8 * D_MODEL // 3 + 127) // 128)
CONV_W = 3
Q_BLOCK = 128
ROPE_BASE = 10000.0
EPS = 1e-6

kernel_name = 'hybrid_na_diff_mla_dit_block'


def rms_norm(x, g):
    xf = x.astype(jnp.float32)
    y = xf * lax.rsqrt(jnp.mean(xf * xf, axis=-1, keepdims=True) + EPS)
    return (y * g.astype(jnp.float32)).astype(x.dtype)


def modulate(x, g, shift, scale):
    return rms_norm(x, g) * (1 + scale) + shift


def softmax32(s):
    return jax.nn.softmax(s.astype(jnp.float32), axis=-1)


def rope_1d(x, pos):
    half = x.shape[-1] // 2
    freqs = ROPE_BASE ** (-jnp.arange(half, dtype=jnp.float32) / half)
    ang = pos.astype(jnp.float32)[:, None] * freqs[None, :]
    cos = jnp.concatenate([jnp.cos(ang), jnp.cos(ang)], -1)[None, :, None, :].astype(x.dtype)
    sin = jnp.concatenate([jnp.sin(ang), jnp.sin(ang)], -1)[None, :, None, :].astype(x.dtype)
    rot = jnp.concatenate([-x[..., half:], x[..., :half]], -1)
    return x * cos + rot * sin


def rope_2d(x):
    L = x.shape[1]
    t = jnp.arange(L, dtype=jnp.int32)
    h = x.shape[-1] // 2
    return jnp.concatenate([rope_1d(x[..., :h], t // GRID_W), rope_1d(x[..., h:], t % GRID_W)], -1)


def map_query_blocks(fn, *qs):
    B, L = qs[0].shape[:2]
    nb = L // Q_BLOCK
    xs = tuple(jnp.moveaxis(q.reshape((B, nb, Q_BLOCK) + q.shape[2:]), 1, 0) for q in qs)
    out = lax.map(lambda a: fn(*a), xs)
    out = jnp.moveaxis(out, 0, 1)
    return out.reshape((B, L) + out.shape[3:])


def sdpa(q, k, v):
    s = jnp.einsum('bqhd,bkhd->bhqk', q, k) * (q.shape[-1] ** -0.5)
    return jnp.einsum('bhqk,bkhd->bqhd', softmax32(s).astype(v.dtype), v)


def diff_attend(q1, q2, k1, k2, v, lam):
    scale = q1.shape[-1] ** -0.5
    p1 = softmax32(jnp.einsum('bqhd,bkhd->bhqk', q1, k1) * scale)
    p2 = softmax32(jnp.einsum('bqhd,bkhd->bhqk', q2, k2) * scale)
    p = (p1 - lam * p2).astype(v.dtype)
    return jnp.einsum('bhqk,bkhd->bqhd', p, v)


def neighbourhood_attention(q, k, v, kc, vc, rpb):
    B, L, H, d = q.shape
    rows = L // GRID_W
    wh = min(NA_WIN_H, rows)
    ww = NA_WIN_W
    nk = wh * ww
    scale = d ** -0.5
    r = jnp.arange(rows, dtype=jnp.int32)
    key_rows = jnp.clip(r - wh // 2, 0, rows - wh)[:, None] + jnp.arange(wh, dtype=jnp.int32)[None, :]
    cidx = jnp.arange(GRID_W, dtype=jnp.int32)
    key_cols = jnp.clip(cidx - ww // 2, 0, GRID_W - ww)[:, None] + jnp.arange(ww, dtype=jnp.int32)[None, :]
    dc = key_cols - cidx[:, None] + (NA_WIN_W - 1)
    q_rows = jnp.moveaxis(q.reshape(B, rows, GRID_W, H, d), 1, 0)

    def row_block(args):
        q_r, kr, ri = args
        idx = (kr[None, :, None] * GRID_W + key_cols[:, None, :]).reshape(GRID_W, nk)
        kg = k[:, idx]
        vg = v[:, idx]
        dr = (kr - ri + (NA_WIN_H - 1))[None, :, None]
        bias = rpb[:, dr, dc[:, None, :]].reshape(H, GRID_W, nk)
        s_win = jnp.einsum('bqhd,bqkhd->bhqk', q_r, kg) * scale + bias[None]
        s_ctx = jnp.einsum('bqhd,bkhd->bhqk', q_r, kc) * scale
        p = softmax32(jnp.concatenate([s_win, s_ctx], -1)).astype(v.dtype)
        return (jnp.einsum('bhqk,bqkhd->bqhd', p[..., :nk], vg)
                + jnp.einsum('bhqk,bkhd->bqhd', p[..., nk:], vc))

    out = lax.map(row_block, (q_rows, key_rows, r))
    return jnp.moveaxis(out, 0, 1).reshape(B, L, H, d)


def even_mixer(h, hc, w_in, w_out, qn_a, kn_a, rpb, qn_b, kn_b, diff_lam, subln_b, lam_init, with_ctx):
    def project(t):
        B, L, _ = t.shape
        qa, ka, va, qb, kb, vb = jnp.split(
            t @ w_in, [NA_WIDTH, 2 * NA_WIDTH, 3 * NA_WIDTH, 3 * NA_WIDTH + DIFF_WIDTH, 3 * NA_WIDTH + 2 * DIFF_WIDTH], axis=-1)
        qa = rms_norm(qa.reshape(B, L, NA_HEADS, HEAD_DIM), qn_a)
        ka = rms_norm(ka.reshape(B, L, NA_HEADS, HEAD_DIM), kn_a)
        va = va.reshape(B, L, NA_HEADS, HEAD_DIM)
        qb = rms_norm(qb.reshape(B, L, 2 * DIFF_HEADS, DIFF_HEAD_DIM), qn_b)
        kb = rms_norm(kb.reshape(B, L, 2 * DIFF_HEADS, DIFF_HEAD_DIM), kn_b)
        vb = vb.reshape(B, L, DIFF_HEADS, 2 * DIFF_HEAD_DIM)
        return qa, ka, va, qb, kb, vb

    def halves(t):
        t = t.reshape(t.shape[0], t.shape[1], DIFF_HEADS, 2, DIFF_HEAD_DIM)
        return t[:, :, :, 0], t[:, :, :, 1]

    def merge(oa, ob):
        ob = rms_norm(ob, subln_b) * (1 - lam_init)
        o = jnp.concatenate([oa.reshape(oa.shape[0], oa.shape[1], -1), ob.reshape(ob.shape[0], ob.shape[1], -1)], -1)
        return o @ w_out

    qa, ka, va, qb, kb, vb = project(h)
    qa_c, ka_c, va_c, qb_c, kb_c, vb_c = project(hc)
    qb = rope_2d(qb)
    kb = rope_2d(kb)
    lf = diff_lam.astype(jnp.float32)
    lam = jnp.exp(jnp.sum(lf[0] * lf[1])) - jnp.exp(jnp.sum(lf[2] * lf[3])) + lam_init

    oa = neighbourhood_attention(qa, ka, va, ka_c, va_c, rpb)
    k1, k2 = halves(jnp.concatenate([kb, kb_c], 1))
    v_all = jnp.concatenate([vb, vb_c], 1)
    q1, q2 = halves(qb)
    ob = map_query_blocks(lambda a1, a2: diff_attend(a1, a2, k1, k2, v_all, lam), q1, q2)
    y = merge(oa, ob)
    yc = None
    if with_ctx:
        q1c, q2c = halves(qb_c)
        k1c, k2c = halves(kb_c)
        yc = merge(sdpa(qa_c, ka_c, va_c), diff_attend(q1c, q2c, k1c, k2c, vb_c, lam))
    return y, yc


def odd_mixer(h, hc, w_down, q_a_norm, kv_a_norm, w_uq, w_ukv, qn_nope, qn_rope, kn_nope, kn_rope, w_out, with_ctx):
    scale = (MLA_NOPE + MLA_ROPE) ** -0.5

    def queries(down, rotate):
        B, L, _ = down.shape
        q = (rms_norm(down[..., :MLA_Q_RANK], q_a_norm) @ w_uq).reshape(B, L, MLA_HEADS, MLA_NOPE + MLA_ROPE)
        q_nope = rms_norm(q[..., :MLA_NOPE], qn_nope)
        q_rope = rms_norm(q[..., MLA_NOPE:], qn_rope)
        if rotate:
            q_rope = rope_2d(q_rope)
        return q_nope, q_rope

    def keys_values(down, rotate):
        B, L, _ = down.shape
        kv = (rms_norm(down[..., MLA_Q_RANK:MLA_Q_RANK + MLA_KV_RANK], kv_a_norm) @ w_ukv).reshape(
            B, L, MLA_HEADS, MLA_NOPE + MLA_V)
        k_nope = rms_norm(kv[..., :MLA_NOPE], kn_nope)
        k_rope = rms_norm(down[:, :, None, MLA_Q_RANK + MLA_KV_RANK:], kn_rope)
        if rotate:
            k_rope = rope_2d(k_rope)
        return k_nope, k_rope[:, :, 0], kv[..., MLA_NOPE:]

    def attend(qn, qr, kn, kr, v):
        s = (jnp.einsum('bqhd,bkhd->bhqk', qn, kn) + jnp.einsum('bqhr,bkr->bhqk', qr, kr)) * scale
        return jnp.einsum('bhqk,bkhd->bqhd', softmax32(s).astype(v.dtype), v)

    down = h @ w_down
    down_c = hc @ w_down
    qn, qr = queries(down, True)
    kn, kr, v = keys_values(down, True)
    kn_c, kr_c, v_c = keys_values(down_c, False)
    kn_all = jnp.concatenate([kn, kn_c], 1)
    kr_all = jnp.concatenate([kr, kr_c], 1)
    v_all = jnp.concatenate([v, v_c], 1)
    o = map_query_blocks(lambda a, b: attend(a, b, kn_all, kr_all, v_all), qn, qr)
    y = o.reshape(o.shape[0], o.shape[1], -1) @ w_out
    yc = None
    if with_ctx:
        qn_c, qr_c = queries(down_c, False)
        oc = attend(qn_c, qr_c, kn_c, kr_c, v_c)
        yc = oc.reshape(oc.shape[0], oc.shape[1], -1) @ w_out
    return y, yc


def conv_ffn(h, w_in, conv_w, w_out):
    u = h @ w_in
    up = jnp.pad(u, ((0, 0), (1, 1), (0, 0)))
    u = up[:, :-2] * conv_w[0] + up[:, 1:-1] * conv_w[1] + up[:, 2:] * conv_w[2]
    a, b = jnp.split(u, 2, axis=-1)
    return (jax.nn.silu(a) * b) @ w_out


def setup_inputs(seed: int = 0) -> dict:
    key = jax.random.key(seed)
    ks = jax.random.split(key, 32)
    D = D_MODEL

    def nrm(k, shape, scale):
        return jax.random.normal(k, shape, jnp.float32) * scale

    def gain(k, shape):
        return 1.0 + 0.1 * jax.random.normal(k, shape, jnp.float32)

    return {
        'x': nrm(ks[0], (BATCH, SEQ, D), 1.0),
        'c': nrm(ks[1], (BATCH, D), 1.0),
        'ctx': nrm(ks[2], (BATCH, CTX_LEN, D), 1.0),
        'c_ctx': nrm(ks[3], (D,), 1.0),
        'ada_w': nrm(ks[4], (DEPTH, D, 6 * D), 0.5 * D ** -0.5),
        'ada_b': nrm(ks[5], (DEPTH, 6 * D), 0.02),
        'norm_mix': gain(ks[6], (DEPTH, D)),
        'norm_ffn': gain(ks[7], (DEPTH, D)),
        'ffn_w_in': nrm(ks[8], (DEPTH, D, 2 * D_FF), D ** -0.5),
        'ffn_conv': nrm(ks[9], (DEPTH, CONV_W, 2 * D_FF), CONV_W ** -0.5),
        'ffn_w_out': nrm(ks[10], (DEPTH, D_FF, D), D_FF ** -0.5),
        'even_w_in': nrm(ks[11], (N_EVEN, D, EVEN_PROJ), D ** -0.5),
        'even_w_out': nrm(ks[12], (N_EVEN, NA_WIDTH + DIFF_WIDTH, D), (NA_WIDTH + DIFF_WIDTH) ** -0.5),
        'na_q_norm': gain(ks[13], (N_EVEN, HEAD_DIM)),
        'na_k_norm': gain(ks[14], (N_EVEN, HEAD_DIM)),
        'na_rpb': nrm(ks[15], (N_EVEN, NA_HEADS, 2 * NA_WIN_H - 1, 2 * NA_WIN_W - 1), 0.5),
        'diff_q_norm': gain(ks[16], (N_EVEN, DIFF_HEAD_DIM)),
        'diff_k_norm': gain(ks[17], (N_EVEN, DIFF_HEAD_DIM)),
        'diff_lambda': nrm(ks[18], (N_EVEN, 4, DIFF_HEAD_DIM), 0.1),
        'diff_subln': gain(ks[19], (N_EVEN, 2 * DIFF_HEAD_DIM)),
        'mla_w_down': nrm(ks[20], (N_ODD, D, MLA_DOWN), D ** -0.5),
        'mla_q_a_norm': gain(ks[21], (N_ODD, MLA_Q_RANK)),
        'mla_kv_a_norm': gain(ks[22], (N_ODD, MLA_KV_RANK)),
        'mla_w_uq': nrm(ks[23], (N_ODD, MLA_Q_RANK, MLA_HEADS * (MLA_NOPE + MLA_ROPE)), MLA_Q_RANK ** -0.5),
        'mla_w_ukv': nrm(ks[24], (N_ODD, MLA_KV_RANK, MLA_HEADS * (MLA_NOPE + MLA_V)), MLA_KV_RANK ** -0.5),
        'mla_q_nope_norm': gain(ks[25], (N_ODD, MLA_NOPE)),
        'mla_q_rope_norm': gain(ks[26], (N_ODD, MLA_ROPE)),
        'mla_k_nope_norm': gain(ks[27], (N_ODD, MLA_NOPE)),
        'mla_k_rope_norm': gain(ks[28], (N_ODD, MLA_ROPE)),
        'mla_w_out': nrm(ks[29], (N_ODD, MLA_HEADS * MLA_V, D), (MLA_HEADS * MLA_V) ** -0.5),
    }


def reference(x, c, ctx, c_ctx, ada_w, ada_b, norm_mix, norm_ffn, ffn_w_in, ffn_conv, ffn_w_out,
              even_w_in, even_w_out, na_q_norm, na_k_norm, na_rpb, diff_q_norm, diff_k_norm,
              diff_lambda, diff_subln, mla_w_down, mla_q_a_norm, mla_kv_a_norm, mla_w_uq, mla_w_ukv,
              mla_q_nope_norm, mla_q_rope_norm, mla_k_nope_norm, mla_k_rope_norm, mla_w_out):
    xc = ctx
    s_lat = jax.nn.silu(c)
    s_ctx = jax.nn.silu(c_ctx)
    for l in range(DEPTH):
        with_ctx = l < DEPTH - 1
        mod = (s_lat @ ada_w[l] + ada_b[l])[:, None, :]
        mod_c = s_ctx @ ada_w[l] + ada_b[l]
        sh_m, sc_m, g_m, sh_f, sc_f, g_f = jnp.split(mod, 6, axis=-1)
        shc_m, scc_m, gc_m, shc_f, scc_f, gc_f = jnp.split(mod_c, 6, axis=-1)
        h = modulate(x, norm_mix[l], sh_m, sc_m)
        hc = modulate(xc, norm_mix[l], shc_m, scc_m)
        i = l // 2
        if l % 2 == 0:
            lam_init = 0.8 - 0.6 * math.exp(-0.3 * l)
            y, yc = even_mixer(h, hc, even_w_in[i], even_w_out[i], na_q_norm[i], na_k_norm[i], na_rpb[i],
                               diff_q_norm[i], diff_k_norm[i], diff_lambda[i], diff_subln[i], lam_init, with_ctx)
        else:
            y, yc = odd_mixer(h, hc, mla_w_down[i], mla_q_a_norm[i], mla_kv_a_norm[i], mla_w_uq[i], mla_w_ukv[i],
                              mla_q_nope_norm[i], mla_q_rope_norm[i], mla_k_nope_norm[i], mla_k_rope_norm[i],
                              mla_w_out[i], with_ctx)
        x = x + g_m * y
        x = x + g_f * conv_ffn(modulate(x, norm_ffn[l], sh_f, sc_f), ffn_w_in[l], ffn_conv[l], ffn_w_out[l])
        if with_ctx:
            xc = xc + gc_m * yc
            xc = xc + gc_f * conv_ffn(modulate(xc, norm_ffn[l], shc_f, scc_f), ffn_w_in[l], ffn_conv[l], ffn_w_out[l])
    return x
```

```python
import functools
import math

import jax
import jax.numpy as jnp
from jax import lax
from jax.experimental import pallas as pl
from jax.experimental.pallas import tpu as pltpu

D_MODEL = 2048
BATCH = 4
SEQ = 4096
DEPTH = 2
GRID_W = 64
GRID_H = SEQ // GRID_W
CTX_LEN = 256
HEAD_DIM = 128
NA_HEADS = 8
NA_WIN_H = 8
NA_WIN_W = 16
DIFF_HEADS = 4
NA_WIDTH = 1024
DIFF_WIDTH = 1024
EVEN_PROJ = 6144
MLA_HEADS = 16
MLA_Q_RANK = 512
MLA_KV_RANK = 512
MLA_NOPE = 128
MLA_ROPE = 64
MLA_V = 128
MLA_DOWN = 1088
D_FF = 5504
ROPE_BASE = 10000.0
EPS = 1e-6

LANES = 128
SUBLANES = 8
MOD_ROWS = 8
CTX_MOD_ROW = BATCH
D_FF_PAD = 5632
MLA_DOWN_PAD = 1152
VMEM_LIMIT = 56 * 1024 * 1024

TM = 512
TN = 1024
TN_FF = 512
NA_GROUP_ROWS = 4
NA_KEY_ROWS = 12
NA_Q = NA_GROUP_ROWS * GRID_W
NA_K = NA_KEY_ROWS * GRID_W
TQ_DIFF = 256
TQ_MLA = 256
NEG = -1e30

_NT = (((1,), (1,)), ((), ()))


def _params(*sem):
    return pltpu.CompilerParams(dimension_semantics=sem, vmem_limit_bytes=VMEM_LIMIT)


def _bf(x):
    return x.astype(jnp.bfloat16)


def _dot(a, b):
    return jnp.dot(a, b, preferred_element_type=jnp.float32)


def _dot_nt(a, b):
    return lax.dot_general(a, b, _NT, preferred_element_type=jnp.float32)


def _modulate(x, gain, shift, scale):
    ms = jnp.mean(x * x, axis=-1, keepdims=True)
    y = x * lax.rsqrt(ms + EPS) * gain
    return y * (1.0 + scale) + shift


def _ada_kernel(c_ref, w_ref, b_ref, o_ref):
    c = c_ref[...]
    s = c * (1.0 / (1.0 + jnp.exp(-c)))
    o_ref[0] = _dot(_bf(s), _bf(w_ref[0])) + b_ref[0]


def _ada_call(cond, ada_w, ada_b):
    tn = 512
    n = ada_w.shape[-1]
    return pl.pallas_call(
        _ada_kernel,
        grid=(DEPTH, n // tn),
        in_specs=[
            pl.BlockSpec((MOD_ROWS, D_MODEL), lambda l, j: (0, 0)),
            pl.BlockSpec((1, D_MODEL, tn), lambda l, j: (l, 0, j)),
            pl.BlockSpec((1, 1, tn), lambda l, j: (l, 0, j)),
        ],
        out_specs=pl.BlockSpec((1, MOD_ROWS, tn), lambda l, j: (l, 0, j)),
        out_shape=jax.ShapeDtypeStruct((DEPTH, MOD_ROWS, n), jnp.float32),
        compiler_params=_params("parallel", "parallel"),
        name="ada_mod",
    )(cond, ada_w, ada_b.reshape(DEPTH, 1, n))


def _head_epilogue(acc, spec, gains_ref, rope_refs):
    norm, gain_row, nvalid, rope_half = spec
    tn = acc.shape[1]
    outs = []
    for c in range(tn // LANES):
        xh = acc[:, c * LANES:(c + 1) * LANES]
        if norm:
            ms = jnp.sum(xh * xh, axis=-1, keepdims=True) * (1.0 / nvalid)
            xh = xh * lax.rsqrt(ms + EPS) * gains_ref[gain_row:gain_row + 1, :]
        if rope_half and rope_refs is not None:
            cos_ref, sa_ref, sb_ref = rope_refs
            xh = (xh * cos_ref[...]
                  + pltpu.roll(xh, LANES - rope_half, axis=1) * sa_ref[...]
                  + pltpu.roll(xh, rope_half, axis=1) * sb_ref[...])
        outs.append(xh)
    return outs


def _proj_kernel(*refs, mod, segs, seg_tiles, rope, n_out, out_tiles):
    it = iter(refs)
    x_ref = next(it)
    if mod:
        g_ref, sh_ref, sc_ref = next(it), next(it), next(it)
    w_ref = next(it)
    gains_ref = next(it)
    rope_refs = (next(it), next(it), next(it)) if rope else None
    out_refs = [next(it) for _ in range(n_out)]
    h_ref = next(it) if mod else None
    j = pl.program_id(1)

    if mod:
        @pl.when(j == 0)
        def _():
            h_ref[...] = _bf(_modulate(x_ref[...], g_ref[...], sh_ref[0], sc_ref[0]))
        a = h_ref[...]
    else:
        a = x_ref[...]
    acc = _dot(a, w_ref[...])
    seg = j // seg_tiles
    for s, spec in enumerate(segs):
        @pl.when(seg == s)
        def _(spec=spec, s=s):
            outs = _head_epilogue(acc, spec, gains_ref, rope_refs)
            o_ref = out_refs[out_tiles[s]]
            for c, xh in enumerate(outs):
                o_ref[:, c * LANES:(c + 1) * LANES] = xh.astype(o_ref.dtype)


def _proj_call(x, w, gains, segs, seg_width, *, mod=None, rope=None, mod_row=None,
               tm=TM, tn=TN, outs=None, name="proj"):
    m, k = x.shape
    n = w.shape[1]
    tm = min(tm, m)
    tn = min(tn, seg_width)
    assert m % tm == 0 and seg_width % tn == 0 and n % seg_width == 0
    seg_tiles = seg_width // tn
    bpb = SEQ // tm

    in_specs = [pl.BlockSpec((tm, k), lambda i, j: (i, 0))]
    args = [x]
    if mod is not None:
        gain, modarr, sh_c, sc_c = mod
        in_specs += [
            pl.BlockSpec((1, k), lambda i, j: (0, 0)),
            pl.BlockSpec((1, 1, k), lambda i, j: (mod_row(i), 0, sh_c)),
            pl.BlockSpec((1, 1, k), lambda i, j: (mod_row(i), 0, sc_c)),
        ]
        args += [gain, modarr, modarr]
    in_specs.append(pl.BlockSpec((k, tn), lambda i, j: (0, j)))
    args.append(w)
    in_specs.append(pl.BlockSpec(gains.shape, lambda i, j: (0, 0)))
    args.append(gains)
    if rope is not None:
        for t in rope:
            in_specs.append(pl.BlockSpec((tm, LANES), lambda i, j: (i % bpb, 0)))
            args.append(t)

    if outs is None:
        out_tiles = [0] * len(segs)
        out_specs = [pl.BlockSpec((tm, tn), lambda i, j: (i, j))]
        out_shape = [jax.ShapeDtypeStruct((m, n), jnp.bfloat16)]
    else:
        out_tiles, out_specs, out_shape = [], [], []
        start = 0
        for o_idx, nseg in enumerate(outs):
            out_tiles += [o_idx] * nseg
            first_tile = start * seg_tiles
            last_tile = (start + nseg) * seg_tiles - 1
            out_specs.append(pl.BlockSpec(
                (tm, tn),
                lambda i, j, f=first_tile, l=last_tile: (i, jnp.clip(j, f, l) - f)))
            out_shape.append(jax.ShapeDtypeStruct((m, nseg * seg_width), jnp.bfloat16))
            start += nseg
    kern = functools.partial(
        _proj_kernel, mod=mod is not None, segs=tuple(segs), seg_tiles=seg_tiles,
        rope=rope is not None, n_out=len(out_shape), out_tiles=tuple(out_tiles))
    res = pl.pallas_call(
        kern,
        grid=(m // tm, n // tn),
        in_specs=in_specs,
        out_specs=out_specs,
        out_shape=out_shape,
        scratch_shapes=[pltpu.VMEM((tm, k), jnp.bfloat16)] if mod is not None else [],
        compiler_params=_params("parallel", "arbitrary"),
        name=name,
    )(*args)
    return res[0] if outs is None else res


def _outproj_kernel(*refs, n_a):
    a_refs = refs[:n_a]
    w_refs = refs[n_a:2 * n_a]
    x_ref, g_ref, o_ref = refs[2 * n_a:]
    acc = _dot(a_refs[0][...], w_refs[0][...])
    for a_ref, w_ref in zip(a_refs[1:], w_refs[1:]):
        acc = acc + _dot(a_ref[...], w_ref[...])
    o_ref[...] = x_ref[...] + g_ref[0] * acc


def _outproj_call(a_list, w, x, modarr, gate_chunk, mod_row, *, tm=TM, tn=TN, name="outproj"):
    m, n = x.shape
    tm = min(tm, m)
    n_a = len(a_list)
    in_specs, args = [], []
    for a in a_list:
        in_specs.append(pl.BlockSpec((tm, a.shape[1]), lambda i, j: (i, 0)))
        args.append(a)
    row = 0
    for a in a_list:
        ka = a.shape[1]
        in_specs.append(pl.BlockSpec((ka, tn), lambda i, j, r=row // ka: (r, j)))
        args.append(w)
        row += ka
    nt = n // tn
    in_specs += [
        pl.BlockSpec((tm, tn), lambda i, j: (i, j)),
        pl.BlockSpec((1, 1, tn), lambda i, j: (mod_row(i), 0, gate_chunk * nt + j)),
    ]
    args += [x, modarr]
    return pl.pallas_call(
        functools.partial(_outproj_kernel, n_a=n_a),
        grid=(m // tm, nt),
        in_specs=in_specs,
        out_specs=pl.BlockSpec((tm, tn), lambda i, j: (i, j)),
        out_shape=jax.ShapeDtypeStruct((m, n), jnp.float32),
        compiler_params=_params("parallel", "arbitrary"),
        name=name,
    )(*args)


def _ffn_kernel(x_ref, xp_ref, xn_ref, g_ref, sh_ref, sc_ref, gt_ref, wa_ref, wb_ref,
                ca_ref, cb_ref, wo_ref, o_ref, h_ref, acc_ref, *, tm, seq_len):
    i = pl.program_id(0)
    j = pl.program_id(1)
    nj = pl.num_programs(1)

    @pl.when(j == 0)
    def _():
        gain, shift, scale = g_ref[...], sh_ref[0], sc_ref[0]
        h_ref[0:tm, :] = _bf(_modulate(x_ref[...], gain, shift, scale))
        halo = jnp.concatenate([xp_ref[...], xn_ref[...]], axis=0)
        h_ref[tm:tm + 2 * SUBLANES, :] = _bf(_modulate(halo, gain, shift, scale))
        acc_ref[...] = jnp.zeros_like(acc_ref)

    h = h_ref[...]
    row = lax.broadcasted_iota(jnp.int32, (tm, 1), 0)
    pos = (i * tm + row) % seq_len
    has_prev = pos != 0
    has_next = pos != seq_len - 1
    is_first = row == 0
    is_last = row == tm - 1

    def conv(w_ref, c_ref):
        u = _dot(h, w_ref[...])
        um = u[0:tm]
        u_before = u[tm + SUBLANES - 1:tm + SUBLANES]
        u_after = u[tm + SUBLANES:tm + SUBLANES + 1]
        prev = jnp.where(is_first, u_before, pltpu.roll(um, 1, axis=0))
        nxt = jnp.where(is_last, u_after, pltpu.roll(um, tm - 1, axis=0))
        prev = jnp.where(has_prev, prev, 0.0)
        nxt = jnp.where(has_next, nxt, 0.0)
        cw = c_ref[...]
        return prev * cw[0:1] + um * cw[1:2] + nxt * cw[2:3]

    a = conv(wa_ref, ca_ref)
    b = conv(wb_ref, cb_ref)
    act = a * (1.0 / (1.0 + jnp.exp(-a))) * b
    acc_ref[...] += _dot(_bf(act), wo_ref[...])

    @pl.when(j == nj - 1)
    def _():
        o_ref[...] = x_ref[...] + gt_ref[0] * acc_ref[...]


def _ffn_call(x, gain, modarr, w_in, conv_w, w_out, mod_row, seq_len, *, tm=TM, tn=TN_FF,
              name="conv_ffn"):
    m, d = x.shape
    tm = min(tm, m)
    nj = D_FF_PAD // tn
    hb = tm // SUBLANES
    last_hb = m // SUBLANES - 1
    kern = functools.partial(_ffn_kernel, tm=tm, seq_len=seq_len)
    return pl.pallas_call(
        kern,
        grid=(m // tm, nj),
        in_specs=[
            pl.BlockSpec((tm, d), lambda i, j: (i, 0)),
            pl.BlockSpec((SUBLANES, d), lambda i, j: (jnp.maximum(i * hb - 1, 0), 0)),
            pl.BlockSpec((SUBLANES, d), lambda i, j: (jnp.minimum((i + 1) * hb, last_hb), 0)),
            pl.BlockSpec((1, d), lambda i, j: (0, 0)),
            pl.BlockSpec((1, 1, d), lambda i, j: (mod_row(i), 0, 3)),
            pl.BlockSpec((1, 1, d), lambda i, j: (mod_row(i), 0, 4)),
            pl.BlockSpec((1, 1, d), lambda i, j: (mod_row(i), 0, 5)),
            pl.BlockSpec((d, tn), lambda i, j: (0, j)),
            pl.BlockSpec((d, tn), lambda i, j: (0, nj + j)),
            pl.BlockSpec((3, tn), lambda i, j: (0, j)),
            pl.BlockSpec((3, tn), lambda i, j: (0, nj + j)),
            pl.BlockSpec((tn, d), lambda i, j: (j, 0)),
        ],
        out_specs=pl.BlockSpec((tm, d), lambda i, j: (i, 0)),
        out_shape=jax.ShapeDtypeStruct((m, d), jnp.float32),
        scratch_shapes=[
            pltpu.VMEM((tm + 2 * SUBLANES, d), jnp.bfloat16),
            pltpu.VMEM((tm, d), jnp.float32),
        ],
        compiler_params=_params("parallel", "arbitrary"),
        name=name,
    )(x, x, x, gain, modarr, modarr, modarr, w_in, w_in, conv_w, conv_w, w_out)


def _na_kernel(q_ref, k_ref, v_ref, kc_ref, vc_ref, bias_ref, o_ref):
    kc = kc_ref[...]
    vc = vc_ref[...]
    n_groups = GRID_H // NA_GROUP_ROWS

    def body(g, carry):
        r0 = g * NA_GROUP_ROWS
        ks = jnp.clip(r0 - NA_WIN_H // 2, 0, GRID_H - NA_KEY_ROWS)
        q0 = pl.multiple_of(g * NA_Q, NA_Q)
        k0 = pl.multiple_of(ks * GRID_W, GRID_W)
        tb = jnp.where(g == 0, 0, jnp.where(g == n_groups - 1, 2, 1))
        q = q_ref[pl.ds(q0, NA_Q), :]
        kw = k_ref[pl.ds(k0, NA_K), :]
        vw = v_ref[pl.ds(k0, NA_K), :]
        sw = _dot_nt(q, kw) + bias_ref[0, tb]
        sc = _dot_nt(q, kc)
        mx = jnp.maximum(jnp.max(sw, axis=-1, keepdims=True), jnp.max(sc, axis=-1, keepdims=True))
        pw = jnp.exp(sw - mx)
        pc = jnp.exp(sc - mx)
        den = jnp.sum(pw, axis=-1, keepdims=True) + jnp.sum(pc, axis=-1, keepdims=True)
        o = _dot(_bf(pw), vw) + _dot(_bf(pc), vc)
        o_ref[pl.ds(q0, NA_Q), :] = _bf(o / den)
        return carry

    lax.fori_loop(0, n_groups, body, 0)


def _na_call(qkv, qkv_c, bias):
    nb = SEQ // SEQ
    del nb
    return pl.pallas_call(
        _na_kernel,
        grid=(BATCH, NA_HEADS),
        in_specs=[
            pl.BlockSpec((SEQ, LANES), lambda b, h: (b, h)),
            pl.BlockSpec((SEQ, LANES), lambda b, h: (b, NA_HEADS + h)),
            pl.BlockSpec((SEQ, LANES), lambda b, h: (b, 2 * NA_HEADS + h)),
            pl.BlockSpec((CTX_LEN, LANES), lambda b, h: (b, NA_HEADS + h)),
            pl.BlockSpec((CTX_LEN, LANES), lambda b, h: (b, 2 * NA_HEADS + h)),
            pl.BlockSpec((1, 3, NA_Q, NA_K), lambda b, h: (h, 0, 0, 0)),
        ],
        out_specs=pl.BlockSpec((SEQ, LANES), lambda b, h: (b, h)),
        out_shape=jax.ShapeDtypeStruct((BATCH * SEQ, NA_WIDTH), jnp.bfloat16),
        compiler_params=_params("parallel", "parallel"),
        name="na_attn",
    )(qkv, qkv, qkv, qkv_c, qkv_c, bias)


def _na_bias_tables(rpb):
    tabs = []
    qr = jnp.arange(NA_GROUP_ROWS, dtype=jnp.int32)[:, None, None, None]
    qc = jnp.arange(GRID_W, dtype=jnp.int32)[None, :, None, None]
    kr = jnp.arange(NA_KEY_ROWS, dtype=jnp.int32)[None, None, :, None]
    kc = jnp.arange(GRID_W, dtype=jnp.int32)[None, None, None, :]
    n_groups = GRID_H // NA_GROUP_ROWS
    for g in (0, 1, n_groups - 1):
        r0 = g * NA_GROUP_ROWS
        ks = min(max(r0 - NA_WIN_H // 2, 0), GRID_H - NA_KEY_ROWS)
        r = r0 + qr
        kra = ks + kr
        row0 = jnp.clip(r - NA_WIN_H // 2, 0, GRID_H - NA_WIN_H)
        col0 = jnp.clip(qc - NA_WIN_W // 2, 0, GRID_W - NA_WIN_W)
        valid = (kra >= row0) & (kra < row0 + NA_WIN_H) & (kc >= col0) & (kc < col0 + NA_WIN_W)
        dr = jnp.clip(kra - r + (NA_WIN_H - 1), 0, 2 * NA_WIN_H - 2)
        dc = jnp.clip(kc - qc + (NA_WIN_W - 1), 0, 2 * NA_WIN_W - 2)
        dr, dc, valid = jnp.broadcast_arrays(dr, dc, valid)
        b = rpb[:, dr, dc]
        b = jnp.where(valid[None], b, NEG)
        tabs.append(b.reshape(NA_HEADS, NA_Q, NA_K))
    return jnp.stack(tabs, axis=1)


def _attn_kernel(*refs, n_qparts, n_src, n_kparts, src_rows):
    it = iter(refs)
    q_refs = [next(it) for _ in range(n_qparts)]
    srcs = []
    for _ in range(n_src):
        k_refs = [next(it) for _ in range(n_kparts)]
        srcs.append((k_refs, next(it)))
    o_ref = next(it)
    k_scr, v_scr = next(it), next(it)

    @pl.when(pl.program_id(2) == 0)
    def _():
        r = 0
        for (k_refs, v_ref), rows in zip(srcs, src_rows):
            for p, k_ref in enumerate(k_refs):
                k_scr[r:r + rows, p * LANES:(p + 1) * LANES] = k_ref[...]
            v_scr[r:r + rows, :] = v_ref[...]
            r += rows

    q = jnp.concatenate([qr[...] for qr in q_refs], axis=-1) if n_qparts > 1 else q_refs[0][...]
    s = _dot_nt(q, k_scr[...])
    mx = jnp.max(s, axis=-1, keepdims=True)
    p = jnp.exp(s - mx)
    den = jnp.sum(p, axis=-1, keepdims=True)
    o = _dot(_bf(p), v_scr[...])
    o_ref[...] = _bf(o / den)


def _attn_call(q_parts, sources, n_heads, lq, tq, dv, name):
    tq = min(tq, lq)
    nq = lq // tq
    in_specs, args = [], []
    for arr, cf in q_parts:
        in_specs.append(pl.BlockSpec((tq, LANES), lambda b, h, i, cf=cf: (b * nq + i, cf(h))))
        args.append(arr)
    src_rows = []
    n_kparts = len(sources[0][1])
    for rows, k_parts, (v_arr, vcf) in sources:
        src_rows.append(rows)
        for arr, cf in k_parts:
            in_specs.append(pl.BlockSpec((rows, LANES), lambda b, h, i, cf=cf: (b, cf(h))))
            args.append(arr)
        in_specs.append(pl.BlockSpec((rows, dv), lambda b, h, i, cf=vcf: (b, cf(h))))
        args.append(v_arr)
    nk = sum(src_rows)
    kern = functools.partial(_attn_kernel, n_qparts=len(q_parts), n_src=len(sources),
                             n_kparts=n_kparts, src_rows=tuple(src_rows))
    return pl.pallas_call(
        kern,
        grid=(BATCH, n_heads, nq),
        in_specs=in_specs,
        out_specs=pl.BlockSpec((tq, dv), lambda b, h, i: (b * nq + i, h)),
        out_shape=jax.ShapeDtypeStruct((BATCH * lq, n_heads * dv), jnp.bfloat16),
        scratch_shapes=[
            pltpu.VMEM((nk, n_kparts * LANES), jnp.bfloat16),
            pltpu.VMEM((nk, dv), jnp.bfloat16),
        ],
        compiler_params=_params("parallel", "parallel", "arbitrary"),
        name=name,
    )(*args)


def _diff_kernel(*refs, n_src, src_rows, lam_init):
    it = iter(refs)
    q1_ref, q2_ref = next(it), next(it)
    srcs = [(next(it), next(it), next(it)) for _ in range(n_src)]
    lam_ref, sub_ref = next(it), next(it)
    o_ref = next(it)
    k1_scr, k2_scr, v_scr = next(it), next(it), next(it)

    @pl.when(pl.program_id(2) == 0)
    def _():
        r = 0
        for (k1_ref, k2_ref, v_ref), rows in zip(srcs, src_rows):
            k1_scr[r:r + rows, :] = k1_ref[...]
            k2_scr[r:r + rows, :] = k2_ref[...]
            v_scr[r:r + rows, :] = v_ref[...]
            r += rows

    lf = lam_ref[...]
    lam = (jnp.exp(jnp.sum(lf[0:1] * lf[1:2], axis=-1, keepdims=True))
           - jnp.exp(jnp.sum(lf[2:3] * lf[3:4], axis=-1, keepdims=True)) + lam_init)
    v = v_scr[...]

    def branch(q_ref, k_scr):
        s = _dot_nt(q_ref[...], k_scr[...])
        mx = jnp.max(s, axis=-1, keepdims=True)
        p = jnp.exp(s - mx)
        den = jnp.sum(p, axis=-1, keepdims=True)
        return _dot(_bf(p), v) / den

    o = branch(q1_ref, k1_scr) - lam * branch(q2_ref, k2_scr)
    ms = jnp.mean(o * o, axis=-1, keepdims=True)
    o_ref[...] = _bf(o * lax.rsqrt(ms + EPS) * sub_ref[...] * (1.0 - lam_init))


def _diff_call(qkv_q, lq, sources, diff_lambda, subln, lam_init, tq, name):
    tq = min(tq, lq)
    nq = lq // tq
    qb0 = 3 * NA_WIDTH // LANES
    kb0 = (3 * NA_WIDTH + DIFF_WIDTH) // LANES
    vb0 = (3 * NA_WIDTH + 2 * DIFF_WIDTH) // (2 * LANES)
    in_specs = [
        pl.BlockSpec((tq, LANES), lambda b, h, i: (b * nq + i, qb0 + 2 * h)),
        pl.BlockSpec((tq, LANES), lambda b, h, i: (b * nq + i, qb0 + 2 * h + 1)),
    ]
    args = [qkv_q, qkv_q]
    src_rows = []
    for rows, arr in sources:
        src_rows.append(rows)
        in_specs += [
            pl.BlockSpec((rows, LANES), lambda b, h, i: (b, kb0 + 2 * h)),
            pl.BlockSpec((rows, LANES), lambda b, h, i: (b, kb0 + 2 * h + 1)),
            pl.BlockSpec((rows, 2 * LANES), lambda b, h, i: (b, vb0 + h)),
        ]
        args += [arr, arr, arr]
    in_specs += [
        pl.BlockSpec((4, LANES), lambda b, h, i: (0, 0)),
        pl.BlockSpec((1, 2 * LANES), lambda b, h, i: (0, 0)),
    ]
    args += [diff_lambda, subln]
    nk = sum(src_rows)
    kern = functools.partial(_diff_kernel, n_src=len(sources), src_rows=tuple(src_rows),
                             lam_init=lam_init)
    return pl.pallas_call(
        kern,
        grid=(BATCH, DIFF_HEADS, nq),
        in_specs=in_specs,
        out_specs=pl.BlockSpec((tq, 2 * LANES), lambda b, h, i: (b * nq + i, h)),
        out_shape=jax.ShapeDtypeStruct((BATCH * lq, DIFF_WIDTH), jnp.bfloat16),
        scratch_shapes=[
            pltpu.VMEM((nk, LANES), jnp.bfloat16),
            pltpu.VMEM((nk, LANES), jnp.bfloat16),
            pltpu.VMEM((nk, 2 * LANES), jnp.bfloat16),
        ],
        compiler_params=_params("parallel", "parallel", "arbitrary"),
        name=name,
    )(*args)


def _rope_tables(d):
    h = d // 2
    half = h // 2
    t = jnp.arange(SEQ, dtype=jnp.int32)
    freqs = ROPE_BASE ** (-jnp.arange(half, dtype=jnp.float32) / half)

    def cs(pos):
        ang = pos.astype(jnp.float32)[:, None] * freqs[None, :]
        return jnp.cos(ang), jnp.sin(ang)

    cr, sr = cs(t // GRID_W)
    cc, sc = cs(t % GRID_W)
    cos = jnp.concatenate([cr, cr, cc, cc], -1)
    sin = jnp.concatenate([sr, sr, sc, sc], -1)
    first = (jnp.arange(d) % h) < half
    sin_a = jnp.where(first[None, :], -sin, 0.0)
    sin_b = jnp.where(first[None, :], 0.0, sin)
    pad = LANES - d
    if pad:
        cos = jnp.pad(cos, ((0, 0), (0, pad)), constant_values=1.0)
        sin_a = jnp.pad(sin_a, ((0, 0), (0, pad)))
        sin_b = jnp.pad(sin_b, ((0, 0), (0, pad)))
    return (cos, sin_a, sin_b), half


def _pad_lanes(v, fill=0.0):
    return jnp.pad(v, (0, LANES - v.shape[0]), constant_values=fill)


def _lat_row(tm):
    bpb = SEQ // tm
    return lambda i: i // bpb


def _ctx_row(i):
    return CTX_MOD_ROW


def _even_layer(x, xc, modarr, norm_mix, norm_ffn, ffn_w, w_in, w_out, qn_a, kn_a, rpb,
                qn_b, kn_b, diff_lambda, subln, lam_init, with_ctx):
    scale = HEAD_DIM ** -0.5
    gains = jnp.stack([qn_a * scale, kn_a, qn_b * scale, kn_b] + [jnp.ones_like(qn_a)] * 4)
    rope, half = _rope_tables(HEAD_DIM)
    segs = [(True, 0, HEAD_DIM, 0), (True, 1, HEAD_DIM, 0), (False, 0, HEAD_DIM, 0),
            (True, 2, HEAD_DIM, half), (True, 3, HEAD_DIM, half), (False, 0, HEAD_DIM, 0)]
    w_in = _bf(w_in)
    w_out = _bf(w_out)
    gain_mix = norm_mix[None, :]
    lat_row = _lat_row(TM)

    qkv = _proj_call(x, w_in, gains, segs, NA_WIDTH, mod=(gain_mix, modarr, 0, 1), rope=rope,
                     mod_row=lat_row, name="even_proj")
    qkv_c = _proj_call(xc, w_in, gains, segs, NA_WIDTH, mod=(gain_mix, modarr, 0, 1), rope=None,
                       mod_row=_ctx_row, name="even_proj_ctx")

    bias = _na_bias_tables(rpb)
    oa = _na_call(qkv, qkv_c, bias)
    ob = _diff_call(qkv, SEQ, [(SEQ, qkv), (CTX_LEN, qkv_c)], diff_lambda, subln[None, :],
                    lam_init, TQ_DIFF, "diff_attn")
    x = _outproj_call([oa, ob], w_out, x, modarr, 2, lat_row, name="even_out")
    x = _ffn_call(x, norm_ffn[None, :], modarr, *ffn_w, lat_row, SEQ)
    if with_ctx:
        oa_c = _attn_call(
            [(qkv_c, lambda h: h)],
            [(CTX_LEN, [(qkv_c, lambda h: NA_HEADS + h)], (qkv_c, lambda h: 2 * NA_HEADS + h))],
            NA_HEADS, CTX_LEN, CTX_LEN, HEAD_DIM, "ctx_attn")
        ob_c = _diff_call(qkv_c, CTX_LEN, [(CTX_LEN, qkv_c)], diff_lambda, subln[None, :],
                          lam_init, CTX_LEN, "diff_attn_ctx")
        xc = _outproj_call([oa_c, ob_c], w_out, xc, modarr, 2, _ctx_row, name="even_out_ctx")
        xc = _ffn_call(xc, norm_ffn[None, :], modarr, *ffn_w, _ctx_row, CTX_LEN,
                       name="conv_ffn_ctx")
    return x, xc


def _odd_layer(x, xc, modarr, norm_mix, norm_ffn, ffn_w, w_down, q_a_norm, kv_a_norm, w_uq,
               w_ukv, qn_nope, qn_rope, kn_nope, kn_rope, w_out, with_ctx):
    scale = (MLA_NOPE + MLA_ROPE) ** -0.5
    rope, half = _rope_tables(MLA_ROPE)
    gain_mix = norm_mix[None, :]
    lat_row = _lat_row(TM)

    w_down_p = _bf(jnp.pad(w_down, ((0, 0), (0, MLA_DOWN_PAD - MLA_DOWN))))
    zeros_r = jnp.zeros((MLA_Q_RANK,), jnp.float32)
    kr_gain = jnp.concatenate([_pad_lanes(kn_rope), zeros_r[:MLA_Q_RANK - LANES]])
    gains_d = jnp.stack([q_a_norm, kv_a_norm, kr_gain] + [zeros_r] * 5)

    def down(xx, rp, row, name):
        return _down_call(xx, w_down_p, gains_d, (gain_mix, modarr, 0, 1), rp, half, row, name)

    qa, kva, kr = down(x, rope, lat_row, "mla_down")
    _, kva_c, kr_c = down(xc, None, _ctx_row, "mla_down_ctx")

    wq = w_uq.reshape(MLA_Q_RANK, MLA_HEADS, MLA_NOPE + MLA_ROPE)
    wq_rope = jnp.pad(wq[:, :, MLA_NOPE:], ((0, 0), (0, 0), (0, LANES - MLA_ROPE)))
    wq_p = _bf(jnp.concatenate([wq[:, :, :MLA_NOPE].reshape(MLA_Q_RANK, -1),
                                wq_rope.reshape(MLA_Q_RANK, -1)], -1))
    gains_q = jnp.stack([qn_nope * scale, _pad_lanes(qn_rope) * scale]
                        + [jnp.zeros((LANES,), jnp.float32)] * 6)
    segs_q = [(True, 0, MLA_NOPE, 0), (True, 1, MLA_ROPE, half)]
    q = _proj_call(qa, wq_p, gains_q, segs_q, MLA_HEADS * LANES, rope=rope, name="mla_q")

    wkv = w_ukv.reshape(MLA_KV_RANK, MLA_HEADS, MLA_NOPE + MLA_V)
    wkv_p = _bf(jnp.concatenate([wkv[:, :, :MLA_NOPE].reshape(MLA_KV_RANK, -1),
                                 wkv[:, :, MLA_NOPE:].reshape(MLA_KV_RANK, -1)], -1))
    gains_kv = jnp.stack([kn_nope] + [jnp.zeros((LANES,), jnp.float32)] * 7)
    segs_kv = [(True, 0, MLA_NOPE, 0), (False, 0, MLA_NOPE, 0)]
    kv = _proj_call(kva, wkv_p, gains_kv, segs_kv, MLA_HEADS * LANES, name="mla_kv")
    kv_c = _proj_call(kva_c, wkv_p, gains_kv, segs_kv, MLA_HEADS * LANES, name="mla_kv_ctx")

    def src(rows, kv_arr, kr_arr):
        return (rows, [(kv_arr, lambda h: h), (kr_arr, lambda h: 0)],
                (kv_arr, lambda h: MLA_HEADS + h))

    o = _attn_call([(q, lambda h: h), (q, lambda h: MLA_HEADS + h)],
                   [src(SEQ, kv, kr), src(CTX_LEN, kv_c, kr_c)],
                   MLA_HEADS, SEQ, TQ_MLA, MLA_V, "mla_attn")
    x = _outproj_call([o], _bf(w_out), x, modarr, 2, lat_row, name="mla_out")
    x = _ffn_call(x, norm_ffn[None, :], modarr, *ffn_w, lat_row, SEQ)
    assert not with_ctx
    return x, xc


def _down_kernel(x_ref, g_ref, sh_ref, sc_ref, w_ref, gains_ref, *rest, rope, rope_half):
    if rope:
        cos_ref, sa_ref, sb_ref = rest[:3]
        rest = rest[3:]
    qa_ref, kva_ref, kr_ref = rest
    h = _bf(_modulate(x_ref[...], g_ref[...], sh_ref[0], sc_ref[0]))
    acc = _dot(h, w_ref[...])

    def norm(xs, row, nvalid):
        ms = jnp.sum(xs * xs, axis=-1, keepdims=True) * (1.0 / nvalid)
        return xs * lax.rsqrt(ms + EPS) * gains_ref[row:row + 1, 0:xs.shape[1]]

    qa_ref[...] = _bf(norm(acc[:, 0:MLA_Q_RANK], 0, MLA_Q_RANK))
    kva_ref[...] = _bf(norm(acc[:, MLA_Q_RANK:MLA_Q_RANK + MLA_KV_RANK], 1, MLA_KV_RANK))
    kr = norm(acc[:, MLA_Q_RANK + MLA_KV_RANK:MLA_DOWN_PAD], 2, MLA_ROPE)
    if rope:
        kr = (kr * cos_ref[...]
              + pltpu.roll(kr, LANES - rope_half, axis=1) * sa_ref[...]
              + pltpu.roll(kr, rope_half, axis=1) * sb_ref[...])
    kr_ref[...] = _bf(kr)


def _down_call(x, w, gains, mod, rope, rope_half, mod_row, name, tm=TM):
    m, k = x.shape
    tm = min(tm, m)
    bpb = SEQ // tm
    gain, modarr, sh_c, sc_c = mod
    in_specs = [
        pl.BlockSpec((tm, k), lambda i: (i, 0)),
        pl.BlockSpec((1, k), lambda i: (0, 0)),
        pl.BlockSpec((1, 1, k), lambda i: (mod_row(i), 0, sh_c)),
        pl.BlockSpec((1, 1, k), lambda i: (mod_row(i), 0, sc_c)),
        pl.BlockSpec(w.shape, lambda i: (0, 0)),
        pl.BlockSpec(gains.shape, lambda i: (0, 0)),
    ]
    args = [x, gain, modarr, modarr, w, gains]
    if rope is not None:
        for t in rope:
            in_specs.append(pl.BlockSpec((tm, LANES), lambda i: (i % bpb, 0)))
            args.append(t)
    return pl.pallas_call(
        functools.partial(_down_kernel, rope=rope is not None, rope_half=rope_half),
        grid=(m // tm,),
        in_specs=in_specs,
        out_specs=[
            pl.BlockSpec((tm, MLA_Q_RANK), lambda i: (i, 0)),
            pl.BlockSpec((tm, MLA_KV_RANK), lambda i: (i, 0)),
            pl.BlockSpec((tm, LANES), lambda i: (i, 0)),
        ],
        out_shape=[
            jax.ShapeDtypeStruct((m, MLA_Q_RANK), jnp.bfloat16),
            jax.ShapeDtypeStruct((m, MLA_KV_RANK), jnp.bfloat16),
            jax.ShapeDtypeStruct((m, LANES), jnp.bfloat16),
        ],
        compiler_params=_params("parallel"),
        name=name,
    )(*args)


def _ffn_weights(w_in, conv_w, w_out):
    pad = D_FF_PAD - D_FF
    wa = jnp.pad(w_in[:, :D_FF], ((0, 0), (0, pad)))
    wb = jnp.pad(w_in[:, D_FF:], ((0, 0), (0, pad)))
    ca = jnp.pad(conv_w[:, :D_FF], ((0, 0), (0, pad)))
    cb = jnp.pad(conv_w[:, D_FF:], ((0, 0), (0, pad)))
    return (_bf(jnp.concatenate([wa, wb], -1)), jnp.concatenate([ca, cb], -1),
            _bf(jnp.pad(w_out, ((0, pad), (0, 0)))))


def kernel(x, c, ctx, c_ctx, ada_w, ada_b, norm_mix, norm_ffn, ffn_w_in, ffn_conv, ffn_w_out, even_w_in, even_w_out, na_q_norm, na_k_norm, na_rpb, diff_q_norm, diff_k_norm, diff_lambda, diff_subln, mla_w_down, mla_q_a_norm, mla_kv_a_norm, mla_w_uq, mla_w_ukv, mla_q_nope_norm, mla_q_rope_norm, mla_k_nope_norm, mla_k_rope_norm, mla_w_out):
    cond = jnp.concatenate(
        [c, c_ctx[None, :], jnp.zeros((MOD_ROWS - BATCH - 1, D_MODEL), jnp.float32)], 0)
    mod = _ada_call(cond, ada_w, ada_b)
    xl = x.reshape(BATCH * SEQ, D_MODEL)
    xc = ctx.reshape(BATCH * CTX_LEN, D_MODEL)
    for l in range(DEPTH):
        with_ctx = l < DEPTH - 1
        modarr = mod[l].reshape(MOD_ROWS, 1, 6 * D_MODEL)
        ffn_w = _ffn_weights(ffn_w_in[l], ffn_conv[l], ffn_w_out[l])
        i = l // 2
        if l % 2 == 0:
            lam_init = 0.8 - 0.6 * math.exp(-0.3 * l)
            xl, xc = _even_layer(xl, xc, modarr, norm_mix[l], norm_ffn[l], ffn_w, even_w_in[i],
                                 even_w_out[i], na_q_norm[i], na_k_norm[i], na_rpb[i],
                                 diff_q_norm[i], diff_k_norm[i], diff_lambda[i], diff_subln[i],
                                 lam_init, with_ctx)
        else:
            xl, xc = _odd_layer(xl, xc, modarr, norm_mix[l], norm_ffn[l], ffn_w, mla_w_down[i],
                                mla_q_a_norm[i], mla_kv_a_norm[i], mla_w_uq[i], mla_w_ukv[i],
                                mla_q_nope_norm[i], mla_q_rope_norm[i], mla_k_nope_norm[i],
                                mla_k_rope_norm[i], mla_w_out[i], with_ctx)
    return xl.reshape(BATCH, SEQ, D_MODEL)
```

```python
import functools
import math

import jax
import jax.numpy as jnp
from jax import lax
from jax.experimental import pallas as pl
from jax.experimental.pallas import tpu as pltpu

D_MODEL = 2048
BATCH = 4
SEQ = 4096
DEPTH = 2
GRID_W = 64
GRID_H = SEQ // GRID_W
CTX_LEN = 256
HEAD_DIM = 128
NA_HEADS = 8
NA_WIN_H = 8
NA_WIN_W = 16
DIFF_HEADS = 4
NA_WIDTH = 1024
DIFF_WIDTH = 1024
EVEN_PROJ = 6144
MLA_HEADS = 16
MLA_Q_RANK = 512
MLA_KV_RANK = 512
MLA_NOPE = 128
MLA_ROPE = 64
MLA_V = 128
MLA_DOWN = 1088
D_FF = 5504
ROPE_BASE = 10000.0
EPS = 1e-6

LANES = 128
SUBLANES = 8
MOD_ROWS = 8
CTX_MOD_ROW = BATCH
D_FF_PAD = 5632
MLA_DOWN_PAD = 1152
VMEM_LIMIT = 56 * 1024 * 1024

TM = 512
TN = 1024
TN_FF = 512
NA_GROUP_ROWS = 4
NA_KEY_ROWS = 12
NA_Q = NA_GROUP_ROWS * GRID_W
NA_K = NA_KEY_ROWS * GRID_W
TQ_DIFF = 256
TQ_MLA = 256
NEG = -1e30

_NT = (((1,), (1,)), ((), ()))


def _params(*sem):
    return pltpu.CompilerParams(dimension_semantics=sem, vmem_limit_bytes=VMEM_LIMIT)


def _bf(x):
    return x.astype(jnp.bfloat16)


def _dot(a, b):
    return jnp.dot(a, b, preferred_element_type=jnp.float32)


def _dot_nt(a, b):
    return lax.dot_general(a, b, _NT, preferred_element_type=jnp.float32)


def _modulate(x, gain, shift, scale):
    ms = jnp.mean(x * x, axis=-1, keepdims=True)
    y = x * lax.rsqrt(ms + EPS) * gain
    return y * (1.0 + scale) + shift


def _ada_kernel(c_ref, w_ref, b_ref, o_ref):
    c = c_ref[...]
    s = c * (1.0 / (1.0 + jnp.exp(-c)))
    o_ref[0] = _dot(_bf(s), _bf(w_ref[0])) + b_ref[0]


def _ada_call(cond, ada_w, ada_b):
    tn = 512
    n = ada_w.shape[-1]
    return pl.pallas_call(
        _ada_kernel,
        grid=(DEPTH, n // tn),
        in_specs=[
            pl.BlockSpec((MOD_ROWS, D_MODEL), lambda l, j: (0, 0)),
            pl.BlockSpec((1, D_MODEL, tn), lambda l, j: (l, 0, j)),
            pl.BlockSpec((1, 1, tn), lambda l, j: (l, 0, j)),
        ],
        out_specs=pl.BlockSpec((1, MOD_ROWS, tn), lambda l, j: (l, 0, j)),
        out_shape=jax.ShapeDtypeStruct((DEPTH, MOD_ROWS, n), jnp.float32),
        compiler_params=_params("parallel", "parallel"),
        name="ada_mod",
    )(cond, ada_w, ada_b.reshape(DEPTH, 1, n))


def _head_epilogue(acc, spec, gains_ref, rope_refs):
    norm, gain_row, nvalid, rope_half = spec
    tn = acc.shape[1]
    outs = []
    for c in range(tn // LANES):
        xh = acc[:, c * LANES:(c + 1) * LANES]
        if norm:
            ms = jnp.sum(xh * xh, axis=-1, keepdims=True) * (1.0 / nvalid)
            xh = xh * lax.rsqrt(ms + EPS) * gains_ref[gain_row:gain_row + 1, :]
        if rope_half and rope_refs is not None:
            cos_ref, sa_ref, sb_ref = rope_refs
            xh = (xh * cos_ref[...]
                  + pltpu.roll(xh, LANES - rope_half, axis=1) * sa_ref[...]
                  + pltpu.roll(xh, rope_half, axis=1) * sb_ref[...])
        outs.append(xh)
    return outs


def _proj_kernel(*refs, mod, segs, seg_tiles, rope, n_out, out_tiles):
    it = iter(refs)
    x_ref = next(it)
    if mod:
        g_ref, sh_ref, sc_ref = next(it), next(it), next(it)
    w_ref = next(it)
    gains_ref = next(it)
    rope_refs = (next(it), next(it), next(it)) if rope else None
    out_refs = [next(it) for _ in range(n_out)]
    h_ref = next(it) if mod else None
    j = pl.program_id(1)

    if mod:
        @pl.when(j == 0)
        def _():
            h_ref[...] = _bf(_modulate(x_ref[...], g_ref[...], sh_ref[0], sc_ref[0]))
        a = h_ref[...]
    else:
        a = x_ref[...]
    acc = _dot(a, w_ref[...])
    seg = j // seg_tiles
    for s, spec in enumerate(segs):
        @pl.when(seg == s)
        def _(spec=spec, s=s):
            outs = _head_epilogue(acc, spec, gains_ref, rope_refs)
            o_ref = out_refs[out_tiles[s]]
            for c, xh in enumerate(outs):
                o_ref[:, c * LANES:(c + 1) * LANES] = xh.astype(o_ref.dtype)


def _proj_call(x, w, gains, segs, seg_width, *, mod=None, rope=None, mod_row=None,
               tm=TM, tn=TN, outs=None, name="proj"):
    m, k = x.shape
    n = w.shape[1]
    tm = min(tm, m)
    tn = min(tn, seg_width)
    assert m % tm == 0 and seg_width % tn == 0 and n % seg_width == 0
    seg_tiles = seg_width // tn
    bpb = SEQ // tm

    in_specs = [pl.BlockSpec((tm, k), lambda i, j: (i, 0))]
    args = [x]
    if mod is not None:
        gain, modarr, sh_c, sc_c = mod
        in_specs += [
            pl.BlockSpec((1, k), lambda i, j: (0, 0)),
            pl.BlockSpec((1, 1, k), lambda i, j: (mod_row(i), 0, sh_c)),
            pl.BlockSpec((1, 1, k), lambda i, j: (mod_row(i), 0, sc_c)),
        ]
        args += [gain, modarr, modarr]
    in_specs.append(pl.BlockSpec((k, tn), lambda i, j: (0, j)))
    args.append(w)
    in_specs.append(pl.BlockSpec(gains.shape, lambda i, j: (0, 0)))
    args.append(gains)
    if rope is not None:
        for t in rope:
            in_specs.append(pl.BlockSpec((tm, LANES), lambda i, j: (i % bpb, 0)))
            args.append(t)

    if outs is None:
        out_tiles = [0] * len(segs)
        out_specs = [pl.BlockSpec((tm, tn), lambda i, j: (i, j))]
        out_shape = [jax.ShapeDtypeStruct((m, n), jnp.bfloat16)]
    else:
        out_tiles, out_specs, out_shape = [], [], []
        start = 0
        for o_idx, nseg in enumerate(outs):
            out_tiles += [o_idx] * nseg
            first_tile = start * seg_tiles
            last_tile = (start + nseg) * seg_tiles - 1
            out_specs.append(pl.BlockSpec(
                (tm, tn),
                lambda i, j, f=first_tile, l=last_tile: (i, jnp.clip(j, f, l) - f)))
            out_shape.append(jax.ShapeDtypeStruct((m, nseg * seg_width), jnp.bfloat16))
            start += nseg
    kern = functools.partial(
        _proj_kernel, mod=mod is not None, segs=tuple(segs), seg_tiles=seg_tiles,
        rope=rope is not None, n_out=len(out_shape), out_tiles=tuple(out_tiles))
    res = pl.pallas_call(
        kern,
        grid=(m // tm, n // tn),
        in_specs=in_specs,
        out_specs=out_specs,
        out_shape=out_shape,
        scratch_shapes=[pltpu.VMEM((tm, k), jnp.bfloat16)] if mod is not None else [],
        compiler_params=_params("parallel", "arbitrary"),
        name=name,
    )(*args)
    return res[0] if outs is None else res


def _outproj_kernel(*refs, n_a):
    a_refs = refs[:n_a]
    w_refs = refs[n_a:2 * n_a]
    x_ref, g_ref, o_ref = refs[2 * n_a:]
    acc = _dot(a_refs[0][...], w_refs[0][...])
    for a_ref, w_ref in zip(a_refs[1:], w_refs[1:]):
        acc = acc + _dot(a_ref[...], w_ref[...])
    o_ref[...] = x_ref[...] + g_ref[0] * acc


def _outproj_call(a_list, w, x, modarr, gate_chunk, mod_row, *, tm=TM, tn=TN, name="outproj"):
    m, n = x.shape
    tm = min(tm, m)
    n_a = len(a_list)
    in_specs, args = [], []
    for a in a_list:
        in_specs.append(pl.BlockSpec((tm, a.shape[1]), lambda i, j: (i, 0)))
        args.append(a)
    row = 0
    for a in a_list:
        ka = a.shape[1]
        in_specs.append(pl.BlockSpec((ka, tn), lambda i, j, r=row // ka: (r, j)))
        args.append(w)
        row += ka
    nt = n // tn
    in_specs += [
        pl.BlockSpec((tm, tn), lambda i, j: (i, j)),
        pl.BlockSpec((1, 1, tn), lambda i, j: (mod_row(i), 0, gate_chunk * nt + j)),
    ]
    args += [x, modarr]
    return pl.pallas_call(
        functools.partial(_outproj_kernel, n_a=n_a),
        grid=(m // tm, nt),
        in_specs=in_specs,
        out_specs=pl.BlockSpec((tm, tn), lambda i, j: (i, j)),
        out_shape=jax.ShapeDtypeStruct((m, n), jnp.float32),
        compiler_params=_params("parallel", "arbitrary"),
        name=name,
    )(*args)


def _ffn_kernel(x_ref, xp_ref, xn_ref, g_ref, sh_ref, sc_ref, gt_ref, wa_ref, wb_ref,
                ca_ref, cb_ref, wo_ref, o_ref, h_ref, acc_ref, *, tm, seq_len):
    i = pl.program_id(0)
    j = pl.program_id(1)
    nj = pl.num_programs(1)

    @pl.when(j == 0)
    def _():
        gain, shift, scale = g_ref[...], sh_ref[0], sc_ref[0]
        h_ref[0:tm, :] = _bf(_modulate(x_ref[...], gain, shift, scale))
        halo = jnp.concatenate([xp_ref[...], xn_ref[...]], axis=0)
        h_ref[tm:tm + 2 * SUBLANES, :] = _bf(_modulate(halo, gain, shift, scale))
        acc_ref[...] = jnp.zeros_like(acc_ref)

    h = h_ref[...]
    row = lax.broadcasted_iota(jnp.int32, (tm, 1), 0)
    pos = (i * tm + row) % seq_len
    has_prev = pos != 0
    has_next = pos != seq_len - 1
    is_first = row == 0
    is_last = row == tm - 1

    def conv(w_ref, c_ref):
        u = _dot(h, w_ref[...])
        um = u[0:tm]
        u_before = u[tm + SUBLANES - 1:tm + SUBLANES]
        u_after = u[tm + SUBLANES:tm + SUBLANES + 1]
        prev = jnp.where(is_first, u_before, pltpu.roll(um, 1, axis=0))
        nxt = jnp.where(is_last, u_after, pltpu.roll(um, tm - 1, axis=0))
        prev = jnp.where(has_prev, prev, 0.0)
        nxt = jnp.where(has_next, nxt, 0.0)
        cw = c_ref[...]
        return prev * cw[0:1] + um * cw[1:2] + nxt * cw[2:3]

    a = conv(wa_ref, ca_ref)
    b = conv(wb_ref, cb_ref)
    act = a * (1.0 / (1.0 + jnp.exp(-a))) * b
    acc_ref[...] += _dot(_bf(act), wo_ref[...])

    @pl.when(j == nj - 1)
    def _():
        o_ref[...] = x_ref[...] + gt_ref[0] * acc_ref[...]


def _ffn_call(x, gain, modarr, w_in, conv_w, w_out, mod_row, seq_len, *, tm=TM, tn=TN_FF,
              name="conv_ffn"):
    m, d = x.shape
    tm = min(tm, m)
    nj = D_FF_PAD // tn
    hb = tm // SUBLANES
    last_hb = m // SUBLANES - 1
    kern = functools.partial(_ffn_kernel, tm=tm, seq_len=seq_len)
    return pl.pallas_call(
        kern,
        grid=(m // tm, nj),
        in_specs=[
            pl.BlockSpec((tm, d), lambda i, j: (i, 0)),
            pl.BlockSpec((SUBLANES, d), lambda i, j: (jnp.maximum(i * hb - 1, 0), 0)),
            pl.BlockSpec((SUBLANES, d), lambda i, j: (jnp.minimum((i + 1) * hb, last_hb), 0)),
            pl.BlockSpec((1, d), lambda i, j: (0, 0)),
            pl.BlockSpec((1, 1, d), lambda i, j: (mod_row(i), 0, 3)),
            pl.BlockSpec((1, 1, d), lambda i, j: (mod_row(i), 0, 4)),
            pl.BlockSpec((1, 1, d), lambda i, j: (mod_row(i), 0, 5)),
            pl.BlockSpec((d, tn), lambda i, j: (0, j)),
            pl.BlockSpec((d, tn), lambda i, j: (0, nj + j)),
            pl.BlockSpec((3, tn), lambda i, j: (0, j)),
            pl.BlockSpec((3, tn), lambda i, j: (0, nj + j)),
            pl.BlockSpec((tn, d), lambda i, j: (j, 0)),
        ],
        out_specs=pl.BlockSpec((tm, d), lambda i, j: (i, 0)),
        out_shape=jax.ShapeDtypeStruct((m, d), jnp.float32),
        scratch_shapes=[
            pltpu.VMEM((tm + 2 * SUBLANES, d), jnp.bfloat16),
            pltpu.VMEM((tm, d), jnp.float32),
        ],
        compiler_params=_params("parallel", "arbitrary"),
        name=name,
    )(x, x, x, gain, modarr, modarr, modarr, w_in, w_in, conv_w, conv_w, w_out)


def _na_group_geometry(g):
    r0 = g * NA_GROUP_ROWS
    return r0, min(max(r0 - NA_WIN_H // 2, 0), GRID_H - NA_KEY_ROWS)


def _na_build_bias(rpb_ref, h, tc_scr, bias_scr):
    n_dr = 2 * NA_WIN_H - 1
    n_dc = 2 * NA_WIN_W - 1
    qc = lax.broadcasted_iota(jnp.int32, (GRID_W, GRID_W), 0)
    kc = lax.broadcasted_iota(jnp.int32, (GRID_W, GRID_W), 1)
    col0 = jnp.clip(qc - NA_WIN_W // 2, 0, GRID_W - NA_WIN_W)
    col_valid = (kc >= col0) & (kc < col0 + NA_WIN_W)
    delta = kc - qc + (NA_WIN_W - 1)
    base = h * (n_dr * n_dc)
    for dr in range(n_dr):
        acc = jnp.zeros((GRID_W, GRID_W), jnp.float32)
        for e in range(n_dc):
            acc = jnp.where(delta == e, rpb_ref[base + dr * n_dc + e], acc)
        tc_scr[dr] = jnp.where(col_valid, acc, NEG)
    tc_scr[n_dr] = jnp.full((GRID_W, GRID_W), NEG, jnp.float32)
    n_groups = GRID_H // NA_GROUP_ROWS
    for t, g in enumerate((0, 1, n_groups - 1)):
        r0, ks = _na_group_geometry(g)
        for qr in range(NA_GROUP_ROWS):
            r = r0 + qr
            row0 = min(max(r - NA_WIN_H // 2, 0), GRID_H - NA_WIN_H)
            pieces = []
            for kr in range(NA_KEY_ROWS):
                kra = ks + kr
                inside = row0 <= kra < row0 + NA_WIN_H
                pieces.append(tc_scr[kra - r + (NA_WIN_H - 1)] if inside else tc_scr[n_dr])
            bias_scr[t, qr * GRID_W:(qr + 1) * GRID_W, :] = jnp.concatenate(pieces, axis=1)


def _na_kernel(rpb_ref, q_ref, k_ref, v_ref, kc_ref, vc_ref, o_ref, tc_scr, bias_scr):
    @pl.when(pl.program_id(1) == 0)
    def _():
        _na_build_bias(rpb_ref, pl.program_id(0), tc_scr, bias_scr)

    kc = kc_ref[...]
    vc = vc_ref[...]
    n_groups = GRID_H // NA_GROUP_ROWS

    def body(g, carry):
        r0 = g * NA_GROUP_ROWS
        ks = jnp.clip(r0 - NA_WIN_H // 2, 0, GRID_H - NA_KEY_ROWS)
        q0 = pl.multiple_of(g * NA_Q, NA_Q)
        k0 = pl.multiple_of(ks * GRID_W, GRID_W)
        tb = jnp.where(g == 0, 0, jnp.where(g == n_groups - 1, 2, 1))
        q = q_ref[pl.ds(q0, NA_Q), :]
        kw = k_ref[pl.ds(k0, NA_K), :]
        vw = v_ref[pl.ds(k0, NA_K), :]
        sw = _dot_nt(q, kw) + bias_scr[tb]
        sc = _dot_nt(q, kc)
        mx = jnp.maximum(jnp.max(sw, axis=-1, keepdims=True), jnp.max(sc, axis=-1, keepdims=True))
        pw = jnp.exp(sw - mx)
        pc = jnp.exp(sc - mx)
        den = jnp.sum(pw, axis=-1, keepdims=True) + jnp.sum(pc, axis=-1, keepdims=True)
        o = _dot(_bf(pw), vw) + _dot(_bf(pc), vc)
        o_ref[pl.ds(q0, NA_Q), :] = _bf(o / den)
        return carry

    lax.fori_loop(0, n_groups, body, 0)


def _na_call(qkv, qkv_c, rpb):
    n_dr = 2 * NA_WIN_H - 1
    return pl.pallas_call(
        _na_kernel,
        grid=(NA_HEADS, BATCH),
        in_specs=[
            pl.BlockSpec(memory_space=pltpu.SMEM),
            pl.BlockSpec((SEQ, LANES), lambda h, b: (b, h)),
            pl.BlockSpec((SEQ, LANES), lambda h, b: (b, NA_HEADS + h)),
            pl.BlockSpec((SEQ, LANES), lambda h, b: (b, 2 * NA_HEADS + h)),
            pl.BlockSpec((CTX_LEN, LANES), lambda h, b: (b, NA_HEADS + h)),
            pl.BlockSpec((CTX_LEN, LANES), lambda h, b: (b, 2 * NA_HEADS + h)),
        ],
        out_specs=pl.BlockSpec((SEQ, LANES), lambda h, b: (b, h)),
        out_shape=jax.ShapeDtypeStruct((BATCH * SEQ, NA_WIDTH), jnp.bfloat16),
        scratch_shapes=[
            pltpu.VMEM((n_dr + 1, GRID_W, GRID_W), jnp.float32),
            pltpu.VMEM((3, NA_Q, NA_K), jnp.float32),
        ],
        compiler_params=_params("parallel", "arbitrary"),
        name="na_attn",
    )(rpb.reshape(-1), qkv, qkv, qkv, qkv_c, qkv_c)


def _attn_kernel(*refs, n_qparts, n_src, n_kparts, src_rows):
    it = iter(refs)
    q_refs = [next(it) for _ in range(n_qparts)]
    srcs = []
    for _ in range(n_src):
        k_refs = [next(it) for _ in range(n_kparts)]
        srcs.append((k_refs, next(it)))
    o_ref = next(it)
    k_scr, v_scr = next(it), next(it)

    @pl.when(pl.program_id(2) == 0)
    def _():
        r = 0
        for (k_refs, v_ref), rows in zip(srcs, src_rows):
            for p, k_ref in enumerate(k_refs):
                k_scr[r:r + rows, p * LANES:(p + 1) * LANES] = k_ref[...]
            v_scr[r:r + rows, :] = v_ref[...]
            r += rows

    q = jnp.concatenate([qr[...] for qr in q_refs], axis=-1) if n_qparts > 1 else q_refs[0][...]
    s = _dot_nt(q, k_scr[...])
    mx = jnp.max(s, axis=-1, keepdims=True)
    p = jnp.exp(s - mx)
    den = jnp.sum(p, axis=-1, keepdims=True)
    o = _dot(_bf(p), v_scr[...])
    o_ref[...] = _bf(o / den)


def _attn_call(q_parts, sources, n_heads, lq, tq, dv, name):
    tq = min(tq, lq)
    nq = lq // tq
    in_specs, args = [], []
    for arr, cf in q_parts:
        in_specs.append(pl.BlockSpec((tq, LANES), lambda b, h, i, cf=cf: (b * nq + i, cf(h))))
        args.append(arr)
    src_rows = []
    n_kparts = len(sources[0][1])
    for rows, k_parts, (v_arr, vcf) in sources:
        src_rows.append(rows)
        for arr, cf in k_parts:
            in_specs.append(pl.BlockSpec((rows, LANES), lambda b, h, i, cf=cf: (b, cf(h))))
            args.append(arr)
        in_specs.append(pl.BlockSpec((rows, dv), lambda b, h, i, cf=vcf: (b, cf(h))))
        args.append(v_arr)
    nk = sum(src_rows)
    kern = functools.partial(_attn_kernel, n_qparts=len(q_parts), n_src=len(sources),
                             n_kparts=n_kparts, src_rows=tuple(src_rows))
    return pl.pallas_call(
        kern,
        grid=(BATCH, n_heads, nq),
        in_specs=in_specs,
        out_specs=pl.BlockSpec((tq, dv), lambda b, h, i: (b * nq + i, h)),
        out_shape=jax.ShapeDtypeStruct((BATCH * lq, n_heads * dv), jnp.bfloat16),
        scratch_shapes=[
            pltpu.VMEM((nk, n_kparts * LANES), jnp.bfloat16),
            pltpu.VMEM((nk, dv), jnp.bfloat16),
        ],
        compiler_params=_params("parallel", "parallel", "arbitrary"),
        name=name,
    )(*args)


def _diff_kernel(*refs, n_src, src_rows, lam_init):
    it = iter(refs)
    q1_ref, q2_ref = next(it), next(it)
    srcs = [(next(it), next(it), next(it)) for _ in range(n_src)]
    lam_ref, sub_ref = next(it), next(it)
    o_ref = next(it)
    k1_scr, k2_scr, v_scr = next(it), next(it), next(it)

    @pl.when(pl.program_id(2) == 0)
    def _():
        r = 0
        for (k1_ref, k2_ref, v_ref), rows in zip(srcs, src_rows):
            k1_scr[r:r + rows, :] = k1_ref[...]
            k2_scr[r:r + rows, :] = k2_ref[...]
            v_scr[r:r + rows, :] = v_ref[...]
            r += rows

    lf = lam_ref[...]
    lam = (jnp.exp(jnp.sum(lf[0:1] * lf[1:2], axis=-1, keepdims=True))
           - jnp.exp(jnp.sum(lf[2:3] * lf[3:4], axis=-1, keepdims=True)) + lam_init)
    v = v_scr[...]

    def branch(q_ref, k_scr):
        s = _dot_nt(q_ref[...], k_scr[...])
        mx = jnp.max(s, axis=-1, keepdims=True)
        p = jnp.exp(s - mx)
        den = jnp.sum(p, axis=-1, keepdims=True)
        return _dot(_bf(p), v) / den

    o = branch(q1_ref, k1_scr) - lam * branch(q2_ref, k2_scr)
    ms = jnp.mean(o * o, axis=-1, keepdims=True)
    o_ref[...] = _bf(o * lax.rsqrt(ms + EPS) * sub_ref[...] * (1.0 - lam_init))


def _diff_call(qkv_q, lq, sources, diff_lambda, subln, lam_init, tq, name):
    tq = min(tq, lq)
    nq = lq // tq
    qb0 = 3 * NA_WIDTH // LANES
    kb0 = (3 * NA_WIDTH + DIFF_WIDTH) // LANES
    vb0 = (3 * NA_WIDTH + 2 * DIFF_WIDTH) // (2 * LANES)
    in_specs = [
        pl.BlockSpec((tq, LANES), lambda b, h, i: (b * nq + i, qb0 + 2 * h)),
        pl.BlockSpec((tq, LANES), lambda b, h, i: (b * nq + i, qb0 + 2 * h + 1)),
    ]
    args = [qkv_q, qkv_q]
    src_rows = []
    for rows, arr in sources:
        src_rows.append(rows)
        in_specs += [
            pl.BlockSpec((rows, LANES), lambda b, h, i: (b, kb0 + 2 * h)),
            pl.BlockSpec((rows, LANES), lambda b, h, i: (b, kb0 + 2 * h + 1)),
            pl.BlockSpec((rows, 2 * LANES), lambda b, h, i: (b, vb0 + h)),
        ]
        args += [arr, arr, arr]
    in_specs += [
        pl.BlockSpec((4, LANES), lambda b, h, i: (0, 0)),
        pl.BlockSpec((1, 2 * LANES), lambda b, h, i: (0, 0)),
    ]
    args += [diff_lambda, subln]
    nk = sum(src_rows)
    kern = functools.partial(_diff_kernel, n_src=len(sources), src_rows=tuple(src_rows),
                             lam_init=lam_init)
    return pl.pallas_call(
        kern,
        grid=(BATCH, DIFF_HEADS, nq),
        in_specs=in_specs,
        out_specs=pl.BlockSpec((tq, 2 * LANES), lambda b, h, i: (b * nq + i, h)),
        out_shape=jax.ShapeDtypeStruct((BATCH * lq, DIFF_WIDTH), jnp.bfloat16),
        scratch_shapes=[
            pltpu.VMEM((nk, LANES), jnp.bfloat16),
            pltpu.VMEM((nk, LANES), jnp.bfloat16),
            pltpu.VMEM((nk, 2 * LANES), jnp.bfloat16),
        ],
        compiler_params=_params("parallel", "parallel", "arbitrary"),
        name=name,
    )(*args)


def _rope_tables(d):
    h = d // 2
    half = h // 2
    t = jnp.arange(SEQ, dtype=jnp.int32)
    freqs = ROPE_BASE ** (-jnp.arange(half, dtype=jnp.float32) / half)

    def cs(pos):
        ang = pos.astype(jnp.float32)[:, None] * freqs[None, :]
        return jnp.cos(ang), jnp.sin(ang)

    cr, sr = cs(t // GRID_W)
    cc, sc = cs(t % GRID_W)
    cos = jnp.concatenate([cr, cr, cc, cc], -1)
    sin = jnp.concatenate([sr, sr, sc, sc], -1)
    first = (jnp.arange(d) % h) < half
    sin_a = jnp.where(first[None, :], -sin, 0.0)
    sin_b = jnp.where(first[None, :], 0.0, sin)
    pad = LANES - d
    if pad:
        cos = jnp.pad(cos, ((0, 0), (0, pad)), constant_values=1.0)
        sin_a = jnp.pad(sin_a, ((0, 0), (0, pad)))
        sin_b = jnp.pad(sin_b, ((0, 0), (0, pad)))
    return (cos, sin_a, sin_b), half


def _pad_lanes(v, fill=0.0):
    return jnp.pad(v, (0, LANES - v.shape[0]), constant_values=fill)


def _lat_row(tm):
    bpb = SEQ // tm
    return lambda i: i // bpb


def _ctx_row(i):
    return CTX_MOD_ROW


def _even_layer(x, xc, modarr, norm_mix, norm_ffn, ffn_w, w_in, w_out, qn_a, kn_a, rpb,
                qn_b, kn_b, diff_lambda, subln, lam_init, with_ctx):
    scale = HEAD_DIM ** -0.5
    gains = jnp.stack([qn_a * scale, kn_a, qn_b * scale, kn_b] + [jnp.ones_like(qn_a)] * 4)
    rope, half = _rope_tables(HEAD_DIM)
    segs = [(True, 0, HEAD_DIM, 0), (True, 1, HEAD_DIM, 0), (False, 0, HEAD_DIM, 0),
            (True, 2, HEAD_DIM, half), (True, 3, HEAD_DIM, half), (False, 0, HEAD_DIM, 0)]
    w_in = _bf(w_in)
    w_out = _bf(w_out)
    gain_mix = norm_mix[None, :]
    lat_row = _lat_row(TM)

    qkv = _proj_call(x, w_in, gains, segs, NA_WIDTH, mod=(gain_mix, modarr, 0, 1), rope=rope,
                     mod_row=lat_row, name="even_proj")
    qkv_c = _proj_call(xc, w_in, gains, segs, NA_WIDTH, mod=(gain_mix, modarr, 0, 1), rope=None,
                       mod_row=_ctx_row, name="even_proj_ctx")

    oa = _na_call(qkv, qkv_c, rpb)
    ob = _diff_call(qkv, SEQ, [(SEQ, qkv), (CTX_LEN, qkv_c)], diff_lambda, subln[None, :],
                    lam_init, TQ_DIFF, "diff_attn")
    x = _outproj_call([oa, ob], w_out, x, modarr, 2, lat_row, name="even_out")
    x = _ffn_call(x, norm_ffn[None, :], modarr, *ffn_w, lat_row, SEQ)
    if with_ctx:
        oa_c = _attn_call(
            [(qkv_c, lambda h: h)],
            [(CTX_LEN, [(qkv_c, lambda h: NA_HEADS + h)], (qkv_c, lambda h: 2 * NA_HEADS + h))],
            NA_HEADS, CTX_LEN, CTX_LEN, HEAD_DIM, "ctx_attn")
        ob_c = _diff_call(qkv_c, CTX_LEN, [(CTX_LEN, qkv_c)], diff_lambda, subln[None, :],
                          lam_init, CTX_LEN, "diff_attn_ctx")
        xc = _outproj_call([oa_c, ob_c], w_out, xc, modarr, 2, _ctx_row, name="even_out_ctx")
        xc = _ffn_call(xc, norm_ffn[None, :], modarr, *ffn_w, _ctx_row, CTX_LEN,
                       name="conv_ffn_ctx")
    return x, xc


def _odd_layer(x, xc, modarr, norm_mix, norm_ffn, ffn_w, w_down, q_a_norm, kv_a_norm, w_uq,
               w_ukv, qn_nope, qn_rope, kn_nope, kn_rope, w_out, with_ctx):
    scale = (MLA_NOPE + MLA_ROPE) ** -0.5
    rope, half = _rope_tables(MLA_ROPE)
    gain_mix = norm_mix[None, :]
    lat_row = _lat_row(TM)

    w_down_p = _bf(jnp.pad(w_down, ((0, 0), (0, MLA_DOWN_PAD - MLA_DOWN))))
    zeros_r = jnp.zeros((MLA_Q_RANK,), jnp.float32)
    kr_gain = jnp.concatenate([_pad_lanes(kn_rope), zeros_r[:MLA_Q_RANK - LANES]])
    gains_d = jnp.stack([q_a_norm, kv_a_norm, kr_gain] + [zeros_r] * 5)

    def down(xx, rp, row, name):
        return _down_call(xx, w_down_p, gains_d, (gain_mix, modarr, 0, 1), rp, half, row, name)

    qa, kva, kr = down(x, rope, lat_row, "mla_down")
    _, kva_c, kr_c = down(xc, None, _ctx_row, "mla_down_ctx")

    wq = w_uq.reshape(MLA_Q_RANK, MLA_HEADS, MLA_NOPE + MLA_ROPE)
    wq_rope = jnp.pad(wq[:, :, MLA_NOPE:], ((0, 0), (0, 0), (0, LANES - MLA_ROPE)))
    wq_p = _bf(jnp.concatenate([wq[:, :, :MLA_NOPE].reshape(MLA_Q_RANK, -1),
                                wq_rope.reshape(MLA_Q_RANK, -1)], -1))
    gains_q = jnp.stack([qn_nope * scale, _pad_lanes(qn_rope) * scale]
                        + [jnp.zeros((LANES,), jnp.float32)] * 6)
    segs_q = [(True, 0, MLA_NOPE, 0), (True, 1, MLA_ROPE, half)]
    q = _proj_call(qa, wq_p, gains_q, segs_q, MLA_HEADS * LANES, rope=rope, name="mla_q")

    wkv = w_ukv.reshape(MLA_KV_RANK, MLA_HEADS, MLA_NOPE + MLA_V)
    wkv_p = _bf(jnp.concatenate([wkv[:, :, :MLA_NOPE].reshape(MLA_KV_RANK, -1),
                                 wkv[:, :, MLA_NOPE:].reshape(MLA_KV_RANK, -1)], -1))
    gains_kv = jnp.stack([kn_nope] + [jnp.zeros((LANES,), jnp.float32)] * 7)
    segs_kv = [(True, 0, MLA_NOPE, 0), (False, 0, MLA_NOPE, 0)]
    kv = _proj_call(kva, wkv_p, gains_kv, segs_kv, MLA_HEADS * LANES, name="mla_kv")
    kv_c = _proj_call(kva_c, wkv_p, gains_kv, segs_kv, MLA_HEADS * LANES, name="mla_kv_ctx")

    def src(rows, kv_arr, kr_arr):
        return (rows, [(kv_arr, lambda h: h), (kr_arr, lambda h: 0)],
                (kv_arr, lambda h: MLA_HEADS + h))

    o = _attn_call([(q, lambda h: h), (q, lambda h: MLA_HEADS + h)],
                   [src(SEQ, kv, kr), src(CTX_LEN, kv_c, kr_c)],
                   MLA_HEADS, SEQ, TQ_MLA, MLA_V, "mla_attn")
    x = _outproj_call([o], _bf(w_out), x, modarr, 2, lat_row, name="mla_out")
    x = _ffn_call(x, norm_ffn[None, :], modarr, *ffn_w, lat_row, SEQ)
    assert not with_ctx
    return x, xc


def _down_kernel(x_ref, g_ref, sh_ref, sc_ref, w_ref, gains_ref, *rest, rope, rope_half):
    if rope:
        cos_ref, sa_ref, sb_ref = rest[:3]
        rest = rest[3:]
    qa_ref, kva_ref, kr_ref = rest
    h = _bf(_modulate(x_ref[...], g_ref[...], sh_ref[0], sc_ref[0]))
    acc = _dot(h, w_ref[...])

    def norm(xs, row, nvalid):
        ms = jnp.sum(xs * xs, axis=-1, keepdims=True) * (1.0 / nvalid)
        return xs * lax.rsqrt(ms + EPS) * gains_ref[row:row + 1, 0:xs.shape[1]]

    qa_ref[...] = _bf(norm(acc[:, 0:MLA_Q_RANK], 0, MLA_Q_RANK))
    kva_ref[...] = _bf(norm(acc[:, MLA_Q_RANK:MLA_Q_RANK + MLA_KV_RANK], 1, MLA_KV_RANK))
    kr = norm(acc[:, MLA_Q_RANK + MLA_KV_RANK:MLA_DOWN_PAD], 2, MLA_ROPE)
    if rope:
        kr = (kr * cos_ref[...]
              + pltpu.roll(kr, LANES - rope_half, axis=1) * sa_ref[...]
              + pltpu.roll(kr, rope_half, axis=1) * sb_ref[...])
    kr_ref[...] = _bf(kr)


def _down_call(x, w, gains, mod, rope, rope_half, mod_row, name, tm=TM):
    m, k = x.shape
    tm = min(tm, m)
    bpb = SEQ // tm
    gain, modarr, sh_c, sc_c = mod
    in_specs = [
        pl.BlockSpec((tm, k), lambda i: (i, 0)),
        pl.BlockSpec((1, k), lambda i: (0, 0)),
        pl.BlockSpec((1, 1, k), lambda i: (mod_row(i), 0, sh_c)),
        pl.BlockSpec((1, 1, k), lambda i: (mod_row(i), 0, sc_c)),
        pl.BlockSpec(w.shape, lambda i: (0, 0)),
        pl.BlockSpec(gains.shape, lambda i: (0, 0)),
    ]
    args = [x, gain, modarr, modarr, w, gains]
    if rope is not None:
        for t in rope:
            in_specs.append(pl.BlockSpec((tm, LANES), lambda i: (i % bpb, 0)))
            args.append(t)
    return pl.pallas_call(
        functools.partial(_down_kernel, rope=rope is not None, rope_half=rope_half),
        grid=(m // tm,),
        in_specs=in_specs,
        out_specs=[
            pl.BlockSpec((tm, MLA_Q_RANK), lambda i: (i, 0)),
            pl.BlockSpec((tm, MLA_KV_RANK), lambda i: (i, 0)),
            pl.BlockSpec((tm, LANES), lambda i: (i, 0)),
        ],
        out_shape=[
            jax.ShapeDtypeStruct((m, MLA_Q_RANK), jnp.bfloat16),
            jax.ShapeDtypeStruct((m, MLA_KV_RANK), jnp.bfloat16),
            jax.ShapeDtypeStruct((m, LANES), jnp.bfloat16),
        ],
        compiler_params=_params("parallel"),
        name=name,
    )(*args)


def _ffn_weights(w_in, conv_w, w_out):
    pad = D_FF_PAD - D_FF
    wa = jnp.pad(w_in[:, :D_FF], ((0, 0), (0, pad)))
    wb = jnp.pad(w_in[:, D_FF:], ((0, 0), (0, pad)))
    ca = jnp.pad(conv_w[:, :D_FF], ((0, 0), (0, pad)))
    cb = jnp.pad(conv_w[:, D_FF:], ((0, 0), (0, pad)))
    return (_bf(jnp.concatenate([wa, wb], -1)), jnp.concatenate([ca, cb], -1),
            _bf(jnp.pad(w_out, ((0, pad), (0, 0)))))


def kernel(x, c, ctx, c_ctx, ada_w, ada_b, norm_mix, norm_ffn, ffn_w_in, ffn_conv, ffn_w_out, even_w_in, even_w_out, na_q_norm, na_k_norm, na_rpb, diff_q_norm, diff_k_norm, diff_lambda, diff_subln, mla_w_down, mla_q_a_norm, mla_kv_a_norm, mla_w_uq, mla_w_ukv, mla_q_nope_norm, mla_q_rope_norm, mla_k_nope_norm, mla_k_rope_norm, mla_w_out):
    cond = jnp.concatenate(
        [c, c_ctx[None, :], jnp.zeros((MOD_ROWS - BATCH - 1, D_MODEL), jnp.float32)], 0)
    mod = _ada_call(cond, ada_w, ada_b)
    xl = x.reshape(BATCH * SEQ, D_MODEL)
    xc = ctx.reshape(BATCH * CTX_LEN, D_MODEL)
    for l in range(DEPTH):
        with_ctx = l < DEPTH - 1
        modarr = mod[l].reshape(MOD_ROWS, 1, 6 * D_MODEL)
        ffn_w = _ffn_weights(ffn_w_in[l], ffn_conv[l], ffn_w_out[l])
        i = l // 2
        if l % 2 == 0:
            lam_init = 0.8 - 0.6 * math.exp(-0.3 * l)
            xl, xc = _even_layer(xl, xc, modarr, norm_mix[l], norm_ffn[l], ffn_w, even_w_in[i],
                                 even_w_out[i], na_q_norm[i], na_k_norm[i], na_rpb[i],
                                 diff_q_norm[i], diff_k_norm[i], diff_lambda[i], diff_subln[i],
                                 lam_init, with_ctx)
        else:
            xl, xc = _odd_layer(xl, xc, modarr, norm_mix[l], norm_ffn[l], ffn_w, mla_w_down[i],
                                mla_q_a_norm[i], mla_kv_a_norm[i], mla_w_uq[i], mla_w_ukv[i],
                                mla_q_nope_norm[i], mla_q_rope_norm[i], mla_k_nope_norm[i],
                                mla_k_rope_norm[i], mla_w_out[i], with_ctx)
    return xl.reshape(BATCH, SEQ, D_MODEL)
```

```python
import functools
import math

import jax
import jax.numpy as jnp
from jax import lax
from jax.experimental import pallas as pl
from jax.experimental.pallas import tpu as pltpu

D_MODEL = 2048
BATCH = 4
SEQ = 4096
DEPTH = 2
GRID_W = 64
GRID_H = SEQ // GRID_W
CTX_LEN = 256
HEAD_DIM = 128
NA_HEADS = 8
NA_WIN_H = 8
NA_WIN_W = 16
DIFF_HEADS = 4
NA_WIDTH = 1024
DIFF_WIDTH = 1024
EVEN_PROJ = 6144
MLA_HEADS = 16
MLA_Q_RANK = 512
MLA_KV_RANK = 512
MLA_NOPE = 128
MLA_ROPE = 64
MLA_V = 128
MLA_DOWN = 1088
D_FF = 5504
ROPE_BASE = 10000.0
EPS = 1e-6

LANES = 128
SUBLANES = 8
MOD_ROWS = 8
CTX_MOD_ROW = BATCH
D_FF_PAD = 5632
MLA_DOWN_PAD = 1152
VMEM_LIMIT = 56 * 1024 * 1024

TM = 512
TN = 1024
TN_FF = 512
NA_GROUP_ROWS = 4
NA_KEY_ROWS = 12
NA_Q = NA_GROUP_ROWS * GRID_W
NA_K = NA_KEY_ROWS * GRID_W
TQ_DIFF = 256
TQ_MLA = 512
TQ_SUB = 256
KEY_CHUNK = 512
NEG = -1e30
LOG2E = math.log2(math.e)

_NT = (((1,), (1,)), ((), ()))


def _params(*sem):
    return pltpu.CompilerParams(dimension_semantics=sem, vmem_limit_bytes=VMEM_LIMIT)


def _bf(x):
    return x.astype(jnp.bfloat16)


def _dot(a, b):
    return jnp.dot(a, b, preferred_element_type=jnp.float32)


def _dot_nt(a, b):
    return lax.dot_general(a, b, _NT, preferred_element_type=jnp.float32)


def _modulate(x, gain, shift, scale):
    ms = jnp.mean(x * x, axis=-1, keepdims=True)
    y = x * lax.rsqrt(ms + EPS) * gain
    return y * (1.0 + scale) + shift


def _ada_kernel(c_ref, w_ref, b_ref, o_ref):
    c = c_ref[...]
    s = c * (1.0 / (1.0 + jnp.exp(-c)))
    o_ref[0] = _dot(_bf(s), _bf(w_ref[0])) + b_ref[0]


def _ada_call(cond, ada_w, ada_b):
    tn = 512
    n = ada_w.shape[-1]
    return pl.pallas_call(
        _ada_kernel,
        grid=(DEPTH, n // tn),
        in_specs=[
            pl.BlockSpec((MOD_ROWS, D_MODEL), lambda l, j: (0, 0)),
            pl.BlockSpec((1, D_MODEL, tn), lambda l, j: (l, 0, j)),
            pl.BlockSpec((1, 1, tn), lambda l, j: (l, 0, j)),
        ],
        out_specs=pl.BlockSpec((1, MOD_ROWS, tn), lambda l, j: (l, 0, j)),
        out_shape=jax.ShapeDtypeStruct((DEPTH, MOD_ROWS, n), jnp.float32),
        compiler_params=_params("parallel", "parallel"),
        name="ada_mod",
    )(cond, ada_w, ada_b.reshape(DEPTH, 1, n))


def _head_epilogue(acc, spec, gains_ref, rope_refs):
    norm, gain_row, nvalid, rope_half = spec
    tn = acc.shape[1]
    outs = []
    for c in range(tn // LANES):
        xh = acc[:, c * LANES:(c + 1) * LANES]
        if norm:
            ms = jnp.sum(xh * xh, axis=-1, keepdims=True) * (1.0 / nvalid)
            xh = xh * lax.rsqrt(ms + EPS) * gains_ref[gain_row:gain_row + 1, :]
        if rope_half and rope_refs is not None:
            cos_ref, sa_ref, sb_ref = rope_refs
            xh = (xh * cos_ref[...]
                  + pltpu.roll(xh, LANES - rope_half, axis=1) * sa_ref[...]
                  + pltpu.roll(xh, rope_half, axis=1) * sb_ref[...])
        outs.append(xh)
    return outs


def _proj_kernel(*refs, mod, segs, seg_tiles, rope, n_out, out_tiles):
    it = iter(refs)
    x_ref = next(it)
    if mod:
        g_ref, sh_ref, sc_ref = next(it), next(it), next(it)
    w_ref = next(it)
    gains_ref = next(it)
    rope_refs = (next(it), next(it), next(it)) if rope else None
    out_refs = [next(it) for _ in range(n_out)]
    h_ref = next(it) if mod else None
    j = pl.program_id(1)

    if mod:
        @pl.when(j == 0)
        def _():
            h_ref[...] = _bf(_modulate(x_ref[...], g_ref[...], sh_ref[0], sc_ref[0]))
        a = h_ref[...]
    else:
        a = x_ref[...]
    acc = _dot(a, w_ref[...])
    seg = j // seg_tiles
    for s, spec in enumerate(segs):
        @pl.when(seg == s)
        def _(spec=spec, s=s):
            outs = _head_epilogue(acc, spec, gains_ref, rope_refs)
            o_ref = out_refs[out_tiles[s]]
            for c, xh in enumerate(outs):
                o_ref[:, c * LANES:(c + 1) * LANES] = xh.astype(o_ref.dtype)


def _proj_call(x, w, gains, segs, seg_width, *, mod=None, rope=None, mod_row=None,
               tm=TM, tn=TN, outs=None, name="proj"):
    m, k = x.shape
    n = w.shape[1]
    tm = min(tm, m)
    tn = min(tn, seg_width)
    assert m % tm == 0 and seg_width % tn == 0 and n % seg_width == 0
    seg_tiles = seg_width // tn
    bpb = SEQ // tm

    in_specs = [pl.BlockSpec((tm, k), lambda i, j: (i, 0))]
    args = [x]
    if mod is not None:
        gain, modarr, sh_c, sc_c = mod
        in_specs += [
            pl.BlockSpec((1, k), lambda i, j: (0, 0)),
            pl.BlockSpec((1, 1, k), lambda i, j: (mod_row(i), 0, sh_c)),
            pl.BlockSpec((1, 1, k), lambda i, j: (mod_row(i), 0, sc_c)),
        ]
        args += [gain, modarr, modarr]
    in_specs.append(pl.BlockSpec((k, tn), lambda i, j: (0, j)))
    args.append(w)
    in_specs.append(pl.BlockSpec(gains.shape, lambda i, j: (0, 0)))
    args.append(gains)
    if rope is not None:
        for t in rope:
            in_specs.append(pl.BlockSpec((tm, LANES), lambda i, j: (i % bpb, 0)))
            args.append(t)

    if outs is None:
        out_tiles = [0] * len(segs)
        out_specs = [pl.BlockSpec((tm, tn), lambda i, j: (i, j))]
        out_shape = [jax.ShapeDtypeStruct((m, n), jnp.bfloat16)]
    else:
        out_tiles, out_specs, out_shape = [], [], []
        start = 0
        for o_idx, nseg in enumerate(outs):
            out_tiles += [o_idx] * nseg
            first_tile = start * seg_tiles
            last_tile = (start + nseg) * seg_tiles - 1
            out_specs.append(pl.BlockSpec(
                (tm, tn),
                lambda i, j, f=first_tile, l=last_tile: (i, jnp.clip(j, f, l) - f)))
            out_shape.append(jax.ShapeDtypeStruct((m, nseg * seg_width), jnp.bfloat16))
            start += nseg
    kern = functools.partial(
        _proj_kernel, mod=mod is not None, segs=tuple(segs), seg_tiles=seg_tiles,
        rope=rope is not None, n_out=len(out_shape), out_tiles=tuple(out_tiles))
    res = pl.pallas_call(
        kern,
        grid=(m // tm, n // tn),
        in_specs=in_specs,
        out_specs=out_specs,
        out_shape=out_shape,
        scratch_shapes=[pltpu.VMEM((tm, k), jnp.bfloat16)] if mod is not None else [],
        compiler_params=_params("parallel", "arbitrary"),
        name=name,
    )(*args)
    return res[0] if outs is None else res


def _outproj_kernel(*refs, n_a):
    a_refs = refs[:n_a]
    w_refs = refs[n_a:2 * n_a]
    x_ref, g_ref, o_ref = refs[2 * n_a:]
    acc = _dot(a_refs[0][...], w_refs[0][...])
    for a_ref, w_ref in zip(a_refs[1:], w_refs[1:]):
        acc = acc + _dot(a_ref[...], w_ref[...])
    o_ref[...] = x_ref[...] + g_ref[0] * acc


def _outproj_call(a_list, w, x, modarr, gate_chunk, mod_row, *, tm=TM, tn=TN, name="outproj"):
    m, n = x.shape
    tm = min(tm, m)
    n_a = len(a_list)
    in_specs, args = [], []
    for a in a_list:
        in_specs.append(pl.BlockSpec((tm, a.shape[1]), lambda i, j: (i, 0)))
        args.append(a)
    row = 0
    for a in a_list:
        ka = a.shape[1]
        in_specs.append(pl.BlockSpec((ka, tn), lambda i, j, r=row // ka: (r, j)))
        args.append(w)
        row += ka
    nt = n // tn
    in_specs += [
        pl.BlockSpec((tm, tn), lambda i, j: (i, j)),
        pl.BlockSpec((1, 1, tn), lambda i, j: (mod_row(i), 0, gate_chunk * nt + j)),
    ]
    args += [x, modarr]
    return pl.pallas_call(
        functools.partial(_outproj_kernel, n_a=n_a),
        grid=(m // tm, nt),
        in_specs=in_specs,
        out_specs=pl.BlockSpec((tm, tn), lambda i, j: (i, j)),
        out_shape=jax.ShapeDtypeStruct((m, n), jnp.float32),
        compiler_params=_params("parallel", "arbitrary"),
        name=name,
    )(*args)


def _ffn_kernel(x_ref, xp_ref, xn_ref, g_ref, sh_ref, sc_ref, gt_ref, wa_ref, wb_ref,
                ca_ref, cb_ref, wo_ref, o_ref, h_ref, acc_ref, *, tm, seq_len):
    i = pl.program_id(0)
    j = pl.program_id(1)
    nj = pl.num_programs(1)

    @pl.when(j == 0)
    def _():
        gain, shift, scale = g_ref[...], sh_ref[0], sc_ref[0]
        h_ref[0:tm, :] = _bf(_modulate(x_ref[...], gain, shift, scale))
        halo = jnp.concatenate([xn_ref[...], xp_ref[...]], axis=0)
        hrow = lax.broadcasted_iota(jnp.int32, (2 * SUBLANES, 1), 0)
        keep_prev = jnp.where((i * tm) % seq_len == 0, 0.0, 1.0)
        keep_next = jnp.where(((i + 1) * tm) % seq_len == 0, 0.0, 1.0)
        keep = jnp.where(hrow < SUBLANES, keep_next, keep_prev)
        h_ref[tm:tm + 2 * SUBLANES, :] = _bf(keep * _modulate(halo, gain, shift, scale))
        acc_ref[...] = jnp.zeros_like(acc_ref)

    h = h_ref[...]
    rows = tm + 2 * SUBLANES

    def conv(w_ref, c_ref):
        u = _dot(h, w_ref[...])
        cw = c_ref[...]
        prev = pltpu.roll(u, 1, axis=0)[0:tm]
        nxt = pltpu.roll(u, rows - 1, axis=0)[0:tm]
        return prev * cw[0:1] + u[0:tm] * cw[1:2] + nxt * cw[2:3]

    a = conv(wa_ref, ca_ref)
    b = conv(wb_ref, cb_ref)
    act = a * (1.0 / (1.0 + jnp.exp(-a))) * b
    acc_ref[...] += _dot(_bf(act), wo_ref[...])

    @pl.when(j == nj - 1)
    def _():
        o_ref[...] = x_ref[...] + gt_ref[0] * acc_ref[...]


def _ffn_call(x, gain, modarr, w_in, conv_w, w_out, mod_row, seq_len, *, tm=TM, tn=TN_FF,
              name="conv_ffn"):
    m, d = x.shape
    tm = min(tm, seq_len)
    assert seq_len % tm == 0 and m % tm == 0
    nj = D_FF_PAD // tn
    hb = tm // SUBLANES
    last_hb = m // SUBLANES - 1
    kern = functools.partial(_ffn_kernel, tm=tm, seq_len=seq_len)
    return pl.pallas_call(
        kern,
        grid=(m // tm, nj),
        in_specs=[
            pl.BlockSpec((tm, d), lambda i, j: (i, 0)),
            pl.BlockSpec((SUBLANES, d), lambda i, j: (jnp.maximum(i * hb - 1, 0), 0)),
            pl.BlockSpec((SUBLANES, d), lambda i, j: (jnp.minimum((i + 1) * hb, last_hb), 0)),
            pl.BlockSpec((1, d), lambda i, j: (0, 0)),
            pl.BlockSpec((1, 1, d), lambda i, j: (mod_row(i), 0, 3)),
            pl.BlockSpec((1, 1, d), lambda i, j: (mod_row(i), 0, 4)),
            pl.BlockSpec((1, 1, d), lambda i, j: (mod_row(i), 0, 5)),
            pl.BlockSpec((d, tn), lambda i, j: (0, j)),
            pl.BlockSpec((d, tn), lambda i, j: (0, nj + j)),
            pl.BlockSpec((3, tn), lambda i, j: (0, j)),
            pl.BlockSpec((3, tn), lambda i, j: (0, nj + j)),
            pl.BlockSpec((tn, d), lambda i, j: (j, 0)),
        ],
        out_specs=pl.BlockSpec((tm, d), lambda i, j: (i, 0)),
        out_shape=jax.ShapeDtypeStruct((m, d), jnp.float32),
        scratch_shapes=[
            pltpu.VMEM((tm + 2 * SUBLANES, d), jnp.bfloat16),
            pltpu.VMEM((tm, d), jnp.float32),
        ],
        compiler_params=_params("parallel", "arbitrary"),
        name=name,
    )(x, x, x, gain, modarr, modarr, modarr, w_in, w_in, conv_w, conv_w, w_out)


def _na_group_geometry(g):
    r0 = g * NA_GROUP_ROWS
    return r0, min(max(r0 - NA_WIN_H // 2, 0), GRID_H - NA_KEY_ROWS)


def _na_build_bias(rpb_ref, h, tc_scr, bias_scr):
    n_dr = 2 * NA_WIN_H - 1
    n_dc = 2 * NA_WIN_W - 1
    qc = lax.broadcasted_iota(jnp.int32, (GRID_W, GRID_W), 0)
    kc = lax.broadcasted_iota(jnp.int32, (GRID_W, GRID_W), 1)
    col0 = jnp.clip(qc - NA_WIN_W // 2, 0, GRID_W - NA_WIN_W)
    col_valid = (kc >= col0) & (kc < col0 + NA_WIN_W)
    delta = kc - qc + (NA_WIN_W - 1)
    base = h * (n_dr * n_dc)
    for dr in range(n_dr):
        acc = jnp.zeros((GRID_W, GRID_W), jnp.float32)
        for e in range(n_dc):
            acc = jnp.where(delta == e, rpb_ref[base + dr * n_dc + e] * LOG2E, acc)
        tc_scr[dr] = jnp.where(col_valid, acc, NEG)
    tc_scr[n_dr] = jnp.full((GRID_W, GRID_W), NEG, jnp.float32)
    n_groups = GRID_H // NA_GROUP_ROWS
    for t, g in enumerate((0, 1, n_groups - 1)):
        r0, ks = _na_group_geometry(g)
        for qr in range(NA_GROUP_ROWS):
            r = r0 + qr
            row0 = min(max(r - NA_WIN_H // 2, 0), GRID_H - NA_WIN_H)
            pieces = []
            for kr in range(NA_KEY_ROWS):
                kra = ks + kr
                inside = row0 <= kra < row0 + NA_WIN_H
                pieces.append(tc_scr[kra - r + (NA_WIN_H - 1)] if inside else tc_scr[n_dr])
            bias_scr[t, qr * GRID_W:(qr + 1) * GRID_W, :] = jnp.concatenate(pieces, axis=1)


def _na_kernel(rpb_ref, q_ref, k_ref, v_ref, kc_ref, vc_ref, o_ref, tc_scr, bias_scr):
    @pl.when(pl.program_id(1) == 0)
    def _():
        _na_build_bias(rpb_ref, pl.program_id(0), tc_scr, bias_scr)

    kc = kc_ref[...]
    vc = vc_ref[...]
    n_groups = GRID_H // NA_GROUP_ROWS

    def body(g, carry):
        r0 = g * NA_GROUP_ROWS
        ks = jnp.clip(r0 - NA_WIN_H // 2, 0, GRID_H - NA_KEY_ROWS)
        q0 = pl.multiple_of(g * NA_Q, NA_Q)
        k0 = pl.multiple_of(ks * GRID_W, GRID_W)
        tb = jnp.where(g == 0, 0, jnp.where(g == n_groups - 1, 2, 1))
        q = q_ref[pl.ds(q0, NA_Q), :]
        kw = k_ref[pl.ds(k0, NA_K), :]
        vw = v_ref[pl.ds(k0, NA_K), :]
        sw = _dot_nt(q, kw) + bias_scr[tb]
        sc = _dot_nt(q, kc)
        mx = jnp.maximum(jnp.max(sw, axis=-1, keepdims=True), jnp.max(sc, axis=-1, keepdims=True))
        pw = jnp.exp2(sw - mx)
        pc = jnp.exp2(sc - mx)
        den = jnp.sum(pw, axis=-1, keepdims=True) + jnp.sum(pc, axis=-1, keepdims=True)
        o = _dot(_bf(pw), vw) + _dot(_bf(pc), vc)
        o_ref[pl.ds(q0, NA_Q), :] = _bf(o / den)
        return carry

    lax.fori_loop(0, n_groups, body, 0)


def _na_call(qkv, qkv_c, rpb):
    n_dr = 2 * NA_WIN_H - 1
    return pl.pallas_call(
        _na_kernel,
        grid=(NA_HEADS, BATCH),
        in_specs=[
            pl.BlockSpec(memory_space=pltpu.SMEM),
            pl.BlockSpec((SEQ, LANES), lambda h, b: (b, h)),
            pl.BlockSpec((SEQ, LANES), lambda h, b: (b, NA_HEADS + h)),
            pl.BlockSpec((SEQ, LANES), lambda h, b: (b, 2 * NA_HEADS + h)),
            pl.BlockSpec((CTX_LEN, LANES), lambda h, b: (b, NA_HEADS + h)),
            pl.BlockSpec((CTX_LEN, LANES), lambda h, b: (b, 2 * NA_HEADS + h)),
        ],
        out_specs=pl.BlockSpec((SEQ, LANES), lambda h, b: (b, h)),
        out_shape=jax.ShapeDtypeStruct((BATCH * SEQ, NA_WIDTH), jnp.bfloat16),
        scratch_shapes=[
            pltpu.VMEM((n_dr + 1, GRID_W, GRID_W), jnp.float32),
            pltpu.VMEM((3, NA_Q, NA_K), jnp.float32),
        ],
        compiler_params=_params("parallel", "arbitrary"),
        name="na_attn",
    )(rpb.reshape(-1), qkv, qkv, qkv, qkv_c, qkv_c)


def _key_chunks(src_rows):
    chunks, r = [], 0
    for rows in src_rows:
        size = min(KEY_CHUNK, rows)
        assert rows % size == 0
        chunks += [(r + c, size) for c in range(0, rows, size)]
        r += rows
    return chunks


def _softmax_pv_t(q, k_scr, vt_scr, s_scr, chunks):
    s_scr[...] = _dot_nt(k_scr[...], q)
    return _softmax_pv_from_scores(s_scr, vt_scr, chunks)


def _softmax_pv_from_scores(s_scr, vt_scr, chunks):
    tq = s_scr.shape[1]
    dv = vt_scr.shape[0]
    m = jnp.full((1, tq), NEG, jnp.float32)
    for c0, cs in chunks:
        m = jnp.maximum(m, jnp.max(s_scr[c0:c0 + cs, :], axis=0, keepdims=True))
    l = jnp.zeros((1, tq), jnp.float32)
    acc = jnp.zeros((dv, tq), jnp.float32)
    for c0, cs in chunks:
        p = jnp.exp2(s_scr[c0:c0 + cs, :] - m)
        l = l + jnp.sum(p, axis=0, keepdims=True)
        acc = acc + _dot(vt_scr[:, c0:c0 + cs], _bf(p))
    return acc / l


def _attn_kernel(*refs, n_qparts, n_src, n_kparts, src_rows):
    it = iter(refs)
    q_refs = [next(it) for _ in range(n_qparts)]
    srcs = []
    for _ in range(n_src):
        k_refs = [next(it) for _ in range(n_kparts)]
        srcs.append((k_refs, next(it)))
    o_ref = next(it)
    k_scr, vt_scr, s_scr = next(it), next(it), next(it)

    @pl.when(pl.program_id(2) == 0)
    def _():
        r = 0
        for (k_refs, v_ref), rows in zip(srcs, src_rows):
            for p, k_ref in enumerate(k_refs):
                k_scr[r:r + rows, p * LANES:(p + 1) * LANES] = k_ref[...]
            vt_scr[:, r:r + rows] = _bf(v_ref[...].astype(jnp.float32).T)
            r += rows

    q = jnp.concatenate([qr[...] for qr in q_refs], axis=-1) if n_qparts > 1 else q_refs[0][...]
    n_sub, _, sub = s_scr.shape
    k = k_scr[...]
    for u in range(n_sub):
        s_scr[u] = _dot_nt(k, q[u * sub:(u + 1) * sub])
    chunks = _key_chunks(src_rows)
    for u in range(n_sub):
        o_t = _softmax_pv_from_scores(s_scr.at[u], vt_scr, chunks)
        o_ref[u * sub:(u + 1) * sub, :] = _bf(o_t.T)


def _attn_call(q_parts, sources, n_heads, lq, tq, dv, name):
    tq = min(tq, lq)
    sub = min(TQ_SUB, tq)
    nq = lq // tq
    in_specs, args = [], []
    for arr, cf in q_parts:
        in_specs.append(pl.BlockSpec((tq, LANES), lambda b, h, i, cf=cf: (b * nq + i, cf(h))))
        args.append(arr)
    src_rows = []
    n_kparts = len(sources[0][1])
    for rows, k_parts, (v_arr, vcf) in sources:
        src_rows.append(rows)
        for arr, cf in k_parts:
            in_specs.append(pl.BlockSpec((rows, LANES), lambda b, h, i, cf=cf: (b, cf(h))))
            args.append(arr)
        in_specs.append(pl.BlockSpec((rows, dv), lambda b, h, i, cf=vcf: (b, cf(h))))
        args.append(v_arr)
    nk = sum(src_rows)
    kern = functools.partial(_attn_kernel, n_qparts=len(q_parts), n_src=len(sources),
                             n_kparts=n_kparts, src_rows=tuple(src_rows))
    return pl.pallas_call(
        kern,
        grid=(BATCH, n_heads, nq),
        in_specs=in_specs,
        out_specs=pl.BlockSpec((tq, dv), lambda b, h, i: (b * nq + i, h)),
        out_shape=jax.ShapeDtypeStruct((BATCH * lq, n_heads * dv), jnp.bfloat16),
        scratch_shapes=[
            pltpu.VMEM((nk, n_kparts * LANES), jnp.bfloat16),
            pltpu.VMEM((dv, nk), jnp.bfloat16),
            pltpu.VMEM((tq // sub, nk, sub), jnp.float32),
        ],
        compiler_params=_params("parallel", "parallel", "arbitrary"),
        name=name,
    )(*args)


def _diff_kernel(*refs, n_src, src_rows, lam_init):
    it = iter(refs)
    q1_ref, q2_ref = next(it), next(it)
    srcs = [(next(it), next(it), next(it)) for _ in range(n_src)]
    lam_ref, sub_ref = next(it), next(it)
    o_ref = next(it)
    k1_scr, k2_scr, vt_scr, s1_scr, s2_scr = [next(it) for _ in range(5)]

    @pl.when(pl.program_id(2) == 0)
    def _():
        r = 0
        for (k1_ref, k2_ref, v_ref), rows in zip(srcs, src_rows):
            k1_scr[r:r + rows, :] = k1_ref[...]
            k2_scr[r:r + rows, :] = k2_ref[...]
            vt_scr[:, r:r + rows] = _bf(v_ref[...].astype(jnp.float32).T)
            r += rows

    lf = lam_ref[...]
    lam = (jnp.exp(jnp.sum(lf[0:1] * lf[1:2], axis=-1, keepdims=True))
           - jnp.exp(jnp.sum(lf[2:3] * lf[3:4], axis=-1, keepdims=True)) + lam_init)
    chunks = _key_chunks(src_rows)
    s1_scr[...] = _dot_nt(k1_scr[...], q1_ref[...])
    s2_scr[...] = _dot_nt(k2_scr[...], q2_ref[...])
    o_t = (_softmax_pv_from_scores(s1_scr, vt_scr, chunks)
           - lam * _softmax_pv_from_scores(s2_scr, vt_scr, chunks))
    o = o_t.T
    ms = jnp.mean(o * o, axis=-1, keepdims=True)
    o_ref[...] = _bf(o * lax.rsqrt(ms + EPS) * sub_ref[...] * (1.0 - lam_init))


def _diff_call(qkv_q, lq, sources, diff_lambda, subln, lam_init, tq, name):
    tq = min(tq, lq)
    nq = lq // tq
    qb0 = 3 * NA_WIDTH // LANES
    kb0 = (3 * NA_WIDTH + DIFF_WIDTH) // LANES
    vb0 = (3 * NA_WIDTH + 2 * DIFF_WIDTH) // (2 * LANES)
    in_specs = [
        pl.BlockSpec((tq, LANES), lambda b, h, i: (b * nq + i, qb0 + 2 * h)),
        pl.BlockSpec((tq, LANES), lambda b, h, i: (b * nq + i, qb0 + 2 * h + 1)),
    ]
    args = [qkv_q, qkv_q]
    src_rows = []
    for rows, arr in sources:
        src_rows.append(rows)
        in_specs += [
            pl.BlockSpec((rows, LANES), lambda b, h, i: (b, kb0 + 2 * h)),
            pl.BlockSpec((rows, LANES), lambda b, h, i: (b, kb0 + 2 * h + 1)),
            pl.BlockSpec((rows, 2 * LANES), lambda b, h, i: (b, vb0 + h)),
        ]
        args += [arr, arr, arr]
    in_specs += [
        pl.BlockSpec((4, LANES), lambda b, h, i: (0, 0)),
        pl.BlockSpec((1, 2 * LANES), lambda b, h, i: (0, 0)),
    ]
    args += [diff_lambda, subln]
    nk = sum(src_rows)
    kern = functools.partial(_diff_kernel, n_src=len(sources), src_rows=tuple(src_rows),
                             lam_init=lam_init)
    return pl.pallas_call(
        kern,
        grid=(BATCH, DIFF_HEADS, nq),
        in_specs=in_specs,
        out_specs=pl.BlockSpec((tq, 2 * LANES), lambda b, h, i: (b * nq + i, h)),
        out_shape=jax.ShapeDtypeStruct((BATCH * lq, DIFF_WIDTH), jnp.bfloat16),
        scratch_shapes=[
            pltpu.VMEM((nk, LANES), jnp.bfloat16),
            pltpu.VMEM((nk, LANES), jnp.bfloat16),
            pltpu.VMEM((2 * LANES, nk), jnp.bfloat16),
            pltpu.VMEM((nk, tq), jnp.float32),
            pltpu.VMEM((nk, tq), jnp.float32),
        ],
        compiler_params=_params("parallel", "parallel", "arbitrary"),
        name=name,
    )(*args)


def _rope_tables(d):
    h = d // 2
    half = h // 2
    t = jnp.arange(SEQ, dtype=jnp.int32)
    freqs = ROPE_BASE ** (-jnp.arange(half, dtype=jnp.float32) / half)

    def cs(pos):
        ang = pos.astype(jnp.float32)[:, None] * freqs[None, :]
        return jnp.cos(ang), jnp.sin(ang)

    cr, sr = cs(t // GRID_W)
    cc, sc = cs(t % GRID_W)
    cos = jnp.concatenate([cr, cr, cc, cc], -1)
    sin = jnp.concatenate([sr, sr, sc, sc], -1)
    first = (jnp.arange(d) % h) < half
    sin_a = jnp.where(first[None, :], -sin, 0.0)
    sin_b = jnp.where(first[None, :], 0.0, sin)
    pad = LANES - d
    if pad:
        cos = jnp.pad(cos, ((0, 0), (0, pad)), constant_values=1.0)
        sin_a = jnp.pad(sin_a, ((0, 0), (0, pad)))
        sin_b = jnp.pad(sin_b, ((0, 0), (0, pad)))
    return (cos, sin_a, sin_b), half


def _pad_lanes(v, fill=0.0):
    return jnp.pad(v, (0, LANES - v.shape[0]), constant_values=fill)


def _lat_row(tm):
    bpb = SEQ // tm
    return lambda i: i // bpb


def _ctx_row(i):
    return CTX_MOD_ROW


def _even_layer(x, xc, modarr, norm_mix, norm_ffn, ffn_w, w_in, w_out, qn_a, kn_a, rpb,
                qn_b, kn_b, diff_lambda, subln, lam_init, with_ctx):
    scale = HEAD_DIM ** -0.5 * LOG2E
    gains =jnp.stack([qn_a * scale, kn_a, qn_b * scale, kn_b] + [jnp.ones_like(qn_a)] * 4)
    rope, half = _rope_tables(HEAD_DIM)
    segs = [(True, 0, HEAD_DIM, 0), (True, 1, HEAD_DIM, 0), (False, 0, HEAD_DIM, 0),
            (True, 2, HEAD_DIM, half), (True, 3, HEAD_DIM, half), (False, 0, HEAD_DIM, 0)]
    w_in = _bf(w_in)
    w_out = _bf(w_out)
    gain_mix = norm_mix[None, :]
    lat_row = _lat_row(TM)

    qkv = _proj_call(x, w_in, gains, segs, NA_WIDTH, mod=(gain_mix, modarr, 0, 1), rope=rope,
                     mod_row=lat_row, name="even_proj")
    qkv_c = _proj_call(xc, w_in, gains, segs, NA_WIDTH, mod=(gain_mix, modarr, 0, 1), rope=None,
                       mod_row=_ctx_row, name="even_proj_ctx")

    oa = _na_call(qkv, qkv_c, rpb)
    ob = _diff_call(qkv, SEQ, [(SEQ, qkv), (CTX_LEN, qkv_c)], diff_lambda, subln[None, :],
                    lam_init, TQ_DIFF, "diff_attn")
    x = _outproj_call([oa, ob], w_out, x, modarr, 2, lat_row, name="even_out")
    x = _ffn_call(x, norm_ffn[None, :], modarr, *ffn_w, lat_row, SEQ)
    if with_ctx:
        oa_c = _attn_call(
            [(qkv_c, lambda h: h)],
            [(CTX_LEN, [(qkv_c, lambda h: NA_HEADS + h)], (qkv_c, lambda h: 2 * NA_HEADS + h))],
            NA_HEADS, CTX_LEN, CTX_LEN, HEAD_DIM, "ctx_attn")
        ob_c = _diff_call(qkv_c, CTX_LEN, [(CTX_LEN, qkv_c)], diff_lambda, subln[None, :],
                          lam_init, CTX_LEN, "diff_attn_ctx")
        xc = _outproj_call([oa_c, ob_c], w_out, xc, modarr, 2, _ctx_row, name="even_out_ctx")
        xc = _ffn_call(xc, norm_ffn[None, :], modarr, *ffn_w, _ctx_row, CTX_LEN,
                       name="conv_ffn_ctx")
    return x, xc


def _odd_layer(x, xc, modarr, norm_mix, norm_ffn, ffn_w, w_down, q_a_norm, kv_a_norm, w_uq,
               w_ukv, qn_nope, qn_rope, kn_nope, kn_rope, w_out, with_ctx):
    scale = (MLA_NOPE + MLA_ROPE) ** -0.5 * LOG2E
    rope, half = _rope_tables(MLA_ROPE)
    gain_mix = norm_mix[None, :]
    lat_row = _lat_row(TM)

    w_down_p = _bf(jnp.pad(w_down, ((0, 0), (0, MLA_DOWN_PAD - MLA_DOWN))))
    zeros_r = jnp.zeros((MLA_Q_RANK,), jnp.float32)
    kr_gain = jnp.concatenate([_pad_lanes(kn_rope), zeros_r[:MLA_Q_RANK - LANES]])
    gains_d = jnp.stack([q_a_norm, kv_a_norm, kr_gain] + [zeros_r] * 5)

    def down(xx, rp, row, name):
        return _down_call(xx, w_down_p, gains_d, (gain_mix, modarr, 0, 1), rp, half, row, name)

    qa, kva, kr = down(x, rope, lat_row, "mla_down")
    _, kva_c, kr_c = down(xc, None, _ctx_row, "mla_down_ctx")

    wq = w_uq.reshape(MLA_Q_RANK, MLA_HEADS, MLA_NOPE + MLA_ROPE)
    wq_rope = jnp.pad(wq[:, :, MLA_NOPE:], ((0, 0), (0, 0), (0, LANES - MLA_ROPE)))
    wq_p = _bf(jnp.concatenate([wq[:, :, :MLA_NOPE].reshape(MLA_Q_RANK, -1),
                                wq_rope.reshape(MLA_Q_RANK, -1)], -1))
    gains_q = jnp.stack([qn_nope * scale, _pad_lanes(qn_rope) * scale]
                        + [jnp.zeros((LANES,), jnp.float32)] * 6)
    segs_q = [(True, 0, MLA_NOPE, 0), (True, 1, MLA_ROPE, half)]
    q = _proj_call(qa, wq_p, gains_q, segs_q, MLA_HEADS * LANES, rope=rope, name="mla_q")

    wkv = w_ukv.reshape(MLA_KV_RANK, MLA_HEADS, MLA_NOPE + MLA_V)
    wkv_p = _bf(jnp.concatenate([wkv[:, :, :MLA_NOPE].reshape(MLA_KV_RANK, -1),
                                 wkv[:, :, MLA_NOPE:].reshape(MLA_KV_RANK, -1)], -1))
    gains_kv = jnp.stack([kn_nope] + [jnp.zeros((LANES,), jnp.float32)] * 7)
    segs_kv = [(True, 0, MLA_NOPE, 0), (False, 0, MLA_NOPE, 0)]
    kv = _proj_call(kva, wkv_p, gains_kv, segs_kv, MLA_HEADS * LANES, name="mla_kv")
    kv_c = _proj_call(kva_c, wkv_p, gains_kv, segs_kv, MLA_HEADS * LANES, name="mla_kv_ctx")

    def src(rows, kv_arr, kr_arr):
        return (rows, [(kv_arr, lambda h: h), (kr_arr, lambda h: 0)],
                (kv_arr, lambda h: MLA_HEADS + h))

    o = _attn_call([(q, lambda h: h), (q, lambda h: MLA_HEADS + h)],
                   [src(SEQ, kv, kr), src(CTX_LEN, kv_c, kr_c)],
                   MLA_HEADS, SEQ, TQ_MLA, MLA_V, "mla_attn")
    x = _outproj_call([o], _bf(w_out), x, modarr, 2, lat_row, name="mla_out")
    x = _ffn_call(x, norm_ffn[None, :], modarr, *ffn_w, lat_row, SEQ)
    assert not with_ctx
    return x, xc


def _down_kernel(x_ref, g_ref, sh_ref, sc_ref, w_ref, gains_ref, *rest, rope, rope_half):
    if rope:
        cos_ref, sa_ref, sb_ref = rest[:3]
        rest = rest[3:]
    qa_ref, kva_ref, kr_ref = rest
    h = _bf(_modulate(x_ref[...], g_ref[...], sh_ref[0], sc_ref[0]))
    acc = _dot(h, w_ref[...])

    def norm(xs, row, nvalid):
        ms = jnp.sum(xs * xs, axis=-1, keepdims=True) * (1.0 / nvalid)
        return xs * lax.rsqrt(ms + EPS) * gains_ref[row:row + 1, 0:xs.shape[1]]

    qa_ref[...] = _bf(norm(acc[:, 0:MLA_Q_RANK], 0, MLA_Q_RANK))
    kva_ref[...] = _bf(norm(acc[:, MLA_Q_RANK:MLA_Q_RANK + MLA_KV_RANK], 1, MLA_KV_RANK))
    kr = norm(acc[:, MLA_Q_RANK + MLA_KV_RANK:MLA_DOWN_PAD], 2, MLA_ROPE)
    if rope:
        kr = (kr * cos_ref[...]
              + pltpu.roll(kr, LANES - rope_half, axis=1) * sa_ref[...]
              + pltpu.roll(kr, rope_half, axis=1) * sb_ref[...])
    kr_ref[...] = _bf(kr)


def _down_call(x, w, gains, mod, rope, rope_half, mod_row, name, tm=TM):
    m, k = x.shape
    tm = min(tm, m)
    bpb = SEQ // tm
    gain, modarr, sh_c, sc_c = mod
    in_specs = [
        pl.BlockSpec((tm, k), lambda i: (i, 0)),
        pl.BlockSpec((1, k), lambda i: (0, 0)),
        pl.BlockSpec((1, 1, k), lambda i: (mod_row(i), 0, sh_c)),
        pl.BlockSpec((1, 1, k), lambda i: (mod_row(i), 0, sc_c)),
        pl.BlockSpec(w.shape, lambda i: (0, 0)),
        pl.BlockSpec(gains.shape, lambda i: (0, 0)),
    ]
    args = [x, gain, modarr, modarr, w, gains]
    if rope is not None:
        for t in rope:
            in_specs.append(pl.BlockSpec((tm, LANES), lambda i: (i % bpb, 0)))
            args.append(t)
    return pl.pallas_call(
        functools.partial(_down_kernel, rope=rope is not None, rope_half=rope_half),
        grid=(m // tm,),
        in_specs=in_specs,
        out_specs=[
            pl.BlockSpec((tm, MLA_Q_RANK), lambda i: (i, 0)),
            pl.BlockSpec((tm, MLA_KV_RANK), lambda i: (i, 0)),
            pl.BlockSpec((tm, LANES), lambda i: (i, 0)),
        ],
        out_shape=[
            jax.ShapeDtypeStruct((m, MLA_Q_RANK), jnp.bfloat16),
            jax.ShapeDtypeStruct((m, MLA_KV_RANK), jnp.bfloat16),
            jax.ShapeDtypeStruct((m, LANES), jnp.bfloat16),
        ],
        compiler_params=_params("parallel"),
        name=name,
    )(*args)


def _ffn_weights(w_in, conv_w, w_out):
    pad = D_FF_PAD - D_FF
    wa = jnp.pad(w_in[:, :D_FF], ((0, 0), (0, pad)))
    wb = jnp.pad(w_in[:, D_FF:], ((0, 0), (0, pad)))
    ca = jnp.pad(conv_w[:, :D_FF], ((0, 0), (0, pad)))
    cb = jnp.pad(conv_w[:, D_FF:], ((0, 0), (0, pad)))
    return (_bf(jnp.concatenate([wa, wb], -1)), jnp.concatenate([ca, cb], -1),
            _bf(jnp.pad(w_out, ((0, pad), (0, 0)))))


def kernel(x, c, ctx, c_ctx, ada_w, ada_b, norm_mix, norm_ffn, ffn_w_in, ffn_conv, ffn_w_out, even_w_in, even_w_out, na_q_norm, na_k_norm, na_rpb, diff_q_norm, diff_k_norm, diff_lambda, diff_subln, mla_w_down, mla_q_a_norm, mla_kv_a_norm, mla_w_uq, mla_w_ukv, mla_q_nope_norm, mla_q_rope_norm, mla_k_nope_norm, mla_k_rope_norm, mla_w_out):
    cond = jnp.concatenate(
        [c, c_ctx[None, :], jnp.zeros((MOD_ROWS - BATCH - 1, D_MODEL), jnp.float32)], 0)
    mod = _ada_call(cond, ada_w, ada_b)
    xl = x.reshape(BATCH * SEQ, D_MODEL)
    xc = ctx.reshape(BATCH * CTX_LEN, D_MODEL)
    for l in range(DEPTH):
        with_ctx = l < DEPTH - 1
        modarr = mod[l].reshape(MOD_ROWS, 1, 6 * D_MODEL)
        ffn_w = _ffn_weights(ffn_w_in[l], ffn_conv[l], ffn_w_out[l])
        i = l // 2
        if l % 2 == 0:
            lam_init = 0.8 - 0.6 * math.exp(-0.3 * l)
            xl, xc = _even_layer(xl, xc, modarr, norm_mix[l], norm_ffn[l], ffn_w, even_w_in[i],
                                 even_w_out[i], na_q_norm[i], na_k_norm[i], na_rpb[i],
                                 diff_q_norm[i], diff_k_norm[i], diff_lambda[i], diff_subln[i],
                                 lam_init, with_ctx)
        else:
            xl, xc = _odd_layer(xl, xc, modarr, norm_mix[l], norm_ffn[l], ffn_w, mla_w_down[i],
                                mla_q_a_norm[i], mla_kv_a_norm[i], mla_w_uq[i], mla_w_ukv[i],
                                mla_q_nope_norm[i], mla_q_rope_norm[i], mla_k_nope_norm[i],
                                mla_k_rope_norm[i], mla_w_out[i], with_ctx)
    return xl.reshape(BATCH, SEQ, D_MODEL)
```

```python
import functools
import math

import numpy as np
import jax
import jax.numpy as jnp
from jax import lax
from jax.experimental import pallas as pl
from jax.experimental.pallas import tpu as pltpu

D_MODEL = 2048
BATCH = 4
SEQ = 4096
DEPTH = 2
GRID_W = 64
GRID_H = SEQ // GRID_W
CTX_LEN = 256
HEAD_DIM = 128
NA_HEADS = 8
NA_WIN_H = 8
NA_WIN_W = 16
DIFF_HEADS = 4
NA_WIDTH = 1024
DIFF_WIDTH = 1024
EVEN_PROJ = 6144
MLA_HEADS = 16
MLA_Q_RANK = 512
MLA_KV_RANK = 512
MLA_NOPE = 128
MLA_ROPE = 64
MLA_V = 128
MLA_DOWN = 1088
D_FF = 5504
ROPE_BASE = 10000.0
EPS = 1e-6

LANES = 128
SUBLANES = 8
MOD_ROWS = 8
CTX_MOD_ROW = BATCH
D_FF_PAD = 5632
MLA_DOWN_PAD = 1152
VMEM_LIMIT = 56 * 1024 * 1024

TM = 512
TN = 1024
TN_FF = 512
SUB_N = 512
PAIR = 2 * LANES
NA_GROUP_ROWS = 4
NA_KEY_ROWS = 12
NA_Q = NA_GROUP_ROWS * GRID_W
NA_K = NA_KEY_ROWS * GRID_W
TQ_DIFF = 256
TQ_MLA = 512
TQ_SUB = 256
KEY_CHUNK = 512
NEG = -1e30
LOG2E = math.log2(math.e)

_NT = (((1,), (1,)), ((), ()))


def _params(*sem):
    return pltpu.CompilerParams(dimension_semantics=sem, vmem_limit_bytes=VMEM_LIMIT)


def _bf(x):
    return x.astype(jnp.bfloat16)


def _dot(a, b):
    return jnp.dot(a, b, preferred_element_type=jnp.float32)


def _dot_nt(a, b):
    return lax.dot_general(a, b, _NT, preferred_element_type=jnp.float32)


def _modulate(x, gain, shift, scale):
    ms = jnp.mean(x * x, axis=-1, keepdims=True)
    y = x * lax.rsqrt(ms + EPS) * gain
    return y * (1.0 + scale) + shift


def _ada_kernel(c_ref, w_ref, b_ref, o_ref):
    c = c_ref[...]
    s = c * (1.0 / (1.0 + jnp.exp(-c)))
    o_ref[0] = _dot(_bf(s), _bf(w_ref[0])) + b_ref[0]


def _ada_call(cond, ada_w, ada_b):
    tn = 512
    n = ada_w.shape[-1]
    return pl.pallas_call(
        _ada_kernel,
        grid=(DEPTH, n // tn),
        in_specs=[
            pl.BlockSpec((MOD_ROWS, D_MODEL), lambda l, j: (0, 0)),
            pl.BlockSpec((1, D_MODEL, tn), lambda l, j: (l, 0, j)),
            pl.BlockSpec((1, 1, tn), lambda l, j: (l, 0, j)),
        ],
        out_specs=pl.BlockSpec((1, MOD_ROWS, tn), lambda l, j: (l, 0, j)),
        out_shape=jax.ShapeDtypeStruct((DEPTH, MOD_ROWS, n), jnp.float32),
        compiler_params=_params("parallel", "parallel"),
        name="ada_mod",
    )(cond, ada_w, ada_b.reshape(DEPTH, 1, n))


def _pair_epilogue(acc, spec, gains_ref, ones_ref, rope_refs, rope_half):
    norm, gain_row, nvalid, do_rope = spec
    y = acc
    if norm:
        ss = _dot(_bf(acc * acc), ones_ref[...])
        y = acc * lax.rsqrt(ss * (1.0 / nvalid) + EPS) * gains_ref[gain_row:gain_row + 1, :]
    if do_rope and rope_refs is not None:
        cos_ref, sa_ref, sb_ref = rope_refs
        y = (y * cos_ref[...]
             + pltpu.roll(y, PAIR - rope_half, axis=1) * sa_ref[...]
             + pltpu.roll(y, rope_half, axis=1) * sb_ref[...])
    return y


def _proj_kernel(*refs, mod, segs, seg_tiles, rope, rope_half):
    it = iter(refs)
    x_ref = next(it)
    if mod:
        g_ref, sh_ref, sc_ref = next(it), next(it), next(it)
    w_ref = next(it)
    gains_ref, ones_ref = next(it), next(it)
    rope_refs = (next(it), next(it), next(it)) if rope else None
    o_ref = next(it)
    h_ref = next(it) if mod else None
    j = pl.program_id(1)
    tn = o_ref.shape[1]

    if mod:
        @pl.when(j == 0)
        def _():
            h_ref[...] = _bf(_modulate(x_ref[...], g_ref[...], sh_ref[0], sc_ref[0]))
        a = h_ref[...]
    else:
        a = x_ref[...]
    seg = j // seg_tiles
    for s, spec in enumerate(segs):
        @pl.when(seg == s)
        def _(spec=spec):
            for c in range(0, tn, SUB_N):
                acc = _dot(a, w_ref[:, c:c + SUB_N])
                for p in range(0, SUB_N, PAIR):
                    y = _pair_epilogue(acc[:, p:p + PAIR], spec, gains_ref, ones_ref, rope_refs,
                                       rope_half)
                    o_ref[:, c + p:c + p + PAIR] = _bf(y)


def _proj_call(x, w, gains, segs, seg_width, *, mod=None, rope=None, mod_row=None,
               tm=TM, tn=TN, name="proj"):
    m, k = x.shape
    n = w.shape[1]
    tm = min(tm, m)
    tn = min(tn, seg_width)
    assert m % tm == 0 and seg_width % tn == 0 and n % seg_width == 0
    seg_tiles = seg_width // tn
    bpb = SEQ // tm

    in_specs = [pl.BlockSpec((tm, k), lambda i, j: (i, 0))]
    args = [x]
    if mod is not None:
        gain, modarr, sh_c, sc_c = mod
        in_specs += [
            pl.BlockSpec((1, k), lambda i, j: (0, 0)),
            pl.BlockSpec((1, 1, k), lambda i, j: (mod_row(i), 0, sh_c)),
            pl.BlockSpec((1, 1, k), lambda i, j: (mod_row(i), 0, sc_c)),
        ]
        args += [gain, modarr, modarr]
    in_specs.append(pl.BlockSpec((k, tn), lambda i, j: (0, j)))
    args.append(w)
    ones = _pair_blockdiag(np.ones((LANES, LANES), np.float32))
    in_specs += [pl.BlockSpec(gains.shape, lambda i, j: (0, 0)),
                 pl.BlockSpec(ones.shape, lambda i, j: (0, 0))]
    args += [gains, ones]
    if rope is not None:
        tables, rope_half = rope
        for t in tables:
            in_specs.append(pl.BlockSpec((tm, PAIR), lambda i, j: (i % bpb, 0)))
            args.append(t)
    else:
        rope_half = 0
    kern = functools.partial(_proj_kernel, mod=mod is not None, segs=tuple(segs),
                             seg_tiles=seg_tiles, rope=rope is not None, rope_half=rope_half)
    return pl.pallas_call(
        kern,
        grid=(m // tm, n // tn),
        in_specs=in_specs,
        out_specs=pl.BlockSpec((tm, tn), lambda i, j: (i, j)),
        out_shape=jax.ShapeDtypeStruct((m, n), jnp.bfloat16),
        scratch_shapes=[pltpu.VMEM((tm, k), jnp.bfloat16)] if mod is not None else [],
        compiler_params=_params("parallel", "arbitrary"),
        name=name,
    )(*args)


def _outproj_kernel(*refs, n_a):
    a_refs = refs[:n_a]
    w_ref, x_ref, g_ref, o_ref = refs[n_a:]
    n = o_ref.shape[1]
    a_vals = [a_ref[...] for a_ref in a_refs]
    for c in range(0, n, SUB_N):
        acc, row = None, 0
        for a in a_vals:
            part = _dot(a, w_ref[row:row + a.shape[1], c:c + SUB_N])
            acc = part if acc is None else acc + part
            row += a.shape[1]
        o_ref[:, c:c + SUB_N] = x_ref[:, c:c + SUB_N] + g_ref[0][:, c:c + SUB_N] * acc


def _outproj_call(a_list, w, x, modarr, gate_chunk, mod_row, *, tm=TM, name="outproj"):
    m, n = x.shape
    tm = min(tm, m)
    in_specs, args = [], []
    for a in a_list:
        in_specs.append(pl.BlockSpec((tm, a.shape[1]), lambda i: (i, 0)))
        args.append(a)
    in_specs += [
        pl.BlockSpec(w.shape, lambda i: (0, 0)),
        pl.BlockSpec((tm, n), lambda i: (i, 0)),
        pl.BlockSpec((1, 1, n), lambda i: (mod_row(i), 0, gate_chunk)),
    ]
    args += [w, x, modarr]
    return pl.pallas_call(
        functools.partial(_outproj_kernel, n_a=len(a_list)),
        grid=(m // tm,),
        in_specs=in_specs,
        out_specs=pl.BlockSpec((tm, n), lambda i: (i, 0)),
        out_shape=jax.ShapeDtypeStruct((m, n), jnp.float32),
        compiler_params=_params("parallel"),
        name=name,
    )(*args)


def _ffn_kernel(x_ref, xp_ref, xn_ref, g_ref, sh_ref, sc_ref, gt_ref, wa_ref, wb_ref,
                ca_ref, cb_ref, wo_ref, o_ref, h_ref, acc_ref, *, tm, seq_len):
    i = pl.program_id(0)
    j = pl.program_id(1)
    nj = pl.num_programs(1)

    @pl.when(j == 0)
    def _():
        gain, shift, scale = g_ref[...], sh_ref[0], sc_ref[0]
        h_ref[0:tm, :] = _bf(_modulate(x_ref[...], gain, shift, scale))
        halo = jnp.concatenate([xn_ref[...], xp_ref[...]], axis=0)
        hrow = lax.broadcasted_iota(jnp.int32, (2 * SUBLANES, 1), 0)
        keep_prev = jnp.where((i * tm) % seq_len == 0, 0.0, 1.0)
        keep_next = jnp.where(((i + 1) * tm) % seq_len == 0, 0.0, 1.0)
        keep = jnp.where(hrow < SUBLANES, keep_next, keep_prev)
        h_ref[tm:tm + 2 * SUBLANES, :] = _bf(keep * _modulate(halo, gain, shift, scale))
        acc_ref[...] = jnp.zeros_like(acc_ref)

    h = h_ref[...]
    rows = tm + 2 * SUBLANES

    def conv(w_ref, c_ref):
        u = _dot(h, w_ref[...])
        cw = c_ref[...]
        prev = pltpu.roll(u, 1, axis=0)[0:tm]
        nxt = pltpu.roll(u, rows - 1, axis=0)[0:tm]
        return prev * cw[0:1] + u[0:tm] * cw[1:2] + nxt * cw[2:3]

    a = conv(wa_ref, ca_ref)
    b = conv(wb_ref, cb_ref)
    act = a * (1.0 / (1.0 + jnp.exp(-a))) * b
    acc_ref[...] += _dot(_bf(act), wo_ref[...])

    @pl.when(j == nj - 1)
    def _():
        o_ref[...] = x_ref[...] + gt_ref[0] * acc_ref[...]


def _ffn_call(x, gain, modarr, w_in, conv_w, w_out, mod_row, seq_len, *, tm=TM, tn=TN_FF,
              name="conv_ffn"):
    m, d = x.shape
    tm = min(tm, seq_len)
    assert seq_len % tm == 0 and m % tm == 0
    nj = D_FF_PAD // tn
    hb = tm // SUBLANES
    last_hb = m // SUBLANES - 1
    kern = functools.partial(_ffn_kernel, tm=tm, seq_len=seq_len)
    return pl.pallas_call(
        kern,
        grid=(m // tm, nj),
        in_specs=[
            pl.BlockSpec((tm, d), lambda i, j: (i, 0)),
            pl.BlockSpec((SUBLANES, d), lambda i, j: (jnp.maximum(i * hb - 1, 0), 0)),
            pl.BlockSpec((SUBLANES, d), lambda i, j: (jnp.minimum((i + 1) * hb, last_hb), 0)),
            pl.BlockSpec((1, d), lambda i, j: (0, 0)),
            pl.BlockSpec((1, 1, d), lambda i, j: (mod_row(i), 0, 3)),
            pl.BlockSpec((1, 1, d), lambda i, j: (mod_row(i), 0, 4)),
            pl.BlockSpec((1, 1, d), lambda i, j: (mod_row(i), 0, 5)),
            pl.BlockSpec((d, tn), lambda i, j: (0, j)),
            pl.BlockSpec((d, tn), lambda i, j: (0, nj + j)),
            pl.BlockSpec((3, tn), lambda i, j: (0, j)),
            pl.BlockSpec((3, tn), lambda i, j: (0, nj + j)),
            pl.BlockSpec((tn, d), lambda i, j: (j, 0)),
        ],
        out_specs=pl.BlockSpec((tm, d), lambda i, j: (i, 0)),
        out_shape=jax.ShapeDtypeStruct((m, d), jnp.float32),
        scratch_shapes=[
            pltpu.VMEM((tm + 2 * SUBLANES, d), jnp.bfloat16),
            pltpu.VMEM((tm, d), jnp.float32),
        ],
        compiler_params=_params("parallel", "arbitrary"),
        name=name,
    )(x, x, x, gain, modarr, modarr, modarr, w_in, w_in, conv_w, conv_w, w_out)


def _na_group_geometry(g):
    r0 = g * NA_GROUP_ROWS
    return r0, min(max(r0 - NA_WIN_H // 2, 0), GRID_H - NA_KEY_ROWS)


def _na_build_bias(rpb_ref, h, tc_scr, bias_scr):
    n_dr = 2 * NA_WIN_H - 1
    n_dc = 2 * NA_WIN_W - 1
    qc = lax.broadcasted_iota(jnp.int32, (GRID_W, GRID_W), 0)
    kc = lax.broadcasted_iota(jnp.int32, (GRID_W, GRID_W), 1)
    col0 = jnp.clip(qc - NA_WIN_W // 2, 0, GRID_W - NA_WIN_W)
    col_valid = (kc >= col0) & (kc < col0 + NA_WIN_W)
    delta = kc - qc + (NA_WIN_W - 1)
    base = h * (n_dr * n_dc)
    for dr in range(n_dr):
        acc = jnp.zeros((GRID_W, GRID_W), jnp.float32)
        for e in range(n_dc):
            acc = jnp.where(delta == e, rpb_ref[base + dr * n_dc + e] * LOG2E, acc)
        tc_scr[dr] = jnp.where(col_valid, acc, NEG)
    tc_scr[n_dr] = jnp.full((GRID_W, GRID_W), NEG, jnp.float32)
    n_groups = GRID_H // NA_GROUP_ROWS
    for t, g in enumerate((0, 1, n_groups - 1)):
        r0, ks = _na_group_geometry(g)
        for qr in range(NA_GROUP_ROWS):
            r = r0 + qr
            row0 = min(max(r - NA_WIN_H // 2, 0), GRID_H - NA_WIN_H)
            pieces = []
            for kr in range(NA_KEY_ROWS):
                kra = ks + kr
                inside = row0 <= kra < row0 + NA_WIN_H
                pieces.append(tc_scr[kra - r + (NA_WIN_H - 1)] if inside else tc_scr[n_dr])
            bias_scr[t, qr * GRID_W:(qr + 1) * GRID_W, :] = jnp.concatenate(pieces, axis=1)


def _na_kernel(rpb_ref, q_ref, k_ref, v_ref, kc_ref, vc_ref, o_ref, tc_scr, bias_scr):
    @pl.when(pl.program_id(1) == 0)
    def _():
        _na_build_bias(rpb_ref, pl.program_id(0), tc_scr, bias_scr)

    kc = kc_ref[...]
    vc = vc_ref[...]
    n_groups = GRID_H // NA_GROUP_ROWS

    def body(g, carry):
        r0 = g * NA_GROUP_ROWS
        ks = jnp.clip(r0 - NA_WIN_H // 2, 0, GRID_H - NA_KEY_ROWS)
        q0 = pl.multiple_of(g * NA_Q, NA_Q)
        k0 = pl.multiple_of(ks * GRID_W, GRID_W)
        tb = jnp.where(g == 0, 0, jnp.where(g == n_groups - 1, 2, 1))
        q = q_ref[pl.ds(q0, NA_Q), :]
        kw = k_ref[pl.ds(k0, NA_K), :]
        vw = v_ref[pl.ds(k0, NA_K), :]
        sw = _dot_nt(q, kw) + bias_scr[tb]
        sc = _dot_nt(q, kc)
        mx = jnp.maximum(jnp.max(sw, axis=-1, keepdims=True), jnp.max(sc, axis=-1, keepdims=True))
        pw = jnp.exp2(sw - mx)
        pc = jnp.exp2(sc - mx)
        den = jnp.sum(pw, axis=-1, keepdims=True) + jnp.sum(pc, axis=-1, keepdims=True)
        o = _dot(_bf(pw), vw) + _dot(_bf(pc), vc)
        o_ref[pl.ds(q0, NA_Q), :] = _bf(o / den)
        return carry

    lax.fori_loop(0, n_groups, body, 0)


def _na_call(qkv, qkv_c, rpb):
    n_dr = 2 * NA_WIN_H - 1
    return pl.pallas_call(
        _na_kernel,
        grid=(NA_HEADS, BATCH),
        in_specs=[
            pl.BlockSpec(memory_space=pltpu.SMEM),
            pl.BlockSpec((SEQ, LANES), lambda h, b: (b, h)),
            pl.BlockSpec((SEQ, LANES), lambda h, b: (b, NA_HEADS + h)),
            pl.BlockSpec((SEQ, LANES), lambda h, b: (b, 2 * NA_HEADS + h)),
            pl.BlockSpec((CTX_LEN, LANES), lambda h, b: (b, NA_HEADS + h)),
            pl.BlockSpec((CTX_LEN, LANES), lambda h, b: (b, 2 * NA_HEADS + h)),
        ],
        out_specs=pl.BlockSpec((SEQ, LANES), lambda h, b: (b, h)),
        out_shape=jax.ShapeDtypeStruct((BATCH * SEQ, NA_WIDTH), jnp.bfloat16),
        scratch_shapes=[
            pltpu.VMEM((n_dr + 1, GRID_W, GRID_W), jnp.float32),
            pltpu.VMEM((3, NA_Q, NA_K), jnp.float32),
        ],
        compiler_params=_params("parallel", "arbitrary"),
        name="na_attn",
    )(rpb.reshape(-1), qkv, qkv, qkv, qkv_c, qkv_c)


def _key_chunks(src_rows):
    chunks, r = [], 0
    for rows in src_rows:
        size = min(KEY_CHUNK, rows)
        assert rows % size == 0
        chunks += [(r + c, size) for c in range(0, rows, size)]
        r += rows
    return chunks


def _softmax_pv_from_scores(s_scr, vt_scr, chunks):
    tq = s_scr.shape[1]
    dv = vt_scr.shape[0]
    m = jnp.full((1, tq), NEG, jnp.float32)
    for c0, cs in chunks:
        m = jnp.maximum(m, jnp.max(s_scr[c0:c0 + cs, :], axis=0, keepdims=True))
    l = jnp.zeros((1, tq), jnp.float32)
    acc = jnp.zeros((dv, tq), jnp.float32)
    for c0, cs in chunks:
        p = jnp.exp2(s_scr[c0:c0 + cs, :] - m)
        l = l + jnp.sum(p, axis=0, keepdims=True)
        acc = acc + _dot(vt_scr[:, c0:c0 + cs], _bf(p))
    return acc / l


def _attn_kernel(*refs, n_qparts, n_src, n_kparts, src_rows):
    it = iter(refs)
    q_refs = [next(it) for _ in range(n_qparts)]
    srcs = []
    for _ in range(n_src):
        k_refs = [next(it) for _ in range(n_kparts)]
        srcs.append((k_refs, next(it)))
    o_ref = next(it)
    k_scr, vt_scr, s_scr = next(it), next(it), next(it)

    @pl.when(pl.program_id(2) == 0)
    def _():
        r = 0
        for (k_refs, v_ref), rows in zip(srcs, src_rows):
            for p, k_ref in enumerate(k_refs):
                k_scr[r:r + rows, p * LANES:(p + 1) * LANES] = k_ref[...]
            vt_scr[:, r:r + rows] = _bf(v_ref[...].astype(jnp.float32).T)
            r += rows

    q = jnp.concatenate([qr[...] for qr in q_refs], axis=-1) if n_qparts > 1 else q_refs[0][...]
    n_sub, _, sub = s_scr.shape
    k = k_scr[...]
    for u in range(n_sub):
        s_scr[u] = _dot_nt(k, q[u * sub:(u + 1) * sub])
    chunks = _key_chunks(src_rows)
    for u in range(n_sub):
        o_t = _softmax_pv_from_scores(s_scr.at[u], vt_scr, chunks)
        o_ref[u * sub:(u + 1) * sub, :] = _bf(o_t.T)


def _attn_call(q_parts, sources, n_heads, lq, tq, dv, name):
    tq = min(tq, lq)
    sub = min(TQ_SUB, tq)
    nq = lq // tq
    in_specs, args = [], []
    for arr, cf in q_parts:
        in_specs.append(pl.BlockSpec((tq, LANES), lambda b, h, i, cf=cf: (b * nq + i, cf(h))))
        args.append(arr)
    src_rows = []
    n_kparts = len(sources[0][1])
    for rows, k_parts, (v_arr, vcf) in sources:
        src_rows.append(rows)
        for arr, cf in k_parts:
            in_specs.append(pl.BlockSpec((rows, LANES), lambda b, h, i, cf=cf: (b, cf(h))))
            args.append(arr)
        in_specs.append(pl.BlockSpec((rows, dv), lambda b, h, i, cf=vcf: (b, cf(h))))
        args.append(v_arr)
    nk = sum(src_rows)
    kern = functools.partial(_attn_kernel, n_qparts=len(q_parts), n_src=len(sources),
                             n_kparts=n_kparts, src_rows=tuple(src_rows))
    return pl.pallas_call(
        kern,
        grid=(BATCH, n_heads, nq),
        in_specs=in_specs,
        out_specs=pl.BlockSpec((tq, dv), lambda b, h, i: (b * nq + i, h)),
        out_shape=jax.ShapeDtypeStruct((BATCH * lq, n_heads * dv), jnp.bfloat16),
        scratch_shapes=[
            pltpu.VMEM((nk, n_kparts * LANES), jnp.bfloat16),
            pltpu.VMEM((dv, nk), jnp.bfloat16),
            pltpu.VMEM((tq // sub, nk, sub), jnp.float32),
        ],
        compiler_params=_params("parallel", "parallel", "arbitrary"),
        name=name,
    )(*args)


def _diff_kernel(*refs, n_src, src_rows, lam_init):
    it = iter(refs)
    q1_ref, q2_ref = next(it), next(it)
    srcs = [(next(it), next(it), next(it)) for _ in range(n_src)]
    lam_ref, sub_ref = next(it), next(it)
    o_ref = next(it)
    k1_scr, k2_scr, vt_scr, s1_scr, s2_scr = [next(it) for _ in range(5)]

    @pl.when(pl.program_id(2) == 0)
    def _():
        r = 0
        for (k1_ref, k2_ref, v_ref), rows in zip(srcs, src_rows):
            k1_scr[r:r + rows, :] = k1_ref[...]
            k2_scr[r:r + rows, :] = k2_ref[...]
            vt_scr[:, r:r + rows] = _bf(v_ref[...].astype(jnp.float32).T)
            r += rows

    lf = lam_ref[...]
    lam = (jnp.exp(jnp.sum(lf[0:1] * lf[1:2], axis=-1, keepdims=True))
           - jnp.exp(jnp.sum(lf[2:3] * lf[3:4], axis=-1, keepdims=True)) + lam_init)
    chunks = _key_chunks(src_rows)
    s1_scr[...] = _dot_nt(k1_scr[...], q1_ref[...])
    s2_scr[...] = _dot_nt(k2_scr[...], q2_ref[...])
    o_t = (_softmax_pv_from_scores(s1_scr, vt_scr, chunks)
           - lam * _softmax_pv_from_scores(s2_scr, vt_scr, chunks))
    o = o_t.T
    ms = jnp.mean(o * o, axis=-1, keepdims=True)
    o_ref[...] = _bf(o * lax.rsqrt(ms + EPS) * sub_ref[...] * (1.0 - lam_init))


def _diff_call(qkv_q, lq, sources, diff_lambda, subln, lam_init, tq, name):
    tq = min(tq, lq)
    nq = lq // tq
    qb0 = 3 * NA_WIDTH // LANES
    kb0 = (3 * NA_WIDTH + DIFF_WIDTH) // LANES
    vb0 = (3 * NA_WIDTH + 2 * DIFF_WIDTH) // (2 * LANES)
    in_specs = [
        pl.BlockSpec((tq, LANES), lambda b, h, i: (b * nq + i, qb0 + 2 * h)),
        pl.BlockSpec((tq, LANES), lambda b, h, i: (b * nq + i, qb0 + 2 * h + 1)),
    ]
    args = [qkv_q, qkv_q]
    src_rows = []
    for rows, arr in sources:
        src_rows.append(rows)
        in_specs += [
            pl.BlockSpec((rows, LANES), lambda b, h, i: (b, kb0 + 2 * h)),
            pl.BlockSpec((rows, LANES), lambda b, h, i: (b, kb0 + 2 * h + 1)),
            pl.BlockSpec((rows, 2 * LANES), lambda b, h, i: (b, vb0 + h)),
        ]
        args += [arr, arr, arr]
    in_specs += [
        pl.BlockSpec((4, LANES), lambda b, h, i: (0, 0)),
        pl.BlockSpec((1, 2 * LANES), lambda b, h, i: (0, 0)),
    ]
    args += [diff_lambda, subln]
    nk = sum(src_rows)
    kern = functools.partial(_diff_kernel, n_src=len(sources), src_rows=tuple(src_rows),
                             lam_init=lam_init)
    return pl.pallas_call(
        kern,
        grid=(BATCH, DIFF_HEADS, nq),
        in_specs=in_specs,
        out_specs=pl.BlockSpec((tq, 2 * LANES), lambda b, h, i: (b * nq + i, h)),
        out_shape=jax.ShapeDtypeStruct((BATCH * lq, DIFF_WIDTH), jnp.bfloat16),
        scratch_shapes=[
            pltpu.VMEM((nk, LANES), jnp.bfloat16),
            pltpu.VMEM((nk, LANES), jnp.bfloat16),
            pltpu.VMEM((2 * LANES, nk), jnp.bfloat16),
            pltpu.VMEM((nk, tq), jnp.float32),
            pltpu.VMEM((nk, tq), jnp.float32),
        ],
        compiler_params=_params("parallel", "parallel", "arbitrary"),
        name=name,
    )(*args)


def _rope_tables(d):
    h = d // 2
    half = h // 2
    t = jnp.arange(SEQ, dtype=jnp.int32)
    freqs = ROPE_BASE ** (-jnp.arange(half, dtype=jnp.float32) / half)

    def cs(pos):
        ang = pos.astype(jnp.float32)[:, None] * freqs[None, :]
        return jnp.cos(ang), jnp.sin(ang)

    cr, sr = cs(t // GRID_W)
    cc, sc = cs(t % GRID_W)
    cos = jnp.concatenate([cr, cr, cc, cc], -1)
    sin = jnp.concatenate([sr, sr, sc, sc], -1)
    first = (jnp.arange(d) % h) < half
    sin_a = jnp.where(first[None, :], -sin, 0.0)
    sin_b = jnp.where(first[None, :], 0.0, sin)
    pad = LANES - d
    if pad:
        cos = jnp.pad(cos, ((0, 0), (0, pad)), constant_values=1.0)
        sin_a = jnp.pad(sin_a, ((0, 0), (0, pad)))
        sin_b = jnp.pad(sin_b, ((0, 0), (0, pad)))
    pair_tables = tuple(jnp.concatenate([t, t], -1) for t in (cos, sin_a, sin_b))
    return (cos, sin_a, sin_b), half, (pair_tables, half)


def _pair_blockdiag(block):
    z = np.zeros_like(block)
    return jnp.asarray(np.block([[block, z], [z, block]]), dtype=jnp.bfloat16)


def _pad_lanes(v, fill=0.0):
    return jnp.pad(v, (0, LANES - v.shape[0]), constant_values=fill)


def _pair(v):
    return jnp.concatenate([v, v])


def _lat_row(tm):
    bpb = SEQ // tm
    return lambda i: i // bpb


def _ctx_row(i):
    return CTX_MOD_ROW


def _even_layer(x, xc, modarr, norm_mix, norm_ffn, ffn_w, w_in, w_out, qn_a, kn_a, rpb,
                qn_b, kn_b, diff_lambda, subln, lam_init, with_ctx):
    scale = HEAD_DIM ** -0.5 * LOG2E
    gains = jnp.stack([_pair(g) for g in (qn_a * scale, kn_a, qn_b * scale, kn_b)]
                      + [jnp.ones((PAIR,), jnp.float32)] * 4)
    _, _, rope = _rope_tables(HEAD_DIM)
    segs = [(True, 0, HEAD_DIM, False), (True, 1, HEAD_DIM, False), (False, 0, HEAD_DIM, False),
            (True, 2, HEAD_DIM, True), (True, 3, HEAD_DIM, True), (False, 0, HEAD_DIM, False)]
    w_in = _bf(w_in)
    w_out = _bf(w_out)
    gain_mix = norm_mix[None, :]
    lat_row = _lat_row(TM)

    qkv = _proj_call(x, w_in, gains, segs, NA_WIDTH, mod=(gain_mix, modarr, 0, 1), rope=rope,
                     mod_row=lat_row, name="even_proj")
    qkv_c = _proj_call(xc, w_in, gains, segs, NA_WIDTH, mod=(gain_mix, modarr, 0, 1), rope=None,
                       mod_row=_ctx_row, name="even_proj_ctx")

    oa = _na_call(qkv, qkv_c, rpb)
    ob = _diff_call(qkv, SEQ, [(SEQ, qkv), (CTX_LEN, qkv_c)], diff_lambda, subln[None, :],
                    lam_init, TQ_DIFF, "diff_attn")
    x = _outproj_call([oa, ob], w_out, x, modarr, 2, lat_row, name="even_out")
    x = _ffn_call(x, norm_ffn[None, :], modarr, *ffn_w, lat_row, SEQ)
    if with_ctx:
        oa_c = _attn_call(
            [(qkv_c, lambda h: h)],
            [(CTX_LEN, [(qkv_c, lambda h: NA_HEADS + h)], (qkv_c, lambda h: 2 * NA_HEADS + h))],
            NA_HEADS, CTX_LEN, CTX_LEN, HEAD_DIM, "ctx_attn")
        ob_c = _diff_call(qkv_c, CTX_LEN, [(CTX_LEN, qkv_c)], diff_lambda, subln[None, :],
                          lam_init, CTX_LEN, "diff_attn_ctx")
        xc = _outproj_call([oa_c, ob_c], w_out, xc, modarr, 2, _ctx_row, name="even_out_ctx")
        xc = _ffn_call(xc, norm_ffn[None, :], modarr, *ffn_w, _ctx_row, CTX_LEN,
                       name="conv_ffn_ctx")
    return x, xc


def _odd_layer(x, xc, modarr, norm_mix, norm_ffn, ffn_w, w_down, q_a_norm, kv_a_norm, w_uq,
               w_ukv, qn_nope, qn_rope, kn_nope, kn_rope, w_out, with_ctx):
    scale = (MLA_NOPE + MLA_ROPE) ** -0.5 * LOG2E
    rope, half, rope_pair = _rope_tables(MLA_ROPE)
    gain_mix = norm_mix[None, :]
    lat_row = _lat_row(TM)

    w_down_p = _bf(jnp.pad(w_down, ((0, 0), (0, MLA_DOWN_PAD - MLA_DOWN))))
    zeros_r = jnp.zeros((MLA_Q_RANK,), jnp.float32)
    kr_gain = jnp.concatenate([_pad_lanes(kn_rope), zeros_r[:MLA_Q_RANK - LANES]])
    gains_d = jnp.stack([q_a_norm, kv_a_norm, kr_gain] + [zeros_r] * 5)

    def down(xx, rp, row, name):
        return _down_call(xx, w_down_p, gains_d, (gain_mix, modarr, 0, 1), rp, half, row, name)

    qa, kva, kr = down(x, rope, lat_row, "mla_down")
    _, kva_c, kr_c = down(xc, None, _ctx_row, "mla_down_ctx")

    wq = w_uq.reshape(MLA_Q_RANK, MLA_HEADS, MLA_NOPE + MLA_ROPE)
    wq_rope = jnp.pad(wq[:, :, MLA_NOPE:], ((0, 0), (0, 0), (0, LANES - MLA_ROPE)))
    wq_p = _bf(jnp.concatenate([wq[:, :, :MLA_NOPE].reshape(MLA_Q_RANK, -1),
                                wq_rope.reshape(MLA_Q_RANK, -1)], -1))
    gains_q = jnp.stack([_pair(qn_nope * scale), _pair(_pad_lanes(qn_rope) * scale)]
                        + [jnp.zeros((PAIR,), jnp.float32)] * 6)
    segs_q = [(True, 0, MLA_NOPE, False), (True, 1, MLA_ROPE, True)]
    q = _proj_call(qa, wq_p, gains_q, segs_q, MLA_HEADS * LANES, rope=rope_pair, name="mla_q")

    wkv = w_ukv.reshape(MLA_KV_RANK, MLA_HEADS, MLA_NOPE + MLA_V)
    wkv_p = _bf(jnp.concatenate([wkv[:, :, :MLA_NOPE].reshape(MLA_KV_RANK, -1),
                                 wkv[:, :, MLA_NOPE:].reshape(MLA_KV_RANK, -1)], -1))
    gains_kv = jnp.stack([_pair(kn_nope)] + [jnp.zeros((PAIR,), jnp.float32)] * 7)
    segs_kv = [(True, 0, MLA_NOPE, False), (False, 0, MLA_NOPE, False)]
    kv = _proj_call(kva, wkv_p, gains_kv, segs_kv, MLA_HEADS * LANES, name="mla_kv")
    kv_c = _proj_call(kva_c, wkv_p, gains_kv, segs_kv, MLA_HEADS * LANES, name="mla_kv_ctx")

    def src(rows, kv_arr, kr_arr):
        return (rows, [(kv_arr, lambda h: h), (kr_arr, lambda h: 0)],
                (kv_arr, lambda h: MLA_HEADS + h))

    o = _attn_call([(q, lambda h: h), (q, lambda h: MLA_HEADS + h)],
                   [src(SEQ, kv, kr), src(CTX_LEN, kv_c, kr_c)],
                   MLA_HEADS, SEQ, TQ_MLA, MLA_V, "mla_attn")
    x = _outproj_call([o], _bf(w_out), x, modarr, 2, lat_row, name="mla_out")
    x = _ffn_call(x, norm_ffn[None, :], modarr, *ffn_w, lat_row, SEQ)
    assert not with_ctx
    return x, xc


def _down_kernel(x_ref, g_ref, sh_ref, sc_ref, w_ref, gains_ref, *rest, rope, rope_half):
    if rope:
        cos_ref, sa_ref, sb_ref = rest[:3]
        rest = rest[3:]
    qa_ref, kva_ref, kr_ref = rest
    h = _bf(_modulate(x_ref[...], g_ref[...], sh_ref[0], sc_ref[0]))
    acc = _dot(h, w_ref[...])

    def norm(xs, row, nvalid):
        ms = jnp.sum(xs * xs, axis=-1, keepdims=True) * (1.0 / nvalid)
        return xs * lax.rsqrt(ms + EPS) * gains_ref[row:row + 1, 0:xs.shape[1]]

    qa_ref[...] = _bf(norm(acc[:, 0:MLA_Q_RANK], 0, MLA_Q_RANK))
    kva_ref[...] = _bf(norm(acc[:, MLA_Q_RANK:MLA_Q_RANK + MLA_KV_RANK], 1, MLA_KV_RANK))
    kr = norm(acc[:, MLA_Q_RANK + MLA_KV_RANK:MLA_DOWN_PAD], 2, MLA_ROPE)
    if rope:
        kr = (kr * cos_ref[...]
              + pltpu.roll(kr, LANES - rope_half, axis=1) * sa_ref[...]
              + pltpu.roll(kr, rope_half, axis=1) * sb_ref[...])
    kr_ref[...] = _bf(kr)


def _down_call(x, w, gains, mod, rope, rope_half, mod_row, name, tm=TM):
    m, k = x.shape
    tm = min(tm, m)
    bpb = SEQ // tm
    gain, modarr, sh_c, sc_c = mod
    in_specs = [
        pl.BlockSpec((tm, k), lambda i: (i, 0)),
        pl.BlockSpec((1, k), lambda i: (0, 0)),
        pl.BlockSpec((1, 1, k), lambda i: (mod_row(i), 0, sh_c)),
        pl.BlockSpec((1, 1, k), lambda i: (mod_row(i), 0, sc_c)),
        pl.BlockSpec(w.shape, lambda i: (0, 0)),
        pl.BlockSpec(gains.shape, lambda i: (0, 0)),
    ]
    args = [x, gain, modarr, modarr, w, gains]
    if rope is not None:
        for t in rope:
            in_specs.append(pl.BlockSpec((tm, LANES), lambda i: (i % bpb, 0)))
            args.append(t)
    return pl.pallas_call(
        functools.partial(_down_kernel, rope=rope is not None, rope_half=rope_half),
        grid=(m // tm,),
        in_specs=in_specs,
        out_specs=[
            pl.BlockSpec((tm, MLA_Q_RANK), lambda i: (i, 0)),
            pl.BlockSpec((tm, MLA_KV_RANK), lambda i: (i, 0)),
            pl.BlockSpec((tm, LANES), lambda i: (i, 0)),
        ],
        out_shape=[
            jax.ShapeDtypeStruct((m, MLA_Q_RANK), jnp.bfloat16),
            jax.ShapeDtypeStruct((m, MLA_KV_RANK), jnp.bfloat16),
            jax.ShapeDtypeStruct((m, LANES), jnp.bfloat16),
        ],
        compiler_params=_params("parallel"),
        name=name,
    )(*args)


def _ffn_weights(w_in, conv_w, w_out):
    pad = D_FF_PAD - D_FF
    wa = jnp.pad(w_in[:, :D_FF], ((0, 0), (0, pad)))
    wb = jnp.pad(w_in[:, D_FF:], ((0, 0), (0, pad)))
    ca = jnp.pad(conv_w[:, :D_FF], ((0, 0), (0, pad)))
    cb = jnp.pad(conv_w[:, D_FF:], ((0, 0), (0, pad)))
    return (_bf(jnp.concatenate([wa, wb], -1)), jnp.concatenate([ca, cb], -1),
            _bf(jnp.pad(w_out, ((0, pad), (0, 0)))))


def kernel(x, c, ctx, c_ctx, ada_w, ada_b, norm_mix, norm_ffn, ffn_w_in, ffn_conv, ffn_w_out, even_w_in, even_w_out, na_q_norm, na_k_norm, na_rpb, diff_q_norm, diff_k_norm, diff_lambda, diff_subln, mla_w_down, mla_q_a_norm, mla_kv_a_norm, mla_w_uq, mla_w_ukv, mla_q_nope_norm, mla_q_rope_norm, mla_k_nope_norm, mla_k_rope_norm, mla_w_out):
    cond = jnp.concatenate(
        [c, c_ctx[None, :], jnp.zeros((MOD_ROWS - BATCH - 1, D_MODEL), jnp.float32)], 0)
    mod = _ada_call(cond, ada_w, ada_b)
    xl = x.reshape(BATCH * SEQ, D_MODEL)
    xc = ctx.reshape(BATCH * CTX_LEN, D_MODEL)
    for l in range(DEPTH):
        with_ctx = l < DEPTH - 1
        modarr = mod[l].reshape(MOD_ROWS, 1, 6 * D_MODEL)
        ffn_w = _ffn_weights(ffn_w_in[l], ffn_conv[l], ffn_w_out[l])
        i = l // 2
        if l % 2 == 0:
            lam_init = 0.8 - 0.6 * math.exp(-0.3 * l)
            xl, xc = _even_layer(xl, xc, modarr, norm_mix[l], norm_ffn[l], ffn_w, even_w_in[i],
                                 even_w_out[i], na_q_norm[i], na_k_norm[i], na_rpb[i],
                                 diff_q_norm[i], diff_k_norm[i], diff_lambda[i], diff_subln[i],
                                 lam_init, with_ctx)
        else:
            xl, xc = _odd_layer(xl, xc, modarr, norm_mix[l], norm_ffn[l], ffn_w, mla_w_down[i],
                                mla_q_a_norm[i], mla_kv_a_norm[i], mla_w_uq[i], mla_w_ukv[i],
                                mla_q_nope_norm[i], mla_q_rope_norm[i], mla_k_nope_norm[i],
                                mla_k_rope_norm[i], mla_w_out[i], with_ctx)
    return xl.reshape(BATCH, SEQ, D_MODEL)
```

```python
import functools
import math

import numpy as np
import jax
import jax.numpy as jnp
from jax import lax
from jax.experimental import pallas as pl
from jax.experimental.pallas import tpu as pltpu

D_MODEL = 2048
BATCH = 4
SEQ = 4096
DEPTH = 2
GRID_W = 64
GRID_H = SEQ // GRID_W
CTX_LEN = 256
HEAD_DIM = 128
NA_HEADS = 8
NA_WIN_H = 8
NA_WIN_W = 16
DIFF_HEADS = 4
NA_WIDTH = 1024
DIFF_WIDTH = 1024
EVEN_PROJ = 6144
MLA_HEADS = 16
MLA_Q_RANK = 512
MLA_KV_RANK = 512
MLA_NOPE = 128
MLA_ROPE = 64
MLA_V = 128
MLA_DOWN = 1088
D_FF = 5504
ROPE_BASE = 10000.0
EPS = 1e-6

LANES = 128
SUBLANES = 8
MOD_ROWS = 8
CTX_MOD_ROW = BATCH
D_FF_PAD = 5632
MLA_DOWN_PAD = 1152
VMEM_LIMIT = 56 * 1024 * 1024

TM = 512
TN = 1024
TN_FF = 512
TM_FF = 1024
HM_FF = 512
SUB_N = 512
PAIR = 2 * LANES
NA_GROUP_ROWS = 4
NA_KEY_ROWS = 12
NA_Q = NA_GROUP_ROWS * GRID_W
NA_K = NA_KEY_ROWS * GRID_W
TQ_DIFF = 256
TQ_MLA = 512
TQ_SUB = 256
KEY_CHUNK = 512
NEG = -1e30
LOG2E = math.log2(math.e)

_NT = (((1,), (1,)), ((), ()))


def _params(*sem):
    return pltpu.CompilerParams(dimension_semantics=sem, vmem_limit_bytes=VMEM_LIMIT)


def _bf(x):
    return x.astype(jnp.bfloat16)


def _dot(a, b):
    return jnp.dot(a, b, preferred_element_type=jnp.float32)


def _dot_nt(a, b):
    return lax.dot_general(a, b, _NT, preferred_element_type=jnp.float32)


def _modulate(x, gain, shift, scale):
    ms = jnp.mean(x * x, axis=-1, keepdims=True)
    y = x * lax.rsqrt(ms + EPS) * gain
    return y * (1.0 + scale) + shift


def _ada_kernel(c_ref, w_ref, b_ref, o_ref):
    c = c_ref[...]
    s = c * (1.0 / (1.0 + jnp.exp(-c)))
    o_ref[0] = _dot(_bf(s), _bf(w_ref[0])) + b_ref[0]


def _ada_call(cond, ada_w, ada_b):
    tn = 512
    n = ada_w.shape[-1]
    return pl.pallas_call(
        _ada_kernel,
        grid=(DEPTH, n // tn),
        in_specs=[
            pl.BlockSpec((MOD_ROWS, D_MODEL), lambda l, j: (0, 0)),
            pl.BlockSpec((1, D_MODEL, tn), lambda l, j: (l, 0, j)),
            pl.BlockSpec((1, 1, tn), lambda l, j: (l, 0, j)),
        ],
        out_specs=pl.BlockSpec((1, MOD_ROWS, tn), lambda l, j: (l, 0, j)),
        out_shape=jax.ShapeDtypeStruct((DEPTH, MOD_ROWS, n), jnp.float32),
        compiler_params=_params("parallel", "parallel"),
        name="ada_mod",
    )(cond, ada_w, ada_b.reshape(DEPTH, 1, n))


def _pair_epilogue(acc, spec, gains_ref, ones_ref, rope_refs, rope_half):
    norm, gain_row, nvalid, do_rope = spec
    y = acc
    if norm:
        ss = _dot(_bf(acc * acc), ones_ref[...])
        y = acc * lax.rsqrt(ss * (1.0 / nvalid) + EPS) * gains_ref[gain_row:gain_row + 1, :]
    if do_rope and rope_refs is not None:
        cos_ref, sa_ref, sb_ref = rope_refs
        y = (y * cos_ref[...]
             + pltpu.roll(y, PAIR - rope_half, axis=1) * sa_ref[...]
             + pltpu.roll(y, rope_half, axis=1) * sb_ref[...])
    return y


def _proj_kernel(*refs, mod, segs, seg_tiles, rope, rope_half):
    it = iter(refs)
    x_ref = next(it)
    if mod:
        g_ref, sh_ref, sc_ref = next(it), next(it), next(it)
    w_ref = next(it)
    gains_ref, ones_ref = next(it), next(it)
    rope_refs = (next(it), next(it), next(it)) if rope else None
    o_ref = next(it)
    h_ref = next(it) if mod else None
    j = pl.program_id(1)
    tn = o_ref.shape[1]

    if mod:
        @pl.when(j == 0)
        def _():
            h_ref[...] = _bf(_modulate(x_ref[...], g_ref[...], sh_ref[0], sc_ref[0]))
        a = h_ref[...]
    else:
        a = x_ref[...]
    seg = j // seg_tiles
    for s, spec in enumerate(segs):
        @pl.when(seg == s)
        def _(spec=spec):
            for c in range(0, tn, SUB_N):
                acc = _dot(a, w_ref[:, c:c + SUB_N])
                for p in range(0, SUB_N, PAIR):
                    y = _pair_epilogue(acc[:, p:p + PAIR], spec, gains_ref, ones_ref, rope_refs,
                                       rope_half)
                    o_ref[:, c + p:c + p + PAIR] = _bf(y)


def _proj_call(x, w, gains, segs, seg_width, *, mod=None, rope=None, mod_row=None,
               tm=TM, tn=TN, name="proj"):
    m, k = x.shape
    n = w.shape[1]
    tm = min(tm, m)
    tn = min(tn, seg_width)
    assert m % tm == 0 and seg_width % tn == 0 and n % seg_width == 0
    seg_tiles = seg_width // tn
    bpb = SEQ // tm

    in_specs = [pl.BlockSpec((tm, k), lambda i, j: (i, 0))]
    args = [x]
    if mod is not None:
        gain, modarr, sh_c, sc_c = mod
        in_specs += [
            pl.BlockSpec((1, k), lambda i, j: (0, 0)),
            pl.BlockSpec((1, 1, k), lambda i, j: (mod_row(i), 0, sh_c)),
            pl.BlockSpec((1, 1, k), lambda i, j: (mod_row(i), 0, sc_c)),
        ]
        args += [gain, modarr, modarr]
    in_specs.append(pl.BlockSpec((k, tn), lambda i, j: (0, j)))
    args.append(w)
    ones = _pair_blockdiag(np.ones((LANES, LANES), np.float32))
    in_specs += [pl.BlockSpec(gains.shape, lambda i, j: (0, 0)),
                 pl.BlockSpec(ones.shape, lambda i, j: (0, 0))]
    args += [gains, ones]
    if rope is not None:
        tables, rope_half = rope
        for t in tables:
            in_specs.append(pl.BlockSpec((tm, PAIR), lambda i, j: (i % bpb, 0)))
            args.append(t)
    else:
        rope_half = 0
    kern = functools.partial(_proj_kernel, mod=mod is not None, segs=tuple(segs),
                             seg_tiles=seg_tiles, rope=rope is not None, rope_half=rope_half)
    return pl.pallas_call(
        kern,
        grid=(m // tm, n // tn),
        in_specs=in_specs,
        out_specs=pl.BlockSpec((tm, tn), lambda i, j: (i, j)),
        out_shape=jax.ShapeDtypeStruct((m, n), jnp.bfloat16),
        scratch_shapes=[pltpu.VMEM((tm, k), jnp.bfloat16)] if mod is not None else [],
        compiler_params=_params("parallel", "arbitrary"),
        name=name,
    )(*args)


def _outproj_kernel(*refs, n_a):
    a_refs = refs[:n_a]
    w_ref, x_ref, g_ref, o_ref = refs[n_a:]
    n = o_ref.shape[1]
    a_vals = [a_ref[...] for a_ref in a_refs]
    for c in range(0, n, SUB_N):
        acc, row = None, 0
        for a in a_vals:
            part = _dot(a, w_ref[row:row + a.shape[1], c:c + SUB_N])
            acc = part if acc is None else acc + part
            row += a.shape[1]
        o_ref[:, c:c + SUB_N] = x_ref[:, c:c + SUB_N] + g_ref[0][:, c:c + SUB_N] * acc


def _outproj_call(a_list, w, x, modarr, gate_chunk, mod_row, *, tm=TM, name="outproj"):
    m, n = x.shape
    tm = min(tm, m)
    in_specs, args = [], []
    for a in a_list:
        in_specs.append(pl.BlockSpec((tm, a.shape[1]), lambda i: (i, 0)))
        args.append(a)
    in_specs += [
        pl.BlockSpec(w.shape, lambda i: (0, 0)),
        pl.BlockSpec((tm, n), lambda i: (i, 0)),
        pl.BlockSpec((1, 1, n), lambda i: (mod_row(i), 0, gate_chunk)),
    ]
    args += [w, x, modarr]
    return pl.pallas_call(
        functools.partial(_outproj_kernel, n_a=len(a_list)),
        grid=(m // tm,),
        in_specs=in_specs,
        out_specs=pl.BlockSpec((tm, n), lambda i: (i, 0)),
        out_shape=jax.ShapeDtypeStruct((m, n), jnp.float32),
        compiler_params=_params("parallel"),
        name=name,
    )(*args)


def _ffn_kernel(x_ref, xp_ref, xn_ref, g_ref, sh_ref, sc_ref, gt_ref, wa_ref, wb_ref,
                ca_ref, cb_ref, wo_ref, o_ref, h_ref, *, tm, hm, seq_len):
    i = pl.program_id(0)
    j = pl.program_id(1)
    nj = pl.num_programs(1)
    n_grp = tm // hm
    rows = hm + 2 * SUBLANES

    @pl.when(j == 0)
    def _():
        gain, shift, scale = g_ref[...], sh_ref[0], sc_ref[0]
        keep_prev = jnp.where((i * tm) % seq_len == 0, 0.0, 1.0)
        keep_next = jnp.where(((i + 1) * tm) % seq_len == 0, 0.0, 1.0)
        for r in range(n_grp):
            r0 = r * hm
            h_ref[r * rows:r * rows + hm, :] = _bf(
                _modulate(x_ref[r0:r0 + hm, :], gain, shift, scale))
            if r + 1 < n_grp:
                nxt = _modulate(x_ref[r0 + hm:r0 + hm + SUBLANES, :], gain, shift, scale)
            else:
                nxt = keep_next * _modulate(xn_ref[...], gain, shift, scale)
            if r > 0:
                prv = _modulate(x_ref[r0 - SUBLANES:r0, :], gain, shift, scale)
            else:
                prv = keep_prev * _modulate(xp_ref[...], gain, shift, scale)
            h_ref[r * rows + hm:(r + 1) * rows, :] = _bf(jnp.concatenate([nxt, prv], axis=0))
        o_ref[...] = jnp.zeros_like(o_ref)

    ca, cb = ca_ref[...], cb_ref[...]
    ups = []
    for r in range(n_grp):
        h = h_ref[r * rows:(r + 1) * rows, :]
        ups.append((_dot(h, wa_ref[...]), _dot(h, wb_ref[...])))

    def conv(u, cw):
        prev = pltpu.roll(u, 1, axis=0)[0:hm]
        nxt = pltpu.roll(u, rows - 1, axis=0)[0:hm]
        return prev * cw[0:1] + u[0:hm] * cw[1:2] + nxt * cw[2:3]

    for r, (ua, ub) in enumerate(ups):
        a = conv(ua, ca)
        b = conv(ub, cb)
        act = a * (1.0 / (1.0 + jnp.exp(-a))) * b
        o_ref[r * hm:(r + 1) * hm, :] += _dot(_bf(act), wo_ref[...])

    @pl.when(j == nj - 1)
    def _():
        o_ref[...] = x_ref[...] + gt_ref[0] * o_ref[...]


def _ffn_call(x, gain, modarr, w_in, conv_w, w_out, mod_row, seq_len, *, tm=TM_FF, tn=TN_FF,
              name="conv_ffn"):
    m, d = x.shape
    tm = min(tm, seq_len)
    hm = min(HM_FF, tm)
    assert seq_len % tm == 0 and m % tm == 0 and tm % hm == 0
    nj = D_FF_PAD // tn
    hb = tm // SUBLANES
    last_hb = m // SUBLANES - 1
    kern = functools.partial(_ffn_kernel, tm=tm, hm=hm, seq_len=seq_len)
    return pl.pallas_call(
        kern,
        grid=(m // tm, nj),
        in_specs=[
            pl.BlockSpec((tm, d), lambda i, j: (i, 0)),
            pl.BlockSpec((SUBLANES, d), lambda i, j: (jnp.maximum(i * hb - 1, 0), 0)),
            pl.BlockSpec((SUBLANES, d), lambda i, j: (jnp.minimum((i + 1) * hb, last_hb), 0)),
            pl.BlockSpec((1, d), lambda i, j: (0, 0)),
            pl.BlockSpec((1, 1, d), lambda i, j: (mod_row(i), 0, 3)),
            pl.BlockSpec((1, 1, d), lambda i, j: (mod_row(i), 0, 4)),
            pl.BlockSpec((1, 1, d), lambda i, j: (mod_row(i), 0, 5)),
            pl.BlockSpec((d, tn), lambda i, j: (0, j)),
            pl.BlockSpec((d, tn), lambda i, j: (0, nj + j)),
            pl.BlockSpec((3, tn), lambda i, j: (0, j)),
            pl.BlockSpec((3, tn), lambda i, j: (0, nj + j)),
            pl.BlockSpec((tn, d), lambda i, j: (j, 0)),
        ],
        out_specs=pl.BlockSpec((tm, d), lambda i, j: (i, 0)),
        out_shape=jax.ShapeDtypeStruct((m, d), jnp.float32),
        scratch_shapes=[
            pltpu.VMEM(((tm // hm) * (hm + 2 * SUBLANES), d), jnp.bfloat16),
        ],
        compiler_params=_params("parallel", "arbitrary"),
        name=name,
    )(x, x, x, gain, modarr, modarr, modarr, w_in, w_in, conv_w, conv_w, w_out)


def _na_group_geometry(g):
    r0 = g * NA_GROUP_ROWS
    return r0, min(max(r0 - NA_WIN_H // 2, 0), GRID_H - NA_KEY_ROWS)


def _na_build_bias(rpb_ref, h, tc_scr, bias_scr):
    n_dr = 2 * NA_WIN_H - 1
    n_dc = 2 * NA_WIN_W - 1
    qc = lax.broadcasted_iota(jnp.int32, (GRID_W, GRID_W), 0)
    kc = lax.broadcasted_iota(jnp.int32, (GRID_W, GRID_W), 1)
    col0 = jnp.clip(qc - NA_WIN_W // 2, 0, GRID_W - NA_WIN_W)
    col_valid = (kc >= col0) & (kc < col0 + NA_WIN_W)
    delta = kc - qc + (NA_WIN_W - 1)
    base = h * (n_dr * n_dc)
    for dr in range(n_dr):
        acc = jnp.zeros((GRID_W, GRID_W), jnp.float32)
        for e in range(n_dc):
            acc = jnp.where(delta == e, rpb_ref[base + dr * n_dc + e] * LOG2E, acc)
        tc_scr[dr] = jnp.where(col_valid, acc, NEG)
    tc_scr[n_dr] = jnp.full((GRID_W, GRID_W), NEG, jnp.float32)
    n_groups = GRID_H // NA_GROUP_ROWS
    for t, g in enumerate((0, 1, n_groups - 1)):
        r0, ks = _na_group_geometry(g)
        for qr in range(NA_GROUP_ROWS):
            r = r0 + qr
            row0 = min(max(r - NA_WIN_H // 2, 0), GRID_H - NA_WIN_H)
            pieces = []
            for kr in range(NA_KEY_ROWS):
                kra = ks + kr
                inside = row0 <= kra < row0 + NA_WIN_H
                pieces.append(tc_scr[kra - r + (NA_WIN_H - 1)] if inside else tc_scr[n_dr])
            bias_scr[t, qr * GRID_W:(qr + 1) * GRID_W, :] = jnp.concatenate(pieces, axis=1)


def _na_kernel(rpb_ref, q_ref, k_ref, v_ref, kc_ref, vc_ref, o_ref, tc_scr, bias_scr):
    @pl.when(pl.program_id(1) == 0)
    def _():
        _na_build_bias(rpb_ref, pl.program_id(0), tc_scr, bias_scr)

    kc = kc_ref[...]
    vc = vc_ref[...]
    n_groups = GRID_H // NA_GROUP_ROWS

    def body(g, carry):
        r0 = g * NA_GROUP_ROWS
        ks = jnp.clip(r0 - NA_WIN_H // 2, 0, GRID_H - NA_KEY_ROWS)
        q0 = pl.multiple_of(g * NA_Q, NA_Q)
        k0 = pl.multiple_of(ks * GRID_W, GRID_W)
        tb = jnp.where(g == 0, 0, jnp.where(g == n_groups - 1, 2, 1))
        q = q_ref[pl.ds(q0, NA_Q), :]
        kw = k_ref[pl.ds(k0, NA_K), :]
        vw = v_ref[pl.ds(k0, NA_K), :]
        sw = _dot_nt(q, kw) + bias_scr[tb]
        sc = _dot_nt(q, kc)
        mx = jnp.maximum(jnp.max(sw, axis=-1, keepdims=True), jnp.max(sc, axis=-1, keepdims=True))
        pw = jnp.exp2(sw - mx)
        pc = jnp.exp2(sc - mx)
        den = jnp.sum(pw, axis=-1, keepdims=True) + jnp.sum(pc, axis=-1, keepdims=True)
        o = _dot(_bf(pw), vw) + _dot(_bf(pc), vc)
        o_ref[pl.ds(q0, NA_Q), :] = _bf(o / den)
        return carry

    lax.fori_loop(0, n_groups, body, 0)


def _na_call(qkv, qkv_c, rpb):
    n_dr = 2 * NA_WIN_H - 1
    return pl.pallas_call(
        _na_kernel,
        grid=(NA_HEADS, BATCH),
        in_specs=[
            pl.BlockSpec(memory_space=pltpu.SMEM),
            pl.BlockSpec((SEQ, LANES), lambda h, b: (b, h)),
            pl.BlockSpec((SEQ, LANES), lambda h, b: (b, NA_HEADS + h)),
            pl.BlockSpec((SEQ, LANES), lambda h, b: (b, 2 * NA_HEADS + h)),
            pl.BlockSpec((CTX_LEN, LANES), lambda h, b: (b, NA_HEADS + h)),
            pl.BlockSpec((CTX_LEN, LANES), lambda h, b: (b, 2 * NA_HEADS + h)),
        ],
        out_specs=pl.BlockSpec((SEQ, LANES), lambda h, b: (b, h)),
        out_shape=jax.ShapeDtypeStruct((BATCH * SEQ, NA_WIDTH), jnp.bfloat16),
        scratch_shapes=[
            pltpu.VMEM((n_dr + 1, GRID_W, GRID_W), jnp.float32),
            pltpu.VMEM((3, NA_Q, NA_K), jnp.float32),
        ],
        compiler_params=_params("parallel", "arbitrary"),
        name="na_attn",
    )(rpb.reshape(-1), qkv, qkv, qkv, qkv_c, qkv_c)


def _key_chunks(src_rows):
    chunks, r = [], 0
    for rows in src_rows:
        size = min(KEY_CHUNK, rows)
        assert rows % size == 0
        chunks += [(r + c, size) for c in range(0, rows, size)]
        r += rows
    return chunks


def _softmax_pv_from_scores(s_scr, vt_scr, chunks):
    tq = s_scr.shape[1]
    dv = vt_scr.shape[0]
    m = jnp.full((1, tq), NEG, jnp.float32)
    for c0, cs in chunks:
        m = jnp.maximum(m, jnp.max(s_scr[c0:c0 + cs, :], axis=0, keepdims=True))
    l = jnp.zeros((1, tq), jnp.float32)
    acc = jnp.zeros((dv, tq), jnp.float32)
    for c0, cs in chunks:
        p = jnp.exp2(s_scr[c0:c0 + cs, :] - m)
        l = l + jnp.sum(p, axis=0, keepdims=True)
        acc = acc + _dot(vt_scr[:, c0:c0 + cs], _bf(p))
    return acc / l


def _attn_kernel(*refs, n_qparts, n_src, n_kparts, src_rows):
    it = iter(refs)
    q_refs = [next(it) for _ in range(n_qparts)]
    srcs = []
    for _ in range(n_src):
        k_refs = [next(it) for _ in range(n_kparts)]
        srcs.append((k_refs, next(it)))
    o_ref = next(it)
    k_scr, vt_scr, s_scr = next(it), next(it), next(it)

    @pl.when(pl.program_id(2) == 0)
    def _():
        r = 0
        for (k_refs, v_ref), rows in zip(srcs, src_rows):
            for p, k_ref in enumerate(k_refs):
                k_scr[r:r + rows, p * LANES:(p + 1) * LANES] = k_ref[...]
            vt_scr[:, r:r + rows] = _bf(v_ref[...].astype(jnp.float32).T)
            r += rows

    q = jnp.concatenate([qr[...] for qr in q_refs], axis=-1) if n_qparts > 1 else q_refs[0][...]
    n_sub, _, sub = s_scr.shape
    k = k_scr[...]
    for u in range(n_sub):
        s_scr[u] = _dot_nt(k, q[u * sub:(u + 1) * sub])
    chunks = _key_chunks(src_rows)
    for u in range(n_sub):
        o_t = _softmax_pv_from_scores(s_scr.at[u], vt_scr, chunks)
        o_ref[u * sub:(u + 1) * sub, :] = _bf(o_t.T)


def _attn_call(q_parts, sources, n_heads, lq, tq, dv, name):
    tq = min(tq, lq)
    sub = min(TQ_SUB, tq)
    nq = lq // tq
    in_specs, args = [], []
    for arr, cf in q_parts:
        in_specs.append(pl.BlockSpec((tq, LANES), lambda b, h, i, cf=cf: (b * nq + i, cf(h))))
        args.append(arr)
    src_rows = []
    n_kparts = len(sources[0][1])
    for rows, k_parts, (v_arr, vcf) in sources:
        src_rows.append(rows)
        for arr, cf in k_parts:
            in_specs.append(pl.BlockSpec((rows, LANES), lambda b, h, i, cf=cf: (b, cf(h))))
            args.append(arr)
        in_specs.append(pl.BlockSpec((rows, dv), lambda b, h, i, cf=vcf: (b, cf(h))))
        args.append(v_arr)
    nk = sum(src_rows)
    kern = functools.partial(_attn_kernel, n_qparts=len(q_parts), n_src=len(sources),
                             n_kparts=n_kparts, src_rows=tuple(src_rows))
    return pl.pallas_call(
        kern,
        grid=(BATCH, n_heads, nq),
        in_specs=in_specs,
        out_specs=pl.BlockSpec((tq, dv), lambda b, h, i: (b * nq + i, h)),
        out_shape=jax.ShapeDtypeStruct((BATCH * lq, n_heads * dv), jnp.bfloat16),
        scratch_shapes=[
            pltpu.VMEM((nk, n_kparts * LANES), jnp.bfloat16),
            pltpu.VMEM((dv, nk), jnp.bfloat16),
            pltpu.VMEM((tq // sub, nk, sub), jnp.float32),
        ],
        compiler_params=_params("parallel", "parallel", "arbitrary"),
        name=name,
    )(*args)


def _diff_kernel(*refs, n_src, src_rows, lam_init):
    it = iter(refs)
    q1_ref, q2_ref = next(it), next(it)
    srcs = [(next(it), next(it), next(it)) for _ in range(n_src)]
    lam_ref, sub_ref = next(it), next(it)
    o_ref = next(it)
    k1_scr, k2_scr, vt_scr, s1_scr, s2_scr = [next(it) for _ in range(5)]

    @pl.when(pl.program_id(2) == 0)
    def _():
        r = 0
        for (k1_ref, k2_ref, v_ref), rows in zip(srcs, src_rows):
            k1_scr[r:r + rows, :] = k1_ref[...]
            k2_scr[r:r + rows, :] = k2_ref[...]
            vt_scr[:, r:r + rows] = _bf(v_ref[...].astype(jnp.float32).T)
            r += rows

    lf = lam_ref[...]
    lam = (jnp.exp(jnp.sum(lf[0:1] * lf[1:2], axis=-1, keepdims=True))
           - jnp.exp(jnp.sum(lf[2:3] * lf[3:4], axis=-1, keepdims=True)) + lam_init)
    chunks = _key_chunks(src_rows)
    s1_scr[...] = _dot_nt(k1_scr[...], q1_ref[...])
    s2_scr[...] = _dot_nt(k2_scr[...], q2_ref[...])
    o_t = (_softmax_pv_from_scores(s1_scr, vt_scr, chunks)
           - lam * _softmax_pv_from_scores(s2_scr, vt_scr, chunks))
    o = o_t.T
    ms = jnp.mean(o * o, axis=-1, keepdims=True)
    o_ref[...] = _bf(o * lax.rsqrt(ms + EPS) * sub_ref[...] * (1.0 - lam_init))


def _diff_call(qkv_q, lq, sources, diff_lambda, subln, lam_init, tq, name):
    tq = min(tq, lq)
    nq = lq // tq
    qb0 = 3 * NA_WIDTH // LANES
    kb0 = (3 * NA_WIDTH + DIFF_WIDTH) // LANES
    vb0 = (3 * NA_WIDTH + 2 * DIFF_WIDTH) // (2 * LANES)
    in_specs = [
        pl.BlockSpec((tq, LANES), lambda b, h, i: (b * nq + i, qb0 + 2 * h)),
        pl.BlockSpec((tq, LANES), lambda b, h, i: (b * nq + i, qb0 + 2 * h + 1)),
    ]
    args = [qkv_q, qkv_q]
    src_rows = []
    for rows, arr in sources:
        src_rows.append(rows)
        in_specs += [
            pl.BlockSpec((rows, LANES), lambda b, h, i: (b, kb0 + 2 * h)),
            pl.BlockSpec((rows, LANES), lambda b, h, i: (b, kb0 + 2 * h + 1)),
            pl.BlockSpec((rows, 2 * LANES), lambda b, h, i: (b, vb0 + h)),
        ]
        args += [arr, arr, arr]
    in_specs += [
        pl.BlockSpec((4, LANES), lambda b, h, i: (0, 0)),
        pl.BlockSpec((1, 2 * LANES), lambda b, h, i: (0, 0)),
    ]
    args += [diff_lambda, subln]
    nk = sum(src_rows)
    kern = functools.partial(_diff_kernel, n_src=len(sources), src_rows=tuple(src_rows),
                             lam_init=lam_init)
    return pl.pallas_call(
        kern,
        grid=(BATCH, DIFF_HEADS, nq),
        in_specs=in_specs,
        out_specs=pl.BlockSpec((tq, 2 * LANES), lambda b, h, i: (b * nq + i, h)),
        out_shape=jax.ShapeDtypeStruct((BATCH * lq, DIFF_WIDTH), jnp.bfloat16),
        scratch_shapes=[
            pltpu.VMEM((nk, LANES), jnp.bfloat16),
            pltpu.VMEM((nk, LANES), jnp.bfloat16),
            pltpu.VMEM((2 * LANES, nk), jnp.bfloat16),
            pltpu.VMEM((nk, tq), jnp.float32),
            pltpu.VMEM((nk, tq), jnp.float32),
        ],
        compiler_params=_params("parallel", "parallel", "arbitrary"),
        name=name,
    )(*args)


def _rope_tables(d):
    h = d // 2
    half = h // 2
    t = jnp.arange(SEQ, dtype=jnp.int32)
    freqs = ROPE_BASE ** (-jnp.arange(half, dtype=jnp.float32) / half)

    def cs(pos):
        ang = pos.astype(jnp.float32)[:, None] * freqs[None, :]
        return jnp.cos(ang), jnp.sin(ang)

    cr, sr = cs(t // GRID_W)
    cc, sc = cs(t % GRID_W)
    cos = jnp.concatenate([cr, cr, cc, cc], -1)
    sin = jnp.concatenate([sr, sr, sc, sc], -1)
    first = (jnp.arange(d) % h) < half
    sin_a = jnp.where(first[None, :], -sin, 0.0)
    sin_b = jnp.where(first[None, :], 0.0, sin)
    pad = LANES - d
    if pad:
        cos = jnp.pad(cos, ((0, 0), (0, pad)), constant_values=1.0)
        sin_a = jnp.pad(sin_a, ((0, 0), (0, pad)))
        sin_b = jnp.pad(sin_b, ((0, 0), (0, pad)))
    pair_tables = tuple(jnp.concatenate([t, t], -1) for t in (cos, sin_a, sin_b))
    return (cos, sin_a, sin_b), half, (pair_tables, half)


def _pair_blockdiag(block):
    z = np.zeros_like(block)
    return jnp.asarray(np.block([[block, z], [z, block]]), dtype=jnp.bfloat16)


def _pad_lanes(v, fill=0.0):
    return jnp.pad(v, (0, LANES - v.shape[0]), constant_values=fill)


def _pair(v):
    return jnp.concatenate([v, v])


def _lat_row(tm):
    bpb = SEQ // tm
    return lambda i: i // bpb


def _ctx_row(i):
    return CTX_MOD_ROW


def _even_layer(x, xc, modarr, norm_mix, norm_ffn, ffn_w, w_in, w_out, qn_a, kn_a, rpb,
                qn_b, kn_b, diff_lambda, subln, lam_init, with_ctx):
    scale = HEAD_DIM ** -0.5 * LOG2E
    gains = jnp.stack([_pair(g) for g in (qn_a * scale, kn_a, qn_b * scale, kn_b)]
                      + [jnp.ones((PAIR,), jnp.float32)] * 4)
    _, _, rope = _rope_tables(HEAD_DIM)
    segs = [(True, 0, HEAD_DIM, False), (True, 1, HEAD_DIM, False), (False, 0, HEAD_DIM, False),
            (True, 2, HEAD_DIM, True), (True, 3, HEAD_DIM, True), (False, 0, HEAD_DIM, False)]
    w_in = _bf(w_in)
    w_out = _bf(w_out)
    gain_mix = norm_mix[None, :]
    lat_row = _lat_row(TM)

    qkv = _proj_call(x, w_in, gains, segs, NA_WIDTH, mod=(gain_mix, modarr, 0, 1), rope=rope,
                     mod_row=lat_row, name="even_proj")
    qkv_c = _proj_call(xc, w_in, gains, segs, NA_WIDTH, mod=(gain_mix, modarr, 0, 1), rope=None,
                       mod_row=_ctx_row, name="even_proj_ctx")

    oa = _na_call(qkv, qkv_c, rpb)
    ob = _diff_call(qkv, SEQ, [(SEQ, qkv), (CTX_LEN, qkv_c)], diff_lambda, subln[None, :],
                    lam_init, TQ_DIFF, "diff_attn")
    x = _outproj_call([oa, ob], w_out, x, modarr, 2, lat_row, name="even_out")
    x = _ffn_call(x, norm_ffn[None, :], modarr, *ffn_w, _lat_row(TM_FF), SEQ)
    if with_ctx:
        oa_c = _attn_call(
            [(qkv_c, lambda h: h)],
            [(CTX_LEN, [(qkv_c, lambda h: NA_HEADS + h)], (qkv_c, lambda h: 2 * NA_HEADS + h))],
            NA_HEADS, CTX_LEN, CTX_LEN, HEAD_DIM, "ctx_attn")
        ob_c = _diff_call(qkv_c, CTX_LEN, [(CTX_LEN, qkv_c)], diff_lambda, subln[None, :],
                          lam_init, CTX_LEN, "diff_attn_ctx")
        xc = _outproj_call([oa_c, ob_c], w_out, xc, modarr, 2, _ctx_row, name="even_out_ctx")
        xc = _ffn_call(xc, norm_ffn[None, :], modarr, *ffn_w, _ctx_row, CTX_LEN,
                       name="conv_ffn_ctx")
    return x, xc


def _odd_layer(x, xc, modarr, norm_mix, norm_ffn, ffn_w, w_down, q_a_norm, kv_a_norm, w_uq,
               w_ukv, qn_nope, qn_rope, kn_nope, kn_rope, w_out, with_ctx):
    scale = (MLA_NOPE + MLA_ROPE) ** -0.5 * LOG2E
    rope, half, rope_pair = _rope_tables(MLA_ROPE)
    gain_mix = norm_mix[None, :]
    lat_row = _lat_row(TM)

    w_down_p = _bf(jnp.pad(w_down, ((0, 0), (0, MLA_DOWN_PAD - MLA_DOWN))))
    zeros_r = jnp.zeros((MLA_Q_RANK,), jnp.float32)
    kr_gain = jnp.concatenate([_pad_lanes(kn_rope), zeros_r[:MLA_Q_RANK - LANES]])
    gains_d = jnp.stack([q_a_norm, kv_a_norm, kr_gain] + [zeros_r] * 5)

    def down(xx, rp, row, name):
        return _down_call(xx, w_down_p, gains_d, (gain_mix, modarr, 0, 1), rp, half, row, name)

    qa, kva, kr = down(x, rope, lat_row, "mla_down")
    _, kva_c, kr_c = down(xc, None, _ctx_row, "mla_down_ctx")

    wq = w_uq.reshape(MLA_Q_RANK, MLA_HEADS, MLA_NOPE + MLA_ROPE)
    wq_rope = jnp.pad(wq[:, :, MLA_NOPE:], ((0, 0), (0, 0), (0, LANES - MLA_ROPE)))
    wq_p = _bf(jnp.concatenate([wq[:, :, :MLA_NOPE].reshape(MLA_Q_RANK, -1),
                                wq_rope.reshape(MLA_Q_RANK, -1)], -1))
    gains_q = jnp.stack([_pair(qn_nope * scale), _pair(_pad_lanes(qn_rope) * scale)]
                        + [jnp.zeros((PAIR,), jnp.float32)] * 6)
    segs_q = [(True, 0, MLA_NOPE, False), (True, 1, MLA_ROPE, True)]
    q = _proj_call(qa, wq_p, gains_q, segs_q, MLA_HEADS * LANES, rope=rope_pair, name="mla_q")

    wkv = w_ukv.reshape(MLA_KV_RANK, MLA_HEADS, MLA_NOPE + MLA_V)
    wkv_p = _bf(jnp.concatenate([wkv[:, :, :MLA_NOPE].reshape(MLA_KV_RANK, -1),
                                 wkv[:, :, MLA_NOPE:].reshape(MLA_KV_RANK, -1)], -1))
    gains_kv = jnp.stack([_pair(kn_nope)] + [jnp.zeros((PAIR,), jnp.float32)] * 7)
    segs_kv = [(True, 0, MLA_NOPE, False), (False, 0, MLA_NOPE, False)]
    kv = _proj_call(kva, wkv_p, gains_kv, segs_kv, MLA_HEADS * LANES, name="mla_kv")
    kv_c = _proj_call(kva_c, wkv_p, gains_kv, segs_kv, MLA_HEADS * LANES, name="mla_kv_ctx")

    def src(rows, kv_arr, kr_arr):
        return (rows, [(kv_arr, lambda h: h), (kr_arr, lambda h: 0)],
                (kv_arr, lambda h: MLA_HEADS + h))

    o = _attn_call([(q, lambda h: h), (q, lambda h: MLA_HEADS + h)],
                   [src(SEQ, kv, kr), src(CTX_LEN, kv_c, kr_c)],
                   MLA_HEADS, SEQ, TQ_MLA, MLA_V, "mla_attn")
    x = _outproj_call([o], _bf(w_out), x, modarr, 2, lat_row, name="mla_out")
    x = _ffn_call(x, norm_ffn[None, :], modarr, *ffn_w, _lat_row(TM_FF), SEQ)
    assert not with_ctx
    return x, xc


def _down_kernel(x_ref, g_ref, sh_ref, sc_ref, w_ref, gains_ref, *rest, rope, rope_half):
    if rope:
        cos_ref, sa_ref, sb_ref = rest[:3]
        rest = rest[3:]
    qa_ref, kva_ref, kr_ref = rest
    h = _bf(_modulate(x_ref[...], g_ref[...], sh_ref[0], sc_ref[0]))
    acc = _dot(h, w_ref[...])

    def norm(xs, row, nvalid):
        ms = jnp.sum(xs * xs, axis=-1, keepdims=True) * (1.0 / nvalid)
        return xs * lax.rsqrt(ms + EPS) * gains_ref[row:row + 1, 0:xs.shape[1]]

    qa_ref[...] = _bf(norm(acc[:, 0:MLA_Q_RANK], 0, MLA_Q_RANK))
    kva_ref[...] = _bf(norm(acc[:, MLA_Q_RANK:MLA_Q_RANK + MLA_KV_RANK], 1, MLA_KV_RANK))
    kr = norm(acc[:, MLA_Q_RANK + MLA_KV_RANK:MLA_DOWN_PAD], 2, MLA_ROPE)
    if rope:
        kr = (kr * cos_ref[...]
              + pltpu.roll(kr, LANES - rope_half, axis=1) * sa_ref[...]
              + pltpu.roll(kr, rope_half, axis=1) * sb_ref[...])
    kr_ref[...] = _bf(kr)


def _down_call(x, w, gains, mod, rope, rope_half, mod_row, name, tm=TM):
    m, k = x.shape
    tm = min(tm, m)
    bpb = SEQ // tm
    gain, modarr, sh_c, sc_c = mod
    in_specs = [
        pl.BlockSpec((tm, k), lambda i: (i, 0)),
        pl.BlockSpec((1, k), lambda i: (0, 0)),
        pl.BlockSpec((1, 1, k), lambda i: (mod_row(i), 0, sh_c)),
        pl.BlockSpec((1, 1, k), lambda i: (mod_row(i), 0, sc_c)),
        pl.BlockSpec(w.shape, lambda i: (0, 0)),
        pl.BlockSpec(gains.shape, lambda i: (0, 0)),
    ]
    args = [x, gain, modarr, modarr, w, gains]
    if rope is not None:
        for t in rope:
            in_specs.append(pl.BlockSpec((tm, LANES), lambda i: (i % bpb, 0)))
            args.append(t)
    return pl.pallas_call(
        functools.partial(_down_kernel, rope=rope is not None, rope_half=rope_half),
        grid=(m // tm,),
        in_specs=in_specs,
        out_specs=[
            pl.BlockSpec((tm, MLA_Q_RANK), lambda i: (i, 0)),
            pl.BlockSpec((tm, MLA_KV_RANK), lambda i: (i, 0)),
            pl.BlockSpec((tm, LANES), lambda i: (i, 0)),
        ],
        out_shape=[
            jax.ShapeDtypeStruct((m, MLA_Q_RANK), jnp.bfloat16),
            jax.ShapeDtypeStruct((m, MLA_KV_RANK), jnp.bfloat16),
            jax.ShapeDtypeStruct((m, LANES), jnp.bfloat16),
        ],
        compiler_params=_params("parallel"),
        name=name,
    )(*args)


def _ffn_weights(w_in, conv_w, w_out):
    pad = D_FF_PAD - D_FF
    wa = jnp.pad(w_in[:, :D_FF], ((0, 0), (0, pad)))
    wb = jnp.pad(w_in[:, D_FF:], ((0, 0), (0, pad)))
    ca = jnp.pad(conv_w[:, :D_FF], ((0, 0), (0, pad)))
    cb = jnp.pad(conv_w[:, D_FF:], ((0, 0), (0, pad)))
    return (_bf(jnp.concatenate([wa, wb], -1)), jnp.concatenate([ca, cb], -1),
            _bf(jnp.pad(w_out, ((0, pad), (0, 0)))))


def kernel(x, c, ctx, c_ctx, ada_w, ada_b, norm_mix, norm_ffn, ffn_w_in, ffn_conv, ffn_w_out, even_w_in, even_w_out, na_q_norm, na_k_norm, na_rpb, diff_q_norm, diff_k_norm, diff_lambda, diff_subln, mla_w_down, mla_q_a_norm, mla_kv_a_norm, mla_w_uq, mla_w_ukv, mla_q_nope_norm, mla_q_rope_norm, mla_k_nope_norm, mla_k_rope_norm, mla_w_out):
    cond = jnp.concatenate(
        [c, c_ctx[None, :], jnp.zeros((MOD_ROWS - BATCH - 1, D_MODEL), jnp.float32)], 0)
    mod = _ada_call(cond, ada_w, ada_b)
    xl = x.reshape(BATCH * SEQ, D_MODEL)
    xc = ctx.reshape(BATCH * CTX_LEN, D_MODEL)
    for l in range(DEPTH):
        with_ctx = l < DEPTH - 1
        modarr = mod[l].reshape(MOD_ROWS, 1, 6 * D_MODEL)
        ffn_w = _ffn_weights(ffn_w_in[l], ffn_conv[l], ffn_w_out[l])
        i = l // 2
        if l % 2 == 0:
            lam_init = 0.8 - 0.6 * math.exp(-0.3 * l)
            xl, xc = _even_layer(xl, xc, modarr, norm_mix[l], norm_ffn[l], ffn_w, even_w_in[i],
                                 even_w_out[i], na_q_norm[i], na_k_norm[i], na_rpb[i],
                                 diff_q_norm[i], diff_k_norm[i], diff_lambda[i], diff_subln[i],
                                 lam_init, with_ctx)
        else:
            xl, xc = _odd_layer(xl, xc, modarr, norm_mix[l], norm_ffn[l], ffn_w, mla_w_down[i],
                                mla_q_a_norm[i], mla_kv_a_norm[i], mla_w_uq[i], mla_w_ukv[i],
                                mla_q_nope_norm[i], mla_q_rope_norm[i], mla_k_nope_norm[i],
                                mla_k_rope_norm[i], mla_w_out[i], with_ctx)
    return xl.reshape(BATCH, SEQ, D_MODEL)
```

```python
import functools
import math

import numpy as np
import jax
import jax.numpy as jnp
from jax import lax
from jax.experimental import pallas as pl
from jax.experimental.pallas import tpu as pltpu

D_MODEL = 2048
BATCH = 4
SEQ = 4096
DEPTH = 2
GRID_W = 64
GRID_H = SEQ // GRID_W
CTX_LEN = 256
HEAD_DIM = 128
NA_HEADS = 8
NA_WIN_H = 8
NA_WIN_W = 16
DIFF_HEADS = 4
NA_WIDTH = 1024
DIFF_WIDTH = 1024
EVEN_PROJ = 6144
MLA_HEADS = 16
MLA_Q_RANK = 512
MLA_KV_RANK = 512
MLA_NOPE = 128
MLA_ROPE = 64
MLA_V = 128
MLA_DOWN = 1088
D_FF = 5504
ROPE_BASE = 10000.0
EPS = 1e-6

LANES = 128
SUBLANES = 8
MOD_ROWS = 8
CTX_MOD_ROW = BATCH
D_FF_PAD = 5632
MLA_DOWN_PAD = 1152
VMEM_LIMIT = 56 * 1024 * 1024

TM = 512
TN = 1024
TN_FF = 512
TM_FF = 1024
HM_FF = 512
SUB_N = 512
PAIR = 2 * LANES
NA_GROUP_ROWS = 4
NA_KEY_ROWS = 12
NA_Q = NA_GROUP_ROWS * GRID_W
NA_K = NA_KEY_ROWS * GRID_W
TQ_DIFF = 512
TQ_MLA = 1024
TQ_SUB = 256
KEY_CHUNK_ATTN = 256
KEY_CHUNK_DIFF = 512
NEG = -1e30
LOG2E = math.log2(math.e)

_NT = (((1,), (1,)), ((), ()))


def _params(*sem):
    return pltpu.CompilerParams(dimension_semantics=sem, vmem_limit_bytes=VMEM_LIMIT)


def _bf(x):
    return x.astype(jnp.bfloat16)


def _dot(a, b):
    return jnp.dot(a, b, preferred_element_type=jnp.float32)


def _dot_nt(a, b):
    return lax.dot_general(a, b, _NT, preferred_element_type=jnp.float32)


def _modulate(x, gain, shift, scale):
    ms = jnp.mean(x * x, axis=-1, keepdims=True)
    y = x * lax.rsqrt(ms + EPS) * gain
    return y * (1.0 + scale) + shift


def _ada_kernel(c_ref, w_ref, b_ref, o_ref):
    c = c_ref[...]
    s = c * (1.0 / (1.0 + jnp.exp(-c)))
    o_ref[0] = _dot(_bf(s), _bf(w_ref[0])) + b_ref[0]


def _ada_call(cond, ada_w, ada_b):
    tn = 512
    n = ada_w.shape[-1]
    return pl.pallas_call(
        _ada_kernel,
        grid=(DEPTH, n // tn),
        in_specs=[
            pl.BlockSpec((MOD_ROWS, D_MODEL), lambda l, j: (0, 0)),
            pl.BlockSpec((1, D_MODEL, tn), lambda l, j: (l, 0, j)),
            pl.BlockSpec((1, 1, tn), lambda l, j: (l, 0, j)),
        ],
        out_specs=pl.BlockSpec((1, MOD_ROWS, tn), lambda l, j: (l, 0, j)),
        out_shape=jax.ShapeDtypeStruct((DEPTH, MOD_ROWS, n), jnp.float32),
        compiler_params=_params("parallel", "parallel"),
        name="ada_mod",
    )(cond, ada_w, ada_b.reshape(DEPTH, 1, n))


def _pair_epilogue(acc, spec, gains_ref, ones_ref, rope_refs, rope_half):
    norm, gain_row, nvalid, do_rope = spec
    y = acc
    if norm:
        ss = _dot(_bf(acc * acc), ones_ref[...])
        y = acc * lax.rsqrt(ss * (1.0 / nvalid) + EPS) * gains_ref[gain_row:gain_row + 1, :]
    if do_rope and rope_refs is not None:
        cos_ref, sa_ref, sb_ref = rope_refs
        y = (y * cos_ref[...]
             + pltpu.roll(y, PAIR - rope_half, axis=1) * sa_ref[...]
             + pltpu.roll(y, rope_half, axis=1) * sb_ref[...])
    return y


def _proj_kernel(*refs, mod, segs, seg_tiles, rope, rope_half):
    it = iter(refs)
    x_ref = next(it)
    if mod:
        g_ref, sh_ref, sc_ref = next(it), next(it), next(it)
    w_ref = next(it)
    gains_ref, ones_ref = next(it), next(it)
    rope_refs = (next(it), next(it), next(it)) if rope else None
    o_ref = next(it)
    h_ref = next(it) if mod else None
    j = pl.program_id(1)
    tn = o_ref.shape[1]

    if mod:
        @pl.when(j == 0)
        def _():
            h_ref[...] = _bf(_modulate(x_ref[...], g_ref[...], sh_ref[0], sc_ref[0]))
        a = h_ref[...]
    else:
        a = x_ref[...]
    seg = j // seg_tiles
    for s, spec in enumerate(segs):
        @pl.when(seg == s)
        def _(spec=spec):
            for c in range(0, tn, SUB_N):
                acc = _dot(a, w_ref[:, c:c + SUB_N])
                for p in range(0, SUB_N, PAIR):
                    y = _pair_epilogue(acc[:, p:p + PAIR], spec, gains_ref, ones_ref, rope_refs,
                                       rope_half)
                    o_ref[:, c + p:c + p + PAIR] = _bf(y)


def _proj_call(x, w, gains, segs, seg_width, *, mod=None, rope=None, mod_row=None,
               tm=TM, tn=TN, name="proj"):
    m, k = x.shape
    n = w.shape[1]
    tm = min(tm, m)
    tn = min(tn, seg_width)
    assert m % tm == 0 and seg_width % tn == 0 and n % seg_width == 0
    seg_tiles = seg_width // tn
    bpb = SEQ // tm

    in_specs = [pl.BlockSpec((tm, k), lambda i, j: (i, 0))]
    args = [x]
    if mod is not None:
        gain, modarr, sh_c, sc_c = mod
        in_specs += [
            pl.BlockSpec((1, k), lambda i, j: (0, 0)),
            pl.BlockSpec((1, 1, k), lambda i, j: (mod_row(i), 0, sh_c)),
            pl.BlockSpec((1, 1, k), lambda i, j: (mod_row(i), 0, sc_c)),
        ]
        args += [gain, modarr, modarr]
    in_specs.append(pl.BlockSpec((k, tn), lambda i, j: (0, j)))
    args.append(w)
    ones = _pair_blockdiag(np.ones((LANES, LANES), np.float32))
    in_specs += [pl.BlockSpec(gains.shape, lambda i, j: (0, 0)),
                 pl.BlockSpec(ones.shape, lambda i, j: (0, 0))]
    args += [gains, ones]
    if rope is not None:
        tables, rope_half = rope
        for t in tables:
            in_specs.append(pl.BlockSpec((tm, PAIR), lambda i, j: (i % bpb, 0)))
            args.append(t)
    else:
        rope_half = 0
    kern = functools.partial(_proj_kernel, mod=mod is not None, segs=tuple(segs),
                             seg_tiles=seg_tiles, rope=rope is not None, rope_half=rope_half)
    return pl.pallas_call(
        kern,
        grid=(m // tm, n // tn),
        in_specs=in_specs,
        out_specs=pl.BlockSpec((tm, tn), lambda i, j: (i, j)),
        out_shape=jax.ShapeDtypeStruct((m, n), jnp.bfloat16),
        scratch_shapes=[pltpu.VMEM((tm, k), jnp.bfloat16)] if mod is not None else [],
        compiler_params=_params("parallel", "arbitrary"),
        name=name,
    )(*args)


def _outproj_kernel(*refs, n_a):
    a_refs = refs[:n_a]
    w_ref, x_ref, g_ref, o_ref = refs[n_a:]
    n = o_ref.shape[1]
    a_vals = [a_ref[...] for a_ref in a_refs]
    for c in range(0, n, SUB_N):
        acc, row = None, 0
        for a in a_vals:
            part = _dot(a, w_ref[row:row + a.shape[1], c:c + SUB_N])
            acc = part if acc is None else acc + part
            row += a.shape[1]
        o_ref[:, c:c + SUB_N] = x_ref[:, c:c + SUB_N] + g_ref[0][:, c:c + SUB_N] * acc


def _outproj_call(a_list, w, x, modarr, gate_chunk, mod_row, *, tm=TM, name="outproj"):
    m, n = x.shape
    tm = min(tm, m)
    in_specs, args = [], []
    for a in a_list:
        in_specs.append(pl.BlockSpec((tm, a.shape[1]), lambda i: (i, 0)))
        args.append(a)
    in_specs += [
        pl.BlockSpec(w.shape, lambda i: (0, 0)),
        pl.BlockSpec((tm, n), lambda i: (i, 0)),
        pl.BlockSpec((1, 1, n), lambda i: (mod_row(i), 0, gate_chunk)),
    ]
    args += [w, x, modarr]
    return pl.pallas_call(
        functools.partial(_outproj_kernel, n_a=len(a_list)),
        grid=(m // tm,),
        in_specs=in_specs,
        out_specs=pl.BlockSpec((tm, n), lambda i: (i, 0)),
        out_shape=jax.ShapeDtypeStruct((m, n), jnp.float32),
        compiler_params=_params("parallel"),
        name=name,
    )(*args)


def _ffn_kernel(x_ref, xp_ref, xn_ref, g_ref, sh_ref, sc_ref, gt_ref, wa_ref, wb_ref,
                ca_ref, cb_ref, wo_ref, o_ref, h_ref, *, tm, hm, seq_len):
    i = pl.program_id(0)
    j = pl.program_id(1)
    nj = pl.num_programs(1)
    n_grp = tm // hm
    rows = hm + 2 * SUBLANES

    @pl.when(j == 0)
    def _():
        gain, shift, scale = g_ref[...], sh_ref[0], sc_ref[0]
        keep_prev = jnp.where((i * tm) % seq_len == 0, 0.0, 1.0)
        keep_next = jnp.where(((i + 1) * tm) % seq_len == 0, 0.0, 1.0)
        for r in range(n_grp):
            r0 = r * hm
            h_ref[r * rows:r * rows + hm, :] = _bf(
                _modulate(x_ref[r0:r0 + hm, :], gain, shift, scale))
            if r + 1 < n_grp:
                nxt = _modulate(x_ref[r0 + hm:r0 + hm + SUBLANES, :], gain, shift, scale)
            else:
                nxt = keep_next * _modulate(xn_ref[...], gain, shift, scale)
            if r > 0:
                prv = _modulate(x_ref[r0 - SUBLANES:r0, :], gain, shift, scale)
            else:
                prv = keep_prev * _modulate(xp_ref[...], gain, shift, scale)
            h_ref[r * rows + hm:(r + 1) * rows, :] = _bf(jnp.concatenate([nxt, prv], axis=0))
        o_ref[...] = jnp.zeros_like(o_ref)

    ca, cb = ca_ref[...], cb_ref[...]
    ups = []
    for r in range(n_grp):
        h = h_ref[r * rows:(r + 1) * rows, :]
        ups.append((_dot(h, wa_ref[...]), _dot(h, wb_ref[...])))

    def conv(u, cw):
        prev = pltpu.roll(u, 1, axis=0)[0:hm]
        nxt = pltpu.roll(u, rows - 1, axis=0)[0:hm]
        return prev * cw[0:1] + u[0:hm] * cw[1:2] + nxt * cw[2:3]

    for r, (ua, ub) in enumerate(ups):
        a = conv(ua, ca)
        b = conv(ub, cb)
        act = a * (1.0 / (1.0 + jnp.exp(-a))) * b
        o_ref[r * hm:(r + 1) * hm, :] += _dot(_bf(act), wo_ref[...])

    @pl.when(j == nj - 1)
    def _():
        o_ref[...] = x_ref[...] + gt_ref[0] * o_ref[...]


def _ffn_call(x, gain, modarr, w_in, conv_w, w_out, mod_row, seq_len, *, tm=TM_FF, tn=TN_FF,
              name="conv_ffn"):
    m, d = x.shape
    tm = min(tm, seq_len)
    hm = min(HM_FF, tm)
    assert seq_len % tm == 0 and m % tm == 0 and tm % hm == 0
    nj = D_FF_PAD // tn
    hb = tm // SUBLANES
    last_hb = m // SUBLANES - 1
    kern = functools.partial(_ffn_kernel, tm=tm, hm=hm, seq_len=seq_len)
    return pl.pallas_call(
        kern,
        grid=(m // tm, nj),
        in_specs=[
            pl.BlockSpec((tm, d), lambda i, j: (i, 0)),
            pl.BlockSpec((SUBLANES, d), lambda i, j: (jnp.maximum(i * hb - 1, 0), 0)),
            pl.BlockSpec((SUBLANES, d), lambda i, j: (jnp.minimum((i + 1) * hb, last_hb), 0)),
            pl.BlockSpec((1, d), lambda i, j: (0, 0)),
            pl.BlockSpec((1, 1, d), lambda i, j: (mod_row(i), 0, 3)),
            pl.BlockSpec((1, 1, d), lambda i, j: (mod_row(i), 0, 4)),
            pl.BlockSpec((1, 1, d), lambda i, j: (mod_row(i), 0, 5)),
            pl.BlockSpec((d, tn), lambda i, j: (0, j)),
            pl.BlockSpec((d, tn), lambda i, j: (0, nj + j)),
            pl.BlockSpec((3, tn), lambda i, j: (0, j)),
            pl.BlockSpec((3, tn), lambda i, j: (0, nj + j)),
            pl.BlockSpec((tn, d), lambda i, j: (j, 0)),
        ],
        out_specs=pl.BlockSpec((tm, d), lambda i, j: (i, 0)),
        out_shape=jax.ShapeDtypeStruct((m, d), jnp.float32),
        scratch_shapes=[
            pltpu.VMEM(((tm // hm) * (hm + 2 * SUBLANES), d), jnp.bfloat16),
        ],
        compiler_params=_params("parallel", "arbitrary"),
        name=name,
    )(x, x, x, gain, modarr, modarr, modarr, w_in, w_in, conv_w, conv_w, w_out)


def _na_group_geometry(g):
    r0 = g * NA_GROUP_ROWS
    return r0, min(max(r0 - NA_WIN_H // 2, 0), GRID_H - NA_KEY_ROWS)


def _na_build_bias(rpb_ref, h, tc_scr, bias_scr):
    n_dr = 2 * NA_WIN_H - 1
    n_dc = 2 * NA_WIN_W - 1
    qc = lax.broadcasted_iota(jnp.int32, (GRID_W, GRID_W), 0)
    kc = lax.broadcasted_iota(jnp.int32, (GRID_W, GRID_W), 1)
    col0 = jnp.clip(qc - NA_WIN_W // 2, 0, GRID_W - NA_WIN_W)
    col_valid = (kc >= col0) & (kc < col0 + NA_WIN_W)
    delta = kc - qc + (NA_WIN_W - 1)
    base = h * (n_dr * n_dc)
    for dr in range(n_dr):
        acc = jnp.zeros((GRID_W, GRID_W), jnp.float32)
        for e in range(n_dc):
            acc = jnp.where(delta == e, rpb_ref[base + dr * n_dc + e] * LOG2E, acc)
        tc_scr[dr] = jnp.where(col_valid, acc, NEG)
    tc_scr[n_dr] = jnp.full((GRID_W, GRID_W), NEG, jnp.float32)
    n_groups = GRID_H // NA_GROUP_ROWS
    for t, g in enumerate((0, 1, n_groups - 1)):
        r0, ks = _na_group_geometry(g)
        for qr in range(NA_GROUP_ROWS):
            r = r0 + qr
            row0 = min(max(r - NA_WIN_H // 2, 0), GRID_H - NA_WIN_H)
            pieces = []
            for kr in range(NA_KEY_ROWS):
                kra = ks + kr
                inside = row0 <= kra < row0 + NA_WIN_H
                pieces.append(tc_scr[kra - r + (NA_WIN_H - 1)] if inside else tc_scr[n_dr])
            bias_scr[t, qr * GRID_W:(qr + 1) * GRID_W, :] = jnp.concatenate(pieces, axis=1)


def _na_kernel(rpb_ref, q_ref, k_ref, v_ref, kc_ref, vc_ref, o_ref, tc_scr, bias_scr):
    @pl.when(pl.program_id(1) == 0)
    def _():
        _na_build_bias(rpb_ref, pl.program_id(0), tc_scr, bias_scr)

    kc = kc_ref[...]
    vc = vc_ref[...]
    n_groups = GRID_H // NA_GROUP_ROWS

    def body(g, carry):
        r0 = g * NA_GROUP_ROWS
        ks = jnp.clip(r0 - NA_WIN_H // 2, 0, GRID_H - NA_KEY_ROWS)
        q0 = pl.multiple_of(g * NA_Q, NA_Q)
        k0 = pl.multiple_of(ks * GRID_W, GRID_W)
        tb = jnp.where(g == 0, 0, jnp.where(g == n_groups - 1, 2, 1))
        q = q_ref[pl.ds(q0, NA_Q), :]
        kw = k_ref[pl.ds(k0, NA_K), :]
        vw = v_ref[pl.ds(k0, NA_K), :]
        sw = _dot_nt(q, kw) + bias_scr[tb]
        sc = _dot_nt(q, kc)
        mx = jnp.maximum(jnp.max(sw, axis=-1, keepdims=True), jnp.max(sc, axis=-1, keepdims=True))
        pw = jnp.exp2(sw - mx)
        pc = jnp.exp2(sc - mx)
        den = jnp.sum(pw, axis=-1, keepdims=True) + jnp.sum(pc, axis=-1, keepdims=True)
        o = _dot(_bf(pw), vw) + _dot(_bf(pc), vc)
        o_ref[pl.ds(q0, NA_Q), :] = _bf(o / den)
        return carry

    lax.fori_loop(0, n_groups, body, 0)


def _na_call(qkv, qkv_c, rpb):
    n_dr = 2 * NA_WIN_H - 1
    return pl.pallas_call(
        _na_kernel,
        grid=(NA_HEADS, BATCH),
        in_specs=[
            pl.BlockSpec(memory_space=pltpu.SMEM),
            pl.BlockSpec((SEQ, LANES), lambda h, b: (b, h)),
            pl.BlockSpec((SEQ, LANES), lambda h, b: (b, NA_HEADS + h)),
            pl.BlockSpec((SEQ, LANES), lambda h, b: (b, 2 * NA_HEADS + h)),
            pl.BlockSpec((CTX_LEN, LANES), lambda h, b: (b, NA_HEADS + h)),
            pl.BlockSpec((CTX_LEN, LANES), lambda h, b: (b, 2 * NA_HEADS + h)),
        ],
        out_specs=pl.BlockSpec((SEQ, LANES), lambda h, b: (b, h)),
        out_shape=jax.ShapeDtypeStruct((BATCH * SEQ, NA_WIDTH), jnp.bfloat16),
        scratch_shapes=[
            pltpu.VMEM((n_dr + 1, GRID_W, GRID_W), jnp.float32),
            pltpu.VMEM((3, NA_Q, NA_K), jnp.float32),
        ],
        compiler_params=_params("parallel", "arbitrary"),
        name="na_attn",
    )(rpb.reshape(-1), qkv, qkv, qkv, qkv_c, qkv_c)


def _key_chunks(src_rows, chunk):
    chunks, r = [], 0
    for rows in src_rows:
        size = min(chunk, rows)
        assert rows % size == 0
        chunks += [(r + c, size) for c in range(0, rows, size)]
        r += rows
    return chunks


def _pipelined_attention(streams, vt_scr, chunks):
    dv = vt_scr.shape[0]
    sub = streams[0][2].shape[1]
    outs, m_prev = [], None
    for u in range(len(streams) + 1):
        m = jnp.full((1, sub), NEG, jnp.float32)
        l = jnp.zeros((1, sub), jnp.float32)
        acc = jnp.zeros((dv, sub), jnp.float32)
        for c0, cs in chunks:
            if u < len(streams):
                q_t, k_scr, s_ref = streams[u]
                s = _dot(k_scr[c0:c0 + cs, :], q_t)
                s_ref[c0:c0 + cs, :] = s
                m = jnp.maximum(m, jnp.max(s, axis=0, keepdims=True))
            if u > 0:
                p = jnp.exp2(streams[u - 1][2][c0:c0 + cs, :] - m_prev)
                l = l + jnp.sum(p, axis=0, keepdims=True)
                acc = acc + _dot(vt_scr[:, c0:c0 + cs], _bf(p))
        if u > 0:
            outs.append(acc / l)
        m_prev = m
    return outs


def _transpose_bf16(x):
    return _bf(x.astype(jnp.float32).T)


def _attn_kernel(*refs, n_qparts, n_src, n_kparts, src_rows):
    it = iter(refs)
    q_refs = [next(it) for _ in range(n_qparts)]
    srcs = []
    for _ in range(n_src):
        k_refs = [next(it) for _ in range(n_kparts)]
        srcs.append((k_refs, next(it)))
    o_ref = next(it)
    k_scr, vt_scr = next(it), next(it)
    s_scrs = list(it)

    @pl.when(pl.program_id(2) == 0)
    def _():
        r = 0
        for (k_refs, v_ref), rows in zip(srcs, src_rows):
            for p, k_ref in enumerate(k_refs):
                k_scr[r:r + rows, p * LANES:(p + 1) * LANES] = k_ref[...]
            vt_scr[:, r:r + rows] = _transpose_bf16(v_ref[...])
            r += rows

    q = jnp.concatenate([qr[...] for qr in q_refs], axis=-1) if n_qparts > 1 else q_refs[0][...]
    sub = s_scrs[0].shape[1]
    streams = [(_transpose_bf16(q[u * sub:(u + 1) * sub]), k_scr, s_scr)
               for u, s_scr in enumerate(s_scrs)]
    outs = _pipelined_attention(streams, vt_scr, _key_chunks(src_rows, KEY_CHUNK_ATTN))
    for u, o_t in enumerate(outs):
        o_ref[u * sub:(u + 1) * sub, :] = _bf(o_t.T)


def _attn_call(q_parts, sources, n_heads, lq, tq, dv, name):
    tq = min(tq, lq)
    sub = min(TQ_SUB, tq)
    nq = lq // tq
    in_specs, args = [], []
    for arr, cf in q_parts:
        in_specs.append(pl.BlockSpec((tq, LANES), lambda b, h, i, cf=cf: (b * nq + i, cf(h))))
        args.append(arr)
    src_rows = []
    n_kparts = len(sources[0][1])
    for rows, k_parts, (v_arr, vcf) in sources:
        src_rows.append(rows)
        for arr, cf in k_parts:
            in_specs.append(pl.BlockSpec((rows, LANES), lambda b, h, i, cf=cf: (b, cf(h))))
            args.append(arr)
        in_specs.append(pl.BlockSpec((rows, dv), lambda b, h, i, cf=vcf: (b, cf(h))))
        args.append(v_arr)
    nk = sum(src_rows)
    kern = functools.partial(_attn_kernel, n_qparts=len(q_parts), n_src=len(sources),
                             n_kparts=n_kparts, src_rows=tuple(src_rows))
    return pl.pallas_call(
        kern,
        grid=(BATCH, n_heads, nq),
        in_specs=in_specs,
        out_specs=pl.BlockSpec((tq, dv), lambda b, h, i: (b * nq + i, h)),
        out_shape=jax.ShapeDtypeStruct((BATCH * lq, n_heads * dv), jnp.bfloat16),
        scratch_shapes=[
            pltpu.VMEM((nk, n_kparts * LANES), jnp.bfloat16),
            pltpu.VMEM((dv, nk), jnp.bfloat16),
        ] + [pltpu.VMEM((nk, sub), jnp.float32)] * (tq // sub),
        compiler_params=_params("parallel", "parallel", "arbitrary"),
        name=name,
    )(*args)


def _diff_kernel(*refs, n_src, src_rows, lam_init):
    it = iter(refs)
    q1_ref, q2_ref = next(it), next(it)
    srcs = [(next(it), next(it), next(it)) for _ in range(n_src)]
    lam_ref, sub_ref = next(it), next(it)
    o_ref = next(it)
    k1_scr, k2_scr, vt_scr, s1_scr, s2_scr = [next(it) for _ in range(5)]

    @pl.when(pl.program_id(2) == 0)
    def _():
        r = 0
        for (k1_ref, k2_ref, v_ref), rows in zip(srcs, src_rows):
            k1_scr[r:r + rows, :] = k1_ref[...]
            k2_scr[r:r + rows, :] = k2_ref[...]
            vt_scr[:, r:r + rows] = _transpose_bf16(v_ref[...])
            r += rows

    lf = lam_ref[...]
    lam = (jnp.exp(jnp.sum(lf[0:1] * lf[1:2], axis=-1, keepdims=True))
           - jnp.exp(jnp.sum(lf[2:3] * lf[3:4], axis=-1, keepdims=True)) + lam_init)
    chunks = _key_chunks(src_rows, KEY_CHUNK_DIFF)
    n_sub, _, sub = s1_scr.shape
    streams = []
    for u in range(n_sub):
        rows = slice(u * sub, (u + 1) * sub)
        streams.append((_transpose_bf16(q1_ref[rows, :]), k1_scr, s1_scr.at[u]))
        streams.append((_transpose_bf16(q2_ref[rows, :]), k2_scr, s2_scr.at[u]))
    outs = _pipelined_attention(streams, vt_scr, chunks)
    o_t = jnp.concatenate([outs[2 * u] - lam * outs[2 * u + 1] for u in range(n_sub)],
                          axis=1)
    o = o_t.T
    ms = jnp.mean(o * o, axis=-1, keepdims=True)
    o_ref[...] = _bf(o * lax.rsqrt(ms + EPS) * sub_ref[...] * (1.0 - lam_init))


def _diff_call(qkv_q, lq, sources, diff_lambda, subln, lam_init, tq, name):
    tq = min(tq, lq)
    sub = min(TQ_SUB, tq)
    nq = lq // tq
    qb0 = 3 * NA_WIDTH // LANES
    kb0 = (3 * NA_WIDTH + DIFF_WIDTH) // LANES
    vb0 = (3 * NA_WIDTH + 2 * DIFF_WIDTH) // (2 * LANES)
    in_specs = [
        pl.BlockSpec((tq, LANES), lambda b, h, i: (b * nq + i, qb0 + 2 * h)),
        pl.BlockSpec((tq, LANES), lambda b, h, i: (b * nq + i, qb0 + 2 * h + 1)),
    ]
    args = [qkv_q, qkv_q]
    src_rows = []
    for rows, arr in sources:
        src_rows.append(rows)
        in_specs += [
            pl.BlockSpec((rows, LANES), lambda b, h, i: (b, kb0 + 2 * h)),
            pl.BlockSpec((rows, LANES), lambda b, h, i: (b, kb0 + 2 * h + 1)),
            pl.BlockSpec((rows, 2 * LANES), lambda b, h, i: (b, vb0 + h)),
        ]
        args += [arr, arr, arr]
    in_specs += [
        pl.BlockSpec((4, LANES), lambda b, h, i: (0, 0)),
        pl.BlockSpec((1, 2 * LANES), lambda b, h, i: (0, 0)),
    ]
    args += [diff_lambda, subln]
    nk = sum(src_rows)
    kern = functools.partial(_diff_kernel, n_src=len(sources), src_rows=tuple(src_rows),
                             lam_init=lam_init)
    return pl.pallas_call(
        kern,
        grid=(BATCH, DIFF_HEADS, nq),
        in_specs=in_specs,
        out_specs=pl.BlockSpec((tq, 2 * LANES), lambda b, h, i: (b * nq + i, h)),
        out_shape=jax.ShapeDtypeStruct((BATCH * lq, DIFF_WIDTH), jnp.bfloat16),
        scratch_shapes=[
            pltpu.VMEM((nk, LANES), jnp.bfloat16),
            pltpu.VMEM((nk, LANES), jnp.bfloat16),
            pltpu.VMEM((2 * LANES, nk), jnp.bfloat16),
            pltpu.VMEM((tq // sub, nk, sub), jnp.float32),
            pltpu.VMEM((tq // sub, nk, sub), jnp.float32),
        ],
        compiler_params=_params("parallel", "parallel", "arbitrary"),
        name=name,
    )(*args)


def _rope_tables(d):
    h = d // 2
    half = h // 2
    t = jnp.arange(SEQ, dtype=jnp.int32)
    freqs = ROPE_BASE ** (-jnp.arange(half, dtype=jnp.float32) / half)

    def cs(pos):
        ang = pos.astype(jnp.float32)[:, None] * freqs[None, :]
        return jnp.cos(ang), jnp.sin(ang)

    cr, sr = cs(t // GRID_W)
    cc, sc = cs(t % GRID_W)
    cos = jnp.concatenate([cr, cr, cc, cc], -1)
    sin = jnp.concatenate([sr, sr, sc, sc], -1)
    first = (jnp.arange(d) % h) < half
    sin_a = jnp.where(first[None, :], -sin, 0.0)
    sin_b = jnp.where(first[None, :], 0.0, sin)
    pad = LANES - d
    if pad:
        cos = jnp.pad(cos, ((0, 0), (0, pad)), constant_values=1.0)
        sin_a = jnp.pad(sin_a, ((0, 0), (0, pad)))
        sin_b = jnp.pad(sin_b, ((0, 0), (0, pad)))
    pair_tables = tuple(jnp.concatenate([t, t], -1) for t in (cos, sin_a, sin_b))
    return (cos, sin_a, sin_b), half, (pair_tables, half)


def _pair_blockdiag(block):
    z = np.zeros_like(block)
    return jnp.asarray(np.block([[block, z], [z, block]]), dtype=jnp.bfloat16)


def _pad_lanes(v, fill=0.0):
    return jnp.pad(v, (0, LANES - v.shape[0]), constant_values=fill)


def _pair(v):
    return jnp.concatenate([v, v])


def _lat_row(tm):
    bpb = SEQ // tm
    return lambda i: i // bpb


def _ctx_row(i):
    return CTX_MOD_ROW


def _even_layer(x, xc, modarr, norm_mix, norm_ffn, ffn_w, w_in, w_out, qn_a, kn_a, rpb,
                qn_b, kn_b, diff_lambda, subln, lam_init, with_ctx):
    scale = HEAD_DIM ** -0.5 * LOG2E
    gains = jnp.stack([_pair(g) for g in (qn_a * scale, kn_a, qn_b * scale, kn_b)]
                      + [jnp.ones((PAIR,), jnp.float32)] * 4)
    _, _, rope = _rope_tables(HEAD_DIM)
    segs = [(True, 0, HEAD_DIM, False), (True, 1, HEAD_DIM, False), (False, 0, HEAD_DIM, False),
            (True, 2, HEAD_DIM, True), (True, 3, HEAD_DIM, True), (False, 0, HEAD_DIM, False)]
    w_in = _bf(w_in)
    w_out = _bf(w_out)
    gain_mix = norm_mix[None, :]
    lat_row = _lat_row(TM)

    qkv = _proj_call(x, w_in, gains, segs, NA_WIDTH, mod=(gain_mix, modarr, 0, 1), rope=rope,
                     mod_row=lat_row, name="even_proj")
    qkv_c = _proj_call(xc, w_in, gains, segs, NA_WIDTH, mod=(gain_mix, modarr, 0, 1), rope=None,
                       mod_row=_ctx_row, name="even_proj_ctx")

    oa = _na_call(qkv, qkv_c, rpb)
    ob = _diff_call(qkv, SEQ, [(SEQ, qkv), (CTX_LEN, qkv_c)], diff_lambda, subln[None, :],
                    lam_init, TQ_DIFF, "diff_attn")
    x = _outproj_call([oa, ob], w_out, x, modarr, 2, lat_row, name="even_out")
    x = _ffn_call(x, norm_ffn[None, :], modarr, *ffn_w, _lat_row(TM_FF), SEQ)
    if with_ctx:
        oa_c = _attn_call(
            [(qkv_c, lambda h: h)],
            [(CTX_LEN, [(qkv_c, lambda h: NA_HEADS + h)], (qkv_c, lambda h: 2 * NA_HEADS + h))],
            NA_HEADS, CTX_LEN, CTX_LEN, HEAD_DIM, "ctx_attn")
        ob_c = _diff_call(qkv_c, CTX_LEN, [(CTX_LEN, qkv_c)], diff_lambda, subln[None, :],
                          lam_init, CTX_LEN, "diff_attn_ctx")
        xc = _outproj_call([oa_c, ob_c], w_out, xc, modarr, 2, _ctx_row, name="even_out_ctx")
        xc = _ffn_call(xc, norm_ffn[None, :], modarr, *ffn_w, _ctx_row, CTX_LEN,
                       name="conv_ffn_ctx")
    return x, xc


def _odd_layer(x, xc, modarr, norm_mix, norm_ffn, ffn_w, w_down, q_a_norm, kv_a_norm, w_uq,
               w_ukv, qn_nope, qn_rope, kn_nope, kn_rope, w_out, with_ctx):
    scale = (MLA_NOPE + MLA_ROPE) ** -0.5 * LOG2E
    rope, half, rope_pair = _rope_tables(MLA_ROPE)
    gain_mix = norm_mix[None, :]
    lat_row = _lat_row(TM)

    w_down_p = _bf(jnp.pad(w_down, ((0, 0), (0, MLA_DOWN_PAD - MLA_DOWN))))
    zeros_r = jnp.zeros((MLA_Q_RANK,), jnp.float32)
    kr_gain = jnp.concatenate([_pad_lanes(kn_rope), zeros_r[:MLA_Q_RANK - LANES]])
    gains_d = jnp.stack([q_a_norm, kv_a_norm, kr_gain] + [zeros_r] * 5)

    def down(xx, rp, row, name):
        return _down_call(xx, w_down_p, gains_d, (gain_mix, modarr, 0, 1), rp, half, row, name)

    qa, kva, kr = down(x, rope, lat_row, "mla_down")
    _, kva_c, kr_c = down(xc, None, _ctx_row, "mla_down_ctx")

    wq = w_uq.reshape(MLA_Q_RANK, MLA_HEADS, MLA_NOPE + MLA_ROPE)
    wq_rope = jnp.pad(wq[:, :, MLA_NOPE:], ((0, 0), (0, 0), (0, LANES - MLA_ROPE)))
    wq_p = _bf(jnp.concatenate([wq[:, :, :MLA_NOPE].reshape(MLA_Q_RANK, -1),
                                wq_rope.reshape(MLA_Q_RANK, -1)], -1))
    gains_q = jnp.stack([_pair(qn_nope * scale), _pair(_pad_lanes(qn_rope) * scale)]
                        + [jnp.zeros((PAIR,), jnp.float32)] * 6)
    segs_q = [(True, 0, MLA_NOPE, False), (True, 1, MLA_ROPE, True)]
    q = _proj_call(qa, wq_p, gains_q, segs_q, MLA_HEADS * LANES, rope=rope_pair, name="mla_q")

    wkv = w_ukv.reshape(MLA_KV_RANK, MLA_HEADS, MLA_NOPE + MLA_V)
    wkv_p = _bf(jnp.concatenate([wkv[:, :, :MLA_NOPE].reshape(MLA_KV_RANK, -1),
                                 wkv[:, :, MLA_NOPE:].reshape(MLA_KV_RANK, -1)], -1))
    gains_kv = jnp.stack([_pair(kn_nope)] + [jnp.zeros((PAIR,), jnp.float32)] * 7)
    segs_kv = [(True, 0, MLA_NOPE, False), (False, 0, MLA_NOPE, False)]
    kv = _proj_call(kva, wkv_p, gains_kv, segs_kv, MLA_HEADS * LANES, name="mla_kv")
    kv_c = _proj_call(kva_c, wkv_p, gains_kv, segs_kv, MLA_HEADS * LANES, name="mla_kv_ctx")

    def src(rows, kv_arr, kr_arr):
        return (rows, [(kv_arr, lambda h: h), (kr_arr, lambda h: 0)],
                (kv_arr, lambda h: MLA_HEADS + h))

    o = _attn_call([(q, lambda h: h), (q, lambda h: MLA_HEADS + h)],
                   [src(SEQ, kv, kr), src(CTX_LEN, kv_c, kr_c)],
                   MLA_HEADS, SEQ, TQ_MLA, MLA_V, "mla_attn")
    x = _outproj_call([o], _bf(w_out), x, modarr, 2, lat_row, name="mla_out")
    x = _ffn_call(x, norm_ffn[None, :], modarr, *ffn_w, _lat_row(TM_FF), SEQ)
    assert not with_ctx
    return x, xc


def _down_kernel(x_ref, g_ref, sh_ref, sc_ref, w_ref, gains_ref, *rest, rope, rope_half):
    if rope:
        cos_ref, sa_ref, sb_ref = rest[:3]
        rest = rest[3:]
    qa_ref, kva_ref, kr_ref = rest
    h = _bf(_modulate(x_ref[...], g_ref[...], sh_ref[0], sc_ref[0]))
    acc = _dot(h, w_ref[...])

    def norm(xs, row, nvalid):
        ms = jnp.sum(xs * xs, axis=-1, keepdims=True) * (1.0 / nvalid)
        return xs * lax.rsqrt(ms + EPS) * gains_ref[row:row + 1, 0:xs.shape[1]]

    qa_ref[...] = _bf(norm(acc[:, 0:MLA_Q_RANK], 0, MLA_Q_RANK))
    kva_ref[...] = _bf(norm(acc[:, MLA_Q_RANK:MLA_Q_RANK + MLA_KV_RANK], 1, MLA_KV_RANK))
    kr = norm(acc[:, MLA_Q_RANK + MLA_KV_RANK:MLA_DOWN_PAD], 2, MLA_ROPE)
    if rope:
        kr = (kr * cos_ref[...]
              + pltpu.roll(kr, LANES - rope_half, axis=1) * sa_ref[...]
              + pltpu.roll(kr, rope_half, axis=1) * sb_ref[...])
    kr_ref[...] = _bf(kr)


def _down_call(x, w, gains, mod, rope, rope_half, mod_row, name, tm=TM):
    m, k = x.shape
    tm = min(tm, m)
    bpb = SEQ // tm
    gain, modarr, sh_c, sc_c = mod
    in_specs = [
        pl.BlockSpec((tm, k), lambda i: (i, 0)),
        pl.BlockSpec((1, k), lambda i: (0, 0)),
        pl.BlockSpec((1, 1, k), lambda i: (mod_row(i), 0, sh_c)),
        pl.BlockSpec((1, 1, k), lambda i: (mod_row(i), 0, sc_c)),
        pl.BlockSpec(w.shape, lambda i: (0, 0)),
        pl.BlockSpec(gains.shape, lambda i: (0, 0)),
    ]
    args = [x, gain, modarr, modarr, w, gains]
    if rope is not None:
        for t in rope:
            in_specs.append(pl.BlockSpec((tm, LANES), lambda i: (i % bpb, 0)))
            args.append(t)
    return pl.pallas_call(
        functools.partial(_down_kernel, rope=rope is not None, rope_half=rope_half),
        grid=(m // tm,),
        in_specs=in_specs,
        out_specs=[
            pl.BlockSpec((tm, MLA_Q_RANK), lambda i: (i, 0)),
            pl.BlockSpec((tm, MLA_KV_RANK), lambda i: (i, 0)),
            pl.BlockSpec((tm, LANES), lambda i: (i, 0)),
        ],
        out_shape=[
            jax.ShapeDtypeStruct((m, MLA_Q_RANK), jnp.bfloat16),
            jax.ShapeDtypeStruct((m, MLA_KV_RANK), jnp.bfloat16),
            jax.ShapeDtypeStruct((m, LANES), jnp.bfloat16),
        ],
        compiler_params=_params("parallel"),
        name=name,
    )(*args)


def _ffn_weights(w_in, conv_w, w_out):
    pad = D_FF_PAD - D_FF
    w16 = _bf(w_in)
    z = jnp.zeros((w_in.shape[0], pad), jnp.bfloat16)
    zc = jnp.zeros((conv_w.shape[0], pad), conv_w.dtype)
    return (jnp.concatenate([w16[:, :D_FF], z, w16[:, D_FF:], z], -1),
            jnp.concatenate([conv_w[:, :D_FF], zc, conv_w[:, D_FF:], zc], -1),
            jnp.pad(_bf(w_out), ((0, pad), (0, 0))))


def kernel(x, c, ctx, c_ctx, ada_w, ada_b, norm_mix, norm_ffn, ffn_w_in, ffn_conv, ffn_w_out, even_w_in, even_w_out, na_q_norm, na_k_norm, na_rpb, diff_q_norm, diff_k_norm, diff_lambda, diff_subln, mla_w_down, mla_q_a_norm, mla_kv_a_norm, mla_w_uq, mla_w_ukv, mla_q_nope_norm, mla_q_rope_norm, mla_k_nope_norm, mla_k_rope_norm, mla_w_out):
    cond = jnp.concatenate(
        [c, c_ctx[None, :], jnp.zeros((MOD_ROWS - BATCH - 1, D_MODEL), jnp.float32)], 0)
    mod = _ada_call(cond, ada_w, ada_b)
    xl = x.reshape(BATCH * SEQ, D_MODEL)
    xc = ctx.reshape(BATCH * CTX_LEN, D_MODEL)
    for l in range(DEPTH):
        with_ctx = l < DEPTH - 1
        modarr = mod[l].reshape(MOD_ROWS, 1, 6 * D_MODEL)
        ffn_w = _ffn_weights(ffn_w_in[l], ffn_conv[l], ffn_w_out[l])
        i = l // 2
        if l % 2 == 0:
            lam_init = 0.8 - 0.6 * math.exp(-0.3 * l)
            xl, xc = _even_layer(xl, xc, modarr, norm_mix[l], norm_ffn[l], ffn_w, even_w_in[i],
                                 even_w_out[i], na_q_norm[i], na_k_norm[i], na_rpb[i],
                                 diff_q_norm[i], diff_k_norm[i], diff_lambda[i], diff_subln[i],
                                 lam_init, with_ctx)
        else:
            xl, xc = _odd_layer(xl, xc, modarr, norm_mix[l], norm_ffn[l], ffn_w, mla_w_down[i],
                                mla_q_a_norm[i], mla_kv_a_norm[i], mla_w_uq[i], mla_w_ukv[i],
                                mla_q_nope_norm[i], mla_q_rope_norm[i], mla_k_nope_norm[i],
                                mla_k_rope_norm[i], mla_w_out[i], with_ctx)
    return xl.reshape(BATCH, SEQ, D_MODEL)
```

```python
import functools
import math

import numpy as np
import jax
import jax.numpy as jnp
from jax import lax
from jax.experimental import pallas as pl
from jax.experimental.pallas import tpu as pltpu

D_MODEL = 2048
BATCH = 4
SEQ = 4096
DEPTH = 2
GRID_W = 64
GRID_H = SEQ // GRID_W
CTX_LEN = 256
HEAD_DIM = 128
NA_HEADS = 8
NA_WIN_H = 8
NA_WIN_W = 16
DIFF_HEADS = 4
NA_WIDTH = 1024
DIFF_WIDTH = 1024
EVEN_PROJ = 6144
MLA_HEADS = 16
MLA_Q_RANK = 512
MLA_KV_RANK = 512
MLA_NOPE = 128
MLA_ROPE = 64
MLA_V = 128
MLA_DOWN = 1088
D_FF = 5504
ROPE_BASE = 10000.0
EPS = 1e-6

LANES = 128
SUBLANES = 8
MOD_ROWS = 8
CTX_MOD_ROW = BATCH
D_FF_PAD = 5632
MLA_DOWN_PAD = 1152
VMEM_LIMIT = 56 * 1024 * 1024

TM = 512
TN = 1024
TN_FF = 512
TM_FF = 1024
TM_MLA_PROJ = 256
HM_FF = 512
SUB_N = 512
PAIR = 2 * LANES
NA_GROUP_ROWS = 4
NA_KEY_ROWS = 12
NA_Q = NA_GROUP_ROWS * GRID_W
NA_K = NA_KEY_ROWS * GRID_W
TQ_DIFF = 512
TQ_MLA = 1024
TQ_SUB = 256
KEY_CHUNK_ATTN = 256
KEY_CHUNK_DIFF = 512
NEG = -1e30
LOG2E = math.log2(math.e)

_NT = (((1,), (1,)), ((), ()))


def _params(*sem):
    return pltpu.CompilerParams(dimension_semantics=sem, vmem_limit_bytes=VMEM_LIMIT)


def _bf(x):
    return x.astype(jnp.bfloat16)


def _dot(a, b):
    return jnp.dot(a, b, preferred_element_type=jnp.float32)


def _dot_nt(a, b):
    return lax.dot_general(a, b, _NT, preferred_element_type=jnp.float32)


def _modulate(x, gain, shift, scale):
    ms = jnp.mean(x * x, axis=-1, keepdims=True)
    y = x * lax.rsqrt(ms + EPS) * gain
    return y * (1.0 + scale) + shift


def _ada_kernel(c_ref, w_ref, b_ref, o_ref):
    c = c_ref[...]
    s = c * (1.0 / (1.0 + jnp.exp(-c)))
    o_ref[0] = _dot(_bf(s), _bf(w_ref[0])) + b_ref[0]


def _ada_call(cond, ada_w, ada_b):
    tn = 512
    n = ada_w.shape[-1]
    return pl.pallas_call(
        _ada_kernel,
        grid=(DEPTH, n // tn),
        in_specs=[
            pl.BlockSpec((MOD_ROWS, D_MODEL), lambda l, j: (0, 0)),
            pl.BlockSpec((1, D_MODEL, tn), lambda l, j: (l, 0, j)),
            pl.BlockSpec((1, 1, tn), lambda l, j: (l, 0, j)),
        ],
        out_specs=pl.BlockSpec((1, MOD_ROWS, tn), lambda l, j: (l, 0, j)),
        out_shape=jax.ShapeDtypeStruct((DEPTH, MOD_ROWS, n), jnp.float32),
        compiler_params=_params("parallel", "parallel"),
        name="ada_mod",
    )(cond, ada_w, ada_b.reshape(DEPTH, 1, n))


def _pair_epilogue(acc, spec, gains_ref, ones_ref, rope_refs, rope_half):
    norm, gain_row, nvalid, do_rope = spec
    y = acc
    if norm:
        ss = _dot(_bf(acc * acc), ones_ref[...])
        y = acc * lax.rsqrt(ss * (1.0 / nvalid) + EPS) * gains_ref[gain_row:gain_row + 1, :]
    if do_rope and rope_refs is not None:
        cos_ref, sa_ref, sb_ref = rope_refs
        y = (y * cos_ref[...]
             + pltpu.roll(y, PAIR - rope_half, axis=1) * sa_ref[...]
             + pltpu.roll(y, rope_half, axis=1) * sb_ref[...])
    return y


def _proj_kernel(*refs, mod, segs, seg_tiles, rope, rope_half):
    it = iter(refs)
    x_ref = next(it)
    if mod:
        g_ref, sh_ref, sc_ref = next(it), next(it), next(it)
    w_ref = next(it)
    gains_ref, ones_ref = next(it), next(it)
    rope_refs = (next(it), next(it), next(it)) if rope else None
    o_ref = next(it)
    h_ref = next(it) if mod else None
    j = pl.program_id(1)
    tn = o_ref.shape[1]

    if mod:
        @pl.when(j == 0)
        def _():
            h_ref[...] = _bf(_modulate(x_ref[...], g_ref[...], sh_ref[0], sc_ref[0]))
        a = h_ref[...]
    else:
        a = x_ref[...]
    seg = j // seg_tiles
    for s, spec in enumerate(segs):
        @pl.when(seg == s)
        def _(spec=spec):
            for c in range(0, tn, SUB_N):
                acc = _dot(a, w_ref[:, c:c + SUB_N])
                for p in range(0, SUB_N, PAIR):
                    y = _pair_epilogue(acc[:, p:p + PAIR], spec, gains_ref, ones_ref, rope_refs,
                                       rope_half)
                    o_ref[:, c + p:c + p + PAIR] = _bf(y)


def _proj_call(x, w, gains, segs, seg_width, *, mod=None, rope=None, mod_row=None,
               tm=TM, tn=TN, name="proj"):
    m, k = x.shape
    n = w.shape[1]
    tm = min(tm, m)
    tn = min(tn, seg_width)
    assert m % tm == 0 and seg_width % tn == 0 and n % seg_width == 0
    seg_tiles = seg_width // tn
    bpb = SEQ // tm

    in_specs = [pl.BlockSpec((tm, k), lambda i, j: (i, 0))]
    args = [x]
    if mod is not None:
        gain, modarr, sh_c, sc_c = mod
        in_specs += [
            pl.BlockSpec((1, k), lambda i, j: (0, 0)),
            pl.BlockSpec((1, 1, k), lambda i, j: (mod_row(i), 0, sh_c)),
            pl.BlockSpec((1, 1, k), lambda i, j: (mod_row(i), 0, sc_c)),
        ]
        args += [gain, modarr, modarr]
    in_specs.append(pl.BlockSpec((k, tn), lambda i, j: (0, j)))
    args.append(w)
    ones = _pair_blockdiag(np.ones((LANES, LANES), np.float32))
    in_specs += [pl.BlockSpec(gains.shape, lambda i, j: (0, 0)),
                 pl.BlockSpec(ones.shape, lambda i, j: (0, 0))]
    args += [gains, ones]
    if rope is not None:
        tables, rope_half = rope
        for t in tables:
            in_specs.append(pl.BlockSpec((tm, PAIR), lambda i, j: (i % bpb, 0)))
            args.append(t)
    else:
        rope_half = 0
    kern = functools.partial(_proj_kernel, mod=mod is not None, segs=tuple(segs),
                             seg_tiles=seg_tiles, rope=rope is not None, rope_half=rope_half)
    return pl.pallas_call(
        kern,
        grid=(m // tm, n // tn),
        in_specs=in_specs,
        out_specs=pl.BlockSpec((tm, tn), lambda i, j: (i, j)),
        out_shape=jax.ShapeDtypeStruct((m, n), jnp.bfloat16),
        scratch_shapes=[pltpu.VMEM((tm, k), jnp.bfloat16)] if mod is not None else [],
        compiler_params=_params("parallel", "arbitrary"),
        name=name,
    )(*args)


def _outproj_kernel(*refs, n_a):
    a_refs = refs[:n_a]
    w_ref, x_ref, g_ref, o_ref = refs[n_a:]
    n = o_ref.shape[1]
    a_vals = [a_ref[...] for a_ref in a_refs]
    for c in range(0, n, SUB_N):
        acc, row = None, 0
        for a in a_vals:
            part = _dot(a, w_ref[row:row + a.shape[1], c:c + SUB_N])
            acc = part if acc is None else acc + part
            row += a.shape[1]
        o_ref[:, c:c + SUB_N] = x_ref[:, c:c + SUB_N] + g_ref[0][:, c:c + SUB_N] * acc


def _outproj_call(a_list, w, x, modarr, gate_chunk, mod_row, *, tm=TM, name="outproj"):
    m, n = x.shape
    tm = min(tm, m)
    in_specs, args = [], []
    for a in a_list:
        in_specs.append(pl.BlockSpec((tm, a.shape[1]), lambda i: (i, 0)))
        args.append(a)
    in_specs += [
        pl.BlockSpec(w.shape, lambda i: (0, 0)),
        pl.BlockSpec((tm, n), lambda i: (i, 0)),
        pl.BlockSpec((1, 1, n), lambda i: (mod_row(i), 0, gate_chunk)),
    ]
    args += [w, x, modarr]
    return pl.pallas_call(
        functools.partial(_outproj_kernel, n_a=len(a_list)),
        grid=(m // tm,),
        in_specs=in_specs,
        out_specs=pl.BlockSpec((tm, n), lambda i: (i, 0)),
        out_shape=jax.ShapeDtypeStruct((m, n), jnp.float32),
        compiler_params=_params("parallel"),
        name=name,
    )(*args)


def _ffn_kernel(x_ref, xp_ref, xn_ref, g_ref, sh_ref, sc_ref, gt_ref, wa_ref, wb_ref,
                ca_ref, cb_ref, wo_ref, o_ref, h_ref, *, tm, hm, seq_len):
    i = pl.program_id(0)
    j = pl.program_id(1)
    nj = pl.num_programs(1)
    n_grp = tm // hm
    rows = hm + 2 * SUBLANES

    @pl.when(j == 0)
    def _():
        gain, shift, scale = g_ref[...], sh_ref[0], sc_ref[0]
        keep_prev = jnp.where((i * tm) % seq_len == 0, 0.0, 1.0)
        keep_next = jnp.where(((i + 1) * tm) % seq_len == 0, 0.0, 1.0)
        for r in range(n_grp):
            r0 = r * hm
            h_ref[r * rows:r * rows + hm, :] = _bf(
                _modulate(x_ref[r0:r0 + hm, :], gain, shift, scale))
            if r + 1 < n_grp:
                nxt = _modulate(x_ref[r0 + hm:r0 + hm + SUBLANES, :], gain, shift, scale)
            else:
                nxt = keep_next * _modulate(xn_ref[...], gain, shift, scale)
            if r > 0:
                prv = _modulate(x_ref[r0 - SUBLANES:r0, :], gain, shift, scale)
            else:
                prv = keep_prev * _modulate(xp_ref[...], gain, shift, scale)
            h_ref[r * rows + hm:(r + 1) * rows, :] = _bf(jnp.concatenate([nxt, prv], axis=0))
        o_ref[...] = jnp.zeros_like(o_ref)

    ca, cb = ca_ref[...], cb_ref[...]
    ups = []
    for r in range(n_grp):
        h = h_ref[r * rows:(r + 1) * rows, :]
        ups.append((_dot(h, wa_ref[...]), _dot(h, wb_ref[...])))

    def conv(u, cw):
        prev = pltpu.roll(u, 1, axis=0)[0:hm]
        nxt = pltpu.roll(u, rows - 1, axis=0)[0:hm]
        return prev * cw[0:1] + u[0:hm] * cw[1:2] + nxt * cw[2:3]

    for r, (ua, ub) in enumerate(ups):
        a = conv(ua, ca)
        b = conv(ub, cb)
        act = a * (1.0 / (1.0 + jnp.exp(-a))) * b
        o_ref[r * hm:(r + 1) * hm, :] += _dot(_bf(act), wo_ref[...])

    @pl.when(j == nj - 1)
    def _():
        o_ref[...] = x_ref[...] + gt_ref[0] * o_ref[...]


def _ffn_call(x, gain, modarr, w_in, conv_w, w_out, layer, mod_row, seq_len, *, tm=TM_FF,
              tn=TN_FF, name="conv_ffn"):
    m, d = x.shape
    tm = min(tm, seq_len)
    hm = min(HM_FF, tm)
    assert seq_len % tm == 0 and m % tm == 0 and tm % hm == 0
    nj = D_FF_PAD // tn
    hb = tm // SUBLANES
    last_hb = m // SUBLANES - 1
    kern = functools.partial(_ffn_kernel, tm=tm, hm=hm, seq_len=seq_len)
    return pl.pallas_call(
        kern,
        grid=(m // tm, nj),
        in_specs=[
            pl.BlockSpec((tm, d), lambda i, j: (i, 0)),
            pl.BlockSpec((SUBLANES, d), lambda i, j: (jnp.maximum(i * hb - 1, 0), 0)),
            pl.BlockSpec((SUBLANES, d), lambda i, j: (jnp.minimum((i + 1) * hb, last_hb), 0)),
            pl.BlockSpec((1, d), lambda i, j: (0, 0)),
            pl.BlockSpec((1, 1, d), lambda i, j: (mod_row(i), 0, 3)),
            pl.BlockSpec((1, 1, d), lambda i, j: (mod_row(i), 0, 4)),
            pl.BlockSpec((1, 1, d), lambda i, j: (mod_row(i), 0, 5)),
            pl.BlockSpec((None, d, tn), lambda i, j: (layer, 0, j)),
            pl.BlockSpec((None, d, tn), lambda i, j: (layer, 0, nj + j)),
            pl.BlockSpec((None, 3, tn), lambda i, j: (layer, 0, j)),
            pl.BlockSpec((None, 3, tn), lambda i, j: (layer, 0, nj + j)),
            pl.BlockSpec((None, tn, d), lambda i, j: (layer, j, 0)),
        ],
        out_specs=pl.BlockSpec((tm, d), lambda i, j: (i, 0)),
        out_shape=jax.ShapeDtypeStruct((m, d), jnp.float32),
        scratch_shapes=[
            pltpu.VMEM(((tm // hm) * (hm + 2 * SUBLANES), d), jnp.bfloat16),
        ],
        compiler_params=_params("parallel", "arbitrary"),
        name=name,
    )(x, x, x, gain, modarr, modarr, modarr, w_in, w_in, conv_w, conv_w, w_out)


def _na_group_geometry(g):
    r0 = g * NA_GROUP_ROWS
    return r0, min(max(r0 - NA_WIN_H // 2, 0), GRID_H - NA_KEY_ROWS)


def _na_build_bias(rpb_ref, h, tc_scr, bias_scr):
    n_dr = 2 * NA_WIN_H - 1
    n_dc = 2 * NA_WIN_W - 1
    qc = lax.broadcasted_iota(jnp.int32, (GRID_W, GRID_W), 0)
    kc = lax.broadcasted_iota(jnp.int32, (GRID_W, GRID_W), 1)
    col0 = jnp.clip(qc - NA_WIN_W // 2, 0, GRID_W - NA_WIN_W)
    col_valid = (kc >= col0) & (kc < col0 + NA_WIN_W)
    delta = kc - qc + (NA_WIN_W - 1)
    base = h * (n_dr * n_dc)
    for dr in range(n_dr):
        acc = jnp.zeros((GRID_W, GRID_W), jnp.float32)
        for e in range(n_dc):
            acc = jnp.where(delta == e, rpb_ref[base + dr * n_dc + e] * LOG2E, acc)
        tc_scr[dr] = jnp.where(col_valid, acc, NEG)
    tc_scr[n_dr] = jnp.full((GRID_W, GRID_W), NEG, jnp.float32)
    n_groups = GRID_H // NA_GROUP_ROWS
    for t, g in enumerate((0, 1, n_groups - 1)):
        r0, ks = _na_group_geometry(g)
        for qr in range(NA_GROUP_ROWS):
            r = r0 + qr
            row0 = min(max(r - NA_WIN_H // 2, 0), GRID_H - NA_WIN_H)
            pieces = []
            for kr in range(NA_KEY_ROWS):
                kra = ks + kr
                inside = row0 <= kra < row0 + NA_WIN_H
                pieces.append(tc_scr[kra - r + (NA_WIN_H - 1)] if inside else tc_scr[n_dr])
            bias_scr[t, qr * GRID_W:(qr + 1) * GRID_W, :] = jnp.concatenate(pieces, axis=1)


def _na_kernel(rpb_ref, q_ref, k_ref, v_ref, kc_ref, vc_ref, o_ref, tc_scr, bias_scr):
    @pl.when(pl.program_id(1) == 0)
    def _():
        _na_build_bias(rpb_ref, pl.program_id(0), tc_scr, bias_scr)

    kc = kc_ref[...]
    vc = vc_ref[...]
    n_groups = GRID_H // NA_GROUP_ROWS

    def body(g, carry):
        r0 = g * NA_GROUP_ROWS
        ks = jnp.clip(r0 - NA_WIN_H // 2, 0, GRID_H - NA_KEY_ROWS)
        q0 = pl.multiple_of(g * NA_Q, NA_Q)
        k0 = pl.multiple_of(ks * GRID_W, GRID_W)
        tb = jnp.where(g == 0, 0, jnp.where(g == n_groups - 1, 2, 1))
        q = q_ref[pl.ds(q0, NA_Q), :]
        kw = k_ref[pl.ds(k0, NA_K), :]
        vw = v_ref[pl.ds(k0, NA_K), :]
        sw = _dot_nt(q, kw) + bias_scr[tb]
        sc = _dot_nt(q, kc)
        mx = jnp.maximum(jnp.max(sw, axis=-1, keepdims=True), jnp.max(sc, axis=-1, keepdims=True))
        pw = jnp.exp2(sw - mx)
        pc = jnp.exp2(sc - mx)
        den = jnp.sum(pw, axis=-1, keepdims=True) + jnp.sum(pc, axis=-1, keepdims=True)
        o = _dot(_bf(pw), vw) + _dot(_bf(pc), vc)
        o_ref[pl.ds(q0, NA_Q), :] = _bf(o / den)
        return carry

    lax.fori_loop(0, n_groups, body, 0)


def _na_call(qkv, qkv_c, rpb):
    n_dr = 2 * NA_WIN_H - 1
    return pl.pallas_call(
        _na_kernel,
        grid=(NA_HEADS, BATCH),
        in_specs=[
            pl.BlockSpec(memory_space=pltpu.SMEM),
            pl.BlockSpec((SEQ, LANES), lambda h, b: (b, h)),
            pl.BlockSpec((SEQ, LANES), lambda h, b: (b, NA_HEADS + h)),
            pl.BlockSpec((SEQ, LANES), lambda h, b: (b, 2 * NA_HEADS + h)),
            pl.BlockSpec((CTX_LEN, LANES), lambda h, b: (b, NA_HEADS + h)),
            pl.BlockSpec((CTX_LEN, LANES), lambda h, b: (b, 2 * NA_HEADS + h)),
        ],
        out_specs=pl.BlockSpec((SEQ, LANES), lambda h, b: (b, h)),
        out_shape=jax.ShapeDtypeStruct((BATCH * SEQ, NA_WIDTH), jnp.bfloat16),
        scratch_shapes=[
            pltpu.VMEM((n_dr + 1, GRID_W, GRID_W), jnp.float32),
            pltpu.VMEM((3, NA_Q, NA_K), jnp.float32),
        ],
        compiler_params=_params("parallel", "arbitrary"),
        name="na_attn",
    )(rpb.reshape(-1), qkv, qkv, qkv, qkv_c, qkv_c)


def _key_chunks(src_rows, chunk):
    chunks, r = [], 0
    for rows in src_rows:
        size = min(chunk, rows)
        assert rows % size == 0
        chunks += [(r + c, size) for c in range(0, rows, size)]
        r += rows
    return chunks


def _pipelined_attention(streams, vt_scr, chunks):
    dv = vt_scr.shape[0]
    sub = streams[0][2].shape[1]
    outs, m_prev = [], None
    for u in range(len(streams) + 1):
        m = jnp.full((1, sub), NEG, jnp.float32)
        l = jnp.zeros((1, sub), jnp.float32)
        acc = jnp.zeros((dv, sub), jnp.float32)
        for c0, cs in chunks:
            if u < len(streams):
                q_t, k_scr, s_ref = streams[u]
                s = _dot(k_scr[c0:c0 + cs, :], q_t)
                s_ref[c0:c0 + cs, :] = s
                m = jnp.maximum(m, jnp.max(s, axis=0, keepdims=True))
            if u > 0:
                p = jnp.exp2(streams[u - 1][2][c0:c0 + cs, :] - m_prev)
                l = l + jnp.sum(p, axis=0, keepdims=True)
                acc = acc + _dot(vt_scr[:, c0:c0 + cs], _bf(p))
        if u > 0:
            outs.append(acc / l)
        m_prev = m
    return outs


def _transpose_bf16(x):
    return _bf(x.astype(jnp.float32).T)


def _attn_kernel(*refs, n_qparts, n_src, n_kparts, src_rows):
    it = iter(refs)
    q_refs = [next(it) for _ in range(n_qparts)]
    srcs = []
    for _ in range(n_src):
        k_refs = [next(it) for _ in range(n_kparts)]
        srcs.append((k_refs, next(it)))
    o_ref = next(it)
    k_scr, vt_scr = next(it), next(it)
    s_scrs = list(it)

    @pl.when(pl.program_id(2) == 0)
    def _():
        r = 0
        for (k_refs, v_ref), rows in zip(srcs, src_rows):
            for p, k_ref in enumerate(k_refs):
                k_scr[r:r + rows, p * LANES:(p + 1) * LANES] = k_ref[...]
            vt_scr[:, r:r + rows] = _transpose_bf16(v_ref[...])
            r += rows

    q = jnp.concatenate([qr[...] for qr in q_refs], axis=-1) if n_qparts > 1 else q_refs[0][...]
    sub = s_scrs[0].shape[1]
    streams = [(_transpose_bf16(q[u * sub:(u + 1) * sub]), k_scr, s_scr)
               for u, s_scr in enumerate(s_scrs)]
    outs = _pipelined_attention(streams, vt_scr, _key_chunks(src_rows, KEY_CHUNK_ATTN))
    for u, o_t in enumerate(outs):
        o_ref[u * sub:(u + 1) * sub, :] = _bf(o_t.T)


def _attn_call(q_parts, sources, n_heads, lq, tq, dv, name):
    tq = min(tq, lq)
    sub = min(TQ_SUB, tq)
    nq = lq // tq
    in_specs, args = [], []
    for arr, cf in q_parts:
        in_specs.append(pl.BlockSpec((tq, LANES), lambda b, h, i, cf=cf: (b * nq + i, cf(h))))
        args.append(arr)
    src_rows = []
    n_kparts = len(sources[0][1])
    for rows, k_parts, (v_arr, vcf) in sources:
        src_rows.append(rows)
        for arr, cf in k_parts:
            in_specs.append(pl.BlockSpec((rows, LANES), lambda b, h, i, cf=cf: (b, cf(h))))
            args.append(arr)
        in_specs.append(pl.BlockSpec((rows, dv), lambda b, h, i, cf=vcf: (b, cf(h))))
        args.append(v_arr)
    nk = sum(src_rows)
    kern = functools.partial(_attn_kernel, n_qparts=len(q_parts), n_src=len(sources),
                             n_kparts=n_kparts, src_rows=tuple(src_rows))
    return pl.pallas_call(
        kern,
        grid=(BATCH, n_heads, nq),
        in_specs=in_specs,
        out_specs=pl.BlockSpec((tq, dv), lambda b, h, i: (b * nq + i, h)),
        out_shape=jax.ShapeDtypeStruct((BATCH * lq, n_heads * dv), jnp.bfloat16),
        scratch_shapes=[
            pltpu.VMEM((nk, n_kparts * LANES), jnp.bfloat16),
            pltpu.VMEM((dv, nk), jnp.bfloat16),
        ] + [pltpu.VMEM((nk, sub), jnp.float32)] * (tq // sub),
        compiler_params=_params("parallel", "parallel", "arbitrary"),
        name=name,
    )(*args)


def _diff_kernel(*refs, n_src, src_rows, lam_init):
    it = iter(refs)
    q1_ref, q2_ref = next(it), next(it)
    srcs = [(next(it), next(it), next(it)) for _ in range(n_src)]
    lam_ref, sub_ref = next(it), next(it)
    o_ref = next(it)
    k1_scr, k2_scr, vt_scr, s1_scr, s2_scr = [next(it) for _ in range(5)]

    @pl.when(pl.program_id(2) == 0)
    def _():
        r = 0
        for (k1_ref, k2_ref, v_ref), rows in zip(srcs, src_rows):
            k1_scr[r:r + rows, :] = k1_ref[...]
            k2_scr[r:r + rows, :] = k2_ref[...]
            vt_scr[:, r:r + rows] = _transpose_bf16(v_ref[...])
            r += rows

    lf = lam_ref[...]
    lam = (jnp.exp(jnp.sum(lf[0:1] * lf[1:2], axis=-1, keepdims=True))
           - jnp.exp(jnp.sum(lf[2:3] * lf[3:4], axis=-1, keepdims=True)) + lam_init)
    chunks = _key_chunks(src_rows, KEY_CHUNK_DIFF)
    n_sub, _, sub = s1_scr.shape
    streams = []
    for u in range(n_sub):
        rows = slice(u * sub, (u + 1) * sub)
        streams.append((_transpose_bf16(q1_ref[rows, :]), k1_scr, s1_scr.at[u]))
        streams.append((_transpose_bf16(q2_ref[rows, :]), k2_scr, s2_scr.at[u]))
    outs = _pipelined_attention(streams, vt_scr, chunks)
    o_t = jnp.concatenate([outs[2 * u] - lam * outs[2 * u + 1] for u in range(n_sub)],
                          axis=1)
    o = o_t.T
    ms = jnp.mean(o * o, axis=-1, keepdims=True)
    o_ref[...] = _bf(o * lax.rsqrt(ms + EPS) * sub_ref[...] * (1.0 - lam_init))


def _diff_call(qkv_q, lq, sources, diff_lambda, subln, lam_init, tq, name):
    tq = min(tq, lq)
    sub = min(TQ_SUB, tq)
    nq = lq // tq
    qb0 = 3 * NA_WIDTH // LANES
    kb0 = (3 * NA_WIDTH + DIFF_WIDTH) // LANES
    vb0 = (3 * NA_WIDTH + 2 * DIFF_WIDTH) // (2 * LANES)
    in_specs = [
        pl.BlockSpec((tq, LANES), lambda b, h, i: (b * nq + i, qb0 + 2 * h)),
        pl.BlockSpec((tq, LANES), lambda b, h, i: (b * nq + i, qb0 + 2 * h + 1)),
    ]
    args = [qkv_q, qkv_q]
    src_rows = []
    for rows, arr in sources:
        src_rows.append(rows)
        in_specs += [
            pl.BlockSpec((rows, LANES), lambda b, h, i: (b, kb0 + 2 * h)),
            pl.BlockSpec((rows, LANES), lambda b, h, i: (b, kb0 + 2 * h + 1)),
            pl.BlockSpec((rows, 2 * LANES), lambda b, h, i: (b, vb0 + h)),
        ]
        args += [arr, arr, arr]
    in_specs += [
        pl.BlockSpec((4, LANES), lambda b, h, i: (0, 0)),
        pl.BlockSpec((1, 2 * LANES), lambda b, h, i: (0, 0)),
    ]
    args += [diff_lambda, subln]
    nk = sum(src_rows)
    kern = functools.partial(_diff_kernel, n_src=len(sources), src_rows=tuple(src_rows),
                             lam_init=lam_init)
    return pl.pallas_call(
        kern,
        grid=(BATCH, DIFF_HEADS, nq),
        in_specs=in_specs,
        out_specs=pl.BlockSpec((tq, 2 * LANES), lambda b, h, i: (b * nq + i, h)),
        out_shape=jax.ShapeDtypeStruct((BATCH * lq, DIFF_WIDTH), jnp.bfloat16),
        scratch_shapes=[
            pltpu.VMEM((nk, LANES), jnp.bfloat16),
            pltpu.VMEM((nk, LANES), jnp.bfloat16),
            pltpu.VMEM((2 * LANES, nk), jnp.bfloat16),
            pltpu.VMEM((tq // sub, nk, sub), jnp.float32),
            pltpu.VMEM((tq // sub, nk, sub), jnp.float32),
        ],
        compiler_params=_params("parallel", "parallel", "arbitrary"),
        name=name,
    )(*args)


def _rope_tables(d):
    h = d // 2
    half = h // 2
    t = jnp.arange(SEQ, dtype=jnp.int32)
    freqs = ROPE_BASE ** (-jnp.arange(half, dtype=jnp.float32) / half)

    def cs(pos):
        ang = pos.astype(jnp.float32)[:, None] * freqs[None, :]
        return jnp.cos(ang), jnp.sin(ang)

    cr, sr = cs(t // GRID_W)
    cc, sc = cs(t % GRID_W)
    cos = jnp.concatenate([cr, cr, cc, cc], -1)
    sin = jnp.concatenate([sr, sr, sc, sc], -1)
    first = (jnp.arange(d) % h) < half
    sin_a = jnp.where(first[None, :], -sin, 0.0)
    sin_b = jnp.where(first[None, :], 0.0, sin)
    pad = LANES - d
    if pad:
        cos = jnp.pad(cos, ((0, 0), (0, pad)), constant_values=1.0)
        sin_a = jnp.pad(sin_a, ((0, 0), (0, pad)))
        sin_b = jnp.pad(sin_b, ((0, 0), (0, pad)))
    pair_tables = tuple(jnp.concatenate([t, t], -1) for t in (cos, sin_a, sin_b))
    return (cos, sin_a, sin_b), half, (pair_tables, half)


def _pair_blockdiag(block):
    z = np.zeros_like(block)
    return jnp.asarray(np.block([[block, z], [z, block]]), dtype=jnp.bfloat16)


def _pad_lanes(v, fill=0.0):
    return jnp.pad(v, (0, LANES - v.shape[0]), constant_values=fill)


def _pair(v):
    return jnp.concatenate([v, v])


def _lat_row(tm):
    bpb = SEQ // tm
    return lambda i: i // bpb


def _ctx_row(i):
    return CTX_MOD_ROW


def _even_layer(x, xc, modarr, norm_mix, norm_ffn, ffn_w, w_in, w_out, qn_a, kn_a, rpb,
                qn_b, kn_b, diff_lambda, subln, lam_init, with_ctx):
    scale = HEAD_DIM ** -0.5 * LOG2E
    gains = jnp.stack([_pair(g) for g in (qn_a * scale, kn_a, qn_b * scale, kn_b)]
                      + [jnp.ones((PAIR,), jnp.float32)] * 4)
    _, _, rope = _rope_tables(HEAD_DIM)
    segs = [(True, 0, HEAD_DIM, False), (True, 1, HEAD_DIM, False), (False, 0, HEAD_DIM, False),
            (True, 2, HEAD_DIM, True), (True, 3, HEAD_DIM, True), (False, 0, HEAD_DIM, False)]
    w_in = _bf(w_in)
    w_out = _bf(w_out)
    gain_mix = norm_mix[None, :]
    lat_row = _lat_row(TM)

    qkv = _proj_call(x, w_in, gains, segs, NA_WIDTH, mod=(gain_mix, modarr, 0, 1), rope=rope,
                     mod_row=lat_row, name="even_proj")
    qkv_c = _proj_call(xc, w_in, gains, segs, NA_WIDTH, mod=(gain_mix, modarr, 0, 1), rope=None,
                       mod_row=_ctx_row, name="even_proj_ctx")

    oa = _na_call(qkv, qkv_c, rpb)
    ob = _diff_call(qkv, SEQ, [(SEQ, qkv), (CTX_LEN, qkv_c)], diff_lambda, subln[None, :],
                    lam_init, TQ_DIFF, "diff_attn")
    x = _outproj_call([oa, ob], w_out, x, modarr, 2, lat_row, name="even_out")
    x = _ffn_call(x, norm_ffn[None, :], modarr, *ffn_w, _lat_row(TM_FF), SEQ)
    if with_ctx:
        oa_c = _attn_call(
            [(qkv_c, lambda h: h)],
            [(CTX_LEN, [(qkv_c, lambda h: NA_HEADS + h)], (qkv_c, lambda h: 2 * NA_HEADS + h))],
            NA_HEADS, CTX_LEN, CTX_LEN, HEAD_DIM, "ctx_attn")
        ob_c = _diff_call(qkv_c, CTX_LEN, [(CTX_LEN, qkv_c)], diff_lambda, subln[None, :],
                          lam_init, CTX_LEN, "diff_attn_ctx")
        xc = _outproj_call([oa_c, ob_c], w_out, xc, modarr, 2, _ctx_row, name="even_out_ctx")
        xc = _ffn_call(xc, norm_ffn[None, :], modarr, *ffn_w, _ctx_row, CTX_LEN,
                       name="conv_ffn_ctx")
    return x, xc


def _odd_layer(x, xc, modarr, norm_mix, norm_ffn, ffn_w, w_down, q_a_norm, kv_a_norm, w_uq,
               w_ukv, qn_nope, qn_rope, kn_nope, kn_rope, w_out, with_ctx):
    scale = (MLA_NOPE + MLA_ROPE) ** -0.5 * LOG2E
    _, _, rope_pair = _rope_tables(MLA_ROPE)
    gain_mix = norm_mix[None, :]
    lat_row = _lat_row(TM)

    w_down_p = _bf(jnp.pad(w_down, ((0, 0), (0, MLA_DOWN_PAD - MLA_DOWN))))
    zeros_r = jnp.zeros((MLA_Q_RANK,), jnp.float32)
    kr_gain = jnp.concatenate([_pad_lanes(kn_rope), zeros_r[:MLA_Q_RANK - LANES]])
    gains_d = jnp.stack([q_a_norm, kv_a_norm, kr_gain] + [zeros_r] * 5)
    wq = w_uq.reshape(MLA_Q_RANK, MLA_HEADS, MLA_NOPE + MLA_ROPE)
    wq_rope = jnp.pad(wq[:, :, MLA_NOPE:], ((0, 0), (0, 0), (0, LANES - MLA_ROPE)))
    wq_p = _bf(jnp.concatenate([wq[:, :, :MLA_NOPE].reshape(MLA_Q_RANK, -1),
                                wq_rope.reshape(MLA_Q_RANK, -1)], -1))
    wkv = w_ukv.reshape(MLA_KV_RANK, MLA_HEADS, MLA_NOPE + MLA_V)
    wkv_p = _bf(jnp.concatenate([wkv[:, :, :MLA_NOPE].reshape(MLA_KV_RANK, -1),
                                 wkv[:, :, MLA_NOPE:].reshape(MLA_KV_RANK, -1)], -1))
    gains_p = jnp.stack([_pair(qn_nope * scale), _pair(_pad_lanes(qn_rope) * scale),
                         _pair(kn_nope)] + [jnp.zeros((PAIR,), jnp.float32)] * 5)
    mod_in = (gain_mix, modarr, 0, 1)
    q, kv, kr = _mla_proj_call(x, w_down_p, wq_p, wkv_p, gains_d, gains_p, mod_in, rope_pair,
                               _lat_row(TM_MLA_PROJ), True, "mla_proj")
    kv_c, kr_c = _mla_proj_call(xc, w_down_p, wq_p, wkv_p, gains_d, gains_p, mod_in, None,
                                _ctx_row, False, "mla_proj_ctx")

    def src(rows, kv_arr, kr_arr):
        return (rows, [(kv_arr, lambda h: h), (kr_arr, lambda h: 0)],
                (kv_arr, lambda h: MLA_HEADS + h))

    o = _attn_call([(q, lambda h: h), (q, lambda h: MLA_HEADS + h)],
                   [src(SEQ, kv, kr), src(CTX_LEN, kv_c, kr_c)],
                   MLA_HEADS, SEQ, TQ_MLA, MLA_V, "mla_attn")
    x = _outproj_call([o], _bf(w_out), x, modarr, 2, lat_row, name="mla_out")
    x = _ffn_call(x, norm_ffn[None, :], modarr, *ffn_w, _lat_row(TM_FF), SEQ)
    assert not with_ctx
    return x, xc


def _mla_proj_kernel(x_ref, g_ref, sh_ref, sc_ref, wd_ref, wq_ref, wkv_ref, gd_ref, gp_ref,
                     ones_ref, *rest, rope, rope_half, with_q):
    rope_refs = rest[:3] if rope else None
    outs = rest[3:] if rope else rest
    if with_q:
        q_ref, kv_ref, kr_ref = outs
    else:
        kv_ref, kr_ref = outs
    h = _bf(_modulate(x_ref[...], g_ref[...], sh_ref[0], sc_ref[0]))
    down = _dot(h, wd_ref[...])

    def latent_norm(xs, row, nvalid):
        ms = jnp.sum(xs * xs, axis=-1, keepdims=True) * (1.0 / nvalid)
        return xs * lax.rsqrt(ms + EPS) * gd_ref[row:row + 1, 0:xs.shape[1]]

    kva = _bf(latent_norm(down[:, MLA_Q_RANK:MLA_Q_RANK + MLA_KV_RANK], 1, MLA_KV_RANK))
    kr = latent_norm(down[:, MLA_Q_RANK + MLA_KV_RANK:MLA_DOWN_PAD], 2, MLA_ROPE)
    if rope:
        cos_ref, sa_ref, sb_ref = rope_refs
        kr = (kr * cos_ref[:, 0:LANES]
              + pltpu.roll(kr, LANES - rope_half, axis=1) * sa_ref[:, 0:LANES]
              + pltpu.roll(kr, rope_half, axis=1) * sb_ref[:, 0:LANES])
    kr_ref[...] = _bf(kr)

    def up(a, w_ref, o_ref, specs):
        width = w_ref.shape[1] // len(specs)
        for c in range(0, w_ref.shape[1], SUB_N):
            acc = _dot(a, w_ref[:, c:c + SUB_N])
            for p in range(0, SUB_N, PAIR):
                y = _pair_epilogue(acc[:, p:p + PAIR], specs[c // width], gp_ref, ones_ref,
                                   rope_refs, rope_half)
                o_ref[:, c + p:c + p + PAIR] = _bf(y)

    up(kva, wkv_ref, kv_ref, [(True, 2, MLA_NOPE, False), (False, 0, MLA_V, False)])
    if with_q:
        qa = _bf(latent_norm(down[:, 0:MLA_Q_RANK], 0, MLA_Q_RANK))
        up(qa, wq_ref, q_ref, [(True, 0, MLA_NOPE, False), (True, 1, MLA_ROPE, True)])


def _mla_proj_call(x, wd, wq, wkv, gains_d, gains_p, mod, rope, mod_row, with_q, name,
                   tm=TM_MLA_PROJ):
    m, k = x.shape
    tm = min(tm, m)
    bpb = SEQ // tm
    gain, modarr, sh_c, sc_c = mod
    ones = _pair_blockdiag(np.ones((LANES, LANES), np.float32))
    in_specs = [
        pl.BlockSpec((tm, k), lambda i: (i, 0)),
        pl.BlockSpec((1, k), lambda i: (0, 0)),
        pl.BlockSpec((1, 1, k), lambda i: (mod_row(i), 0, sh_c)),
        pl.BlockSpec((1, 1, k), lambda i: (mod_row(i), 0, sc_c)),
    ]
    args = [x, gain, modarr, modarr]
    for w in (wd, wq, wkv, gains_d, gains_p, ones):
        in_specs.append(pl.BlockSpec(w.shape, lambda i: (0, 0)))
        args.append(w)
    rope_half = 0
    if rope is not None:
        tables, rope_half = rope
        for t in tables:
            in_specs.append(pl.BlockSpec((tm, PAIR), lambda i: (i % bpb, 0)))
            args.append(t)
    widths = ([wq.shape[1]] if with_q else []) + [wkv.shape[1], LANES]
    return pl.pallas_call(
        functools.partial(_mla_proj_kernel, rope=rope is not None, rope_half=rope_half,
                          with_q=with_q),
        grid=(m // tm,),
        in_specs=in_specs,
        out_specs=[pl.BlockSpec((tm, n), lambda i: (i, 0)) for n in widths],
        out_shape=[jax.ShapeDtypeStruct((m, n), jnp.bfloat16) for n in widths],
        compiler_params=_params("parallel"),
        name=name,
    )(*args)


def _ffn_weights(w_in, conv_w, w_out):
    pad = D_FF_PAD - D_FF
    z = jnp.zeros(w_in.shape[:2] + (pad,), jnp.bfloat16)
    zc = jnp.zeros(conv_w.shape[:2] + (pad,), conv_w.dtype)
    return (jnp.concatenate([_bf(w_in[..., :D_FF]), z, _bf(w_in[..., D_FF:]), z], -1),
            jnp.concatenate([conv_w[..., :D_FF], zc, conv_w[..., D_FF:], zc], -1),
            jnp.pad(_bf(w_out), ((0, 0), (0, pad), (0, 0))))


def kernel(x, c, ctx, c_ctx, ada_w, ada_b, norm_mix, norm_ffn, ffn_w_in, ffn_conv, ffn_w_out, even_w_in, even_w_out, na_q_norm, na_k_norm, na_rpb, diff_q_norm, diff_k_norm, diff_lambda, diff_subln, mla_w_down, mla_q_a_norm, mla_kv_a_norm, mla_w_uq, mla_w_ukv, mla_q_nope_norm, mla_q_rope_norm, mla_k_nope_norm, mla_k_rope_norm, mla_w_out):
    cond = jnp.concatenate(
        [c, c_ctx[None, :], jnp.zeros((MOD_ROWS - BATCH - 1, D_MODEL), jnp.float32)], 0)
    mod = _ada_call(cond, ada_w, ada_b)
    xl = x.reshape(BATCH * SEQ, D_MODEL)
    xc = ctx.reshape(BATCH * CTX_LEN, D_MODEL)
    ffn_all = _ffn_weights(ffn_w_in, ffn_conv, ffn_w_out)
    for l in range(DEPTH):
        with_ctx = l < DEPTH - 1
        modarr = mod[l].reshape(MOD_ROWS, 1, 6 * D_MODEL)
        ffn_w = ffn_all + (l,)
        i = l // 2
        if l % 2 == 0:
            lam_init = 0.8 - 0.6 * math.exp(-0.3 * l)
            xl, xc = _even_layer(xl, xc, modarr, norm_mix[l], norm_ffn[l], ffn_w, even_w_in[i],
                                 even_w_out[i], na_q_norm[i], na_k_norm[i], na_rpb[i],
                                 diff_q_norm[i], diff_k_norm[i], diff_lambda[i], diff_subln[i],
                                 lam_init, with_ctx)
        else:
            xl, xc = _odd_layer(xl, xc, modarr, norm_mix[l], norm_ffn[l], ffn_w, mla_w_down[i],
                                mla_q_a_norm[i], mla_kv_a_norm[i], mla_w_uq[i], mla_w_ukv[i],
                                mla_q_nope_norm[i], mla_q_rope_norm[i], mla_k_nope_norm[i],
                                mla_k_rope_norm[i], mla_w_out[i], with_ctx)
    return xl.reshape(BATCH, SEQ, D_MODEL)
```

```python
import functools
import math

import numpy as np
import jax
import jax.numpy as jnp
from jax import lax
from jax.experimental import pallas as pl
from jax.experimental.pallas import tpu as pltpu

D_MODEL = 2048
BATCH = 4
SEQ = 4096
DEPTH = 2
GRID_W = 64
GRID_H = SEQ // GRID_W
CTX_LEN = 256
HEAD_DIM = 128
NA_HEADS = 8
NA_WIN_H = 8
NA_WIN_W = 16
DIFF_HEADS = 4
NA_WIDTH = 1024
DIFF_WIDTH = 1024
EVEN_PROJ = 6144
MLA_HEADS = 16
MLA_Q_RANK = 512
MLA_KV_RANK = 512
MLA_NOPE = 128
MLA_ROPE = 64
MLA_V = 128
MLA_DOWN = 1088
D_FF = 5504
ROPE_BASE = 10000.0
EPS = 1e-6

LANES = 128
SUBLANES = 8
MOD_ROWS = 8
CTX_MOD_ROW = BATCH
D_FF_PAD = 5632
MLA_DOWN_PAD = 1152
VMEM_LIMIT = 56 * 1024 * 1024

TM = 512
TN = 1024
TN_FF = 512
TM_FF = 1024
TM_EVEN = 1024
TM_MLA_PROJ = 256
HM_FF = 512
SUB_N = 512
PAIR = 2 * LANES
NA_GROUP_ROWS = 4
NA_KEY_ROWS = 12
NA_Q = NA_GROUP_ROWS * GRID_W
NA_K = NA_KEY_ROWS * GRID_W
TQ_DIFF = 512
TQ_MLA = 1024
TQ_SUB = 256
KEY_CHUNK_ATTN = 256
KEY_CHUNK_DIFF = 512
NEG = -1e30
LOG2E = math.log2(math.e)

_NT = (((1,), (1,)), ((), ()))


def _params(*sem):
    return pltpu.CompilerParams(dimension_semantics=sem, vmem_limit_bytes=VMEM_LIMIT)


def _bf(x):
    return x.astype(jnp.bfloat16)


def _dot(a, b):
    return jnp.dot(a, b, preferred_element_type=jnp.float32)


def _dot_nt(a, b):
    return lax.dot_general(a, b, _NT, preferred_element_type=jnp.float32)


def _modulate(x, gain, shift, scale):
    ms = jnp.mean(x * x, axis=-1, keepdims=True)
    y = x * lax.rsqrt(ms + EPS) * gain
    return y * (1.0 + scale) + shift


def _ada_kernel(c_ref, w_ref, b_ref, o_ref):
    c = c_ref[...]
    s = c * (1.0 / (1.0 + jnp.exp(-c)))
    o_ref[0] = _dot(_bf(s), _bf(w_ref[0])) + b_ref[0]


def _ada_call(cond, ada_w, ada_b):
    tn = 512
    n = ada_w.shape[-1]
    return pl.pallas_call(
        _ada_kernel,
        grid=(DEPTH, n // tn),
        in_specs=[
            pl.BlockSpec((MOD_ROWS, D_MODEL), lambda l, j: (0, 0)),
            pl.BlockSpec((1, D_MODEL, tn), lambda l, j: (l, 0, j)),
            pl.BlockSpec((1, 1, tn), lambda l, j: (l, 0, j)),
        ],
        out_specs=pl.BlockSpec((1, MOD_ROWS, tn), lambda l, j: (l, 0, j)),
        out_shape=jax.ShapeDtypeStruct((DEPTH, MOD_ROWS, n), jnp.float32),
        compiler_params=_params("parallel", "parallel"),
        name="ada_mod",
    )(cond, ada_w, ada_b.reshape(DEPTH, 1, n))


def _pair_epilogue(acc, spec, gains_ref, ones_ref, rope_refs, rope_half):
    norm, gain_row, nvalid, do_rope = spec
    y = acc
    if norm:
        ss = _dot(_bf(acc * acc), ones_ref[...])
        y = acc * lax.rsqrt(ss * (1.0 / nvalid) + EPS) * gains_ref[gain_row:gain_row + 1, :]
    if do_rope and rope_refs is not None:
        cos_ref, sa_ref, sb_ref = rope_refs
        y = (y * cos_ref[...]
             + pltpu.roll(y, PAIR - rope_half, axis=1) * sa_ref[...]
             + pltpu.roll(y, rope_half, axis=1) * sb_ref[...])
    return y


def _proj_kernel(*refs, mod, segs, seg_tiles, rope, rope_half):
    it = iter(refs)
    x_ref = next(it)
    if mod:
        g_ref, sh_ref, sc_ref = next(it), next(it), next(it)
    w_ref = next(it)
    gains_ref, ones_ref = next(it), next(it)
    rope_refs = (next(it), next(it), next(it)) if rope else None
    o_ref = next(it)
    h_ref = next(it) if mod else None
    j = pl.program_id(1)
    tn = o_ref.shape[1]

    if mod:
        @pl.when(j == 0)
        def _():
            h_ref[...] = _bf(_modulate(x_ref[...], g_ref[...], sh_ref[0], sc_ref[0]))
        a = h_ref[...]
    else:
        a = x_ref[...]
    seg = j // seg_tiles
    for s, spec in enumerate(segs):
        @pl.when(seg == s)
        def _(spec=spec):
            for c in range(0, tn, SUB_N):
                acc = _dot(a, w_ref[:, c:c + SUB_N])
                for p in range(0, SUB_N, PAIR):
                    y = _pair_epilogue(acc[:, p:p + PAIR], spec, gains_ref, ones_ref, rope_refs,
                                       rope_half)
                    o_ref[:, c + p:c + p + PAIR] = _bf(y)


def _proj_call(x, w, gains, segs, seg_width, *, mod=None, rope=None, mod_row=None,
               tm=TM, tn=TN, name="proj"):
    m, k = x.shape
    n = w.shape[1]
    tm = min(tm, m)
    tn = min(tn, seg_width)
    assert m % tm == 0 and seg_width % tn == 0 and n % seg_width == 0
    seg_tiles = seg_width // tn
    bpb = SEQ // tm

    in_specs = [pl.BlockSpec((tm, k), lambda i, j: (i, 0))]
    args = [x]
    if mod is not None:
        gain, modarr, sh_c, sc_c = mod
        in_specs += [
            pl.BlockSpec((1, k), lambda i, j: (0, 0)),
            pl.BlockSpec((1, 1, k), lambda i, j: (mod_row(i), 0, sh_c)),
            pl.BlockSpec((1, 1, k), lambda i, j: (mod_row(i), 0, sc_c)),
        ]
        args += [gain, modarr, modarr]
    in_specs.append(pl.BlockSpec((k, tn), lambda i, j: (0, j)))
    args.append(w)
    ones = _pair_blockdiag(np.ones((LANES, LANES), np.float32))
    in_specs += [pl.BlockSpec(gains.shape, lambda i, j: (0, 0)),
                 pl.BlockSpec(ones.shape, lambda i, j: (0, 0))]
    args += [gains, ones]
    if rope is not None:
        tables, rope_half = rope
        for t in tables:
            in_specs.append(pl.BlockSpec((tm, PAIR), lambda i, j: (i % bpb, 0)))
            args.append(t)
    else:
        rope_half = 0
    kern = functools.partial(_proj_kernel, mod=mod is not None, segs=tuple(segs),
                             seg_tiles=seg_tiles, rope=rope is not None, rope_half=rope_half)
    return pl.pallas_call(
        kern,
        grid=(m // tm, n // tn),
        in_specs=in_specs,
        out_specs=pl.BlockSpec((tm, tn), lambda i, j: (i, j)),
        out_shape=jax.ShapeDtypeStruct((m, n), jnp.bfloat16),
        scratch_shapes=[pltpu.VMEM((tm, k), jnp.bfloat16)] if mod is not None else [],
        compiler_params=_params("parallel", "arbitrary"),
        name=name,
    )(*args)


def _outproj_kernel(*refs, n_a):
    a_refs = refs[:n_a]
    w_ref, x_ref, g_ref, o_ref = refs[n_a:]
    n = o_ref.shape[1]
    a_vals = [a_ref[...] for a_ref in a_refs]
    for c in range(0, n, SUB_N):
        acc, row = None, 0
        for a in a_vals:
            part = _dot(a, w_ref[row:row + a.shape[1], c:c + SUB_N])
            acc = part if acc is None else acc + part
            row += a.shape[1]
        o_ref[:, c:c + SUB_N] = x_ref[:, c:c + SUB_N] + g_ref[0][:, c:c + SUB_N] * acc


def _outproj_call(a_list, w, x, modarr, gate_chunk, mod_row, *, tm=TM, name="outproj"):
    m, n = x.shape
    tm = min(tm, m)
    in_specs, args = [], []
    for a in a_list:
        in_specs.append(pl.BlockSpec((tm, a.shape[1]), lambda i: (i, 0)))
        args.append(a)
    in_specs += [
        pl.BlockSpec(w.shape, lambda i: (0, 0)),
        pl.BlockSpec((tm, n), lambda i: (i, 0)),
        pl.BlockSpec((1, 1, n), lambda i: (mod_row(i), 0, gate_chunk)),
    ]
    args += [w, x, modarr]
    return pl.pallas_call(
        functools.partial(_outproj_kernel, n_a=len(a_list)),
        grid=(m // tm,),
        in_specs=in_specs,
        out_specs=pl.BlockSpec((tm, n), lambda i: (i, 0)),
        out_shape=jax.ShapeDtypeStruct((m, n), jnp.float32),
        compiler_params=_params("parallel"),
        name=name,
    )(*args)


def _ffn_kernel(x_ref, xp_ref, xn_ref, g_ref, sh_ref, sc_ref, gt_ref, wa_ref, wb_ref,
                ca_ref, cb_ref, wo_ref, o_ref, h_ref, *, tm, hm, seq_len):
    i = pl.program_id(0)
    j = pl.program_id(1)
    nj = pl.num_programs(1)
    n_grp = tm // hm
    rows = hm + 2 * SUBLANES

    @pl.when(j == 0)
    def _():
        gain, shift, scale = g_ref[...], sh_ref[0], sc_ref[0]
        keep_prev = jnp.where((i * tm) % seq_len == 0, 0.0, 1.0)
        keep_next = jnp.where(((i + 1) * tm) % seq_len == 0, 0.0, 1.0)
        for r in range(n_grp):
            r0 = r * hm
            h_ref[r * rows:r * rows + hm, :] = _bf(
                _modulate(x_ref[r0:r0 + hm, :], gain, shift, scale))
            if r + 1 < n_grp:
                nxt = _modulate(x_ref[r0 + hm:r0 + hm + SUBLANES, :], gain, shift, scale)
            else:
                nxt = keep_next * _modulate(xn_ref[...], gain, shift, scale)
            if r > 0:
                prv = _modulate(x_ref[r0 - SUBLANES:r0, :], gain, shift, scale)
            else:
                prv = keep_prev * _modulate(xp_ref[...], gain, shift, scale)
            h_ref[r * rows + hm:(r + 1) * rows, :] = _bf(jnp.concatenate([nxt, prv], axis=0))
        o_ref[...] = jnp.zeros_like(o_ref)

    ca, cb = ca_ref[...], cb_ref[...]
    ups = []
    for r in range(n_grp):
        h = h_ref[r * rows:(r + 1) * rows, :]
        ups.append((_dot(h, wa_ref[...]), _dot(h, wb_ref[...])))

    def conv(u, cw):
        prev = pltpu.roll(u, 1, axis=0)[0:hm]
        nxt = pltpu.roll(u, rows - 1, axis=0)[0:hm]
        return prev * cw[0:1] + u[0:hm] * cw[1:2] + nxt * cw[2:3]

    for r, (ua, ub) in enumerate(ups):
        a = conv(ua, ca)
        b = conv(ub, cb)
        act = a * (1.0 / (1.0 + jnp.exp(-a))) * b
        o_ref[r * hm:(r + 1) * hm, :] += _dot(_bf(act), wo_ref[...])

    @pl.when(j == nj - 1)
    def _():
        o_ref[...] = x_ref[...] + gt_ref[0] * o_ref[...]


def _ffn_call(x, gain, modarr, w_in, conv_w, w_out, layer, mod_row, seq_len, *, tm=TM_FF,
              tn=TN_FF, name="conv_ffn"):
    m, d = x.shape
    tm = min(tm, seq_len)
    hm = min(HM_FF, tm)
    assert seq_len % tm == 0 and m % tm == 0 and tm % hm == 0
    nj = D_FF_PAD // tn
    hb = tm // SUBLANES
    last_hb = m // SUBLANES - 1
    kern = functools.partial(_ffn_kernel, tm=tm, hm=hm, seq_len=seq_len)
    return pl.pallas_call(
        kern,
        grid=(m // tm, nj),
        in_specs=[
            pl.BlockSpec((tm, d), lambda i, j: (i, 0)),
            pl.BlockSpec((SUBLANES, d), lambda i, j: (jnp.maximum(i * hb - 1, 0), 0)),
            pl.BlockSpec((SUBLANES, d), lambda i, j: (jnp.minimum((i + 1) * hb, last_hb), 0)),
            pl.BlockSpec((1, d), lambda i, j: (0, 0)),
            pl.BlockSpec((1, 1, d), lambda i, j: (mod_row(i), 0, 3)),
            pl.BlockSpec((1, 1, d), lambda i, j: (mod_row(i), 0, 4)),
            pl.BlockSpec((1, 1, d), lambda i, j: (mod_row(i), 0, 5)),
            pl.BlockSpec((None, d, tn), lambda i, j: (layer, 0, j)),
            pl.BlockSpec((None, d, tn), lambda i, j: (layer, 0, nj + j)),
            pl.BlockSpec((None, 3, tn), lambda i, j: (layer, 0, j)),
            pl.BlockSpec((None, 3, tn), lambda i, j: (layer, 0, nj + j)),
            pl.BlockSpec((None, tn, d), lambda i, j: (layer, j, 0)),
        ],
        out_specs=pl.BlockSpec((tm, d), lambda i, j: (i, 0)),
        out_shape=jax.ShapeDtypeStruct((m, d), jnp.float32),
        scratch_shapes=[
            pltpu.VMEM(((tm // hm) * (hm + 2 * SUBLANES), d), jnp.bfloat16),
        ],
        compiler_params=_params("parallel", "arbitrary"),
        name=name,
    )(x, x, x, gain, modarr, modarr, modarr, w_in, w_in, conv_w, conv_w, w_out)


def _na_group_geometry(g):
    r0 = g * NA_GROUP_ROWS
    return r0, min(max(r0 - NA_WIN_H // 2, 0), GRID_H - NA_KEY_ROWS)


def _na_build_bias(rpb_ref, h, tc_scr, bias_scr):
    n_dr = 2 * NA_WIN_H - 1
    n_dc = 2 * NA_WIN_W - 1
    qc = lax.broadcasted_iota(jnp.int32, (GRID_W, GRID_W), 0)
    kc = lax.broadcasted_iota(jnp.int32, (GRID_W, GRID_W), 1)
    col0 = jnp.clip(qc - NA_WIN_W // 2, 0, GRID_W - NA_WIN_W)
    col_valid = (kc >= col0) & (kc < col0 + NA_WIN_W)
    delta = kc - qc + (NA_WIN_W - 1)
    base = h * (n_dr * n_dc)
    for dr in range(n_dr):
        acc = jnp.zeros((GRID_W, GRID_W), jnp.float32)
        for e in range(n_dc):
            acc = jnp.where(delta == e, rpb_ref[base + dr * n_dc + e] * LOG2E, acc)
        tc_scr[dr] = jnp.where(col_valid, acc, NEG)
    tc_scr[n_dr] = jnp.full((GRID_W, GRID_W), NEG, jnp.float32)
    n_groups = GRID_H // NA_GROUP_ROWS
    for t, g in enumerate((0, 1, n_groups - 1)):
        r0, ks = _na_group_geometry(g)
        for qr in range(NA_GROUP_ROWS):
            r = r0 + qr
            row0 = min(max(r - NA_WIN_H // 2, 0), GRID_H - NA_WIN_H)
            pieces = []
            for kr in range(NA_KEY_ROWS):
                kra = ks + kr
                inside = row0 <= kra < row0 + NA_WIN_H
                pieces.append(tc_scr[kra - r + (NA_WIN_H - 1)] if inside else tc_scr[n_dr])
            bias_scr[t, qr * GRID_W:(qr + 1) * GRID_W, :] = jnp.concatenate(pieces, axis=1)


def _na_kernel(rpb_ref, q_ref, k_ref, v_ref, kc_ref, vc_ref, o_ref, tc_scr, bias_scr):
    @pl.when(pl.program_id(1) == 0)
    def _():
        _na_build_bias(rpb_ref, pl.program_id(0), tc_scr, bias_scr)

    kc = kc_ref[...]
    vc = vc_ref[...]
    n_groups = GRID_H // NA_GROUP_ROWS

    def body(g, carry):
        r0 = g * NA_GROUP_ROWS
        ks = jnp.clip(r0 - NA_WIN_H // 2, 0, GRID_H - NA_KEY_ROWS)
        q0 = pl.multiple_of(g * NA_Q, NA_Q)
        k0 = pl.multiple_of(ks * GRID_W, GRID_W)
        tb = jnp.where(g == 0, 0, jnp.where(g == n_groups - 1, 2, 1))
        q = q_ref[pl.ds(q0, NA_Q), :]
        kw = k_ref[pl.ds(k0, NA_K), :]
        vw = v_ref[pl.ds(k0, NA_K), :]
        sw = _dot_nt(q, kw) + bias_scr[tb]
        sc = _dot_nt(q, kc)
        mx = jnp.maximum(jnp.max(sw, axis=-1, keepdims=True), jnp.max(sc, axis=-1, keepdims=True))
        pw = jnp.exp2(sw - mx)
        pc = jnp.exp2(sc - mx)
        den = jnp.sum(pw, axis=-1, keepdims=True) + jnp.sum(pc, axis=-1, keepdims=True)
        o = _dot(_bf(pw), vw) + _dot(_bf(pc), vc)
        o_ref[pl.ds(q0, NA_Q), :] = _bf(o / den)
        return carry

    lax.fori_loop(0, n_groups, body, 0)


def _na_call(qkv, qkv_c, rpb):
    n_dr = 2 * NA_WIN_H - 1
    return pl.pallas_call(
        _na_kernel,
        grid=(NA_HEADS, BATCH),
        in_specs=[
            pl.BlockSpec(memory_space=pltpu.SMEM),
            pl.BlockSpec((SEQ, LANES), lambda h, b: (b, h)),
            pl.BlockSpec((SEQ, LANES), lambda h, b: (b, NA_HEADS + h)),
            pl.BlockSpec((SEQ, LANES), lambda h, b: (b, 2 * NA_HEADS + h)),
            pl.BlockSpec((CTX_LEN, LANES), lambda h, b: (b, NA_HEADS + h)),
            pl.BlockSpec((CTX_LEN, LANES), lambda h, b: (b, 2 * NA_HEADS + h)),
        ],
        out_specs=pl.BlockSpec((SEQ, LANES), lambda h, b: (b, h)),
        out_shape=jax.ShapeDtypeStruct((BATCH * SEQ, NA_WIDTH), jnp.bfloat16),
        scratch_shapes=[
            pltpu.VMEM((n_dr + 1, GRID_W, GRID_W), jnp.float32),
            pltpu.VMEM((3, NA_Q, NA_K), jnp.float32),
        ],
        compiler_params=_params("parallel", "arbitrary"),
        name="na_attn",
    )(rpb.reshape(-1), qkv, qkv, qkv, qkv_c, qkv_c)


def _key_chunks(src_rows, chunk):
    chunks, r = [], 0
    for rows in src_rows:
        size = min(chunk, rows)
        assert rows % size == 0
        chunks += [(r + c, size) for c in range(0, rows, size)]
        r += rows
    return chunks


def _pipelined_attention(streams, vt_scr, chunks):
    dv = vt_scr.shape[0]
    sub = streams[0][2].shape[1]
    outs, m_prev = [], None
    for u in range(len(streams) + 1):
        m = jnp.full((1, sub), NEG, jnp.float32)
        l = jnp.zeros((1, sub), jnp.float32)
        acc = jnp.zeros((dv, sub), jnp.float32)
        for c0, cs in chunks:
            if u < len(streams):
                q_t, k_scr, s_ref = streams[u]
                s = _dot(k_scr[c0:c0 + cs, :], q_t)
                s_ref[c0:c0 + cs, :] = s
                m = jnp.maximum(m, jnp.max(s, axis=0, keepdims=True))
            if u > 0:
                p = jnp.exp2(streams[u - 1][2][c0:c0 + cs, :] - m_prev)
                l = l + jnp.sum(p, axis=0, keepdims=True)
                acc = acc + _dot(vt_scr[:, c0:c0 + cs], _bf(p))
        if u > 0:
            outs.append(acc / l)
        m_prev = m
    return outs


def _transpose_bf16(x):
    return _bf(x.astype(jnp.float32).T)


def _attn_kernel(*refs, n_qparts, n_src, n_kparts, src_rows):
    it = iter(refs)
    q_refs = [next(it) for _ in range(n_qparts)]
    srcs = []
    for _ in range(n_src):
        k_refs = [next(it) for _ in range(n_kparts)]
        srcs.append((k_refs, next(it)))
    o_ref = next(it)
    k_scr, vt_scr = next(it), next(it)
    s_scrs = list(it)

    @pl.when(pl.program_id(2) == 0)
    def _():
        r = 0
        for (k_refs, v_ref), rows in zip(srcs, src_rows):
            for p, k_ref in enumerate(k_refs):
                k_scr[r:r + rows, p * LANES:(p + 1) * LANES] = k_ref[...]
            vt_scr[:, r:r + rows] = _transpose_bf16(v_ref[...])
            r += rows

    q = jnp.concatenate([qr[...] for qr in q_refs], axis=-1) if n_qparts > 1 else q_refs[0][...]
    sub = s_scrs[0].shape[1]
    streams = [(_transpose_bf16(q[u * sub:(u + 1) * sub]), k_scr, s_scr)
               for u, s_scr in enumerate(s_scrs)]
    outs = _pipelined_attention(streams, vt_scr, _key_chunks(src_rows, KEY_CHUNK_ATTN))
    for u, o_t in enumerate(outs):
        o_ref[u * sub:(u + 1) * sub, :] = _bf(o_t.T)


def _attn_call(q_parts, sources, n_heads, lq, tq, dv, name):
    tq = min(tq, lq)
    sub = min(TQ_SUB, tq)
    nq = lq // tq
    in_specs, args = [], []
    for arr, cf in q_parts:
        in_specs.append(pl.BlockSpec((tq, LANES), lambda b, h, i, cf=cf: (b * nq + i, cf(h))))
        args.append(arr)
    src_rows = []
    n_kparts = len(sources[0][1])
    for rows, k_parts, (v_arr, vcf) in sources:
        src_rows.append(rows)
        for arr, cf in k_parts:
            in_specs.append(pl.BlockSpec((rows, LANES), lambda b, h, i, cf=cf: (b, cf(h))))
            args.append(arr)
        in_specs.append(pl.BlockSpec((rows, dv), lambda b, h, i, cf=vcf: (b, cf(h))))
        args.append(v_arr)
    nk = sum(src_rows)
    kern = functools.partial(_attn_kernel, n_qparts=len(q_parts), n_src=len(sources),
                             n_kparts=n_kparts, src_rows=tuple(src_rows))
    return pl.pallas_call(
        kern,
        grid=(BATCH, n_heads, nq),
        in_specs=in_specs,
        out_specs=pl.BlockSpec((tq, dv), lambda b, h, i: (b * nq + i, h)),
        out_shape=jax.ShapeDtypeStruct((BATCH * lq, n_heads * dv), jnp.bfloat16),
        scratch_shapes=[
            pltpu.VMEM((nk, n_kparts * LANES), jnp.bfloat16),
            pltpu.VMEM((dv, nk), jnp.bfloat16),
        ] + [pltpu.VMEM((nk, sub), jnp.float32)] * (tq // sub),
        compiler_params=_params("parallel", "parallel", "arbitrary"),
        name=name,
    )(*args)


def _diff_kernel(*refs, n_src, src_rows, lam_init):
    it = iter(refs)
    q1_ref, q2_ref = next(it), next(it)
    srcs = [(next(it), next(it), next(it)) for _ in range(n_src)]
    lam_ref, sub_ref = next(it), next(it)
    o_ref = next(it)
    k1_scr, k2_scr, vt_scr, s1_scr, s2_scr = [next(it) for _ in range(5)]

    @pl.when(pl.program_id(2) == 0)
    def _():
        r = 0
        for (k1_ref, k2_ref, v_ref), rows in zip(srcs, src_rows):
            k1_scr[r:r + rows, :] = k1_ref[...]
            k2_scr[r:r + rows, :] = k2_ref[...]
            vt_scr[:, r:r + rows] = _transpose_bf16(v_ref[...])
            r += rows

    lf = lam_ref[...]
    lam = (jnp.exp(jnp.sum(lf[0:1] * lf[1:2], axis=-1, keepdims=True))
           - jnp.exp(jnp.sum(lf[2:3] * lf[3:4], axis=-1, keepdims=True)) + lam_init)
    chunks = _key_chunks(src_rows, KEY_CHUNK_DIFF)
    n_sub, _, sub = s1_scr.shape
    streams = []
    for u in range(n_sub):
        rows = slice(u * sub, (u + 1) * sub)
        streams.append((_transpose_bf16(q1_ref[rows, :]), k1_scr, s1_scr.at[u]))
        streams.append((_transpose_bf16(q2_ref[rows, :]), k2_scr, s2_scr.at[u]))
    outs = _pipelined_attention(streams, vt_scr, chunks)
    o_t = jnp.concatenate([outs[2 * u] - lam * outs[2 * u + 1] for u in range(n_sub)],
                          axis=1)
    o = o_t.T
    ms = jnp.mean(o * o, axis=-1, keepdims=True)
    o_ref[...] = _bf(o * lax.rsqrt(ms + EPS) * sub_ref[...] * (1.0 - lam_init))


def _diff_call(qkv_q, lq, sources, diff_lambda, subln, lam_init, tq, name):
    tq = min(tq, lq)
    sub = min(TQ_SUB, tq)
    nq = lq // tq
    qb0 = 3 * NA_WIDTH // LANES
    kb0 = (3 * NA_WIDTH + DIFF_WIDTH) // LANES
    vb0 = (3 * NA_WIDTH + 2 * DIFF_WIDTH) // (2 * LANES)
    in_specs = [
        pl.BlockSpec((tq, LANES), lambda b, h, i: (b * nq + i, qb0 + 2 * h)),
        pl.BlockSpec((tq, LANES), lambda b, h, i: (b * nq + i, qb0 + 2 * h + 1)),
    ]
    args = [qkv_q, qkv_q]
    src_rows = []
    for rows, arr in sources:
        src_rows.append(rows)
        in_specs += [
            pl.BlockSpec((rows, LANES), lambda b, h, i: (b, kb0 + 2 * h)),
            pl.BlockSpec((rows, LANES), lambda b, h, i: (b, kb0 + 2 * h + 1)),
            pl.BlockSpec((rows, 2 * LANES), lambda b, h, i: (b, vb0 + h)),
        ]
        args += [arr, arr, arr]
    in_specs += [
        pl.BlockSpec((4, LANES), lambda b, h, i: (0, 0)),
        pl.BlockSpec((1, 2 * LANES), lambda b, h, i: (0, 0)),
    ]
    args += [diff_lambda, subln]
    nk = sum(src_rows)
    kern = functools.partial(_diff_kernel, n_src=len(sources), src_rows=tuple(src_rows),
                             lam_init=lam_init)
    return pl.pallas_call(
        kern,
        grid=(BATCH, DIFF_HEADS, nq),
        in_specs=in_specs,
        out_specs=pl.BlockSpec((tq, 2 * LANES), lambda b, h, i: (b * nq + i, h)),
        out_shape=jax.ShapeDtypeStruct((BATCH * lq, DIFF_WIDTH), jnp.bfloat16),
        scratch_shapes=[
            pltpu.VMEM((nk, LANES), jnp.bfloat16),
            pltpu.VMEM((nk, LANES), jnp.bfloat16),
            pltpu.VMEM((2 * LANES, nk), jnp.bfloat16),
            pltpu.VMEM((tq // sub, nk, sub), jnp.float32),
            pltpu.VMEM((tq // sub, nk, sub), jnp.float32),
        ],
        compiler_params=_params("parallel", "parallel", "arbitrary"),
        name=name,
    )(*args)


def _rope_tables(d):
    h = d // 2
    half = h // 2
    t = jnp.arange(SEQ, dtype=jnp.int32)
    freqs = ROPE_BASE ** (-jnp.arange(half, dtype=jnp.float32) / half)

    def cs(pos):
        ang = pos.astype(jnp.float32)[:, None] * freqs[None, :]
        return jnp.cos(ang), jnp.sin(ang)

    cr, sr = cs(t // GRID_W)
    cc, sc = cs(t % GRID_W)
    cos = jnp.concatenate([cr, cr, cc, cc], -1)
    sin = jnp.concatenate([sr, sr, sc, sc], -1)
    first = (jnp.arange(d) % h) < half
    sin_a = jnp.where(first[None, :], -sin, 0.0)
    sin_b = jnp.where(first[None, :], 0.0, sin)
    pad = LANES - d
    if pad:
        cos = jnp.pad(cos, ((0, 0), (0, pad)), constant_values=1.0)
        sin_a = jnp.pad(sin_a, ((0, 0), (0, pad)))
        sin_b = jnp.pad(sin_b, ((0, 0), (0, pad)))
    pair_tables = tuple(jnp.concatenate([t, t], -1) for t in (cos, sin_a, sin_b))
    return (cos, sin_a, sin_b), half, (pair_tables, half)


def _pair_blockdiag(block):
    z = np.zeros_like(block)
    return jnp.asarray(np.block([[block, z], [z, block]]), dtype=jnp.bfloat16)


def _pad_lanes(v, fill=0.0):
    return jnp.pad(v, (0, LANES - v.shape[0]), constant_values=fill)


def _pair(v):
    return jnp.concatenate([v, v])


def _lat_row(tm):
    bpb = SEQ // tm
    return lambda i: i // bpb


def _ctx_row(i):
    return CTX_MOD_ROW


def _even_layer(x, xc, modarr, norm_mix, norm_ffn, ffn_w, w_in, w_out, qn_a, kn_a, rpb,
                qn_b, kn_b, diff_lambda, subln, lam_init, with_ctx):
    scale = HEAD_DIM ** -0.5 * LOG2E
    gains = jnp.stack([_pair(g) for g in (qn_a * scale, kn_a, qn_b * scale, kn_b)]
                      + [jnp.ones((PAIR,), jnp.float32)] * 4)
    _, _, rope = _rope_tables(HEAD_DIM)
    segs = [(True, 0, HEAD_DIM, False), (True, 1, HEAD_DIM, False), (False, 0, HEAD_DIM, False),
            (True, 2, HEAD_DIM, True), (True, 3, HEAD_DIM, True), (False, 0, HEAD_DIM, False)]
    w_in = _bf(w_in)
    w_out = _bf(w_out)
    gain_mix = norm_mix[None, :]
    lat_row = _lat_row(TM)

    qkv = _proj_call(x, w_in, gains, segs, NA_WIDTH, mod=(gain_mix, modarr, 0, 1), rope=rope,
                     mod_row=_lat_row(TM_EVEN), tm=TM_EVEN, name="even_proj")
    qkv_c = _proj_call(xc, w_in, gains, segs, NA_WIDTH, mod=(gain_mix, modarr, 0, 1), rope=None,
                       mod_row=_ctx_row, name="even_proj_ctx")

    oa = _na_call(qkv, qkv_c, rpb)
    ob = _diff_call(qkv, SEQ, [(SEQ, qkv), (CTX_LEN, qkv_c)], diff_lambda, subln[None, :],
                    lam_init, TQ_DIFF, "diff_attn")
    x = _outproj_call([oa, ob], w_out, x, modarr, 2, lat_row, name="even_out")
    x = _ffn_call(x, norm_ffn[None, :], modarr, *ffn_w, _lat_row(TM_FF), SEQ)
    if with_ctx:
        oa_c = _attn_call(
            [(qkv_c, lambda h: h)],
            [(CTX_LEN, [(qkv_c, lambda h: NA_HEADS + h)], (qkv_c, lambda h: 2 * NA_HEADS + h))],
            NA_HEADS, CTX_LEN, CTX_LEN, HEAD_DIM, "ctx_attn")
        ob_c = _diff_call(qkv_c, CTX_LEN, [(CTX_LEN, qkv_c)], diff_lambda, subln[None, :],
                          lam_init, CTX_LEN, "diff_attn_ctx")
        xc = _outproj_call([oa_c, ob_c], w_out, xc, modarr, 2, _ctx_row, name="even_out_ctx")
        xc = _ffn_call(xc, norm_ffn[None, :], modarr, *ffn_w, _ctx_row, CTX_LEN,
                       name="conv_ffn_ctx")
    return x, xc


def _odd_layer(x, xc, modarr, norm_mix, norm_ffn, ffn_w, w_down, q_a_norm, kv_a_norm, w_uq,
               w_ukv, qn_nope, qn_rope, kn_nope, kn_rope, w_out, with_ctx):
    scale = (MLA_NOPE + MLA_ROPE) ** -0.5 * LOG2E
    _, _, rope_pair = _rope_tables(MLA_ROPE)
    gain_mix = norm_mix[None, :]
    lat_row = _lat_row(TM)

    w_down_p = _bf(jnp.pad(w_down, ((0, 0), (0, MLA_DOWN_PAD - MLA_DOWN))))
    zeros_r = jnp.zeros((MLA_Q_RANK,), jnp.float32)
    kr_gain = jnp.concatenate([_pad_lanes(kn_rope), zeros_r[:MLA_Q_RANK - LANES]])
    gains_d = jnp.stack([q_a_norm, kv_a_norm, kr_gain] + [zeros_r] * 5)
    wq = w_uq.reshape(MLA_Q_RANK, MLA_HEADS, MLA_NOPE + MLA_ROPE)
    wq_rope = jnp.pad(wq[:, :, MLA_NOPE:], ((0, 0), (0, 0), (0, LANES - MLA_ROPE)))
    wq_p = _bf(jnp.concatenate([wq[:, :, :MLA_NOPE].reshape(MLA_Q_RANK, -1),
                                wq_rope.reshape(MLA_Q_RANK, -1)], -1))
    wkv = w_ukv.reshape(MLA_KV_RANK, MLA_HEADS, MLA_NOPE + MLA_V)
    wkv_p = _bf(jnp.concatenate([wkv[:, :, :MLA_NOPE].reshape(MLA_KV_RANK, -1),
                                 wkv[:, :, MLA_NOPE:].reshape(MLA_KV_RANK, -1)], -1))
    gains_p = jnp.stack([_pair(qn_nope * scale), _pair(_pad_lanes(qn_rope) * scale),
                         _pair(kn_nope)] + [jnp.zeros((PAIR,), jnp.float32)] * 5)
    mod_in = (gain_mix, modarr, 0, 1)
    q, kv, kr = _mla_proj_call(x, w_down_p, wq_p, wkv_p, gains_d, gains_p, mod_in, rope_pair,
                               _lat_row(TM_MLA_PROJ), True, "mla_proj")
    kv_c, kr_c = _mla_proj_call(xc, w_down_p, wq_p, wkv_p, gains_d, gains_p, mod_in, None,
                                _ctx_row, False, "mla_proj_ctx")

    def src(rows, kv_arr, kr_arr):
        return (rows, [(kv_arr, lambda h: h), (kr_arr, lambda h: 0)],
                (kv_arr, lambda h: MLA_HEADS + h))

    o = _attn_call([(q, lambda h: h), (q, lambda h: MLA_HEADS + h)],
                   [src(SEQ, kv, kr), src(CTX_LEN, kv_c, kr_c)],
                   MLA_HEADS, SEQ, TQ_MLA, MLA_V, "mla_attn")
    x = _outproj_call([o], _bf(w_out), x, modarr, 2, lat_row, name="mla_out")
    x = _ffn_call(x, norm_ffn[None, :], modarr, *ffn_w, _lat_row(TM_FF), SEQ)
    assert not with_ctx
    return x, xc


def _mla_proj_kernel(x_ref, g_ref, sh_ref, sc_ref, wd_ref, wq_ref, wkv_ref, gd_ref, gp_ref,
                     ones_ref, *rest, rope, rope_half, with_q):
    rope_refs = rest[:3] if rope else None
    outs = rest[3:] if rope else rest
    if with_q:
        q_ref, kv_ref, kr_ref = outs
    else:
        kv_ref, kr_ref = outs
    h = _bf(_modulate(x_ref[...], g_ref[...], sh_ref[0], sc_ref[0]))
    down = _dot(h, wd_ref[...])

    def latent_norm(xs, row, nvalid):
        ms = jnp.sum(xs * xs, axis=-1, keepdims=True) * (1.0 / nvalid)
        return xs * lax.rsqrt(ms + EPS) * gd_ref[row:row + 1, 0:xs.shape[1]]

    kva = _bf(latent_norm(down[:, MLA_Q_RANK:MLA_Q_RANK + MLA_KV_RANK], 1, MLA_KV_RANK))
    kr = latent_norm(down[:, MLA_Q_RANK + MLA_KV_RANK:MLA_DOWN_PAD], 2, MLA_ROPE)
    if rope:
        cos_ref, sa_ref, sb_ref = rope_refs
        kr = (kr * cos_ref[:, 0:LANES]
              + pltpu.roll(kr, LANES - rope_half, axis=1) * sa_ref[:, 0:LANES]
              + pltpu.roll(kr, rope_half, axis=1) * sb_ref[:, 0:LANES])
    kr_ref[...] = _bf(kr)

    def up(a, w_ref, o_ref, specs):
        width = w_ref.shape[1] // len(specs)
        for c in range(0, w_ref.shape[1], SUB_N):
            acc = _dot(a, w_ref[:, c:c + SUB_N])
            for p in range(0, SUB_N, PAIR):
                y = _pair_epilogue(acc[:, p:p + PAIR], specs[c // width], gp_ref, ones_ref,
                                   rope_refs, rope_half)
                o_ref[:, c + p:c + p + PAIR] = _bf(y)

    up(kva, wkv_ref, kv_ref, [(True, 2, MLA_NOPE, False), (False, 0, MLA_V, False)])
    if with_q:
        qa = _bf(latent_norm(down[:, 0:MLA_Q_RANK], 0, MLA_Q_RANK))
        up(qa, wq_ref, q_ref, [(True, 0, MLA_NOPE, False), (True, 1, MLA_ROPE, True)])


def _mla_proj_call(x, wd, wq, wkv, gains_d, gains_p, mod, rope, mod_row, with_q, name,
                   tm=TM_MLA_PROJ):
    m, k = x.shape
    tm = min(tm, m)
    bpb = SEQ // tm
    gain, modarr, sh_c, sc_c = mod
    ones = _pair_blockdiag(np.ones((LANES, LANES), np.float32))
    in_specs = [
        pl.BlockSpec((tm, k), lambda i: (i, 0)),
        pl.BlockSpec((1, k), lambda i: (0, 0)),
        pl.BlockSpec((1, 1, k), lambda i: (mod_row(i), 0, sh_c)),
        pl.BlockSpec((1, 1, k), lambda i: (mod_row(i), 0, sc_c)),
    ]
    args = [x, gain, modarr, modarr]
    for w in (wd, wq, wkv, gains_d, gains_p, ones):
        in_specs.append(pl.BlockSpec(w.shape, lambda i: (0, 0)))
        args.append(w)
    rope_half = 0
    if rope is not None:
        tables, rope_half = rope
        for t in tables:
            in_specs.append(pl.BlockSpec((tm, PAIR), lambda i: (i % bpb, 0)))
            args.append(t)
    widths = ([wq.shape[1]] if with_q else []) + [wkv.shape[1], LANES]
    return pl.pallas_call(
        functools.partial(_mla_proj_kernel, rope=rope is not None, rope_half=rope_half,
                          with_q=with_q),
        grid=(m // tm,),
        in_specs=in_specs,
        out_specs=[pl.BlockSpec((tm, n), lambda i: (i, 0)) for n in widths],
        out_shape=[jax.ShapeDtypeStruct((m, n), jnp.bfloat16) for n in widths],
        compiler_params=_params("parallel"),
        name=name,
    )(*args)


def _w_in_prep_kernel(x_ref, o_ref):
    pad = jnp.zeros((x_ref.shape[0], D_FF_PAD - D_FF), jnp.bfloat16)
    o_ref[:, 0:D_FF] = _bf(x_ref[:, 0:D_FF])
    o_ref[:, D_FF:D_FF_PAD] = pad
    o_ref[:, D_FF_PAD:D_FF_PAD + D_FF] = _bf(x_ref[:, D_FF:2 * D_FF])
    o_ref[:, D_FF_PAD + D_FF:2 * D_FF_PAD] = pad


def _w_out_prep_kernel(x_ref, o_ref):
    o_ref[0:D_FF, :] = _bf(x_ref[...])
    o_ref[D_FF:D_FF_PAD, :] = jnp.zeros((D_FF_PAD - D_FF, x_ref.shape[1]), jnp.bfloat16)


def _ffn_weights(w_in, conv_w, w_out):
    depth, d, _ = w_in.shape
    rows, cols = 256, 256
    w_in_p = pl.pallas_call(
        _w_in_prep_kernel,
        grid=(depth, d // rows),
        in_specs=[pl.BlockSpec((None, rows, 2 * D_FF), lambda l, i: (l, i, 0))],
        out_specs=pl.BlockSpec((None, rows, 2 * D_FF_PAD), lambda l, i: (l, i, 0)),
        out_shape=jax.ShapeDtypeStruct((depth, d, 2 * D_FF_PAD), jnp.bfloat16),
        compiler_params=_params("parallel", "parallel"),
        name="ffn_w_in_prep",
    )(w_in)
    w_out_p = pl.pallas_call(
        _w_out_prep_kernel,
        grid=(depth, d // cols),
        in_specs=[pl.BlockSpec((None, D_FF, cols), lambda l, j: (l, 0, j))],
        out_specs=pl.BlockSpec((None, D_FF_PAD, cols), lambda l, j: (l, 0, j)),
        out_shape=jax.ShapeDtypeStruct((depth, D_FF_PAD, d), jnp.bfloat16),
        compiler_params=_params("parallel", "parallel"),
        name="ffn_w_out_prep",
    )(w_out)
    pad = D_FF_PAD - D_FF
    zc = jnp.zeros(conv_w.shape[:2] + (pad,), conv_w.dtype)
    conv_p = jnp.concatenate([conv_w[..., :D_FF], zc, conv_w[..., D_FF:], zc], -1)
    return w_in_p, conv_p, w_out_p


def kernel(x, c, ctx, c_ctx, ada_w, ada_b, norm_mix, norm_ffn, ffn_w_in, ffn_conv, ffn_w_out, even_w_in, even_w_out, na_q_norm, na_k_norm, na_rpb, diff_q_norm, diff_k_norm, diff_lambda, diff_subln, mla_w_down, mla_q_a_norm, mla_kv_a_norm, mla_w_uq, mla_w_ukv, mla_q_nope_norm, mla_q_rope_norm, mla_k_nope_norm, mla_k_rope_norm, mla_w_out):
    cond = jnp.concatenate(
        [c, c_ctx[None, :], jnp.zeros((MOD_ROWS - BATCH - 1, D_MODEL), jnp.float32)], 0)
    mod = _ada_call(cond, ada_w, ada_b)
    xl = x.reshape(BATCH * SEQ, D_MODEL)
    xc = ctx.reshape(BATCH * CTX_LEN, D_MODEL)
    ffn_all = _ffn_weights(ffn_w_in, ffn_conv, ffn_w_out)
    for l in range(DEPTH):
        with_ctx = l < DEPTH - 1
        modarr = mod[l].reshape(MOD_ROWS, 1, 6 * D_MODEL)
        ffn_w = ffn_all + (l,)
        i = l // 2
        if l % 2 == 0:
            lam_init = 0.8 - 0.6 * math.exp(-0.3 * l)
            xl, xc = _even_layer(xl, xc, modarr, norm_mix[l], norm_ffn[l], ffn_w, even_w_in[i],
                                 even_w_out[i], na_q_norm[i], na_k_norm[i], na_rpb[i],
                                 diff_q_norm[i], diff_k_norm[i], diff_lambda[i], diff_subln[i],
                                 lam_init, with_ctx)
        else:
            xl, xc = _odd_layer(xl, xc, modarr, norm_mix[l], norm_ffn[l], ffn_w, mla_w_down[i],
                                mla_q_a_norm[i], mla_kv_a_norm[i], mla_w_uq[i], mla_w_ukv[i],
                                mla_q_nope_norm[i], mla_q_rope_norm[i], mla_k_nope_norm[i],
                                mla_k_rope_norm[i], mla_w_out[i], with_ctx)
    return xl.reshape(BATCH, SEQ, D_MODEL)
```

```python
import functools
import math

import numpy as np
import jax
import jax.numpy as jnp
from jax import lax
from jax.experimental import pallas as pl
from jax.experimental.pallas import tpu as pltpu

D_MODEL = 2048
BATCH = 4
SEQ = 4096
DEPTH = 2
GRID_W = 64
GRID_H = SEQ // GRID_W
CTX_LEN = 256
HEAD_DIM = 128
NA_HEADS = 8
NA_WIN_H = 8
NA_WIN_W = 16
DIFF_HEADS = 4
NA_WIDTH = 1024
DIFF_WIDTH = 1024
EVEN_PROJ = 6144
MLA_HEADS = 16
MLA_Q_RANK = 512
MLA_KV_RANK = 512
MLA_NOPE = 128
MLA_ROPE = 64
MLA_V = 128
MLA_DOWN = 1088
D_FF = 5504
ROPE_BASE = 10000.0
EPS = 1e-6

LANES = 128
SUBLANES = 8
MOD_ROWS = 8
CTX_MOD_ROW = BATCH
D_FF_PAD = 5632
MLA_DOWN_PAD = 1152
VMEM_LIMIT = 56 * 1024 * 1024

TM = 512
TN = 1024
TN_FF = 512
TM_FF = 1024
TM_EVEN = 1024
TM_MLA_PROJ = 256
HM_FF = 512
SUB_N = 512
PAIR = 2 * LANES
NA_GROUP_ROWS = 4
NA_KEY_ROWS = 12
NA_UNROLL = 2
NA_Q = NA_GROUP_ROWS * GRID_W
NA_K = NA_KEY_ROWS * GRID_W
TQ_DIFF = 512
TQ_MLA = 1024
TQ_SUB = 256
KEY_CHUNK_ATTN = 256
KEY_CHUNK_DIFF = 512
NEG = -1e30
LOG2E = math.log2(math.e)

_NT = (((1,), (1,)), ((), ()))


def _params(*sem):
    return pltpu.CompilerParams(dimension_semantics=sem, vmem_limit_bytes=VMEM_LIMIT)


def _bf(x):
    return x.astype(jnp.bfloat16)


def _dot(a, b):
    return jnp.dot(a, b, preferred_element_type=jnp.float32)


def _dot_nt(a, b):
    return lax.dot_general(a, b, _NT, preferred_element_type=jnp.float32)


def _modulate(x, gain, shift, scale):
    ms = jnp.mean(x * x, axis=-1, keepdims=True)
    y = x * lax.rsqrt(ms + EPS) * gain
    return y * (1.0 + scale) + shift


def _ada_kernel(c_ref, w_ref, b_ref, o_ref):
    c = c_ref[...]
    s = c * (1.0 / (1.0 + jnp.exp(-c)))
    o_ref[0] = _dot(_bf(s), _bf(w_ref[0])) + b_ref[0]


def _ada_call(cond, ada_w, ada_b):
    tn = 512
    n = ada_w.shape[-1]
    return pl.pallas_call(
        _ada_kernel,
        grid=(DEPTH, n // tn),
        in_specs=[
            pl.BlockSpec((MOD_ROWS, D_MODEL), lambda l, j: (0, 0)),
            pl.BlockSpec((1, D_MODEL, tn), lambda l, j: (l, 0, j)),
            pl.BlockSpec((1, 1, tn), lambda l, j: (l, 0, j)),
        ],
        out_specs=pl.BlockSpec((1, MOD_ROWS, tn), lambda l, j: (l, 0, j)),
        out_shape=jax.ShapeDtypeStruct((DEPTH, MOD_ROWS, n), jnp.float32),
        compiler_params=_params("parallel", "parallel"),
        name="ada_mod",
    )(cond, ada_w, ada_b.reshape(DEPTH, 1, n))


def _pair_epilogue(acc, spec, gains_ref, ones_ref, rope_refs, rope_half):
    norm, gain_row, nvalid, do_rope = spec
    y = acc
    if norm:
        ss = _dot(_bf(acc * acc), ones_ref[...])
        y = acc * lax.rsqrt(ss * (1.0 / nvalid) + EPS) * gains_ref[gain_row:gain_row + 1, :]
    if do_rope and rope_refs is not None:
        cos_ref, sa_ref, sb_ref = rope_refs
        y = (y * cos_ref[...]
             + pltpu.roll(y, PAIR - rope_half, axis=1) * sa_ref[...]
             + pltpu.roll(y, rope_half, axis=1) * sb_ref[...])
    return y


def _proj_kernel(*refs, mod, segs, seg_tiles, rope, rope_half):
    it = iter(refs)
    x_ref = next(it)
    if mod:
        g_ref, sh_ref, sc_ref = next(it), next(it), next(it)
    w_ref = next(it)
    gains_ref, ones_ref = next(it), next(it)
    rope_refs = (next(it), next(it), next(it)) if rope else None
    o_ref = next(it)
    h_ref = next(it) if mod else None
    j = pl.program_id(1)
    tn = o_ref.shape[1]

    if mod:
        @pl.when(j == 0)
        def _():
            h_ref[...] = _bf(_modulate(x_ref[...], g_ref[...], sh_ref[0], sc_ref[0]))
        a = h_ref[...]
    else:
        a = x_ref[...]
    seg = j // seg_tiles
    for s, spec in enumerate(segs):
        @pl.when(seg == s)
        def _(spec=spec):
            for c in range(0, tn, SUB_N):
                acc = _dot(a, w_ref[:, c:c + SUB_N])
                for p in range(0, SUB_N, PAIR):
                    y = _pair_epilogue(acc[:, p:p + PAIR], spec, gains_ref, ones_ref, rope_refs,
                                       rope_half)
                    o_ref[:, c + p:c + p + PAIR] = _bf(y)


def _proj_call(x, w, gains, segs, seg_width, *, mod=None, rope=None, mod_row=None,
               tm=TM, tn=TN, name="proj"):
    m, k = x.shape
    n = w.shape[1]
    tm = min(tm, m)
    tn = min(tn, seg_width)
    assert m % tm == 0 and seg_width % tn == 0 and n % seg_width == 0
    seg_tiles = seg_width // tn
    bpb = SEQ // tm

    in_specs = [pl.BlockSpec((tm, k), lambda i, j: (i, 0))]
    args = [x]
    if mod is not None:
        gain, modarr, sh_c, sc_c = mod
        in_specs += [
            pl.BlockSpec((1, k), lambda i, j: (0, 0)),
            pl.BlockSpec((1, 1, k), lambda i, j: (mod_row(i), 0, sh_c)),
            pl.BlockSpec((1, 1, k), lambda i, j: (mod_row(i), 0, sc_c)),
        ]
        args += [gain, modarr, modarr]
    in_specs.append(pl.BlockSpec((k, tn), lambda i, j: (0, j)))
    args.append(w)
    ones = _pair_blockdiag(np.ones((LANES, LANES), np.float32))
    in_specs += [pl.BlockSpec(gains.shape, lambda i, j: (0, 0)),
                 pl.BlockSpec(ones.shape, lambda i, j: (0, 0))]
    args += [gains, ones]
    if rope is not None:
        tables, rope_half = rope
        for t in tables:
            in_specs.append(pl.BlockSpec((tm, PAIR), lambda i, j: (i % bpb, 0)))
            args.append(t)
    else:
        rope_half = 0
    kern = functools.partial(_proj_kernel, mod=mod is not None, segs=tuple(segs),
                             seg_tiles=seg_tiles, rope=rope is not None, rope_half=rope_half)
    return pl.pallas_call(
        kern,
        grid=(m // tm, n // tn),
        in_specs=in_specs,
        out_specs=pl.BlockSpec((tm, tn), lambda i, j: (i, j)),
        out_shape=jax.ShapeDtypeStruct((m, n), jnp.bfloat16),
        scratch_shapes=[pltpu.VMEM((tm, k), jnp.bfloat16)] if mod is not None else [],
        compiler_params=_params("parallel", "arbitrary"),
        name=name,
    )(*args)


def _outproj_kernel(*refs, n_a):
    a_refs = refs[:n_a]
    w_ref, x_ref, g_ref, o_ref = refs[n_a:]
    n = o_ref.shape[1]
    a_vals = [a_ref[...] for a_ref in a_refs]
    for c in range(0, n, SUB_N):
        acc, row = None, 0
        for a in a_vals:
            part = _dot(a, w_ref[row:row + a.shape[1], c:c + SUB_N])
            acc = part if acc is None else acc + part
            row += a.shape[1]
        o_ref[:, c:c + SUB_N] = x_ref[:, c:c + SUB_N] + g_ref[0][:, c:c + SUB_N] * acc


def _outproj_call(a_list, w, x, modarr, gate_chunk, mod_row, *, tm=TM, name="outproj"):
    m, n = x.shape
    tm = min(tm, m)
    in_specs, args = [], []
    for a in a_list:
        in_specs.append(pl.BlockSpec((tm, a.shape[1]), lambda i: (i, 0)))
        args.append(a)
    in_specs += [
        pl.BlockSpec(w.shape, lambda i: (0, 0)),
        pl.BlockSpec((tm, n), lambda i: (i, 0)),
        pl.BlockSpec((1, 1, n), lambda i: (mod_row(i), 0, gate_chunk)),
    ]
    args += [w, x, modarr]
    return pl.pallas_call(
        functools.partial(_outproj_kernel, n_a=len(a_list)),
        grid=(m // tm,),
        in_specs=in_specs,
        out_specs=pl.BlockSpec((tm, n), lambda i: (i, 0)),
        out_shape=jax.ShapeDtypeStruct((m, n), jnp.float32),
        compiler_params=_params("parallel"),
        name=name,
    )(*args)


def _ffn_kernel(x_ref, xp_ref, xn_ref, g_ref, sh_ref, sc_ref, gt_ref, wa_ref, wb_ref,
                ca_ref, cb_ref, wo_ref, o_ref, h_ref, *, tm, hm, seq_len):
    i = pl.program_id(0)
    j = pl.program_id(1)
    nj = pl.num_programs(1)
    n_grp = tm // hm
    rows = hm + 2 * SUBLANES

    @pl.when(j == 0)
    def _():
        gain, shift, scale = g_ref[...], sh_ref[0], sc_ref[0]
        for r in range(n_grp):
            r0 = r * hm
            h_ref[r * rows:r * rows + hm, :] = _bf(
                _modulate(x_ref[r0:r0 + hm, :], gain, shift, scale))
            nxt_src = x_ref[r0 + hm:r0 + hm + SUBLANES, :] if r + 1 < n_grp else xn_ref[...]
            prv_src = x_ref[r0 - SUBLANES:r0, :] if r > 0 else xp_ref[...]
            keep_next = jnp.where((i * tm + r0 + hm) % seq_len == 0, 0.0, 1.0)
            keep_prev = jnp.where((i * tm + r0) % seq_len == 0, 0.0, 1.0)
            nxt = keep_next * _modulate(nxt_src, gain, shift, scale)
            prv = keep_prev * _modulate(prv_src, gain, shift, scale)
            h_ref[r * rows + hm:(r + 1) * rows, :] = _bf(jnp.concatenate([nxt, prv], axis=0))
        o_ref[...] = jnp.zeros_like(o_ref)

    ca, cb = ca_ref[...], cb_ref[...]
    ups = []
    for r in range(n_grp):
        h = h_ref[r * rows:(r + 1) * rows, :]
        ups.append((_dot(h, wa_ref[...]), _dot(h, wb_ref[...])))

    def conv(u, cw):
        prev = pltpu.roll(u, 1, axis=0)[0:hm]
        nxt = pltpu.roll(u, rows - 1, axis=0)[0:hm]
        return prev * cw[0:1] + u[0:hm] * cw[1:2] + nxt * cw[2:3]

    for r, (ua, ub) in enumerate(ups):
        a = conv(ua, ca)
        b = conv(ub, cb)
        act = a * (1.0 / (1.0 + jnp.exp(-a))) * b
        o_ref[r * hm:(r + 1) * hm, :] += _dot(_bf(act), wo_ref[...])

    @pl.when(j == nj - 1)
    def _():
        o_ref[...] = x_ref[...] + gt_ref[0] * o_ref[...]


def _ffn_call(x, gain, modarr, w_in, conv_w, w_out, layer, mod_row, seq_len, *, tm=TM_FF,
              tn=TN_FF, name="conv_ffn"):
    m, d = x.shape
    tm = min(tm, m)
    hm = min(HM_FF, seq_len)
    assert m % tm == 0 and tm % hm == 0 and seq_len % hm == 0
    nj = D_FF_PAD // tn
    hb = tm // SUBLANES
    last_hb = m // SUBLANES - 1
    kern = functools.partial(_ffn_kernel, tm=tm, hm=hm, seq_len=seq_len)
    return pl.pallas_call(
        kern,
        grid=(m // tm, nj),
        in_specs=[
            pl.BlockSpec((tm, d), lambda i, j: (i, 0)),
            pl.BlockSpec((SUBLANES, d), lambda i, j: (jnp.maximum(i * hb - 1, 0), 0)),
            pl.BlockSpec((SUBLANES, d), lambda i, j: (jnp.minimum((i + 1) * hb, last_hb), 0)),
            pl.BlockSpec((1, d), lambda i, j: (0, 0)),
            pl.BlockSpec((1, 1, d), lambda i, j: (mod_row(i), 0, 3)),
            pl.BlockSpec((1, 1, d), lambda i, j: (mod_row(i), 0, 4)),
            pl.BlockSpec((1, 1, d), lambda i, j: (mod_row(i), 0, 5)),
            pl.BlockSpec((None, d, tn), lambda i, j: (layer, 0, j)),
            pl.BlockSpec((None, d, tn), lambda i, j: (layer, 0, nj + j)),
            pl.BlockSpec((None, 3, tn), lambda i, j: (layer, 0, j)),
            pl.BlockSpec((None, 3, tn), lambda i, j: (layer, 0, nj + j)),
            pl.BlockSpec((None, tn, d), lambda i, j: (layer, j, 0)),
        ],
        out_specs=pl.BlockSpec((tm, d), lambda i, j: (i, 0)),
        out_shape=jax.ShapeDtypeStruct((m, d), jnp.float32),
        scratch_shapes=[
            pltpu.VMEM(((tm // hm) * (hm + 2 * SUBLANES), d), jnp.bfloat16),
        ],
        compiler_params=_params("parallel", "arbitrary"),
        name=name,
    )(x, x, x, gain, modarr, modarr, modarr, w_in, w_in, conv_w, conv_w, w_out)


def _na_group_geometry(g):
    r0 = g * NA_GROUP_ROWS
    return r0, min(max(r0 - NA_WIN_H // 2, 0), GRID_H - NA_KEY_ROWS)


def _na_build_bias(rpb_ref, h, tc_scr, bias_scr):
    n_dr = 2 * NA_WIN_H - 1
    n_dc = 2 * NA_WIN_W - 1
    qc = lax.broadcasted_iota(jnp.int32, (GRID_W, GRID_W), 0)
    kc = lax.broadcasted_iota(jnp.int32, (GRID_W, GRID_W), 1)
    col0 = jnp.clip(qc - NA_WIN_W // 2, 0, GRID_W - NA_WIN_W)
    col_valid = (kc >= col0) & (kc < col0 + NA_WIN_W)
    delta = kc - qc + (NA_WIN_W - 1)
    base = h * (n_dr * n_dc)
    for dr in range(n_dr):
        acc = jnp.zeros((GRID_W, GRID_W), jnp.float32)
        for e in range(n_dc):
            acc = jnp.where(delta == e, rpb_ref[base + dr * n_dc + e] * LOG2E, acc)
        tc_scr[dr] = jnp.where(col_valid, acc, NEG)
    tc_scr[n_dr] = jnp.full((GRID_W, GRID_W), NEG, jnp.float32)
    n_groups = GRID_H // NA_GROUP_ROWS
    for t, g in enumerate((0, 1, n_groups - 1)):
        r0, ks = _na_group_geometry(g)
        for qr in range(NA_GROUP_ROWS):
            r = r0 + qr
            row0 = min(max(r - NA_WIN_H // 2, 0), GRID_H - NA_WIN_H)
            pieces = []
            for kr in range(NA_KEY_ROWS):
                kra = ks + kr
                inside = row0 <= kra < row0 + NA_WIN_H
                pieces.append(tc_scr[kra - r + (NA_WIN_H - 1)] if inside else tc_scr[n_dr])
            bias_scr[t, qr * GRID_W:(qr + 1) * GRID_W, :] = jnp.concatenate(pieces, axis=1)


def _na_kernel(rpb_ref, q_ref, k_ref, v_ref, kc_ref, vc_ref, o_ref, tc_scr, bias_scr):
    @pl.when(pl.program_id(1) == 0)
    def _():
        _na_build_bias(rpb_ref, pl.program_id(0), tc_scr, bias_scr)

    kc = kc_ref[...]
    vc = vc_ref[...]
    n_groups = GRID_H // NA_GROUP_ROWS

    def group(g):
        r0 = g * NA_GROUP_ROWS
        ks = jnp.clip(r0 - NA_WIN_H // 2, 0, GRID_H - NA_KEY_ROWS)
        q0 = pl.multiple_of(g * NA_Q, NA_Q)
        k0 = pl.multiple_of(ks * GRID_W, GRID_W)
        tb = jnp.where(g == 0, 0, jnp.where(g == n_groups - 1, 2, 1))
        q = q_ref[pl.ds(q0, NA_Q), :]
        kw = k_ref[pl.ds(k0, NA_K), :]
        vw = v_ref[pl.ds(k0, NA_K), :]
        sw = _dot_nt(q, kw) + bias_scr[tb]
        sc = _dot_nt(q, kc)
        mx = jnp.maximum(jnp.max(sw, axis=-1, keepdims=True), jnp.max(sc, axis=-1, keepdims=True))
        pw = jnp.exp2(sw - mx)
        pc = jnp.exp2(sc - mx)
        den = jnp.sum(pw, axis=-1, keepdims=True) + jnp.sum(pc, axis=-1, keepdims=True)
        o = _dot(_bf(pw), vw) + _dot(_bf(pc), vc)
        o_ref[pl.ds(q0, NA_Q), :] = _bf(o / den)

    def body(t, carry):
        for u in range(NA_UNROLL):
            group(t * NA_UNROLL + u)
        return carry

    lax.fori_loop(0, n_groups // NA_UNROLL, body, 0)


def _na_call(qkv, qkv_c, rpb):
    n_dr = 2 * NA_WIN_H - 1
    return pl.pallas_call(
        _na_kernel,
        grid=(NA_HEADS, BATCH),
        in_specs=[
            pl.BlockSpec(memory_space=pltpu.SMEM),
            pl.BlockSpec((SEQ, LANES), lambda h, b: (b, h)),
            pl.BlockSpec((SEQ, LANES), lambda h, b: (b, NA_HEADS + h)),
            pl.BlockSpec((SEQ, LANES), lambda h, b: (b, 2 * NA_HEADS + h)),
            pl.BlockSpec((CTX_LEN, LANES), lambda h, b: (b, NA_HEADS + h)),
            pl.BlockSpec((CTX_LEN, LANES), lambda h, b: (b, 2 * NA_HEADS + h)),
        ],
        out_specs=pl.BlockSpec((SEQ, LANES), lambda h, b: (b, h)),
        out_shape=jax.ShapeDtypeStruct((BATCH * SEQ, NA_WIDTH), jnp.bfloat16),
        scratch_shapes=[
            pltpu.VMEM((n_dr + 1, GRID_W, GRID_W), jnp.float32),
            pltpu.VMEM((3, NA_Q, NA_K), jnp.float32),
        ],
        compiler_params=_params("parallel", "arbitrary"),
        name="na_attn",
    )(rpb.reshape(-1), qkv, qkv, qkv, qkv_c, qkv_c)


def _key_chunks(src_rows, chunk):
    chunks, r = [], 0
    for rows in src_rows:
        size = min(chunk, rows)
        assert rows % size == 0
        chunks += [(r + c, size) for c in range(0, rows, size)]
        r += rows
    return chunks


def _pipelined_attention(streams, vt_scr, chunks):
    dv = vt_scr.shape[0]
    sub = streams[0][2].shape[1]
    outs, m_prev = [], None
    for u in range(len(streams) + 1):
        m = jnp.full((1, sub), NEG, jnp.float32)
        l = jnp.zeros((1, sub), jnp.float32)
        acc = jnp.zeros((dv, sub), jnp.float32)
        for c0, cs in chunks:
            if u < len(streams):
                q_t, k_scr, s_ref = streams[u]
                s = _dot(k_scr[c0:c0 + cs, :], q_t)
                s_ref[c0:c0 + cs, :] = s
                m = jnp.maximum(m, jnp.max(s, axis=0, keepdims=True))
            if u > 0:
                p = jnp.exp2(streams[u - 1][2][c0:c0 + cs, :] - m_prev)
                l = l + jnp.sum(p, axis=0, keepdims=True)
                acc = acc + _dot(vt_scr[:, c0:c0 + cs], _bf(p))
        if u > 0:
            outs.append(acc / l)
        m_prev = m
    return outs


def _transpose_bf16(x):
    return _bf(x.astype(jnp.float32).T)


def _attn_kernel(*refs, n_qparts, n_src, n_kparts, src_rows):
    it = iter(refs)
    q_refs = [next(it) for _ in range(n_qparts)]
    srcs = []
    for _ in range(n_src):
        k_refs = [next(it) for _ in range(n_kparts)]
        srcs.append((k_refs, next(it)))
    o_ref = next(it)
    k_scr, vt_scr = next(it), next(it)
    s_scrs = list(it)

    @pl.when(pl.program_id(2) == 0)
    def _():
        r = 0
        for (k_refs, v_ref), rows in zip(srcs, src_rows):
            for p, k_ref in enumerate(k_refs):
                k_scr[r:r + rows, p * LANES:(p + 1) * LANES] = k_ref[...]
            vt_scr[:, r:r + rows] = _transpose_bf16(v_ref[...])
            r += rows

    q = jnp.concatenate([qr[...] for qr in q_refs], axis=-1) if n_qparts > 1 else q_refs[0][...]
    sub = s_scrs[0].shape[1]
    streams = [(_transpose_bf16(q[u * sub:(u + 1) * sub]), k_scr, s_scr)
               for u, s_scr in enumerate(s_scrs)]
    outs = _pipelined_attention(streams, vt_scr, _key_chunks(src_rows, KEY_CHUNK_ATTN))
    for u, o_t in enumerate(outs):
        o_ref[u * sub:(u + 1) * sub, :] = _bf(o_t.T)


def _attn_call(q_parts, sources, n_heads, lq, tq, dv, name):
    tq = min(tq, lq)
    sub = min(TQ_SUB, tq)
    nq = lq // tq
    in_specs, args = [], []
    for arr, cf in q_parts:
        in_specs.append(pl.BlockSpec((tq, LANES), lambda b, h, i, cf=cf: (b * nq + i, cf(h))))
        args.append(arr)
    src_rows = []
    n_kparts = len(sources[0][1])
    for rows, k_parts, (v_arr, vcf) in sources:
        src_rows.append(rows)
        for arr, cf in k_parts:
            in_specs.append(pl.BlockSpec((rows, LANES), lambda b, h, i, cf=cf: (b, cf(h))))
            args.append(arr)
        in_specs.append(pl.BlockSpec((rows, dv), lambda b, h, i, cf=vcf: (b, cf(h))))
        args.append(v_arr)
    nk = sum(src_rows)
    kern = functools.partial(_attn_kernel, n_qparts=len(q_parts), n_src=len(sources),
                             n_kparts=n_kparts, src_rows=tuple(src_rows))
    return pl.pallas_call(
        kern,
        grid=(BATCH, n_heads, nq),
        in_specs=in_specs,
        out_specs=pl.BlockSpec((tq, dv), lambda b, h, i: (b * nq + i, h)),
        out_shape=jax.ShapeDtypeStruct((BATCH * lq, n_heads * dv), jnp.bfloat16),
        scratch_shapes=[
            pltpu.VMEM((nk, n_kparts * LANES), jnp.bfloat16),
            pltpu.VMEM((dv, nk), jnp.bfloat16),
        ] + [pltpu.VMEM((nk, sub), jnp.float32)] * (tq // sub),
        compiler_params=_params("parallel", "parallel", "arbitrary"),
        name=name,
    )(*args)


def _diff_kernel(*refs, n_src, src_rows, lam_init):
    it = iter(refs)
    q1_ref, q2_ref = next(it), next(it)
    srcs = [(next(it), next(it), next(it)) for _ in range(n_src)]
    lam_ref, sub_ref = next(it), next(it)
    o_ref = next(it)
    k1_scr, k2_scr, vt_scr, s1_scr, s2_scr = [next(it) for _ in range(5)]

    @pl.when(pl.program_id(2) == 0)
    def _():
        r = 0
        for (k1_ref, k2_ref, v_ref), rows in zip(srcs, src_rows):
            k1_scr[r:r + rows, :] = k1_ref[...]
            k2_scr[r:r + rows, :] = k2_ref[...]
            vt_scr[:, r:r + rows] = _transpose_bf16(v_ref[...])
            r += rows

    lf = lam_ref[...]
    lam = (jnp.exp(jnp.sum(lf[0:1] * lf[1:2], axis=-1, keepdims=True))
           - jnp.exp(jnp.sum(lf[2:3] * lf[3:4], axis=-1, keepdims=True)) + lam_init)
    chunks = _key_chunks(src_rows, KEY_CHUNK_DIFF)
    n_sub, _, sub = s1_scr.shape
    streams = []
    for u in range(n_sub):
        rows = slice(u * sub, (u + 1) * sub)
        streams.append((_transpose_bf16(q1_ref[rows, :]), k1_scr, s1_scr.at[u]))
        streams.append((_transpose_bf16(q2_ref[rows, :]), k2_scr, s2_scr.at[u]))
    outs = _pipelined_attention(streams, vt_scr, chunks)
    o_t = jnp.concatenate([outs[2 * u] - lam * outs[2 * u + 1] for u in range(n_sub)],
                          axis=1)
    o = o_t.T
    ms = jnp.mean(o * o, axis=-1, keepdims=True)
    o_ref[...] = _bf(o * lax.rsqrt(ms + EPS) * sub_ref[...] * (1.0 - lam_init))


def _diff_call(qkv_q, lq, sources, diff_lambda, subln, lam_init, tq, name):
    tq = min(tq, lq)
    sub = min(TQ_SUB, tq)
    nq = lq // tq
    qb0 = 3 * NA_WIDTH // LANES
    kb0 = (3 * NA_WIDTH + DIFF_WIDTH) // LANES
    vb0 = (3 * NA_WIDTH + 2 * DIFF_WIDTH) // (2 * LANES)
    in_specs = [
        pl.BlockSpec((tq, LANES), lambda b, h, i: (b * nq + i, qb0 + 2 * h)),
        pl.BlockSpec((tq, LANES), lambda b, h, i: (b * nq + i, qb0 + 2 * h + 1)),
    ]
    args = [qkv_q, qkv_q]
    src_rows = []
    for rows, arr in sources:
        src_rows.append(rows)
        in_specs += [
            pl.BlockSpec((rows, LANES), lambda b, h, i: (b, kb0 + 2 * h)),
            pl.BlockSpec((rows, LANES), lambda b, h, i: (b, kb0 + 2 * h + 1)),
            pl.BlockSpec((rows, 2 * LANES), lambda b, h, i: (b, vb0 + h)),
        ]
        args += [arr, arr, arr]
    in_specs += [
        pl.BlockSpec((4, LANES), lambda b, h, i: (0, 0)),
        pl.BlockSpec((1, 2 * LANES), lambda b, h, i: (0, 0)),
    ]
    args += [diff_lambda, subln]
    nk = sum(src_rows)
    kern = functools.partial(_diff_kernel, n_src=len(sources), src_rows=tuple(src_rows),
                             lam_init=lam_init)
    return pl.pallas_call(
        kern,
        grid=(BATCH, DIFF_HEADS, nq),
        in_specs=in_specs,
        out_specs=pl.BlockSpec((tq, 2 * LANES), lambda b, h, i: (b * nq + i, h)),
        out_shape=jax.ShapeDtypeStruct((BATCH * lq, DIFF_WIDTH), jnp.bfloat16),
        scratch_shapes=[
            pltpu.VMEM((nk, LANES), jnp.bfloat16),
            pltpu.VMEM((nk, LANES), jnp.bfloat16),
            pltpu.VMEM((2 * LANES, nk), jnp.bfloat16),
            pltpu.VMEM((tq // sub, nk, sub), jnp.float32),
            pltpu.VMEM((tq // sub, nk, sub), jnp.float32),
        ],
        compiler_params=_params("parallel", "parallel", "arbitrary"),
        name=name,
    )(*args)


def _rope_tables(d):
    h = d // 2
    half = h // 2
    t = jnp.arange(SEQ, dtype=jnp.int32)
    freqs = ROPE_BASE ** (-jnp.arange(half, dtype=jnp.float32) / half)

    def cs(pos):
        ang = pos.astype(jnp.float32)[:, None] * freqs[None, :]
        return jnp.cos(ang), jnp.sin(ang)

    cr, sr = cs(t // GRID_W)
    cc, sc = cs(t % GRID_W)
    cos = jnp.concatenate([cr, cr, cc, cc], -1)
    sin = jnp.concatenate([sr, sr, sc, sc], -1)
    first = (jnp.arange(d) % h) < half
    sin_a = jnp.where(first[None, :], -sin, 0.0)
    sin_b = jnp.where(first[None, :], 0.0, sin)
    pad = LANES - d
    if pad:
        cos = jnp.pad(cos, ((0, 0), (0, pad)), constant_values=1.0)
        sin_a = jnp.pad(sin_a, ((0, 0), (0, pad)))
        sin_b = jnp.pad(sin_b, ((0, 0), (0, pad)))
    pair_tables = tuple(jnp.concatenate([t, t], -1) for t in (cos, sin_a, sin_b))
    return (cos, sin_a, sin_b), half, (pair_tables, half)


def _pair_blockdiag(block):
    z = np.zeros_like(block)
    return jnp.asarray(np.block([[block, z], [z, block]]), dtype=jnp.bfloat16)


def _pad_lanes(v, fill=0.0):
    return jnp.pad(v, (0, LANES - v.shape[0]), constant_values=fill)


def _pair(v):
    return jnp.concatenate([v, v])


def _lat_row(tm):
    bpb = SEQ // tm
    return lambda i: i // bpb


def _ctx_row(i):
    return CTX_MOD_ROW


def _even_layer(x, xc, modarr, norm_mix, norm_ffn, ffn_w, w_in, w_out, qn_a, kn_a, rpb,
                qn_b, kn_b, diff_lambda, subln, lam_init, with_ctx):
    scale = HEAD_DIM ** -0.5 * LOG2E
    gains = jnp.stack([_pair(g) for g in (qn_a * scale, kn_a, qn_b * scale, kn_b)]
                      + [jnp.ones((PAIR,), jnp.float32)] * 4)
    _, _, rope = _rope_tables(HEAD_DIM)
    segs = [(True, 0, HEAD_DIM, False), (True, 1, HEAD_DIM, False), (False, 0, HEAD_DIM, False),
            (True, 2, HEAD_DIM, True), (True, 3, HEAD_DIM, True), (False, 0, HEAD_DIM, False)]
    w_in = _bf(w_in)
    w_out = _bf(w_out)
    gain_mix = norm_mix[None, :]
    lat_row = _lat_row(TM)

    qkv = _proj_call(x, w_in, gains, segs, NA_WIDTH, mod=(gain_mix, modarr, 0, 1), rope=rope,
                     mod_row=_lat_row(TM_EVEN), tm=TM_EVEN, name="even_proj")
    qkv_c = _proj_call(xc, w_in, gains, segs, NA_WIDTH, mod=(gain_mix, modarr, 0, 1), rope=None,
                       mod_row=_ctx_row, tm=TM_EVEN, name="even_proj_ctx")

    oa = _na_call(qkv, qkv_c, rpb)
    ob = _diff_call(qkv, SEQ, [(SEQ, qkv), (CTX_LEN, qkv_c)], diff_lambda, subln[None, :],
                    lam_init, TQ_DIFF, "diff_attn")
    x = _outproj_call([oa, ob], w_out, x, modarr, 2, lat_row, name="even_out")
    x = _ffn_call(x, norm_ffn[None, :], modarr, *ffn_w, _lat_row(TM_FF), SEQ)
    if with_ctx:
        oa_c = _attn_call(
            [(qkv_c, lambda h: h)],
            [(CTX_LEN, [(qkv_c, lambda h: NA_HEADS + h)], (qkv_c, lambda h: 2 * NA_HEADS + h))],
            NA_HEADS, CTX_LEN, CTX_LEN, HEAD_DIM, "ctx_attn")
        ob_c = _diff_call(qkv_c, CTX_LEN, [(CTX_LEN, qkv_c)], diff_lambda, subln[None, :],
                          lam_init, CTX_LEN, "diff_attn_ctx")
        xc = _outproj_call([oa_c, ob_c], w_out, xc, modarr, 2, _ctx_row, name="even_out_ctx")
        xc = _ffn_call(xc, norm_ffn[None, :], modarr, *ffn_w, _ctx_row, CTX_LEN,
                       name="conv_ffn_ctx")
    return x, xc


def _odd_layer(x, xc, modarr, norm_mix, norm_ffn, ffn_w, w_down, q_a_norm, kv_a_norm, w_uq,
               w_ukv, qn_nope, qn_rope, kn_nope, kn_rope, w_out, with_ctx):
    scale = (MLA_NOPE + MLA_ROPE) ** -0.5 * LOG2E
    _, _, rope_pair = _rope_tables(MLA_ROPE)
    gain_mix = norm_mix[None, :]
    lat_row = _lat_row(TM)

    w_down_p = _bf(jnp.pad(w_down, ((0, 0), (0, MLA_DOWN_PAD - MLA_DOWN))))
    zeros_r = jnp.zeros((MLA_Q_RANK,), jnp.float32)
    kr_gain = jnp.concatenate([_pad_lanes(kn_rope), zeros_r[:MLA_Q_RANK - LANES]])
    gains_d = jnp.stack([q_a_norm, kv_a_norm, kr_gain] + [zeros_r] * 5)
    wq = w_uq.reshape(MLA_Q_RANK, MLA_HEADS, MLA_NOPE + MLA_ROPE)
    wq_rope = jnp.pad(wq[:, :, MLA_NOPE:], ((0, 0), (0, 0), (0, LANES - MLA_ROPE)))
    wq_p = _bf(jnp.concatenate([wq[:, :, :MLA_NOPE].reshape(MLA_Q_RANK, -1),
                                wq_rope.reshape(MLA_Q_RANK, -1)], -1))
    wkv = w_ukv.reshape(MLA_KV_RANK, MLA_HEADS, MLA_NOPE + MLA_V)
    wkv_p = _bf(jnp.concatenate([wkv[:, :, :MLA_NOPE].reshape(MLA_KV_RANK, -1),
                                 wkv[:, :, MLA_NOPE:].reshape(MLA_KV_RANK, -1)], -1))
    gains_p = jnp.stack([_pair(qn_nope * scale), _pair(_pad_lanes(qn_rope) * scale),
                         _pair(kn_nope)] + [jnp.zeros((PAIR,), jnp.float32)] * 5)
    mod_in = (gain_mix, modarr, 0, 1)
    q, kv, kr = _mla_proj_call(x, w_down_p, wq_p, wkv_p, gains_d, gains_p, mod_in, rope_pair,
                               _lat_row(TM_MLA_PROJ), True, "mla_proj")
    kv_c, kr_c = _mla_proj_call(xc, w_down_p, wq_p, wkv_p, gains_d, gains_p, mod_in, None,
                                _ctx_row, False, "mla_proj_ctx")

    def src(rows, kv_arr, kr_arr):
        return (rows, [(kv_arr, lambda h: h), (kr_arr, lambda h: 0)],
                (kv_arr, lambda h: MLA_HEADS + h))

    o = _attn_call([(q, lambda h: h), (q, lambda h: MLA_HEADS + h)],
                   [src(SEQ, kv, kr), src(CTX_LEN, kv_c, kr_c)],
                   MLA_HEADS, SEQ, TQ_MLA, MLA_V, "mla_attn")
    x = _outproj_call([o], _bf(w_out), x, modarr, 2, lat_row, name="mla_out")
    x = _ffn_call(x, norm_ffn[None, :], modarr, *ffn_w, _lat_row(TM_FF), SEQ)
    assert not with_ctx
    return x, xc


def _mla_proj_kernel(x_ref, g_ref, sh_ref, sc_ref, wd_ref, wq_ref, wkv_ref, gd_ref, gp_ref,
                     ones_ref, *rest, rope, rope_half, with_q):
    rope_refs = rest[:3] if rope else None
    outs = rest[3:] if rope else rest
    if with_q:
        q_ref, kv_ref, kr_ref = outs
    else:
        kv_ref, kr_ref = outs
    h = _bf(_modulate(x_ref[...], g_ref[...], sh_ref[0], sc_ref[0]))
    down = _dot(h, wd_ref[...])

    def latent_norm(xs, row, nvalid):
        ms = jnp.sum(xs * xs, axis=-1, keepdims=True) * (1.0 / nvalid)
        return xs * lax.rsqrt(ms + EPS) * gd_ref[row:row + 1, 0:xs.shape[1]]

    kva = _bf(latent_norm(down[:, MLA_Q_RANK:MLA_Q_RANK + MLA_KV_RANK], 1, MLA_KV_RANK))
    kr = latent_norm(down[:, MLA_Q_RANK + MLA_KV_RANK:MLA_DOWN_PAD], 2, MLA_ROPE)
    if rope:
        cos_ref, sa_ref, sb_ref = rope_refs
        kr = (kr * cos_ref[:, 0:LANES]
              + pltpu.roll(kr, LANES - rope_half, axis=1) * sa_ref[:, 0:LANES]
              + pltpu.roll(kr, rope_half, axis=1) * sb_ref[:, 0:LANES])
    kr_ref[...] = _bf(kr)

    def up(a, w_ref, o_ref, specs):
        width = w_ref.shape[1] // len(specs)
        for c in range(0, w_ref.shape[1], SUB_N):
            acc = _dot(a, w_ref[:, c:c + SUB_N])
            for p in range(0, SUB_N, PAIR):
                y = _pair_epilogue(acc[:, p:p + PAIR], specs[c // width], gp_ref, ones_ref,
                                   rope_refs, rope_half)
                o_ref[:, c + p:c + p + PAIR] = _bf(y)

    up(kva, wkv_ref, kv_ref, [(True, 2, MLA_NOPE, False), (False, 0, MLA_V, False)])
    if with_q:
        qa = _bf(latent_norm(down[:, 0:MLA_Q_RANK], 0, MLA_Q_RANK))
        up(qa, wq_ref, q_ref, [(True, 0, MLA_NOPE, False), (True, 1, MLA_ROPE, True)])


def _mla_proj_call(x, wd, wq, wkv, gains_d, gains_p, mod, rope, mod_row, with_q, name,
                   tm=TM_MLA_PROJ):
    m, k = x.shape
    tm = min(tm, m)
    bpb = SEQ // tm
    gain, modarr, sh_c, sc_c = mod
    ones = _pair_blockdiag(np.ones((LANES, LANES), np.float32))
    in_specs = [
        pl.BlockSpec((tm, k), lambda i: (i, 0)),
        pl.BlockSpec((1, k), lambda i: (0, 0)),
        pl.BlockSpec((1, 1, k), lambda i: (mod_row(i), 0, sh_c)),
        pl.BlockSpec((1, 1, k), lambda i: (mod_row(i), 0, sc_c)),
    ]
    args = [x, gain, modarr, modarr]
    for w in (wd, wq, wkv, gains_d, gains_p, ones):
        in_specs.append(pl.BlockSpec(w.shape, lambda i: (0, 0)))
        args.append(w)
    rope_half = 0
    if rope is not None:
        tables, rope_half = rope
        for t in tables:
            in_specs.append(pl.BlockSpec((tm, PAIR), lambda i: (i % bpb, 0)))
            args.append(t)
    widths = ([wq.shape[1]] if with_q else []) + [wkv.shape[1], LANES]
    return pl.pallas_call(
        functools.partial(_mla_proj_kernel, rope=rope is not None, rope_half=rope_half,
                          with_q=with_q),
        grid=(m // tm,),
        in_specs=in_specs,
        out_specs=[pl.BlockSpec((tm, n), lambda i: (i, 0)) for n in widths],
        out_shape=[jax.ShapeDtypeStruct((m, n), jnp.bfloat16) for n in widths],
        compiler_params=_params("parallel"),
        name=name,
    )(*args)


def _w_in_prep_kernel(x_ref, o_ref):
    pad = jnp.zeros((x_ref.shape[0], D_FF_PAD - D_FF), jnp.bfloat16)
    o_ref[:, 0:D_FF] = _bf(x_ref[:, 0:D_FF])
    o_ref[:, D_FF:D_FF_PAD] = pad
    o_ref[:, D_FF_PAD:D_FF_PAD + D_FF] = _bf(x_ref[:, D_FF:2 * D_FF])
    o_ref[:, D_FF_PAD + D_FF:2 * D_FF_PAD] = pad


def _w_out_prep_kernel(x_ref, o_ref):
    o_ref[0:D_FF, :] = _bf(x_ref[...])
    o_ref[D_FF:D_FF_PAD, :] = jnp.zeros((D_FF_PAD - D_FF, x_ref.shape[1]), jnp.bfloat16)


def _ffn_weights(w_in, conv_w, w_out):
    depth, d, _ = w_in.shape
    rows, cols = 256, 256
    w_in_p = pl.pallas_call(
        _w_in_prep_kernel,
        grid=(depth, d // rows),
        in_specs=[pl.BlockSpec((None, rows, 2 * D_FF), lambda l, i: (l, i, 0))],
        out_specs=pl.BlockSpec((None, rows, 2 * D_FF_PAD), lambda l, i: (l, i, 0)),
        out_shape=jax.ShapeDtypeStruct((depth, d, 2 * D_FF_PAD), jnp.bfloat16),
        compiler_params=_params("parallel", "parallel"),
        name="ffn_w_in_prep",
    )(w_in)
    w_out_p = pl.pallas_call(
        _w_out_prep_kernel,
        grid=(depth, d // cols),
        in_specs=[pl.BlockSpec((None, D_FF, cols), lambda l, j: (l, 0, j))],
        out_specs=pl.BlockSpec((None, D_FF_PAD, cols), lambda l, j: (l, 0, j)),
        out_shape=jax.ShapeDtypeStruct((depth, D_FF_PAD, d), jnp.bfloat16),
        compiler_params=_params("parallel", "parallel"),
        name="ffn_w_out_prep",
    )(w_out)
    pad = D_FF_PAD - D_FF
    zc = jnp.zeros(conv_w.shape[:2] + (pad,), conv_w.dtype)
    conv_p = jnp.concatenate([conv_w[..., :D_FF], zc, conv_w[..., D_FF:], zc], -1)
    return w_in_p, conv_p, w_out_p


def kernel(x, c, ctx, c_ctx, ada_w, ada_b, norm_mix, norm_ffn, ffn_w_in, ffn_conv, ffn_w_out, even_w_in, even_w_out, na_q_norm, na_k_norm, na_rpb, diff_q_norm, diff_k_norm, diff_lambda, diff_subln, mla_w_down, mla_q_a_norm, mla_kv_a_norm, mla_w_uq, mla_w_ukv, mla_q_nope_norm, mla_q_rope_norm, mla_k_nope_norm, mla_k_rope_norm, mla_w_out):
    cond = jnp.concatenate(
        [c, c_ctx[None, :], jnp.zeros((MOD_ROWS - BATCH - 1, D_MODEL), jnp.float32)], 0)
    mod = _ada_call(cond, ada_w, ada_b)
    xl = x.reshape(BATCH * SEQ, D_MODEL)
    xc = ctx.reshape(BATCH * CTX_LEN, D_MODEL)
    ffn_all = _ffn_weights(ffn_w_in, ffn_conv, ffn_w_out)
    for l in range(DEPTH):
        with_ctx = l < DEPTH - 1
        modarr = mod[l].reshape(MOD_ROWS, 1, 6 * D_MODEL)
        ffn_w = ffn_all + (l,)
        i = l // 2
        if l % 2 == 0:
            lam_init = 0.8 - 0.6 * math.exp(-0.3 * l)
            xl, xc = _even_layer(xl, xc, modarr, norm_mix[l], norm_ffn[l], ffn_w, even_w_in[i],
                                 even_w_out[i], na_q_norm[i], na_k_norm[i], na_rpb[i],
                                 diff_q_norm[i], diff_k_norm[i], diff_lambda[i], diff_subln[i],
                                 lam_init, with_ctx)
        else:
            xl, xc = _odd_layer(xl, xc, modarr, norm_mix[l], norm_ffn[l], ffn_w, mla_w_down[i],
                                mla_q_a_norm[i], mla_kv_a_norm[i], mla_w_uq[i], mla_w_ukv[i],
                                mla_q_nope_norm[i], mla_q_rope_norm[i], mla_k_nope_norm[i],
                                mla_k_rope_norm[i], mla_w_out[i], with_ctx)
    return xl.reshape(BATCH, SEQ, D_MODEL)
```

```python
import functools
import math

import numpy as np
import jax
import jax.numpy as jnp
from jax import lax
from jax.experimental import pallas as pl
from jax.experimental.pallas import tpu as pltpu

D_MODEL = 2048
BATCH = 4
SEQ = 4096
DEPTH = 2
GRID_W = 64
GRID_H = SEQ // GRID_W
CTX_LEN = 256
HEAD_DIM = 128
NA_HEADS = 8
NA_WIN_H = 8
NA_WIN_W = 16
DIFF_HEADS = 4
NA_WIDTH = 1024
DIFF_WIDTH = 1024
EVEN_PROJ = 6144
MLA_HEADS = 16
MLA_Q_RANK = 512
MLA_KV_RANK = 512
MLA_NOPE = 128
MLA_ROPE = 64
MLA_V = 128
MLA_DOWN = 1088
D_FF = 5504
ROPE_BASE = 10000.0
EPS = 1e-6

LANES = 128
SUBLANES = 8
MOD_ROWS = 8
CTX_MOD_ROW = BATCH
D_FF_PAD = 5632
MLA_DOWN_PAD = 1152
VMEM_LIMIT = 56 * 1024 * 1024

TM = 512
TN = 1024
TN_FF = 512
TM_FF = 1024
TM_EVEN = 1024
TM_MLA_PROJ = 256
HM_FF = 512
SUB_N = 512
PAIR = 2 * LANES
NA_GROUP_ROWS = 4
NA_KEY_ROWS = 12
NA_UNROLL = 4
NA_Q = NA_GROUP_ROWS * GRID_W
NA_K = NA_KEY_ROWS * GRID_W
TQ_DIFF = 1024
TQ_MLA = 2048
TQ_SUB = 256
KEY_CHUNK_ATTN = 256
KEY_CHUNK_DIFF = 512
NEG = -1e30
LOG2E = math.log2(math.e)

_NT = (((1,), (1,)), ((), ()))


def _params(*sem):
    return pltpu.CompilerParams(dimension_semantics=sem, vmem_limit_bytes=VMEM_LIMIT)


def _bf(x):
    return x.astype(jnp.bfloat16)


def _dot(a, b):
    return jnp.dot(a, b, preferred_element_type=jnp.float32)


def _dot_nt(a, b):
    return lax.dot_general(a, b, _NT, preferred_element_type=jnp.float32)


def _modulate(x, gain, shift, scale):
    ms = jnp.mean(x * x, axis=-1, keepdims=True)
    y = x * lax.rsqrt(ms + EPS) * gain
    return y * (1.0 + scale) + shift


def _ada_kernel(c_ref, w_ref, b_ref, o_ref):
    c = c_ref[...]
    s = c * (1.0 / (1.0 + jnp.exp(-c)))
    o_ref[0] = _dot(_bf(s), _bf(w_ref[0])) + b_ref[0]


def _ada_call(cond, ada_w, ada_b):
    tn = 512
    n = ada_w.shape[-1]
    return pl.pallas_call(
        _ada_kernel,
        grid=(DEPTH, n // tn),
        in_specs=[
            pl.BlockSpec((MOD_ROWS, D_MODEL), lambda l, j: (0, 0)),
            pl.BlockSpec((1, D_MODEL, tn), lambda l, j: (l, 0, j)),
            pl.BlockSpec((1, 1, tn), lambda l, j: (l, 0, j)),
        ],
        out_specs=pl.BlockSpec((1, MOD_ROWS, tn), lambda l, j: (l, 0, j)),
        out_shape=jax.ShapeDtypeStruct((DEPTH, MOD_ROWS, n), jnp.float32),
        compiler_params=_params("parallel", "parallel"),
        name="ada_mod",
    )(cond, ada_w, ada_b.reshape(DEPTH, 1, n))


def _pair_epilogue(acc, spec, gains_ref, ones_ref, rope_refs, rope_half):
    norm, gain_row, nvalid, do_rope = spec
    y = acc
    if norm:
        ss = _dot(_bf(acc * acc), ones_ref[...])
        y = acc * lax.rsqrt(ss * (1.0 / nvalid) + EPS) * gains_ref[gain_row:gain_row + 1, :]
    if do_rope and rope_refs is not None:
        cos_ref, sa_ref, sb_ref = rope_refs
        y = (y * cos_ref[...]
             + pltpu.roll(y, PAIR - rope_half, axis=1) * sa_ref[...]
             + pltpu.roll(y, rope_half, axis=1) * sb_ref[...])
    return y


def _proj_kernel(*refs, mod, segs, seg_tiles, rope, rope_half):
    it = iter(refs)
    x_ref = next(it)
    if mod:
        g_ref, sh_ref, sc_ref = next(it), next(it), next(it)
    w_ref = next(it)
    gains_ref, ones_ref = next(it), next(it)
    rope_refs = (next(it), next(it), next(it)) if rope else None
    o_ref = next(it)
    h_ref = next(it) if mod else None
    j = pl.program_id(1)
    tn = o_ref.shape[1]

    if mod:
        @pl.when(j == 0)
        def _():
            h_ref[...] = _bf(_modulate(x_ref[...], g_ref[...], sh_ref[0], sc_ref[0]))
        a = h_ref[...]
    else:
        a = x_ref[...]
    seg = j // seg_tiles
    for s, spec in enumerate(segs):
        @pl.when(seg == s)
        def _(spec=spec):
            for c in range(0, tn, SUB_N):
                acc = _dot(a, w_ref[:, c:c + SUB_N])
                for p in range(0, SUB_N, PAIR):
                    y = _pair_epilogue(acc[:, p:p + PAIR], spec, gains_ref, ones_ref, rope_refs,
                                       rope_half)
                    o_ref[:, c + p:c + p + PAIR] = _bf(y)


def _proj_call(x, w, gains, segs, seg_width, *, mod=None, rope=None, mod_row=None,
               tm=TM, tn=TN, name="proj"):
    m, k = x.shape
    n = w.shape[1]
    tm = min(tm, m)
    tn = min(tn, seg_width)
    assert m % tm == 0 and seg_width % tn == 0 and n % seg_width == 0
    seg_tiles = seg_width // tn
    bpb = SEQ // tm

    in_specs = [pl.BlockSpec((tm, k), lambda i, j: (i, 0))]
    args = [x]
    if mod is not None:
        gain, modarr, sh_c, sc_c = mod
        in_specs += [
            pl.BlockSpec((1, k), lambda i, j: (0, 0)),
            pl.BlockSpec((1, 1, k), lambda i, j: (mod_row(i), 0, sh_c)),
            pl.BlockSpec((1, 1, k), lambda i, j: (mod_row(i), 0, sc_c)),
        ]
        args += [gain, modarr, modarr]
    in_specs.append(pl.BlockSpec((k, tn), lambda i, j: (0, j)))
    args.append(w)
    ones = _pair_blockdiag(np.ones((LANES, LANES), np.float32))
    in_specs += [pl.BlockSpec(gains.shape, lambda i, j: (0, 0)),
                 pl.BlockSpec(ones.shape, lambda i, j: (0, 0))]
    args += [gains, ones]
    if rope is not None:
        tables, rope_half = rope
        for t in tables:
            in_specs.append(pl.BlockSpec((tm, PAIR), lambda i, j: (i % bpb, 0)))
            args.append(t)
    else:
        rope_half = 0
    kern = functools.partial(_proj_kernel, mod=mod is not None, segs=tuple(segs),
                             seg_tiles=seg_tiles, rope=rope is not None, rope_half=rope_half)
    return pl.pallas_call(
        kern,
        grid=(m // tm, n // tn),
        in_specs=in_specs,
        out_specs=pl.BlockSpec((tm, tn), lambda i, j: (i, j)),
        out_shape=jax.ShapeDtypeStruct((m, n), jnp.bfloat16),
        scratch_shapes=[pltpu.VMEM((tm, k), jnp.bfloat16)] if mod is not None else [],
        compiler_params=_params("parallel", "arbitrary"),
        name=name,
    )(*args)


def _outproj_kernel(*refs, n_a):
    a_refs = refs[:n_a]
    w_ref, x_ref, g_ref, o_ref = refs[n_a:]
    n = o_ref.shape[1]
    a_vals = [a_ref[...] for a_ref in a_refs]
    for c in range(0, n, SUB_N):
        acc, row = None, 0
        for a in a_vals:
            part = _dot(a, w_ref[row:row + a.shape[1], c:c + SUB_N])
            acc = part if acc is None else acc + part
            row += a.shape[1]
        o_ref[:, c:c + SUB_N] = x_ref[:, c:c + SUB_N] + g_ref[0][:, c:c + SUB_N] * acc


def _outproj_call(a_list, w, x, modarr, gate_chunk, mod_row, *, tm=TM, name="outproj"):
    m, n = x.shape
    tm = min(tm, m)
    in_specs, args = [], []
    for a in a_list:
        in_specs.append(pl.BlockSpec((tm, a.shape[1]), lambda i: (i, 0)))
        args.append(a)
    in_specs += [
        pl.BlockSpec(w.shape, lambda i: (0, 0)),
        pl.BlockSpec((tm, n), lambda i: (i, 0)),
        pl.BlockSpec((1, 1, n), lambda i: (mod_row(i), 0, gate_chunk)),
    ]
    args += [w, x, modarr]
    return pl.pallas_call(
        functools.partial(_outproj_kernel, n_a=len(a_list)),
        grid=(m // tm,),
        in_specs=in_specs,
        out_specs=pl.BlockSpec((tm, n), lambda i: (i, 0)),
        out_shape=jax.ShapeDtypeStruct((m, n), jnp.float32),
        compiler_params=_params("parallel"),
        name=name,
    )(*args)


def _ffn_kernel(x_ref, xp_ref, xn_ref, g_ref, sh_ref, sc_ref, gt_ref, wa_ref, wb_ref,
                ca_ref, cb_ref, wo_ref, o_ref, h_ref, *, tm, hm, seq_len):
    i = pl.program_id(0)
    j = pl.program_id(1)
    nj = pl.num_programs(1)
    n_grp = tm // hm
    rows = hm + 2 * SUBLANES

    @pl.when(j == 0)
    def _():
        gain, shift, scale = g_ref[...], sh_ref[0], sc_ref[0]
        for r in range(n_grp):
            r0 = r * hm
            h_ref[r * rows:r * rows + hm, :] = _bf(
                _modulate(x_ref[r0:r0 + hm, :], gain, shift, scale))
            nxt_src = x_ref[r0 + hm:r0 + hm + SUBLANES, :] if r + 1 < n_grp else xn_ref[...]
            prv_src = x_ref[r0 - SUBLANES:r0, :] if r > 0 else xp_ref[...]
            keep_next = jnp.where((i * tm + r0 + hm) % seq_len == 0, 0.0, 1.0)
            keep_prev = jnp.where((i * tm + r0) % seq_len == 0, 0.0, 1.0)
            nxt = keep_next * _modulate(nxt_src, gain, shift, scale)
            prv = keep_prev * _modulate(prv_src, gain, shift, scale)
            h_ref[r * rows + hm:(r + 1) * rows, :] = _bf(jnp.concatenate([nxt, prv], axis=0))
        o_ref[...] = jnp.zeros_like(o_ref)

    ca, cb = ca_ref[...], cb_ref[...]
    ups = []
    for r in range(n_grp):
        h = h_ref[r * rows:(r + 1) * rows, :]
        ups.append((_dot(h, wa_ref[...]), _dot(h, wb_ref[...])))

    def conv(u, cw):
        prev = pltpu.roll(u, 1, axis=0)[0:hm]
        nxt = pltpu.roll(u, rows - 1, axis=0)[0:hm]
        return prev * cw[0:1] + u[0:hm] * cw[1:2] + nxt * cw[2:3]

    for r, (ua, ub) in enumerate(ups):
        a = conv(ua, ca)
        b = conv(ub, cb)
        act = a * (1.0 / (1.0 + jnp.exp(-a))) * b
        o_ref[r * hm:(r + 1) * hm, :] += _dot(_bf(act), wo_ref[...])

    @pl.when(j == nj - 1)
    def _():
        o_ref[...] = x_ref[...] + gt_ref[0] * o_ref[...]


def _ffn_call(x, gain, modarr, w_in, conv_w, w_out, layer, mod_row, seq_len, *, tm=TM_FF,
              tn=TN_FF, name="conv_ffn"):
    m, d = x.shape
    tm = min(tm, m)
    hm = min(HM_FF, seq_len)
    assert m % tm == 0 and tm % hm == 0 and seq_len % hm == 0
    nj = D_FF_PAD // tn
    hb = tm // SUBLANES
    last_hb = m // SUBLANES - 1
    kern = functools.partial(_ffn_kernel, tm=tm, hm=hm, seq_len=seq_len)
    return pl.pallas_call(
        kern,
        grid=(m // tm, nj),
        in_specs=[
            pl.BlockSpec((tm, d), lambda i, j: (i, 0)),
            pl.BlockSpec((SUBLANES, d), lambda i, j: (jnp.maximum(i * hb - 1, 0), 0)),
            pl.BlockSpec((SUBLANES, d), lambda i, j: (jnp.minimum((i + 1) * hb, last_hb), 0)),
            pl.BlockSpec((1, d), lambda i, j: (0, 0)),
            pl.BlockSpec((1, 1, d), lambda i, j: (mod_row(i), 0, 3)),
            pl.BlockSpec((1, 1, d), lambda i, j: (mod_row(i), 0, 4)),
            pl.BlockSpec((1, 1, d), lambda i, j: (mod_row(i), 0, 5)),
            pl.BlockSpec((None, d, tn), lambda i, j: (layer, 0, j)),
            pl.BlockSpec((None, d, tn), lambda i, j: (layer, 0, nj + j)),
            pl.BlockSpec((None, 3, tn), lambda i, j: (layer, 0, j)),
            pl.BlockSpec((None, 3, tn), lambda i, j: (layer, 0, nj + j)),
            pl.BlockSpec((None, tn, d), lambda i, j: (layer, j, 0)),
        ],
        out_specs=pl.BlockSpec((tm, d), lambda i, j: (i, 0)),
        out_shape=jax.ShapeDtypeStruct((m, d), jnp.float32),
        scratch_shapes=[
            pltpu.VMEM(((tm // hm) * (hm + 2 * SUBLANES), d), jnp.bfloat16),
        ],
        compiler_params=_params("parallel", "arbitrary"),
        name=name,
    )(x, x, x, gain, modarr, modarr, modarr, w_in, w_in, conv_w, conv_w, w_out)


def _na_group_geometry(g):
    r0 = g * NA_GROUP_ROWS
    return r0, min(max(r0 - NA_WIN_H // 2, 0), GRID_H - NA_KEY_ROWS)


def _na_build_bias(rpb_ref, h, tc_scr, bias_scr):
    n_dr = 2 * NA_WIN_H - 1
    n_dc = 2 * NA_WIN_W - 1
    qc = lax.broadcasted_iota(jnp.int32, (GRID_W, GRID_W), 0)
    kc = lax.broadcasted_iota(jnp.int32, (GRID_W, GRID_W), 1)
    col0 = jnp.clip(qc - NA_WIN_W // 2, 0, GRID_W - NA_WIN_W)
    col_valid = (kc >= col0) & (kc < col0 + NA_WIN_W)
    delta = kc - qc + (NA_WIN_W - 1)
    base = h * (n_dr * n_dc)
    for dr in range(n_dr):
        acc = jnp.zeros((GRID_W, GRID_W), jnp.float32)
        for e in range(n_dc):
            acc = jnp.where(delta == e, rpb_ref[base + dr * n_dc + e] * LOG2E, acc)
        tc_scr[dr] = jnp.where(col_valid, acc, NEG)
    tc_scr[n_dr] = jnp.full((GRID_W, GRID_W), NEG, jnp.float32)
    n_groups = GRID_H // NA_GROUP_ROWS
    for t, g in enumerate((0, 1, n_groups - 1)):
        r0, ks = _na_group_geometry(g)
        for qr in range(NA_GROUP_ROWS):
            r = r0 + qr
            row0 = min(max(r - NA_WIN_H // 2, 0), GRID_H - NA_WIN_H)
            pieces = []
            for kr in range(NA_KEY_ROWS):
                kra = ks + kr
                inside = row0 <= kra < row0 + NA_WIN_H
                pieces.append(tc_scr[kra - r + (NA_WIN_H - 1)] if inside else tc_scr[n_dr])
            bias_scr[t, qr * GRID_W:(qr + 1) * GRID_W, :] = jnp.concatenate(pieces, axis=1)


def _na_kernel(rpb_ref, q_ref, k_ref, v_ref, kc_ref, vc_ref, o_ref, tc_scr, bias_scr):
    @pl.when(pl.program_id(1) == 0)
    def _():
        _na_build_bias(rpb_ref, pl.program_id(0), tc_scr, bias_scr)

    kc = kc_ref[...]
    vc = vc_ref[...]
    n_groups = GRID_H // NA_GROUP_ROWS

    def group(g):
        r0 = g * NA_GROUP_ROWS
        ks = jnp.clip(r0 - NA_WIN_H // 2, 0, GRID_H - NA_KEY_ROWS)
        q0 = pl.multiple_of(g * NA_Q, NA_Q)
        k0 = pl.multiple_of(ks * GRID_W, GRID_W)
        tb = jnp.where(g == 0, 0, jnp.where(g == n_groups - 1, 2, 1))
        q = q_ref[pl.ds(q0, NA_Q), :]
        kw = k_ref[pl.ds(k0, NA_K), :]
        vw = v_ref[pl.ds(k0, NA_K), :]
        sw = _dot_nt(q, kw) + bias_scr[tb]
        sc = _dot_nt(q, kc)
        mx = jnp.maximum(jnp.max(sw, axis=-1, keepdims=True), jnp.max(sc, axis=-1, keepdims=True))
        pw = jnp.exp2(sw - mx)
        pc = jnp.exp2(sc - mx)
        den = jnp.sum(pw, axis=-1, keepdims=True) + jnp.sum(pc, axis=-1, keepdims=True)
        o = _dot(_bf(pw), vw) + _dot(_bf(pc), vc)
        o_ref[pl.ds(q0, NA_Q), :] = _bf(o / den)

    def body(t, carry):
        for u in range(NA_UNROLL):
            group(t * NA_UNROLL + u)
        return carry

    lax.fori_loop(0, n_groups // NA_UNROLL, body, 0)


def _na_call(qkv, qkv_c, rpb):
    n_dr = 2 * NA_WIN_H - 1
    return pl.pallas_call(
        _na_kernel,
        grid=(NA_HEADS, BATCH),
        in_specs=[
            pl.BlockSpec(memory_space=pltpu.SMEM),
            pl.BlockSpec((SEQ, LANES), lambda h, b: (b, h)),
            pl.BlockSpec((SEQ, LANES), lambda h, b: (b, NA_HEADS + h)),
            pl.BlockSpec((SEQ, LANES), lambda h, b: (b, 2 * NA_HEADS + h)),
            pl.BlockSpec((CTX_LEN, LANES), lambda h, b: (b, NA_HEADS + h)),
            pl.BlockSpec((CTX_LEN, LANES), lambda h, b: (b, 2 * NA_HEADS + h)),
        ],
        out_specs=pl.BlockSpec((SEQ, LANES), lambda h, b: (b, h)),
        out_shape=jax.ShapeDtypeStruct((BATCH * SEQ, NA_WIDTH), jnp.bfloat16),
        scratch_shapes=[
            pltpu.VMEM((n_dr + 1, GRID_W, GRID_W), jnp.float32),
            pltpu.VMEM((3, NA_Q, NA_K), jnp.float32),
        ],
        compiler_params=_params("parallel", "arbitrary"),
        name="na_attn",
    )(rpb.reshape(-1), qkv, qkv, qkv, qkv_c, qkv_c)


def _key_chunks(src_rows, chunk):
    chunks, r = [], 0
    for rows in src_rows:
        size = min(chunk, rows)
        assert rows % size == 0
        chunks += [(r + c, size) for c in range(0, rows, size)]
        r += rows
    return chunks


def _pipelined_attention(streams, vt_scr, chunks):
    dv = vt_scr.shape[0]
    sub = streams[0][2].shape[1]
    outs, m_prev = [], None
    for u in range(len(streams) + 1):
        m = jnp.full((1, sub), NEG, jnp.float32)
        l = jnp.zeros((1, sub), jnp.float32)
        acc = jnp.zeros((dv, sub), jnp.float32)
        for c0, cs in chunks:
            if u < len(streams):
                q_t, k_scr, s_ref = streams[u]
                s = _dot(k_scr[c0:c0 + cs, :], q_t)
                s_ref[c0:c0 + cs, :] = s
                m = jnp.maximum(m, jnp.max(s, axis=0, keepdims=True))
            if u > 0:
                p = jnp.exp2(streams[u - 1][2][c0:c0 + cs, :] - m_prev)
                l = l + jnp.sum(p, axis=0, keepdims=True)
                acc = acc + _dot(vt_scr[:, c0:c0 + cs], _bf(p))
        if u > 0:
            outs.append(acc / l)
        m_prev = m
    return outs


def _transpose_bf16(x):
    return _bf(x.astype(jnp.float32).T)


def _attn_kernel(*refs, n_qparts, n_src, n_kparts, src_rows):
    it = iter(refs)
    q_refs = [next(it) for _ in range(n_qparts)]
    srcs = []
    for _ in range(n_src):
        k_refs = [next(it) for _ in range(n_kparts)]
        srcs.append((k_refs, next(it)))
    o_ref = next(it)
    k_scr, vt_scr = next(it), next(it)
    s_scrs = list(it)

    @pl.when(pl.program_id(2) == 0)
    def _():
        r = 0
        for (k_refs, v_ref), rows in zip(srcs, src_rows):
            for p, k_ref in enumerate(k_refs):
                k_scr[r:r + rows, p * LANES:(p + 1) * LANES] = k_ref[...]
            vt_scr[:, r:r + rows] = _transpose_bf16(v_ref[...])
            r += rows

    q = jnp.concatenate([qr[...] for qr in q_refs], axis=-1) if n_qparts > 1 else q_refs[0][...]
    sub = s_scrs[0].shape[1]
    streams = [(_transpose_bf16(q[u * sub:(u + 1) * sub]), k_scr, s_scrs[u % 2])
               for u in range(q.shape[0] // sub)]
    outs = _pipelined_attention(streams, vt_scr, _key_chunks(src_rows, KEY_CHUNK_ATTN))
    for u, o_t in enumerate(outs):
        o_ref[u * sub:(u + 1) * sub, :] = _bf(o_t.T)


def _attn_call(q_parts, sources, n_heads, lq, tq, dv, name):
    tq = min(tq, lq)
    sub = min(TQ_SUB, tq)
    nq = lq // tq
    in_specs, args = [], []
    for arr, cf in q_parts:
        in_specs.append(pl.BlockSpec((tq, LANES), lambda b, h, i, cf=cf: (b * nq + i, cf(h))))
        args.append(arr)
    src_rows = []
    n_kparts = len(sources[0][1])
    for rows, k_parts, (v_arr, vcf) in sources:
        src_rows.append(rows)
        for arr, cf in k_parts:
            in_specs.append(pl.BlockSpec((rows, LANES), lambda b, h, i, cf=cf: (b, cf(h))))
            args.append(arr)
        in_specs.append(pl.BlockSpec((rows, dv), lambda b, h, i, cf=vcf: (b, cf(h))))
        args.append(v_arr)
    nk = sum(src_rows)
    kern = functools.partial(_attn_kernel, n_qparts=len(q_parts), n_src=len(sources),
                             n_kparts=n_kparts, src_rows=tuple(src_rows))
    return pl.pallas_call(
        kern,
        grid=(BATCH, n_heads, nq),
        in_specs=in_specs,
        out_specs=pl.BlockSpec((tq, dv), lambda b, h, i: (b * nq + i, h)),
        out_shape=jax.ShapeDtypeStruct((BATCH * lq, n_heads * dv), jnp.bfloat16),
        scratch_shapes=[
            pltpu.VMEM((nk, n_kparts * LANES), jnp.bfloat16),
            pltpu.VMEM((dv, nk), jnp.bfloat16),
        ] + [pltpu.VMEM((nk, sub), jnp.float32)] * 2,
        compiler_params=_params("parallel", "parallel", "arbitrary"),
        name=name,
    )(*args)


def _diff_kernel(*refs, n_src, src_rows, lam_init):
    it = iter(refs)
    q1_ref, q2_ref = next(it), next(it)
    srcs = [(next(it), next(it), next(it)) for _ in range(n_src)]
    lam_ref, sub_ref = next(it), next(it)
    o_ref = next(it)
    k1_scr, k2_scr, vt_scr, s1_scr, s2_scr = [next(it) for _ in range(5)]

    @pl.when(pl.program_id(2) == 0)
    def _():
        r = 0
        for (k1_ref, k2_ref, v_ref), rows in zip(srcs, src_rows):
            k1_scr[r:r + rows, :] = k1_ref[...]
            k2_scr[r:r + rows, :] = k2_ref[...]
            vt_scr[:, r:r + rows] = _transpose_bf16(v_ref[...])
            r += rows

    lf = lam_ref[...]
    lam = (jnp.exp(jnp.sum(lf[0:1] * lf[1:2], axis=-1, keepdims=True))
           - jnp.exp(jnp.sum(lf[2:3] * lf[3:4], axis=-1, keepdims=True)) + lam_init)
    chunks = _key_chunks(src_rows, KEY_CHUNK_DIFF)
    sub = s1_scr.shape[1]
    n_sub = q1_ref.shape[0] // sub
    streams = []
    for u in range(n_sub):
        rows = slice(u * sub, (u + 1) * sub)
        streams.append((_transpose_bf16(q1_ref[rows, :]), k1_scr, s1_scr))
        streams.append((_transpose_bf16(q2_ref[rows, :]), k2_scr, s2_scr))
    outs = _pipelined_attention(streams, vt_scr, chunks)
    o_t = jnp.concatenate([outs[2 * u] - lam * outs[2 * u + 1] for u in range(n_sub)],
                          axis=1)
    o = o_t.T
    ms = jnp.mean(o * o, axis=-1, keepdims=True)
    o_ref[...] = _bf(o * lax.rsqrt(ms + EPS) * sub_ref[...] * (1.0 - lam_init))


def _diff_call(qkv_q, lq, sources, diff_lambda, subln, lam_init, tq, name):
    tq = min(tq, lq)
    sub = min(TQ_SUB, tq)
    nq = lq // tq
    qb0 = 3 * NA_WIDTH // LANES
    kb0 = (3 * NA_WIDTH + DIFF_WIDTH) // LANES
    vb0 = (3 * NA_WIDTH + 2 * DIFF_WIDTH) // (2 * LANES)
    in_specs = [
        pl.BlockSpec((tq, LANES), lambda b, h, i: (b * nq + i, qb0 + 2 * h)),
        pl.BlockSpec((tq, LANES), lambda b, h, i: (b * nq + i, qb0 + 2 * h + 1)),
    ]
    args = [qkv_q, qkv_q]
    src_rows = []
    for rows, arr in sources:
        src_rows.append(rows)
        in_specs += [
            pl.BlockSpec((rows, LANES), lambda b, h, i: (b, kb0 + 2 * h)),
            pl.BlockSpec((rows, LANES), lambda b, h, i: (b, kb0 + 2 * h + 1)),
            pl.BlockSpec((rows, 2 * LANES), lambda b, h, i: (b, vb0 + h)),
        ]
        args += [arr, arr, arr]
    in_specs += [
        pl.BlockSpec((4, LANES), lambda b, h, i: (0, 0)),
        pl.BlockSpec((1, 2 * LANES), lambda b, h, i: (0, 0)),
    ]
    args += [diff_lambda, subln]
    nk = sum(src_rows)
    kern = functools.partial(_diff_kernel, n_src=len(sources), src_rows=tuple(src_rows),
                             lam_init=lam_init)
    return pl.pallas_call(
        kern,
        grid=(BATCH, DIFF_HEADS, nq),
        in_specs=in_specs,
        out_specs=pl.BlockSpec((tq, 2 * LANES), lambda b, h, i: (b * nq + i, h)),
        out_shape=jax.ShapeDtypeStruct((BATCH * lq, DIFF_WIDTH), jnp.bfloat16),
        scratch_shapes=[
            pltpu.VMEM((nk, LANES), jnp.bfloat16),
            pltpu.VMEM((nk, LANES), jnp.bfloat16),
            pltpu.VMEM((2 * LANES, nk), jnp.bfloat16),
            pltpu.VMEM((nk, sub), jnp.float32),
            pltpu.VMEM((nk, sub), jnp.float32),
        ],
        compiler_params=_params("parallel", "parallel", "arbitrary"),
        name=name,
    )(*args)


def _rope_tables(d):
    h = d // 2
    half = h // 2
    t = jnp.arange(SEQ, dtype=jnp.int32)
    freqs = ROPE_BASE ** (-jnp.arange(half, dtype=jnp.float32) / half)

    def cs(pos):
        ang = pos.astype(jnp.float32)[:, None] * freqs[None, :]
        return jnp.cos(ang), jnp.sin(ang)

    cr, sr = cs(t // GRID_W)
    cc, sc = cs(t % GRID_W)
    cos = jnp.concatenate([cr, cr, cc, cc], -1)
    sin = jnp.concatenate([sr, sr, sc, sc], -1)
    first = (jnp.arange(d) % h) < half
    sin_a = jnp.where(first[None, :], -sin, 0.0)
    sin_b = jnp.where(first[None, :], 0.0, sin)
    pad = LANES - d
    if pad:
        cos = jnp.pad(cos, ((0, 0), (0, pad)), constant_values=1.0)
        sin_a = jnp.pad(sin_a, ((0, 0), (0, pad)))
        sin_b = jnp.pad(sin_b, ((0, 0), (0, pad)))
    pair_tables = tuple(jnp.concatenate([t, t], -1) for t in (cos, sin_a, sin_b))
    return (cos, sin_a, sin_b), half, (pair_tables, half)


def _pair_blockdiag(block):
    z = np.zeros_like(block)
    return jnp.asarray(np.block([[block, z], [z, block]]), dtype=jnp.bfloat16)


def _pad_lanes(v, fill=0.0):
    return jnp.pad(v, (0, LANES - v.shape[0]), constant_values=fill)


def _pair(v):
    return jnp.concatenate([v, v])


def _lat_row(tm):
    bpb = SEQ // tm
    return lambda i: i // bpb


def _ctx_row(i):
    return CTX_MOD_ROW


def _even_layer(x, xc, modarr, norm_mix, norm_ffn, ffn_w, w_in, w_out, qn_a, kn_a, rpb,
                qn_b, kn_b, diff_lambda, subln, lam_init, with_ctx):
    scale = HEAD_DIM ** -0.5 * LOG2E
    gains = jnp.stack([_pair(g) for g in (qn_a * scale, kn_a, qn_b * scale, kn_b)]
                      + [jnp.ones((PAIR,), jnp.float32)] * 4)
    _, _, rope = _rope_tables(HEAD_DIM)
    segs = [(True, 0, HEAD_DIM, False), (True, 1, HEAD_DIM, False), (False, 0, HEAD_DIM, False),
            (True, 2, HEAD_DIM, True), (True, 3, HEAD_DIM, True), (False, 0, HEAD_DIM, False)]
    w_in = _bf(w_in)
    w_out = _bf(w_out)
    gain_mix = norm_mix[None, :]
    lat_row = _lat_row(TM)

    qkv = _proj_call(x, w_in, gains, segs, NA_WIDTH, mod=(gain_mix, modarr, 0, 1), rope=rope,
                     mod_row=_lat_row(TM_EVEN), tm=TM_EVEN, name="even_proj")
    qkv_c = _proj_call(xc, w_in, gains, segs, NA_WIDTH, mod=(gain_mix, modarr, 0, 1), rope=None,
                       mod_row=_ctx_row, tm=TM_EVEN, name="even_proj_ctx")

    oa = _na_call(qkv, qkv_c, rpb)
    ob = _diff_call(qkv, SEQ, [(SEQ, qkv), (CTX_LEN, qkv_c)], diff_lambda, subln[None, :],
                    lam_init, TQ_DIFF, "diff_attn")
    x = _outproj_call([oa, ob], w_out, x, modarr, 2, lat_row, name="even_out")
    x = _ffn_call(x, norm_ffn[None, :], modarr, *ffn_w, _lat_row(TM_FF), SEQ)
    if with_ctx:
        oa_c = _attn_call(
            [(qkv_c, lambda h: h)],
            [(CTX_LEN, [(qkv_c, lambda h: NA_HEADS + h)], (qkv_c, lambda h: 2 * NA_HEADS + h))],
            NA_HEADS, CTX_LEN, CTX_LEN, HEAD_DIM, "ctx_attn")
        ob_c = _diff_call(qkv_c, CTX_LEN, [(CTX_LEN, qkv_c)], diff_lambda, subln[None, :],
                          lam_init, CTX_LEN, "diff_attn_ctx")
        xc = _outproj_call([oa_c, ob_c], w_out, xc, modarr, 2, _ctx_row, name="even_out_ctx")
        xc = _ffn_call(xc, norm_ffn[None, :], modarr, *ffn_w, _ctx_row, CTX_LEN,
                       name="conv_ffn_ctx")
    return x, xc


def _odd_layer(x, xc, modarr, norm_mix, norm_ffn, ffn_w, w_down, q_a_norm, kv_a_norm, w_uq,
               w_ukv, qn_nope, qn_rope, kn_nope, kn_rope, w_out, with_ctx):
    scale = (MLA_NOPE + MLA_ROPE) ** -0.5 * LOG2E
    _, _, rope_pair = _rope_tables(MLA_ROPE)
    gain_mix = norm_mix[None, :]
    lat_row = _lat_row(TM)

    w_down_p = _bf(jnp.pad(w_down, ((0, 0), (0, MLA_DOWN_PAD - MLA_DOWN))))
    zeros_r = jnp.zeros((MLA_Q_RANK,), jnp.float32)
    kr_gain = jnp.concatenate([_pad_lanes(kn_rope), zeros_r[:MLA_Q_RANK - LANES]])
    gains_d = jnp.stack([q_a_norm, kv_a_norm, kr_gain] + [zeros_r] * 5)
    wq = w_uq.reshape(MLA_Q_RANK, MLA_HEADS, MLA_NOPE + MLA_ROPE)
    wq_rope = jnp.pad(wq[:, :, MLA_NOPE:], ((0, 0), (0, 0), (0, LANES - MLA_ROPE)))
    wq_p = _bf(jnp.concatenate([wq[:, :, :MLA_NOPE].reshape(MLA_Q_RANK, -1),
                                wq_rope.reshape(MLA_Q_RANK, -1)], -1))
    wkv = w_ukv.reshape(MLA_KV_RANK, MLA_HEADS, MLA_NOPE + MLA_V)
    wkv_p = _bf(jnp.concatenate([wkv[:, :, :MLA_NOPE].reshape(MLA_KV_RANK, -1),
                                 wkv[:, :, MLA_NOPE:].reshape(MLA_KV_RANK, -1)], -1))
    gains_p = jnp.stack([_pair(qn_nope * scale), _pair(_pad_lanes(qn_rope) * scale),
                         _pair(kn_nope)] + [jnp.zeros((PAIR,), jnp.float32)] * 5)
    mod_in = (gain_mix, modarr, 0, 1)
    q, kv, kr = _mla_proj_call(x, w_down_p, wq_p, wkv_p, gains_d, gains_p, mod_in, rope_pair,
                               _lat_row(TM_MLA_PROJ), True, "mla_proj")
    kv_c, kr_c = _mla_proj_call(xc, w_down_p, wq_p, wkv_p, gains_d, gains_p, mod_in, None,
                                _ctx_row, False, "mla_proj_ctx")

    def src(rows, kv_arr, kr_arr):
        return (rows, [(kv_arr, lambda h: h), (kr_arr, lambda h: 0)],
                (kv_arr, lambda h: MLA_HEADS + h))

    o = _attn_call([(q, lambda h: h), (q, lambda h: MLA_HEADS + h)],
                   [src(SEQ, kv, kr), src(CTX_LEN, kv_c, kr_c)],
                   MLA_HEADS, SEQ, TQ_MLA, MLA_V, "mla_attn")
    x = _outproj_call([o], _bf(w_out), x, modarr, 2, lat_row, name="mla_out")
    x = _ffn_call(x, norm_ffn[None, :], modarr, *ffn_w, _lat_row(TM_FF), SEQ)
    assert not with_ctx
    return x, xc


def _mla_proj_kernel(x_ref, g_ref, sh_ref, sc_ref, wd_ref, wq_ref, wkv_ref, gd_ref, gp_ref,
                     ones_ref, *rest, rope, rope_half, with_q):
    rope_refs = rest[:3] if rope else None
    outs = rest[3:] if rope else rest
    if with_q:
        q_ref, kv_ref, kr_ref = outs
    else:
        kv_ref, kr_ref = outs
    h = _bf(_modulate(x_ref[...], g_ref[...], sh_ref[0], sc_ref[0]))
    down = _dot(h, wd_ref[...])

    def latent_norm(xs, row, nvalid):
        ms = jnp.sum(xs * xs, axis=-1, keepdims=True) * (1.0 / nvalid)
        return xs * lax.rsqrt(ms + EPS) * gd_ref[row:row + 1, 0:xs.shape[1]]

    kva = _bf(latent_norm(down[:, MLA_Q_RANK:MLA_Q_RANK + MLA_KV_RANK], 1, MLA_KV_RANK))
    kr = latent_norm(down[:, MLA_Q_RANK + MLA_KV_RANK:MLA_DOWN_PAD], 2, MLA_ROPE)
    if rope:
        cos_ref, sa_ref, sb_ref = rope_refs
        kr = (kr * cos_ref[:, 0:LANES]
              + pltpu.roll(kr, LANES - rope_half, axis=1) * sa_ref[:, 0:LANES]
              + pltpu.roll(kr, rope_half, axis=1) * sb_ref[:, 0:LANES])
    kr_ref[...] = _bf(kr)

    def up(a, w_ref, o_ref, specs):
        width = w_ref.shape[1] // len(specs)
        for c in range(0, w_ref.shape[1], SUB_N):
            acc = _dot(a, w_ref[:, c:c + SUB_N])
            for p in range(0, SUB_N, PAIR):
                y = _pair_epilogue(acc[:, p:p + PAIR], specs[c // width], gp_ref, ones_ref,
                                   rope_refs, rope_half)
                o_ref[:, c + p:c + p + PAIR] = _bf(y)

    up(kva, wkv_ref, kv_ref, [(True, 2, MLA_NOPE, False), (False, 0, MLA_V, False)])
    if with_q:
        qa = _bf(latent_norm(down[:, 0:MLA_Q_RANK], 0, MLA_Q_RANK))
        up(qa, wq_ref, q_ref, [(True, 0, MLA_NOPE, False), (True, 1, MLA_ROPE, True)])


def _mla_proj_call(x, wd, wq, wkv, gains_d, gains_p, mod, rope, mod_row, with_q, name,
                   tm=TM_MLA_PROJ):
    m, k = x.shape
    tm = min(tm, m)
    bpb = SEQ // tm
    gain, modarr, sh_c, sc_c = mod
    ones = _pair_blockdiag(np.ones((LANES, LANES), np.float32))
    in_specs = [
        pl.BlockSpec((tm, k), lambda i: (i, 0)),
        pl.BlockSpec((1, k), lambda i: (0, 0)),
        pl.BlockSpec((1, 1, k), lambda i: (mod_row(i), 0, sh_c)),
        pl.BlockSpec((1, 1, k), lambda i: (mod_row(i), 0, sc_c)),
    ]
    args = [x, gain, modarr, modarr]
    for w in (wd, wq, wkv, gains_d, gains_p, ones):
        in_specs.append(pl.BlockSpec(w.shape, lambda i: (0, 0)))
        args.append(w)
    rope_half = 0
    if rope is not None:
        tables, rope_half = rope
        for t in tables:
            in_specs.append(pl.BlockSpec((tm, PAIR), lambda i: (i % bpb, 0)))
            args.append(t)
    widths = ([wq.shape[1]] if with_q else []) + [wkv.shape[1], LANES]
    return pl.pallas_call(
        functools.partial(_mla_proj_kernel, rope=rope is not None, rope_half=rope_half,
                          with_q=with_q),
        grid=(m // tm,),
        in_specs=in_specs,
        out_specs=[pl.BlockSpec((tm, n), lambda i: (i, 0)) for n in widths],
        out_shape=[jax.ShapeDtypeStruct((m, n), jnp.bfloat16) for n in widths],
        compiler_params=_params("parallel"),
        name=name,
    )(*args)


def _w_in_prep_kernel(x_ref, o_ref):
    pad = jnp.zeros((x_ref.shape[0], D_FF_PAD - D_FF), jnp.bfloat16)
    o_ref[:, 0:D_FF] = _bf(x_ref[:, 0:D_FF])
    o_ref[:, D_FF:D_FF_PAD] = pad
    o_ref[:, D_FF_PAD:D_FF_PAD + D_FF] = _bf(x_ref[:, D_FF:2 * D_FF])
    o_ref[:, D_FF_PAD + D_FF:2 * D_FF_PAD] = pad


def _w_out_prep_kernel(x_ref, o_ref):
    o_ref[0:D_FF, :] = _bf(x_ref[...])
    o_ref[D_FF:D_FF_PAD, :] = jnp.zeros((D_FF_PAD - D_FF, x_ref.shape[1]), jnp.bfloat16)


def _ffn_weights(w_in, conv_w, w_out):
    depth, d, _ = w_in.shape
    rows, cols = 256, 256
    w_in_p = pl.pallas_call(
        _w_in_prep_kernel,
        grid=(depth, d // rows),
        in_specs=[pl.BlockSpec((None, rows, 2 * D_FF), lambda l, i: (l, i, 0))],
        out_specs=pl.BlockSpec((None, rows, 2 * D_FF_PAD), lambda l, i: (l, i, 0)),
        out_shape=jax.ShapeDtypeStruct((depth, d, 2 * D_FF_PAD), jnp.bfloat16),
        compiler_params=_params("parallel", "parallel"),
        name="ffn_w_in_prep",
    )(w_in)
    w_out_p = pl.pallas_call(
        _w_out_prep_kernel,
        grid=(depth, d // cols),
        in_specs=[pl.BlockSpec((None, D_FF, cols), lambda l, j: (l, 0, j))],
        out_specs=pl.BlockSpec((None, D_FF_PAD, cols), lambda l, j: (l, 0, j)),
        out_shape=jax.ShapeDtypeStruct((depth, D_FF_PAD, d), jnp.bfloat16),
        compiler_params=_params("parallel", "parallel"),
        name="ffn_w_out_prep",
    )(w_out)
    pad = D_FF_PAD - D_FF
    zc = jnp.zeros(conv_w.shape[:2] + (pad,), conv_w.dtype)
    conv_p = jnp.concatenate([conv_w[..., :D_FF], zc, conv_w[..., D_FF:], zc], -1)
    return w_in_p, conv_p, w_out_p


def kernel(x, c, ctx, c_ctx, ada_w, ada_b, norm_mix, norm_ffn, ffn_w_in, ffn_conv, ffn_w_out, even_w_in, even_w_out, na_q_norm, na_k_norm, na_rpb, diff_q_norm, diff_k_norm, diff_lambda, diff_subln, mla_w_down, mla_q_a_norm, mla_kv_a_norm, mla_w_uq, mla_w_ukv, mla_q_nope_norm, mla_q_rope_norm, mla_k_nope_norm, mla_k_rope_norm, mla_w_out):
    cond = jnp.concatenate(
        [c, c_ctx[None, :], jnp.zeros((MOD_ROWS - BATCH - 1, D_MODEL), jnp.float32)], 0)
    mod = _ada_call(cond, ada_w, ada_b)
    xl = x.reshape(BATCH * SEQ, D_MODEL)
    xc = ctx.reshape(BATCH * CTX_LEN, D_MODEL)
    ffn_all = _ffn_weights(ffn_w_in, ffn_conv, ffn_w_out)
    for l in range(DEPTH):
        with_ctx = l < DEPTH - 1
        modarr = mod[l].reshape(MOD_ROWS, 1, 6 * D_MODEL)
        ffn_w = ffn_all + (l,)
        i = l // 2
        if l % 2 == 0:
            lam_init = 0.8 - 0.6 * math.exp(-0.3 * l)
            xl, xc = _even_layer(xl, xc, modarr, norm_mix[l], norm_ffn[l], ffn_w, even_w_in[i],
                                 even_w_out[i], na_q_norm[i], na_k_norm[i], na_rpb[i],
                                 diff_q_norm[i], diff_k_norm[i], diff_lambda[i], diff_subln[i],
                                 lam_init, with_ctx)
        else:
            xl, xc = _odd_layer(xl, xc, modarr, norm_mix[l], norm_ffn[l], ffn_w, mla_w_down[i],
                                mla_q_a_norm[i], mla_kv_a_norm[i], mla_w_uq[i], mla_w_ukv[i],
                                mla_q_nope_norm[i], mla_q_rope_norm[i], mla_k_nope_norm[i],
                                mla_k_rope_norm[i], mla_w_out[i], with_ctx)
    return xl.reshape(BATCH, SEQ, D_MODEL)
```

```python
import functools
import math

import numpy as np
import jax
import jax.numpy as jnp
from jax import lax
from jax.experimental import pallas as pl
from jax.experimental.pallas import tpu as pltpu

D_MODEL = 2048
BATCH = 4
SEQ = 4096
DEPTH = 2
GRID_W = 64
GRID_H = SEQ // GRID_W
CTX_LEN = 256
HEAD_DIM = 128
NA_HEADS = 8
NA_WIN_H = 8
NA_WIN_W = 16
DIFF_HEADS = 4
NA_WIDTH = 1024
DIFF_WIDTH = 1024
EVEN_PROJ = 6144
MLA_HEADS = 16
MLA_Q_RANK = 512
MLA_KV_RANK = 512
MLA_NOPE = 128
MLA_ROPE = 64
MLA_V = 128
MLA_DOWN = 1088
D_FF = 5504
ROPE_BASE = 10000.0
EPS = 1e-6

LANES = 128
SUBLANES = 8
MOD_ROWS = 8
CTX_MOD_ROW = BATCH
D_FF_PAD = 5632
MLA_DOWN_PAD = 1152
VMEM_LIMIT = 56 * 1024 * 1024

TM = 512
TN = 1024
TN_FF = 512
TM_FF = 1024
TM_EVEN = 1024
TM_MLA_PROJ = 256
HM_FF = 512
SUB_N = 512
PAIR = 2 * LANES
NA_GROUP_ROWS = 4
NA_KEY_ROWS = NA_GROUP_ROWS + NA_WIN_H
NA_UNROLL = 4
NA_Q = NA_GROUP_ROWS * GRID_W
NA_K = NA_KEY_ROWS * GRID_W
TQ_DIFF = 1024
TQ_MLA = 2048
TQ_SUB = 256
KEY_CHUNK_ATTN = 256
KEY_CHUNK_DIFF = 512
NEG = -1e30
LOG2E = math.log2(math.e)

_NT = (((1,), (1,)), ((), ()))


def _params(*sem):
    return pltpu.CompilerParams(dimension_semantics=sem, vmem_limit_bytes=VMEM_LIMIT)


def _bf(x):
    return x.astype(jnp.bfloat16)


def _dot(a, b):
    return jnp.dot(a, b, preferred_element_type=jnp.float32)


def _dot_nt(a, b):
    return lax.dot_general(a, b, _NT, preferred_element_type=jnp.float32)


def _modulate(x, gain, shift, scale, mxu_stats=False):
    d = x.shape[1]
    if mxu_stats:
        ss = _dot(_bf(x * x), jnp.ones((d, LANES), jnp.bfloat16))
        r = jnp.tile(lax.rsqrt(ss * (1.0 / d) + EPS), (1, d // LANES))
    else:
        r = lax.rsqrt(jnp.mean(x * x, axis=-1, keepdims=True) + EPS)
    return x * r * (gain * (1.0 + scale)) + shift


def _ada_kernel(c_ref, w_ref, b_ref, o_ref):
    c = c_ref[...]
    s = c * (1.0 / (1.0 + jnp.exp(-c)))
    o_ref[0] = _dot(_bf(s), _bf(w_ref[0])) + b_ref[0]


def _ada_call(cond, ada_w, ada_b):
    tn = 512
    n = ada_w.shape[-1]
    return pl.pallas_call(
        _ada_kernel,
        grid=(DEPTH, n // tn),
        in_specs=[
            pl.BlockSpec((MOD_ROWS, D_MODEL), lambda l, j: (0, 0)),
            pl.BlockSpec((1, D_MODEL, tn), lambda l, j: (l, 0, j)),
            pl.BlockSpec((1, 1, tn), lambda l, j: (l, 0, j)),
        ],
        out_specs=pl.BlockSpec((1, MOD_ROWS, tn), lambda l, j: (l, 0, j)),
        out_shape=jax.ShapeDtypeStruct((DEPTH, MOD_ROWS, n), jnp.float32),
        compiler_params=_params("parallel", "parallel"),
        name="ada_mod",
    )(cond, ada_w, ada_b.reshape(DEPTH, 1, n))


def _pair_epilogue(acc, spec, gains_ref, ones_ref, rope_refs, rope_half):
    norm, gain_row, nvalid, do_rope = spec
    y = acc
    if norm:
        ss = _dot(_bf(acc * acc), ones_ref[...])
        y = acc * lax.rsqrt(ss * (1.0 / nvalid) + EPS) * gains_ref[gain_row:gain_row + 1, :]
    if do_rope and rope_refs is not None:
        cos_ref, sa_ref, sb_ref = rope_refs
        y = (y * cos_ref[...]
             + pltpu.roll(y, PAIR - rope_half, axis=1) * sa_ref[...]
             + pltpu.roll(y, rope_half, axis=1) * sb_ref[...])
    return y


def _proj_kernel(*refs, mod, segs, seg_tiles, rope, rope_half):
    it = iter(refs)
    x_ref = next(it)
    if mod:
        g_ref, sh_ref, sc_ref = next(it), next(it), next(it)
    w_ref = next(it)
    gains_ref, ones_ref = next(it), next(it)
    rope_refs = (next(it), next(it), next(it)) if rope else None
    o_ref = next(it)
    h_ref = next(it) if mod else None
    j = pl.program_id(1)
    tn = o_ref.shape[1]

    if mod:
        @pl.when(j == 0)
        def _():
            h_ref[...] = _bf(_modulate(x_ref[...], g_ref[...], sh_ref[0], sc_ref[0],
                                       mxu_stats=True))
        a = h_ref[...]
    else:
        a = x_ref[...]
    seg = j // seg_tiles
    for s, spec in enumerate(segs):
        @pl.when(seg == s)
        def _(spec=spec):
            for c in range(0, tn, SUB_N):
                acc = _dot(a, w_ref[:, c:c + SUB_N])
                for p in range(0, SUB_N, PAIR):
                    y = _pair_epilogue(acc[:, p:p + PAIR], spec, gains_ref, ones_ref, rope_refs,
                                       rope_half)
                    o_ref[:, c + p:c + p + PAIR] = _bf(y)


def _proj_call(x, w, gains, segs, seg_width, *, mod=None, rope=None, mod_row=None,
               tm=TM, tn=TN, name="proj"):
    m, k = x.shape
    n = w.shape[1]
    tm = min(tm, m)
    tn = min(tn, seg_width)
    assert m % tm == 0 and seg_width % tn == 0 and n % seg_width == 0
    seg_tiles = seg_width // tn
    bpb = SEQ // tm

    in_specs = [pl.BlockSpec((tm, k), lambda i, j: (i, 0))]
    args = [x]
    if mod is not None:
        gain, modarr, sh_c, sc_c = mod
        in_specs += [
            pl.BlockSpec((1, k), lambda i, j: (0, 0)),
            pl.BlockSpec((1, 1, k), lambda i, j: (mod_row(i), 0, sh_c)),
            pl.BlockSpec((1, 1, k), lambda i, j: (mod_row(i), 0, sc_c)),
        ]
        args += [gain, modarr, modarr]
    in_specs.append(pl.BlockSpec((k, tn), lambda i, j: (0, j)))
    args.append(w)
    ones = _pair_blockdiag(np.ones((LANES, LANES), np.float32))
    in_specs += [pl.BlockSpec(gains.shape, lambda i, j: (0, 0)),
                 pl.BlockSpec(ones.shape, lambda i, j: (0, 0))]
    args += [gains, ones]
    if rope is not None:
        tables, rope_half = rope
        for t in tables:
            in_specs.append(pl.BlockSpec((tm, PAIR), lambda i, j: (i % bpb, 0)))
            args.append(t)
    else:
        rope_half = 0
    kern = functools.partial(_proj_kernel, mod=mod is not None, segs=tuple(segs),
                             seg_tiles=seg_tiles, rope=rope is not None, rope_half=rope_half)
    return pl.pallas_call(
        kern,
        grid=(m // tm, n // tn),
        in_specs=in_specs,
        out_specs=pl.BlockSpec((tm, tn), lambda i, j: (i, j)),
        out_shape=jax.ShapeDtypeStruct((m, n), jnp.bfloat16),
        scratch_shapes=[pltpu.VMEM((tm, k), jnp.bfloat16)] if mod is not None else [],
        compiler_params=_params("parallel", "arbitrary"),
        name=name,
    )(*args)


def _outproj_kernel(*refs, n_a):
    a_refs = refs[:n_a]
    w_ref, x_ref, g_ref, o_ref = refs[n_a:]
    n = o_ref.shape[1]
    a_vals = [a_ref[...] for a_ref in a_refs]
    for c in range(0, n, SUB_N):
        acc, row = None, 0
        for a in a_vals:
            part = _dot(a, w_ref[row:row + a.shape[1], c:c + SUB_N])
            acc = part if acc is None else acc + part
            row += a.shape[1]
        o_ref[:, c:c + SUB_N] = x_ref[:, c:c + SUB_N] + g_ref[0][:, c:c + SUB_N] * acc


def _outproj_call(a_list, w, x, modarr, gate_chunk, mod_row, *, tm=TM, name="outproj"):
    m, n = x.shape
    tm = min(tm, m)
    in_specs, args = [], []
    for a in a_list:
        in_specs.append(pl.BlockSpec((tm, a.shape[1]), lambda i: (i, 0)))
        args.append(a)
    in_specs += [
        pl.BlockSpec(w.shape, lambda i: (0, 0)),
        pl.BlockSpec((tm, n), lambda i: (i, 0)),
        pl.BlockSpec((1, 1, n), lambda i: (mod_row(i), 0, gate_chunk)),
    ]
    args += [w, x, modarr]
    return pl.pallas_call(
        functools.partial(_outproj_kernel, n_a=len(a_list)),
        grid=(m // tm,),
        in_specs=in_specs,
        out_specs=pl.BlockSpec((tm, n), lambda i: (i, 0)),
        out_shape=jax.ShapeDtypeStruct((m, n), jnp.float32),
        compiler_params=_params("parallel"),
        name=name,
    )(*args)


def _ffn_kernel(x_ref, xp_ref, xn_ref, g_ref, sh_ref, sc_ref, gt_ref, wa_ref, wb_ref,
                ca_ref, cb_ref, wo_ref, o_ref, h_ref, *, tm, hm, seq_len):
    i = pl.program_id(0)
    j = pl.program_id(1)
    nj = pl.num_programs(1)
    n_grp = tm // hm
    rows = hm + 2 * SUBLANES

    @pl.when(j == 0)
    def _():
        gain, shift, scale = g_ref[...], sh_ref[0], sc_ref[0]
        for r in range(n_grp):
            r0 = r * hm
            h_ref[r * rows:r * rows + hm, :] = _bf(
                _modulate(x_ref[r0:r0 + hm, :], gain, shift, scale, mxu_stats=True))
            nxt_src = x_ref[r0 + hm:r0 + hm + SUBLANES, :] if r + 1 < n_grp else xn_ref[...]
            prv_src = x_ref[r0 - SUBLANES:r0, :] if r > 0 else xp_ref[...]
            keep_next = jnp.where((i * tm + r0 + hm) % seq_len == 0, 0.0, 1.0)
            keep_prev = jnp.where((i * tm + r0) % seq_len == 0, 0.0, 1.0)
            nxt = keep_next * _modulate(nxt_src, gain, shift, scale)
            prv = keep_prev * _modulate(prv_src, gain, shift, scale)
            h_ref[r * rows + hm:(r + 1) * rows, :] = _bf(jnp.concatenate([nxt, prv], axis=0))
        o_ref[...] = jnp.zeros_like(o_ref)

    ca, cb = ca_ref[...], cb_ref[...]
    ups = []
    for r in range(n_grp):
        h = h_ref[r * rows:(r + 1) * rows, :]
        ups.append((_dot(h, wa_ref[...]), _dot(h, wb_ref[...])))

    def conv(u, cw):
        prev = pltpu.roll(u, 1, axis=0)[0:hm]
        nxt = pltpu.roll(u, rows - 1, axis=0)[0:hm]
        return prev * cw[0:1] + u[0:hm] * cw[1:2] + nxt * cw[2:3]

    for r, (ua, ub) in enumerate(ups):
        a = conv(ua, ca)
        b = conv(ub, cb)
        act = a * (1.0 / (1.0 + jnp.exp(-a))) * b
        o_ref[r * hm:(r + 1) * hm, :] += _dot(_bf(act), wo_ref[...])

    @pl.when(j == nj - 1)
    def _():
        o_ref[...] = x_ref[...] + gt_ref[0] * o_ref[...]


def _ffn_call(x, gain, modarr, w_in, conv_w, w_out, layer, mod_row, seq_len, *, tm=TM_FF,
              tn=TN_FF, name="conv_ffn"):
    m, d = x.shape
    tm = min(tm, m)
    hm = min(HM_FF, seq_len)
    assert m % tm == 0 and tm % hm == 0 and seq_len % hm == 0
    nj = D_FF_PAD // tn
    hb = tm // SUBLANES
    last_hb = m // SUBLANES - 1
    kern = functools.partial(_ffn_kernel, tm=tm, hm=hm, seq_len=seq_len)
    return pl.pallas_call(
        kern,
        grid=(m // tm, nj),
        in_specs=[
            pl.BlockSpec((tm, d), lambda i, j: (i, 0)),
            pl.BlockSpec((SUBLANES, d), lambda i, j: (jnp.maximum(i * hb - 1, 0), 0)),
            pl.BlockSpec((SUBLANES, d), lambda i, j: (jnp.minimum((i + 1) * hb, last_hb), 0)),
            pl.BlockSpec((1, d), lambda i, j: (0, 0)),
            pl.BlockSpec((1, 1, d), lambda i, j: (mod_row(i), 0, 3)),
            pl.BlockSpec((1, 1, d), lambda i, j: (mod_row(i), 0, 4)),
            pl.BlockSpec((1, 1, d), lambda i, j: (mod_row(i), 0, 5)),
            pl.BlockSpec((None, d, tn), lambda i, j: (layer, 0, j)),
            pl.BlockSpec((None, d, tn), lambda i, j: (layer, 0, nj + j)),
            pl.BlockSpec((None, 3, tn), lambda i, j: (layer, 0, j)),
            pl.BlockSpec((None, 3, tn), lambda i, j: (layer, 0, nj + j)),
            pl.BlockSpec((None, tn, d), lambda i, j: (layer, j, 0)),
        ],
        out_specs=pl.BlockSpec((tm, d), lambda i, j: (i, 0)),
        out_shape=jax.ShapeDtypeStruct((m, d), jnp.float32),
        scratch_shapes=[
            pltpu.VMEM(((tm // hm) * (hm + 2 * SUBLANES), d), jnp.bfloat16),
        ],
        compiler_params=_params("parallel", "arbitrary"),
        name=name,
    )(x, x, x, gain, modarr, modarr, modarr, w_in, w_in, conv_w, conv_w, w_out)


def _na_group_geometry(g):
    r0 = g * NA_GROUP_ROWS
    return r0, min(max(r0 - NA_WIN_H // 2, 0), GRID_H - NA_KEY_ROWS)


def _na_build_bias(rpb_ref, h, tc_scr, bias_scr):
    n_dr = 2 * NA_WIN_H - 1
    n_dc = 2 * NA_WIN_W - 1
    qc = lax.broadcasted_iota(jnp.int32, (GRID_W, GRID_W), 0)
    kc = lax.broadcasted_iota(jnp.int32, (GRID_W, GRID_W), 1)
    col0 = jnp.clip(qc - NA_WIN_W // 2, 0, GRID_W - NA_WIN_W)
    col_valid = (kc >= col0) & (kc < col0 + NA_WIN_W)
    delta = kc - qc + (NA_WIN_W - 1)
    base = h * (n_dr * n_dc)
    for dr in range(n_dr):
        acc = jnp.zeros((GRID_W, GRID_W), jnp.float32)
        for e in range(n_dc):
            acc = jnp.where(delta == e, rpb_ref[base + dr * n_dc + e] * LOG2E, acc)
        tc_scr[dr] = jnp.where(col_valid, acc, NEG)
    tc_scr[n_dr] = jnp.full((GRID_W, GRID_W), NEG, jnp.float32)
    n_groups = GRID_H // NA_GROUP_ROWS
    for t, g in enumerate((0, 1, n_groups - 1)):
        r0, ks = _na_group_geometry(g)
        for qr in range(NA_GROUP_ROWS):
            r = r0 + qr
            row0 = min(max(r - NA_WIN_H // 2, 0), GRID_H - NA_WIN_H)
            pieces = []
            for kr in range(NA_KEY_ROWS):
                kra = ks + kr
                inside = row0 <= kra < row0 + NA_WIN_H
                pieces.append(tc_scr[kra - r + (NA_WIN_H - 1)] if inside else tc_scr[n_dr])
            bias_scr[t, qr * GRID_W:(qr + 1) * GRID_W, :] = jnp.concatenate(pieces, axis=1)


def _na_kernel(rpb_ref, q_ref, k_ref, v_ref, kc_ref, vc_ref, o_ref, tc_scr, bias_scr):
    @pl.when(pl.program_id(1) == 0)
    def _():
        _na_build_bias(rpb_ref, pl.program_id(0), tc_scr, bias_scr)

    kc = kc_ref[...]
    vc = vc_ref[...]
    n_groups = GRID_H // NA_GROUP_ROWS

    def group(g):
        r0 = g * NA_GROUP_ROWS
        ks = jnp.clip(r0 - NA_WIN_H // 2, 0, GRID_H - NA_KEY_ROWS)
        q0 = pl.multiple_of(g * NA_Q, NA_Q)
        k0 = pl.multiple_of(ks * GRID_W, GRID_W)
        tb = jnp.where(g == 0, 0, jnp.where(g == n_groups - 1, 2, 1))
        q = q_ref[pl.ds(q0, NA_Q), :]
        kw = k_ref[pl.ds(k0, NA_K), :]
        vw = v_ref[pl.ds(k0, NA_K), :]
        sw = _dot_nt(q, kw) + bias_scr[tb]
        sc = _dot_nt(q, kc)
        mx = jnp.maximum(jnp.max(sw, axis=-1, keepdims=True), jnp.max(sc, axis=-1, keepdims=True))
        pw = jnp.exp2(sw - mx)
        pc = jnp.exp2(sc - mx)
        den = jnp.sum(pw, axis=-1, keepdims=True) + jnp.sum(pc, axis=-1, keepdims=True)
        o = _dot(_bf(pw), vw) + _dot(_bf(pc), vc)
        o_ref[pl.ds(q0, NA_Q), :] = _bf(o / den)

    def body(t, carry):
        for u in range(NA_UNROLL):
            group(t * NA_UNROLL + u)
        return carry

    lax.fori_loop(0, n_groups // NA_UNROLL, body, 0)


def _na_call(qkv, qkv_c, rpb):
    n_dr = 2 * NA_WIN_H - 1
    return pl.pallas_call(
        _na_kernel,
        grid=(NA_HEADS, BATCH),
        in_specs=[
            pl.BlockSpec(memory_space=pltpu.SMEM),
            pl.BlockSpec((SEQ, LANES), lambda h, b: (b, h)),
            pl.BlockSpec((SEQ, LANES), lambda h, b: (b, NA_HEADS + h)),
            pl.BlockSpec((SEQ, LANES), lambda h, b: (b, 2 * NA_HEADS + h)),
            pl.BlockSpec((CTX_LEN, LANES), lambda h, b: (b, NA_HEADS + h)),
            pl.BlockSpec((CTX_LEN, LANES), lambda h, b: (b, 2 * NA_HEADS + h)),
        ],
        out_specs=pl.BlockSpec((SEQ, LANES), lambda h, b: (b, h)),
        out_shape=jax.ShapeDtypeStruct((BATCH * SEQ, NA_WIDTH), jnp.bfloat16),
        scratch_shapes=[
            pltpu.VMEM((n_dr + 1, GRID_W, GRID_W), jnp.float32),
            pltpu.VMEM((3, NA_Q, NA_K), jnp.float32),
        ],
        compiler_params=_params("parallel", "arbitrary"),
        name="na_attn",
    )(rpb.reshape(-1), qkv, qkv, qkv, qkv_c, qkv_c)


def _key_chunks(src_rows, chunk):
    chunks, r = [], 0
    for rows in src_rows:
        size = min(chunk, rows)
        assert rows % size == 0
        chunks += [(r + c, size) for c in range(0, rows, size)]
        r += rows
    return chunks


def _pipelined_attention(streams, vt_scr, chunks):
    dv = vt_scr.shape[0]
    sub = streams[0][2].shape[1]
    outs, m_prev = [], None
    for u in range(len(streams) + 1):
        m = jnp.full((1, sub), NEG, jnp.float32)
        l = jnp.zeros((1, sub), jnp.float32)
        acc = jnp.zeros((dv, sub), jnp.float32)
        for c0, cs in chunks:
            if u < len(streams):
                q_t, k_scr, s_ref = streams[u]
                s = _dot(k_scr[c0:c0 + cs, :], q_t)
                s_ref[c0:c0 + cs, :] = s
                m = jnp.maximum(m, jnp.max(s, axis=0, keepdims=True))
            if u > 0:
                p = jnp.exp2(streams[u - 1][2][c0:c0 + cs, :] - m_prev)
                l = l + jnp.sum(p, axis=0, keepdims=True)
                acc = acc + _dot(vt_scr[:, c0:c0 + cs], _bf(p))
        if u > 0:
            outs.append(acc / l)
        m_prev = m
    return outs


def _transpose_bf16(x):
    return _bf(x.astype(jnp.float32).T)


def _attn_kernel(*refs, n_qparts, n_src, n_kparts, src_rows):
    it = iter(refs)
    q_refs = [next(it) for _ in range(n_qparts)]
    srcs = []
    for _ in range(n_src):
        k_refs = [next(it) for _ in range(n_kparts)]
        srcs.append((k_refs, next(it)))
    o_ref = next(it)
    k_scr, vt_scr = next(it), next(it)
    s_scrs = list(it)

    @pl.when(pl.program_id(2) == 0)
    def _():
        r = 0
        for (k_refs, v_ref), rows in zip(srcs, src_rows):
            for p, k_ref in enumerate(k_refs):
                k_scr[r:r + rows, p * LANES:(p + 1) * LANES] = k_ref[...]
            vt_scr[:, r:r + rows] = _transpose_bf16(v_ref[...])
            r += rows

    q = jnp.concatenate([qr[...] for qr in q_refs], axis=-1) if n_qparts > 1 else q_refs[0][...]
    sub = s_scrs[0].shape[1]
    streams = [(_transpose_bf16(q[u * sub:(u + 1) * sub]), k_scr, s_scrs[u % 2])
               for u in range(q.shape[0] // sub)]
    outs = _pipelined_attention(streams, vt_scr, _key_chunks(src_rows, KEY_CHUNK_ATTN))
    for u, o_t in enumerate(outs):
        o_ref[u * sub:(u + 1) * sub, :] = _bf(o_t.T)


def _attn_call(q_parts, sources, n_heads, lq, tq, dv, name):
    tq = min(tq, lq)
    sub = min(TQ_SUB, tq)
    nq = lq // tq
    in_specs, args = [], []
    for arr, cf in q_parts:
        in_specs.append(pl.BlockSpec((tq, LANES), lambda b, h, i, cf=cf: (b * nq + i, cf(h))))
        args.append(arr)
    src_rows = []
    n_kparts = len(sources[0][1])
    for rows, k_parts, (v_arr, vcf) in sources:
        src_rows.append(rows)
        for arr, cf in k_parts:
            in_specs.append(pl.BlockSpec((rows, LANES), lambda b, h, i, cf=cf: (b, cf(h))))
            args.append(arr)
        in_specs.append(pl.BlockSpec((rows, dv), lambda b, h, i, cf=vcf: (b, cf(h))))
        args.append(v_arr)
    nk = sum(src_rows)
    kern = functools.partial(_attn_kernel, n_qparts=len(q_parts), n_src=len(sources),
                             n_kparts=n_kparts, src_rows=tuple(src_rows))
    return pl.pallas_call(
        kern,
        grid=(BATCH, n_heads, nq),
        in_specs=in_specs,
        out_specs=pl.BlockSpec((tq, dv), lambda b, h, i: (b * nq + i, h)),
        out_shape=jax.ShapeDtypeStruct((BATCH * lq, n_heads * dv), jnp.bfloat16),
        scratch_shapes=[
            pltpu.VMEM((nk, n_kparts * LANES), jnp.bfloat16),
            pltpu.VMEM((dv, nk), jnp.bfloat16),
        ] + [pltpu.VMEM((nk, sub), jnp.float32)] * 2,
        compiler_params=_params("parallel", "parallel", "arbitrary"),
        name=name,
    )(*args)


def _diff_kernel(*refs, n_src, src_rows, lam_init):
    it = iter(refs)
    q1_ref, q2_ref = next(it), next(it)
    srcs = [(next(it), next(it), next(it)) for _ in range(n_src)]
    lam_ref, sub_ref = next(it), next(it)
    o_ref = next(it)
    k1_scr, k2_scr, vt_scr, s1_scr, s2_scr = [next(it) for _ in range(5)]

    @pl.when(pl.program_id(2) == 0)
    def _():
        r = 0
        for (k1_ref, k2_ref, v_ref), rows in zip(srcs, src_rows):
            k1_scr[r:r + rows, :] = k1_ref[...]
            k2_scr[r:r + rows, :] = k2_ref[...]
            vt_scr[:, r:r + rows] = _transpose_bf16(v_ref[...])
            r += rows

    lf = lam_ref[...]
    lam = (jnp.exp(jnp.sum(lf[0:1] * lf[1:2], axis=-1, keepdims=True))
           - jnp.exp(jnp.sum(lf[2:3] * lf[3:4], axis=-1, keepdims=True)) + lam_init)
    chunks = _key_chunks(src_rows, KEY_CHUNK_DIFF)
    sub = s1_scr.shape[1]
    n_sub = q1_ref.shape[0] // sub
    streams = []
    for u in range(n_sub):
        rows = slice(u * sub, (u + 1) * sub)
        streams.append((_transpose_bf16(q1_ref[rows, :]), k1_scr, s1_scr))
        streams.append((_transpose_bf16(q2_ref[rows, :]), k2_scr, s2_scr))
    outs = _pipelined_attention(streams, vt_scr, chunks)
    o_t = jnp.concatenate([outs[2 * u] - lam * outs[2 * u + 1] for u in range(n_sub)],
                          axis=1)
    o = o_t.T
    ms = jnp.mean(o * o, axis=-1, keepdims=True)
    o_ref[...] = _bf(o * lax.rsqrt(ms + EPS) * sub_ref[...] * (1.0 - lam_init))


def _diff_call(qkv_q, lq, sources, diff_lambda, subln, lam_init, tq, name):
    tq = min(tq, lq)
    sub = min(TQ_SUB, tq)
    nq = lq // tq
    qb0 = 3 * NA_WIDTH // LANES
    kb0 = (3 * NA_WIDTH + DIFF_WIDTH) // LANES
    vb0 = (3 * NA_WIDTH + 2 * DIFF_WIDTH) // (2 * LANES)
    in_specs = [
        pl.BlockSpec((tq, LANES), lambda b, h, i: (b * nq + i, qb0 + 2 * h)),
        pl.BlockSpec((tq, LANES), lambda b, h, i: (b * nq + i, qb0 + 2 * h + 1)),
    ]
    args = [qkv_q, qkv_q]
    src_rows = []
    for rows, arr in sources:
        src_rows.append(rows)
        in_specs += [
            pl.BlockSpec((rows, LANES), lambda b, h, i: (b, kb0 + 2 * h)),
            pl.BlockSpec((rows, LANES), lambda b, h, i: (b, kb0 + 2 * h + 1)),
            pl.BlockSpec((rows, 2 * LANES), lambda b, h, i: (b, vb0 + h)),
        ]
        args += [arr, arr, arr]
    in_specs += [
        pl.BlockSpec((4, LANES), lambda b, h, i: (0, 0)),
        pl.BlockSpec((1, 2 * LANES), lambda b, h, i: (0, 0)),
    ]
    args += [diff_lambda, subln]
    nk = sum(src_rows)
    kern = functools.partial(_diff_kernel, n_src=len(sources), src_rows=tuple(src_rows),
                             lam_init=lam_init)
    return pl.pallas_call(
        kern,
        grid=(BATCH, DIFF_HEADS, nq),
        in_specs=in_specs,
        out_specs=pl.BlockSpec((tq, 2 * LANES), lambda b, h, i: (b * nq + i, h)),
        out_shape=jax.ShapeDtypeStruct((BATCH * lq, DIFF_WIDTH), jnp.bfloat16),
        scratch_shapes=[
            pltpu.VMEM((nk, LANES), jnp.bfloat16),
            pltpu.VMEM((nk, LANES), jnp.bfloat16),
            pltpu.VMEM((2 * LANES, nk), jnp.bfloat16),
            pltpu.VMEM((nk, sub), jnp.float32),
            pltpu.VMEM((nk, sub), jnp.float32),
        ],
        compiler_params=_params("parallel", "parallel", "arbitrary"),
        name=name,
    )(*args)


def _rope_tables(d):
    h = d // 2
    half = h // 2
    t = jnp.arange(SEQ, dtype=jnp.int32)
    freqs = ROPE_BASE ** (-jnp.arange(half, dtype=jnp.float32) / half)

    def cs(pos):
        ang = pos.astype(jnp.float32)[:, None] * freqs[None, :]
        return jnp.cos(ang), jnp.sin(ang)

    cr, sr = cs(t // GRID_W)
    cc, sc = cs(t % GRID_W)
    cos = jnp.concatenate([cr, cr, cc, cc], -1)
    sin = jnp.concatenate([sr, sr, sc, sc], -1)
    first = (jnp.arange(d) % h) < half
    sin_a = jnp.where(first[None, :], -sin, 0.0)
    sin_b = jnp.where(first[None, :], 0.0, sin)
    pad = LANES - d
    if pad:
        cos = jnp.pad(cos, ((0, 0), (0, pad)), constant_values=1.0)
        sin_a = jnp.pad(sin_a, ((0, 0), (0, pad)))
        sin_b = jnp.pad(sin_b, ((0, 0), (0, pad)))
    pair_tables = tuple(jnp.concatenate([t, t], -1) for t in (cos, sin_a, sin_b))
    return (cos, sin_a, sin_b), half, (pair_tables, half)


def _pair_blockdiag(block):
    z = np.zeros_like(block)
    return jnp.asarray(np.block([[block, z], [z, block]]), dtype=jnp.bfloat16)


def _pad_lanes(v, fill=0.0):
    return jnp.pad(v, (0, LANES - v.shape[0]), constant_values=fill)


def _pair(v):
    return jnp.concatenate([v, v])


def _lat_row(tm):
    bpb = SEQ // tm
    return lambda i: i // bpb


def _ctx_row(i):
    return CTX_MOD_ROW


def _even_layer(x, xc, modarr, norm_mix, norm_ffn, ffn_w, w_in, w_out, qn_a, kn_a, rpb,
                qn_b, kn_b, diff_lambda, subln, lam_init, with_ctx):
    scale = HEAD_DIM ** -0.5 * LOG2E
    gains = jnp.stack([_pair(g) for g in (qn_a * scale, kn_a, qn_b * scale, kn_b)]
                      + [jnp.ones((PAIR,), jnp.float32)] * 4)
    _, _, rope = _rope_tables(HEAD_DIM)
    segs = [(True, 0, HEAD_DIM, False), (True, 1, HEAD_DIM, False), (False, 0, HEAD_DIM, False),
            (True, 2, HEAD_DIM, True), (True, 3, HEAD_DIM, True), (False, 0, HEAD_DIM, False)]
    w_in = _bf(w_in)
    w_out = _bf(w_out)
    gain_mix = norm_mix[None, :]
    lat_row = _lat_row(TM)

    qkv = _proj_call(x, w_in, gains, segs, NA_WIDTH, mod=(gain_mix, modarr, 0, 1), rope=rope,
                     mod_row=_lat_row(TM_EVEN), tm=TM_EVEN, name="even_proj")
    qkv_c = _proj_call(xc, w_in, gains, segs, NA_WIDTH, mod=(gain_mix, modarr, 0, 1), rope=None,
                       mod_row=_ctx_row, tm=TM_EVEN, name="even_proj_ctx")

    oa = _na_call(qkv, qkv_c, rpb)
    ob = _diff_call(qkv, SEQ, [(SEQ, qkv), (CTX_LEN, qkv_c)], diff_lambda, subln[None, :],
                    lam_init, TQ_DIFF, "diff_attn")
    x = _outproj_call([oa, ob], w_out, x, modarr, 2, lat_row, name="even_out")
    x = _ffn_call(x, norm_ffn[None, :], modarr, *ffn_w, _lat_row(TM_FF), SEQ)
    if with_ctx:
        oa_c = _attn_call(
            [(qkv_c, lambda h: h)],
            [(CTX_LEN, [(qkv_c, lambda h: NA_HEADS + h)], (qkv_c, lambda h: 2 * NA_HEADS + h))],
            NA_HEADS, CTX_LEN, CTX_LEN, HEAD_DIM, "ctx_attn")
        ob_c = _diff_call(qkv_c, CTX_LEN, [(CTX_LEN, qkv_c)], diff_lambda, subln[None, :],
                          lam_init, CTX_LEN, "diff_attn_ctx")
        xc = _outproj_call([oa_c, ob_c], w_out, xc, modarr, 2, _ctx_row, name="even_out_ctx")
        xc = _ffn_call(xc, norm_ffn[None, :], modarr, *ffn_w, _ctx_row, CTX_LEN,
                       name="conv_ffn_ctx")
    return x, xc


def _odd_layer(x, xc, modarr, norm_mix, norm_ffn, ffn_w, w_down, q_a_norm, kv_a_norm, w_uq,
               w_ukv, qn_nope, qn_rope, kn_nope, kn_rope, w_out, with_ctx):
    scale = (MLA_NOPE + MLA_ROPE) ** -0.5 * LOG2E
    _, _, rope_pair = _rope_tables(MLA_ROPE)
    gain_mix = norm_mix[None, :]
    lat_row = _lat_row(TM)

    w_down_p = _bf(jnp.pad(w_down, ((0, 0), (0, MLA_DOWN_PAD - MLA_DOWN))))
    zeros_r = jnp.zeros((MLA_Q_RANK,), jnp.float32)
    kr_gain = jnp.concatenate([_pad_lanes(kn_rope), zeros_r[:MLA_Q_RANK - LANES]])
    gains_d = jnp.stack([q_a_norm, kv_a_norm, kr_gain] + [zeros_r] * 5)
    wq = w_uq.reshape(MLA_Q_RANK, MLA_HEADS, MLA_NOPE + MLA_ROPE)
    wq_rope = jnp.pad(wq[:, :, MLA_NOPE:], ((0, 0), (0, 0), (0, LANES - MLA_ROPE)))
    wq_p = _bf(jnp.concatenate([wq[:, :, :MLA_NOPE].reshape(MLA_Q_RANK, -1),
                                wq_rope.reshape(MLA_Q_RANK, -1)], -1))
    wkv = w_ukv.reshape(MLA_KV_RANK, MLA_HEADS, MLA_NOPE + MLA_V)
    wkv_p = _bf(jnp.concatenate([wkv[:, :, :MLA_NOPE].reshape(MLA_KV_RANK, -1),
                                 wkv[:, :, MLA_NOPE:].reshape(MLA_KV_RANK, -1)], -1))
    gains_p = jnp.stack([_pair(qn_nope * scale), _pair(_pad_lanes(qn_rope) * scale),
                         _pair(kn_nope)] + [jnp.zeros((PAIR,), jnp.float32)] * 5)
    mod_in = (gain_mix, modarr, 0, 1)
    q, kv, kr = _mla_proj_call(x, w_down_p, wq_p, wkv_p, gains_d, gains_p, mod_in, rope_pair,
                               _lat_row(TM_MLA_PROJ), True, "mla_proj")
    kv_c, kr_c = _mla_proj_call(xc, w_down_p, wq_p, wkv_p, gains_d, gains_p, mod_in, None,
                                _ctx_row, False, "mla_proj_ctx")

    def src(rows, kv_arr, kr_arr):
        return (rows, [(kv_arr, lambda h: h), (kr_arr, lambda h: 0)],
                (kv_arr, lambda h: MLA_HEADS + h))

    o = _attn_call([(q, lambda h: h), (q, lambda h: MLA_HEADS + h)],
                   [src(SEQ, kv, kr), src(CTX_LEN, kv_c, kr_c)],
                   MLA_HEADS, SEQ, TQ_MLA, MLA_V, "mla_attn")
    x = _outproj_call([o], _bf(w_out), x, modarr, 2, lat_row, name="mla_out")
    x = _ffn_call(x, norm_ffn[None, :], modarr, *ffn_w, _lat_row(TM_FF), SEQ)
    assert not with_ctx
    return x, xc


def _mla_proj_kernel(x_ref, g_ref, sh_ref, sc_ref, wd_ref, wq_ref, wkv_ref, gd_ref, gp_ref,
                     ones_ref, *rest, rope, rope_half, with_q):
    rope_refs = rest[:3] if rope else None
    outs = rest[3:] if rope else rest
    if with_q:
        q_ref, kv_ref, kr_ref = outs
    else:
        kv_ref, kr_ref = outs
    h = _bf(_modulate(x_ref[...], g_ref[...], sh_ref[0], sc_ref[0]))
    down = _dot(h, wd_ref[...])

    def latent_norm(xs, row, nvalid):
        ms = jnp.sum(xs * xs, axis=-1, keepdims=True) * (1.0 / nvalid)
        return xs * lax.rsqrt(ms + EPS) * gd_ref[row:row + 1, 0:xs.shape[1]]

    kva = _bf(latent_norm(down[:, MLA_Q_RANK:MLA_Q_RANK + MLA_KV_RANK], 1, MLA_KV_RANK))
    kr = latent_norm(down[:, MLA_Q_RANK + MLA_KV_RANK:MLA_DOWN_PAD], 2, MLA_ROPE)
    if rope:
        cos_ref, sa_ref, sb_ref = rope_refs
        kr = (kr * cos_ref[:, 0:LANES]
              + pltpu.roll(kr, LANES - rope_half, axis=1) * sa_ref[:, 0:LANES]
              + pltpu.roll(kr, rope_half, axis=1) * sb_ref[:, 0:LANES])
    kr_ref[...] = _bf(kr)

    def up(a, w_ref, o_ref, specs):
        width = w_ref.shape[1] // len(specs)
        for c in range(0, w_ref.shape[1], SUB_N):
            acc = _dot(a, w_ref[:, c:c + SUB_N])
            for p in range(0, SUB_N, PAIR):
                y = _pair_epilogue(acc[:, p:p + PAIR], specs[c // width], gp_ref, ones_ref,
                                   rope_refs, rope_half)
                o_ref[:, c + p:c + p + PAIR] = _bf(y)

    up(kva, wkv_ref, kv_ref, [(True, 2, MLA_NOPE, False), (False, 0, MLA_V, False)])
    if with_q:
        qa = _bf(latent_norm(down[:, 0:MLA_Q_RANK], 0, MLA_Q_RANK))
        up(qa, wq_ref, q_ref, [(True, 0, MLA_NOPE, False), (True, 1, MLA_ROPE, True)])


def _mla_proj_call(x, wd, wq, wkv, gains_d, gains_p, mod, rope, mod_row, with_q, name,
                   tm=TM_MLA_PROJ):
    m, k = x.shape
    tm = min(tm, m)
    bpb = SEQ // tm
    gain, modarr, sh_c, sc_c = mod
    ones = _pair_blockdiag(np.ones((LANES, LANES), np.float32))
    in_specs = [
        pl.BlockSpec((tm, k), lambda i: (i, 0)),
        pl.BlockSpec((1, k), lambda i: (0, 0)),
        pl.BlockSpec((1, 1, k), lambda i: (mod_row(i), 0, sh_c)),
        pl.BlockSpec((1, 1, k), lambda i: (mod_row(i), 0, sc_c)),
    ]
    args = [x, gain, modarr, modarr]
    for w in (wd, wq, wkv, gains_d, gains_p, ones):
        in_specs.append(pl.BlockSpec(w.shape, lambda i: (0, 0)))
        args.append(w)
    rope_half = 0
    if rope is not None:
        tables, rope_half = rope
        for t in tables:
            in_specs.append(pl.BlockSpec((tm, PAIR), lambda i: (i % bpb, 0)))
            args.append(t)
    widths = ([wq.shape[1]] if with_q else []) + [wkv.shape[1], LANES]
    return pl.pallas_call(
        functools.partial(_mla_proj_kernel, rope=rope is not None, rope_half=rope_half,
                          with_q=with_q),
        grid=(m // tm,),
        in_specs=in_specs,
        out_specs=[pl.BlockSpec((tm, n), lambda i: (i, 0)) for n in widths],
        out_shape=[jax.ShapeDtypeStruct((m, n), jnp.bfloat16) for n in widths],
        compiler_params=_params("parallel"),
        name=name,
    )(*args)


def _w_in_prep_kernel(x_ref, o_ref):
    pad = jnp.zeros((x_ref.shape[0], D_FF_PAD - D_FF), jnp.bfloat16)
    o_ref[:, 0:D_FF] = _bf(x_ref[:, 0:D_FF])
    o_ref[:, D_FF:D_FF_PAD] = pad
    o_ref[:, D_FF_PAD:D_FF_PAD + D_FF] = _bf(x_ref[:, D_FF:2 * D_FF])
    o_ref[:, D_FF_PAD + D_FF:2 * D_FF_PAD] = pad


def _w_out_prep_kernel(x_ref, o_ref):
    o_ref[0:D_FF, :] = _bf(x_ref[...])
    o_ref[D_FF:D_FF_PAD, :] = jnp.zeros((D_FF_PAD - D_FF, x_ref.shape[1]), jnp.bfloat16)


def _ffn_weights(w_in, conv_w, w_out):
    depth, d, _ = w_in.shape
    rows, cols = 256, 256
    w_in_p = pl.pallas_call(
        _w_in_prep_kernel,
        grid=(depth, d // rows),
        in_specs=[pl.BlockSpec((None, rows, 2 * D_FF), lambda l, i: (l, i, 0))],
        out_specs=pl.BlockSpec((None, rows, 2 * D_FF_PAD), lambda l, i: (l, i, 0)),
        out_shape=jax.ShapeDtypeStruct((depth, d, 2 * D_FF_PAD), jnp.bfloat16),
        compiler_params=_params("parallel", "parallel"),
        name="ffn_w_in_prep",
    )(w_in)
    w_out_p = pl.pallas_call(
        _w_out_prep_kernel,
        grid=(depth, d // cols),
        in_specs=[pl.BlockSpec((None, D_FF, cols), lambda l, j: (l, 0, j))],
        out_specs=pl.BlockSpec((None, D_FF_PAD, cols), lambda l, j: (l, 0, j)),
        out_shape=jax.ShapeDtypeStruct((depth, D_FF_PAD, d), jnp.bfloat16),
        compiler_params=_params("parallel", "parallel"),
        name="ffn_w_out_prep",
    )(w_out)
    pad = D_FF_PAD - D_FF
    zc = jnp.zeros(conv_w.shape[:2] + (pad,), conv_w.dtype)
    conv_p = jnp.concatenate([conv_w[..., :D_FF], zc, conv_w[..., D_FF:], zc], -1)
    return w_in_p, conv_p, w_out_p


def kernel(x, c, ctx, c_ctx, ada_w, ada_b, norm_mix, norm_ffn, ffn_w_in, ffn_conv, ffn_w_out, even_w_in, even_w_out, na_q_norm, na_k_norm, na_rpb, diff_q_norm, diff_k_norm, diff_lambda, diff_subln, mla_w_down, mla_q_a_norm, mla_kv_a_norm, mla_w_uq, mla_w_ukv, mla_q_nope_norm, mla_q_rope_norm, mla_k_nope_norm, mla_k_rope_norm, mla_w_out):
    cond = jnp.concatenate(
        [c, c_ctx[None, :], jnp.zeros((MOD_ROWS - BATCH - 1, D_MODEL), jnp.float32)], 0)
    mod = _ada_call(cond, ada_w, ada_b)
    xl = x.reshape(BATCH * SEQ, D_MODEL)
    xc = ctx.reshape(BATCH * CTX_LEN, D_MODEL)
    ffn_all = _ffn_weights(ffn_w_in, ffn_conv, ffn_w_out)
    for l in range(DEPTH):
        with_ctx = l < DEPTH - 1
        modarr = mod[l].reshape(MOD_ROWS, 1, 6 * D_MODEL)
        ffn_w = ffn_all + (l,)
        i = l // 2
        if l % 2 == 0:
            lam_init = 0.8 - 0.6 * math.exp(-0.3 * l)
            xl, xc = _even_layer(xl, xc, modarr, norm_mix[l], norm_ffn[l], ffn_w, even_w_in[i],
                                 even_w_out[i], na_q_norm[i], na_k_norm[i], na_rpb[i],
                                 diff_q_norm[i], diff_k_norm[i], diff_lambda[i], diff_subln[i],
                                 lam_init, with_ctx)
        else:
            xl, xc = _odd_layer(xl, xc, modarr, norm_mix[l], norm_ffn[l], ffn_w, mla_w_down[i],
                                mla_q_a_norm[i], mla_kv_a_norm[i], mla_w_uq[i], mla_w_ukv[i],
                                mla_q_nope_norm[i], mla_q_rope_norm[i], mla_k_nope_norm[i],
                                mla_k_rope_norm[i], mla_w_out[i], with_ctx)
    return xl.reshape(BATCH, SEQ, D_MODEL)
```

```python
import functools
import math

import numpy as np
import jax
import jax.numpy as jnp
from jax import lax
from jax.experimental import pallas as pl
from jax.experimental.pallas import tpu as pltpu

D_MODEL = 2048
BATCH = 4
SEQ = 4096
DEPTH = 2
GRID_W = 64
GRID_H = SEQ // GRID_W
CTX_LEN = 256
HEAD_DIM = 128
NA_HEADS = 8
NA_WIN_H = 8
NA_WIN_W = 16
DIFF_HEADS = 4
NA_WIDTH = 1024
DIFF_WIDTH = 1024
EVEN_PROJ = 6144
MLA_HEADS = 16
MLA_Q_RANK = 512
MLA_KV_RANK = 512
MLA_NOPE = 128
MLA_ROPE = 64
MLA_V = 128
MLA_DOWN = 1088
D_FF = 5504
ROPE_BASE = 10000.0
EPS = 1e-6

LANES = 128
SUBLANES = 8
MOD_ROWS = 8
CTX_MOD_ROW = BATCH
D_FF_PAD = 5632
MLA_DOWN_PAD = 1152
VMEM_LIMIT = 56 * 1024 * 1024

TM = 512
TN = 1024
TN_FF = 512
TM_FF = 1024
TM_EVEN = 1024
TM_MLA_PROJ = 512
HM_FF = 512
SUB_N = 512
PAIR = 2 * LANES
NA_GROUP_ROWS = 4
NA_KEY_ROWS = NA_GROUP_ROWS + NA_WIN_H
NA_UNROLL = 4
NA_Q = NA_GROUP_ROWS * GRID_W
NA_K = NA_KEY_ROWS * GRID_W
TQ_DIFF = 1024
TQ_MLA = 2048
TQ_SUB = 256
KEY_CHUNK_ATTN = 256
KEY_CHUNK_DIFF = 512
NEG = -1e30
LOG2E = math.log2(math.e)

_NT = (((1,), (1,)), ((), ()))


def _params(*sem):
    return pltpu.CompilerParams(dimension_semantics=sem, vmem_limit_bytes=VMEM_LIMIT)


def _bf(x):
    return x.astype(jnp.bfloat16)


def _dot(a, b):
    return jnp.dot(a, b, preferred_element_type=jnp.float32)


def _dot_nt(a, b):
    return lax.dot_general(a, b, _NT, preferred_element_type=jnp.float32)


def _modulate(x, gain, shift, scale, mxu_stats=False):
    d = x.shape[1]
    if mxu_stats:
        ss = _dot(_bf(x * x), jnp.ones((d, LANES), jnp.bfloat16))
        r = jnp.tile(lax.rsqrt(ss * (1.0 / d) + EPS), (1, d // LANES))
    else:
        r = lax.rsqrt(jnp.mean(x * x, axis=-1, keepdims=True) + EPS)
    return x * r * (gain * (1.0 + scale)) + shift


def _ada_kernel(c_ref, w_ref, b_ref, o_ref):
    c = c_ref[...]
    s = c * (1.0 / (1.0 + jnp.exp(-c)))
    o_ref[0] = _dot(_bf(s), _bf(w_ref[0])) + b_ref[0]


def _ada_call(cond, ada_w, ada_b):
    tn = 512
    n = ada_w.shape[-1]
    return pl.pallas_call(
        _ada_kernel,
        grid=(DEPTH, n // tn),
        in_specs=[
            pl.BlockSpec((MOD_ROWS, D_MODEL), lambda l, j: (0, 0)),
            pl.BlockSpec((1, D_MODEL, tn), lambda l, j: (l, 0, j)),
            pl.BlockSpec((1, 1, tn), lambda l, j: (l, 0, j)),
        ],
        out_specs=pl.BlockSpec((1, MOD_ROWS, tn), lambda l, j: (l, 0, j)),
        out_shape=jax.ShapeDtypeStruct((DEPTH, MOD_ROWS, n), jnp.float32),
        compiler_params=_params("parallel", "parallel"),
        name="ada_mod",
    )(cond, ada_w, ada_b.reshape(DEPTH, 1, n))


def _pair_epilogue(acc, spec, gains_ref, ones_ref, rope_refs, rope_half):
    norm, gain_row, nvalid, do_rope = spec
    y = acc
    if norm:
        ss = _dot(_bf(acc * acc), ones_ref[...])
        y = acc * lax.rsqrt(ss * (1.0 / nvalid) + EPS) * gains_ref[gain_row:gain_row + 1, :]
    if do_rope and rope_refs is not None:
        cos_ref, sa_ref, sb_ref = rope_refs
        y = (y * cos_ref[...]
             + pltpu.roll(y, PAIR - rope_half, axis=1) * sa_ref[...]
             + pltpu.roll(y, rope_half, axis=1) * sb_ref[...])
    return y


def _proj_kernel(*refs, mod, segs, seg_tiles, rope, rope_half):
    it = iter(refs)
    x_ref = next(it)
    if mod:
        g_ref, sh_ref, sc_ref = next(it), next(it), next(it)
    w_ref = next(it)
    gains_ref, ones_ref = next(it), next(it)
    rope_refs = (next(it), next(it), next(it)) if rope else None
    o_ref = next(it)
    h_ref = next(it) if mod else None
    j = pl.program_id(1)
    tn = o_ref.shape[1]

    if mod:
        @pl.when(j == 0)
        def _():
            h_ref[...] = _bf(_modulate(x_ref[...], g_ref[...], sh_ref[0], sc_ref[0],
                                       mxu_stats=True))
        a = h_ref[...]
    else:
        a = x_ref[...]
    seg = j // seg_tiles
    for s, spec in enumerate(segs):
        @pl.when(seg == s)
        def _(spec=spec):
            for c in range(0, tn, SUB_N):
                acc = _dot(a, w_ref[:, c:c + SUB_N])
                for p in range(0, SUB_N, PAIR):
                    y = _pair_epilogue(acc[:, p:p + PAIR], spec, gains_ref, ones_ref, rope_refs,
                                       rope_half)
                    o_ref[:, c + p:c + p + PAIR] = _bf(y)


def _proj_call(x, w, gains, segs, seg_width, *, mod=None, rope=None, mod_row=None,
               tm=TM, tn=TN, name="proj"):
    m, k = x.shape
    n = w.shape[1]
    tm = min(tm, m)
    tn = min(tn, seg_width)
    assert m % tm == 0 and seg_width % tn == 0 and n % seg_width == 0
    seg_tiles = seg_width // tn
    bpb = SEQ // tm

    in_specs = [pl.BlockSpec((tm, k), lambda i, j: (i, 0))]
    args = [x]
    if mod is not None:
        gain, modarr, sh_c, sc_c = mod
        in_specs += [
            pl.BlockSpec((1, k), lambda i, j: (0, 0)),
            pl.BlockSpec((1, 1, k), lambda i, j: (mod_row(i), 0, sh_c)),
            pl.BlockSpec((1, 1, k), lambda i, j: (mod_row(i), 0, sc_c)),
        ]
        args += [gain, modarr, modarr]
    in_specs.append(pl.BlockSpec((k, tn), lambda i, j: (0, j)))
    args.append(w)
    ones = _pair_blockdiag(np.ones((LANES, LANES), np.float32))
    in_specs += [pl.BlockSpec(gains.shape, lambda i, j: (0, 0)),
                 pl.BlockSpec(ones.shape, lambda i, j: (0, 0))]
    args += [gains, ones]
    if rope is not None:
        tables, rope_half = rope
        for t in tables:
            in_specs.append(pl.BlockSpec((tm, PAIR), lambda i, j: (i % bpb, 0)))
            args.append(t)
    else:
        rope_half = 0
    kern = functools.partial(_proj_kernel, mod=mod is not None, segs=tuple(segs),
                             seg_tiles=seg_tiles, rope=rope is not None, rope_half=rope_half)
    return pl.pallas_call(
        kern,
        grid=(m // tm, n // tn),
        in_specs=in_specs,
        out_specs=pl.BlockSpec((tm, tn), lambda i, j: (i, j)),
        out_shape=jax.ShapeDtypeStruct((m, n), jnp.bfloat16),
        scratch_shapes=[pltpu.VMEM((tm, k), jnp.bfloat16)] if mod is not None else [],
        compiler_params=_params("parallel", "arbitrary"),
        name=name,
    )(*args)


def _outproj_kernel(*refs, n_a):
    a_refs = refs[:n_a]
    w_ref, x_ref, g_ref, o_ref = refs[n_a:]
    n = o_ref.shape[1]
    a_vals = [a_ref[...] for a_ref in a_refs]
    for c in range(0, n, SUB_N):
        acc, row = None, 0
        for a in a_vals:
            part = _dot(a, w_ref[row:row + a.shape[1], c:c + SUB_N])
            acc = part if acc is None else acc + part
            row += a.shape[1]
        o_ref[:, c:c + SUB_N] = x_ref[:, c:c + SUB_N] + g_ref[0][:, c:c + SUB_N] * acc


def _outproj_call(a_list, w, x, modarr, gate_chunk, mod_row, *, tm=TM, name="outproj"):
    m, n = x.shape
    tm = min(tm, m)
    in_specs, args = [], []
    for a in a_list:
        in_specs.append(pl.BlockSpec((tm, a.shape[1]), lambda i: (i, 0)))
        args.append(a)
    in_specs += [
        pl.BlockSpec(w.shape, lambda i: (0, 0)),
        pl.BlockSpec((tm, n), lambda i: (i, 0)),
        pl.BlockSpec((1, 1, n), lambda i: (mod_row(i), 0, gate_chunk)),
    ]
    args += [w, x, modarr]
    return pl.pallas_call(
        functools.partial(_outproj_kernel, n_a=len(a_list)),
        grid=(m // tm,),
        in_specs=in_specs,
        out_specs=pl.BlockSpec((tm, n), lambda i: (i, 0)),
        out_shape=jax.ShapeDtypeStruct((m, n), jnp.float32),
        compiler_params=_params("parallel"),
        name=name,
    )(*args)


def _ffn_kernel(x_ref, xp_ref, xn_ref, g_ref, sh_ref, sc_ref, gt_ref, wa_ref, wb_ref,
                ca_ref, cb_ref, wo_ref, o_ref, h_ref, *, tm, hm, seq_len):
    i = pl.program_id(0)
    j = pl.program_id(1)
    nj = pl.num_programs(1)
    n_grp = tm // hm
    rows = hm + 2 * SUBLANES

    @pl.when(j == 0)
    def _():
        gain, shift, scale = g_ref[...], sh_ref[0], sc_ref[0]
        for r in range(n_grp):
            r0 = r * hm
            h_ref[r * rows:r * rows + hm, :] = _bf(
                _modulate(x_ref[r0:r0 + hm, :], gain, shift, scale, mxu_stats=True))
            nxt_src = x_ref[r0 + hm:r0 + hm + SUBLANES, :] if r + 1 < n_grp else xn_ref[...]
            prv_src = x_ref[r0 - SUBLANES:r0, :] if r > 0 else xp_ref[...]
            keep_next = jnp.where((i * tm + r0 + hm) % seq_len == 0, 0.0, 1.0)
            keep_prev = jnp.where((i * tm + r0) % seq_len == 0, 0.0, 1.0)
            nxt = keep_next * _modulate(nxt_src, gain, shift, scale)
            prv = keep_prev * _modulate(prv_src, gain, shift, scale)
            h_ref[r * rows + hm:(r + 1) * rows, :] = _bf(jnp.concatenate([nxt, prv], axis=0))
        o_ref[...] = jnp.zeros_like(o_ref)

    ca, cb = ca_ref[...], cb_ref[...]
    ups = []
    for r in range(n_grp):
        h = h_ref[r * rows:(r + 1) * rows, :]
        ups.append((_dot(h, wa_ref[...]), _dot(h, wb_ref[...])))

    def conv(u, cw):
        prev = pltpu.roll(u, 1, axis=0)[0:hm]
        nxt = pltpu.roll(u, rows - 1, axis=0)[0:hm]
        return prev * cw[0:1] + u[0:hm] * cw[1:2] + nxt * cw[2:3]

    for r, (ua, ub) in enumerate(ups):
        a = conv(ua, ca)
        b = conv(ub, cb)
        act = a * (1.0 / (1.0 + jnp.exp(-a))) * b
        o_ref[r * hm:(r + 1) * hm, :] += _dot(_bf(act), wo_ref[...])

    @pl.when(j == nj - 1)
    def _():
        o_ref[...] = x_ref[...] + gt_ref[0] * o_ref[...]


def _ffn_call(x, gain, modarr, w_in, conv_w, w_out, layer, mod_row, seq_len, *, tm=TM_FF,
              tn=TN_FF, name="conv_ffn"):
    m, d = x.shape
    tm = min(tm, m)
    hm = min(HM_FF, seq_len)
    assert m % tm == 0 and tm % hm == 0 and seq_len % hm == 0
    nj = D_FF_PAD // tn
    hb = tm // SUBLANES
    last_hb = m // SUBLANES - 1
    kern = functools.partial(_ffn_kernel, tm=tm, hm=hm, seq_len=seq_len)
    return pl.pallas_call(
        kern,
        grid=(m // tm, nj),
        in_specs=[
            pl.BlockSpec((tm, d), lambda i, j: (i, 0)),
            pl.BlockSpec((SUBLANES, d), lambda i, j: (jnp.maximum(i * hb - 1, 0), 0)),
            pl.BlockSpec((SUBLANES, d), lambda i, j: (jnp.minimum((i + 1) * hb, last_hb), 0)),
            pl.BlockSpec((1, d), lambda i, j: (0, 0)),
            pl.BlockSpec((1, 1, d), lambda i, j: (mod_row(i), 0, 3)),
            pl.BlockSpec((1, 1, d), lambda i, j: (mod_row(i), 0, 4)),
            pl.BlockSpec((1, 1, d), lambda i, j: (mod_row(i), 0, 5)),
            pl.BlockSpec((None, d, tn), lambda i, j: (layer, 0, j)),
            pl.BlockSpec((None, d, tn), lambda i, j: (layer, 0, nj + j)),
            pl.BlockSpec((None, 3, tn), lambda i, j: (layer, 0, j)),
            pl.BlockSpec((None, 3, tn), lambda i, j: (layer, 0, nj + j)),
            pl.BlockSpec((None, tn, d), lambda i, j: (layer, j, 0)),
        ],
        out_specs=pl.BlockSpec((tm, d), lambda i, j: (i, 0)),
        out_shape=jax.ShapeDtypeStruct((m, d), jnp.float32),
        scratch_shapes=[
            pltpu.VMEM(((tm // hm) * (hm + 2 * SUBLANES), d), jnp.bfloat16),
        ],
        compiler_params=_params("parallel", "arbitrary"),
        name=name,
    )(x, x, x, gain, modarr, modarr, modarr, w_in, w_in, conv_w, conv_w, w_out)


def _na_group_geometry(g):
    r0 = g * NA_GROUP_ROWS
    return r0, min(max(r0 - NA_WIN_H // 2, 0), GRID_H - NA_KEY_ROWS)


def _na_build_bias(rpb_ref, h, tc_scr, bias_scr):
    n_dr = 2 * NA_WIN_H - 1
    n_dc = 2 * NA_WIN_W - 1
    qc = lax.broadcasted_iota(jnp.int32, (GRID_W, GRID_W), 0)
    kc = lax.broadcasted_iota(jnp.int32, (GRID_W, GRID_W), 1)
    col0 = jnp.clip(qc - NA_WIN_W // 2, 0, GRID_W - NA_WIN_W)
    col_valid = (kc >= col0) & (kc < col0 + NA_WIN_W)
    delta = kc - qc + (NA_WIN_W - 1)
    base = h * (n_dr * n_dc)
    for dr in range(n_dr):
        acc = jnp.zeros((GRID_W, GRID_W), jnp.float32)
        for e in range(n_dc):
            acc = jnp.where(delta == e, rpb_ref[base + dr * n_dc + e] * LOG2E, acc)
        tc_scr[dr] = jnp.where(col_valid, acc, NEG)
    tc_scr[n_dr] = jnp.full((GRID_W, GRID_W), NEG, jnp.float32)
    n_groups = GRID_H // NA_GROUP_ROWS
    for t, g in enumerate((0, 1, n_groups - 1)):
        r0, ks = _na_group_geometry(g)
        for qr in range(NA_GROUP_ROWS):
            r = r0 + qr
            row0 = min(max(r - NA_WIN_H // 2, 0), GRID_H - NA_WIN_H)
            pieces = []
            for kr in range(NA_KEY_ROWS):
                kra = ks + kr
                inside = row0 <= kra < row0 + NA_WIN_H
                pieces.append(tc_scr[kra - r + (NA_WIN_H - 1)] if inside else tc_scr[n_dr])
            bias_scr[t, qr * GRID_W:(qr + 1) * GRID_W, :] = jnp.concatenate(pieces, axis=1)


def _na_kernel(rpb_ref, q_ref, k_ref, v_ref, kc_ref, vc_ref, o_ref, tc_scr, bias_scr):
    @pl.when(pl.program_id(1) == 0)
    def _():
        _na_build_bias(rpb_ref, pl.program_id(0), tc_scr, bias_scr)

    kc = kc_ref[...]
    vc = vc_ref[...]
    n_groups = GRID_H // NA_GROUP_ROWS

    def group(g):
        r0 = g * NA_GROUP_ROWS
        ks = jnp.clip(r0 - NA_WIN_H // 2, 0, GRID_H - NA_KEY_ROWS)
        q0 = pl.multiple_of(g * NA_Q, NA_Q)
        k0 = pl.multiple_of(ks * GRID_W, GRID_W)
        tb = jnp.where(g == 0, 0, jnp.where(g == n_groups - 1, 2, 1))
        q = q_ref[pl.ds(q0, NA_Q), :]
        kw = k_ref[pl.ds(k0, NA_K), :]
        vw = v_ref[pl.ds(k0, NA_K), :]
        sw = _dot_nt(q, kw) + bias_scr[tb]
        sc = _dot_nt(q, kc)
        mx = jnp.maximum(jnp.max(sw, axis=-1, keepdims=True), jnp.max(sc, axis=-1, keepdims=True))
        pw = jnp.exp2(sw - mx)
        pc = jnp.exp2(sc - mx)
        den = jnp.sum(pw, axis=-1, keepdims=True) + jnp.sum(pc, axis=-1, keepdims=True)
        o = _dot(_bf(pw), vw) + _dot(_bf(pc), vc)
        o_ref[pl.ds(q0, NA_Q), :] = _bf(o / den)

    def body(t, carry):
        for u in range(NA_UNROLL):
            group(t * NA_UNROLL + u)
        return carry

    lax.fori_loop(0, n_groups // NA_UNROLL, body, 0)


def _na_call(qkv, qkv_c, rpb):
    n_dr = 2 * NA_WIN_H - 1
    return pl.pallas_call(
        _na_kernel,
        grid=(NA_HEADS, BATCH),
        in_specs=[
            pl.BlockSpec(memory_space=pltpu.SMEM),
            pl.BlockSpec((SEQ, LANES), lambda h, b: (b, h)),
            pl.BlockSpec((SEQ, LANES), lambda h, b: (b, NA_HEADS + h)),
            pl.BlockSpec((SEQ, LANES), lambda h, b: (b, 2 * NA_HEADS + h)),
            pl.BlockSpec((CTX_LEN, LANES), lambda h, b: (b, NA_HEADS + h)),
            pl.BlockSpec((CTX_LEN, LANES), lambda h, b: (b, 2 * NA_HEADS + h)),
        ],
        out_specs=pl.BlockSpec((SEQ, LANES), lambda h, b: (b, h)),
        out_shape=jax.ShapeDtypeStruct((BATCH * SEQ, NA_WIDTH), jnp.bfloat16),
        scratch_shapes=[
            pltpu.VMEM((n_dr + 1, GRID_W, GRID_W), jnp.float32),
            pltpu.VMEM((3, NA_Q, NA_K), jnp.float32),
        ],
        compiler_params=_params("parallel", "arbitrary"),
        name="na_attn",
    )(rpb.reshape(-1), qkv, qkv, qkv, qkv_c, qkv_c)


def _key_chunks(src_rows, chunk):
    chunks, r = [], 0
    for rows in src_rows:
        size = min(chunk, rows)
        assert rows % size == 0
        chunks += [(r + c, size) for c in range(0, rows, size)]
        r += rows
    return chunks


def _pipelined_attention(streams, vt_scr, chunks):
    dv = vt_scr.shape[0]
    sub = streams[0][2].shape[1]
    outs, m_prev = [], None
    for u in range(len(streams) + 1):
        m = jnp.full((1, sub), NEG, jnp.float32)
        l = jnp.zeros((1, sub), jnp.float32)
        acc = jnp.zeros((dv, sub), jnp.float32)
        for c0, cs in chunks:
            if u < len(streams):
                q_t, k_scr, s_ref = streams[u]
                s = _dot(k_scr[c0:c0 + cs, :], q_t)
                s_ref[c0:c0 + cs, :] = s
                m = jnp.maximum(m, jnp.max(s, axis=0, keepdims=True))
            if u > 0:
                p = jnp.exp2(streams[u - 1][2][c0:c0 + cs, :] - m_prev)
                l = l + jnp.sum(p, axis=0, keepdims=True)
                acc = acc + _dot(vt_scr[:, c0:c0 + cs], _bf(p))
        if u > 0:
            outs.append(acc / l)
        m_prev = m
    return outs


def _transpose_bf16(x):
    return _bf(x.astype(jnp.float32).T)


def _attn_kernel(*refs, n_qparts, n_src, n_kparts, src_rows):
    it = iter(refs)
    q_refs = [next(it) for _ in range(n_qparts)]
    srcs = []
    for _ in range(n_src):
        k_refs = [next(it) for _ in range(n_kparts)]
        srcs.append((k_refs, next(it)))
    o_ref = next(it)
    k_scr, vt_scr = next(it), next(it)
    s_scrs = list(it)

    @pl.when(pl.program_id(2) == 0)
    def _():
        r = 0
        for (k_refs, v_ref), rows in zip(srcs, src_rows):
            for p, k_ref in enumerate(k_refs):
                k_scr[r:r + rows, p * LANES:(p + 1) * LANES] = k_ref[...]
            vt_scr[:, r:r + rows] = _transpose_bf16(v_ref[...])
            r += rows

    q = jnp.concatenate([qr[...] for qr in q_refs], axis=-1) if n_qparts > 1 else q_refs[0][...]
    sub = s_scrs[0].shape[1]
    streams = [(_transpose_bf16(q[u * sub:(u + 1) * sub]), k_scr, s_scrs[u % 2])
               for u in range(q.shape[0] // sub)]
    outs = _pipelined_attention(streams, vt_scr, _key_chunks(src_rows, KEY_CHUNK_ATTN))
    for u, o_t in enumerate(outs):
        o_ref[u * sub:(u + 1) * sub, :] = _bf(o_t.T)


def _attn_call(q_parts, sources, n_heads, lq, tq, dv, name):
    tq = min(tq, lq)
    sub = min(TQ_SUB, tq)
    nq = lq // tq
    in_specs, args = [], []
    for arr, cf in q_parts:
        in_specs.append(pl.BlockSpec((tq, LANES), lambda b, h, i, cf=cf: (b * nq + i, cf(h))))
        args.append(arr)
    src_rows = []
    n_kparts = len(sources[0][1])
    for rows, k_parts, (v_arr, vcf) in sources:
        src_rows.append(rows)
        for arr, cf in k_parts:
            in_specs.append(pl.BlockSpec((rows, LANES), lambda b, h, i, cf=cf: (b, cf(h))))
            args.append(arr)
        in_specs.append(pl.BlockSpec((rows, dv), lambda b, h, i, cf=vcf: (b, cf(h))))
        args.append(v_arr)
    nk = sum(src_rows)
    kern = functools.partial(_attn_kernel, n_qparts=len(q_parts), n_src=len(sources),
                             n_kparts=n_kparts, src_rows=tuple(src_rows))
    return pl.pallas_call(
        kern,
        grid=(BATCH, n_heads, nq),
        in_specs=in_specs,
        out_specs=pl.BlockSpec((tq, dv), lambda b, h, i: (b * nq + i, h)),
        out_shape=jax.ShapeDtypeStruct((BATCH * lq, n_heads * dv), jnp.bfloat16),
        scratch_shapes=[
            pltpu.VMEM((nk, n_kparts * LANES), jnp.bfloat16),
            pltpu.VMEM((dv, nk), jnp.bfloat16),
        ] + [pltpu.VMEM((nk, sub), jnp.float32)] * 2,
        compiler_params=_params("parallel", "parallel", "arbitrary"),
        name=name,
    )(*args)


def _diff_kernel(*refs, n_src, src_rows, lam_init):
    it = iter(refs)
    q1_ref, q2_ref = next(it), next(it)
    srcs = [(next(it), next(it), next(it)) for _ in range(n_src)]
    lam_ref, sub_ref = next(it), next(it)
    o_ref = next(it)
    k1_scr, k2_scr, vt_scr, s1_scr, s2_scr = [next(it) for _ in range(5)]

    @pl.when(pl.program_id(2) == 0)
    def _():
        r = 0
        for (k1_ref, k2_ref, v_ref), rows in zip(srcs, src_rows):
            k1_scr[r:r + rows, :] = k1_ref[...]
            k2_scr[r:r + rows, :] = k2_ref[...]
            vt_scr[:, r:r + rows] = _transpose_bf16(v_ref[...])
            r += rows

    lf = lam_ref[...]
    lam = (jnp.exp(jnp.sum(lf[0:1] * lf[1:2], axis=-1, keepdims=True))
           - jnp.exp(jnp.sum(lf[2:3] * lf[3:4], axis=-1, keepdims=True)) + lam_init)
    chunks = _key_chunks(src_rows, KEY_CHUNK_DIFF)
    sub = s1_scr.shape[1]
    n_sub = q1_ref.shape[0] // sub
    streams = []
    for u in range(n_sub):
        rows = slice(u * sub, (u + 1) * sub)
        streams.append((_transpose_bf16(q1_ref[rows, :]), k1_scr, s1_scr))
        streams.append((_transpose_bf16(q2_ref[rows, :]), k2_scr, s2_scr))
    outs = _pipelined_attention(streams, vt_scr, chunks)
    o_t = jnp.concatenate([outs[2 * u] - lam * outs[2 * u + 1] for u in range(n_sub)],
                          axis=1)
    o = o_t.T
    ms = jnp.mean(o * o, axis=-1, keepdims=True)
    o_ref[...] = _bf(o * lax.rsqrt(ms + EPS) * sub_ref[...] * (1.0 - lam_init))


def _diff_call(qkv_q, lq, sources, diff_lambda, subln, lam_init, tq, name):
    tq = min(tq, lq)
    sub = min(TQ_SUB, tq)
    nq = lq // tq
    qb0 = 3 * NA_WIDTH // LANES
    kb0 = (3 * NA_WIDTH + DIFF_WIDTH) // LANES
    vb0 = (3 * NA_WIDTH + 2 * DIFF_WIDTH) // (2 * LANES)
    in_specs = [
        pl.BlockSpec((tq, LANES), lambda b, h, i: (b * nq + i, qb0 + 2 * h)),
        pl.BlockSpec((tq, LANES), lambda b, h, i: (b * nq + i, qb0 + 2 * h + 1)),
    ]
    args = [qkv_q, qkv_q]
    src_rows = []
    for rows, arr in sources:
        src_rows.append(rows)
        in_specs += [
            pl.BlockSpec((rows, LANES), lambda b, h, i: (b, kb0 + 2 * h)),
            pl.BlockSpec((rows, LANES), lambda b, h, i: (b, kb0 + 2 * h + 1)),
            pl.BlockSpec((rows, 2 * LANES), lambda b, h, i: (b, vb0 + h)),
        ]
        args += [arr, arr, arr]
    in_specs += [
        pl.BlockSpec((4, LANES), lambda b, h, i: (0, 0)),
        pl.BlockSpec((1, 2 * LANES), lambda b, h, i: (0, 0)),
    ]
    args += [diff_lambda, subln]
    nk = sum(src_rows)
    kern = functools.partial(_diff_kernel, n_src=len(sources), src_rows=tuple(src_rows),
                             lam_init=lam_init)
    return pl.pallas_call(
        kern,
        grid=(BATCH, DIFF_HEADS, nq),
        in_specs=in_specs,
        out_specs=pl.BlockSpec((tq, 2 * LANES), lambda b, h, i: (b * nq + i, h)),
        out_shape=jax.ShapeDtypeStruct((BATCH * lq, DIFF_WIDTH), jnp.bfloat16),
        scratch_shapes=[
            pltpu.VMEM((nk, LANES), jnp.bfloat16),
            pltpu.VMEM((nk, LANES), jnp.bfloat16),
            pltpu.VMEM((2 * LANES, nk), jnp.bfloat16),
            pltpu.VMEM((nk, sub), jnp.float32),
            pltpu.VMEM((nk, sub), jnp.float32),
        ],
        compiler_params=_params("parallel", "parallel", "arbitrary"),
        name=name,
    )(*args)


def _rope_tables(d):
    h = d // 2
    half = h // 2
    lane = jnp.arange(PAIR, dtype=jnp.int32)[None, :] % LANES
    t = jnp.arange(SEQ, dtype=jnp.int32)[:, None]
    pos = jnp.where(lane < h, t // GRID_W, t % GRID_W).astype(jnp.float32)
    freqs = ROPE_BASE ** (-(lane % half).astype(jnp.float32) / half)
    ang = pos * freqs
    valid = lane < d
    first = (lane % h) < half
    cos = jnp.where(valid, jnp.cos(ang), 1.0)
    sin = jnp.where(valid, jnp.sin(ang), 0.0)
    sin_a = jnp.where(first, -sin, 0.0)
    sin_b = jnp.where(first, 0.0, sin)
    return (cos, sin_a, sin_b), half


def _pair_blockdiag(block):
    z = np.zeros_like(block)
    return jnp.asarray(np.block([[block, z], [z, block]]), dtype=jnp.bfloat16)


def _pad_lanes(v, fill=0.0):
    return jnp.pad(v, (0, LANES - v.shape[0]), constant_values=fill)


def _pair(v):
    return jnp.concatenate([v, v])


def _lat_row(tm):
    bpb = SEQ // tm
    return lambda i: i // bpb


def _ctx_row(i):
    return CTX_MOD_ROW


def _even_layer(x, xc, modarr, norm_mix, norm_ffn, ffn_w, w_in, w_out, qn_a, kn_a, rpb,
                qn_b, kn_b, diff_lambda, subln, lam_init, with_ctx):
    scale = HEAD_DIM ** -0.5 * LOG2E
    gains = jnp.stack([_pair(g) for g in (qn_a * scale, kn_a, qn_b * scale, kn_b)]
                      + [jnp.ones((PAIR,), jnp.float32)] * 4)
    rope = _rope_tables(HEAD_DIM)
    segs = [(True, 0, HEAD_DIM, False), (True, 1, HEAD_DIM, False), (False, 0, HEAD_DIM, False),
            (True, 2, HEAD_DIM, True), (True, 3, HEAD_DIM, True), (False, 0, HEAD_DIM, False)]
    w_in = _bf(w_in)
    w_out = _bf(w_out)
    gain_mix = norm_mix[None, :]
    lat_row = _lat_row(TM)

    qkv = _proj_call(x, w_in, gains, segs, NA_WIDTH, mod=(gain_mix, modarr, 0, 1), rope=rope,
                     mod_row=_lat_row(TM_EVEN), tm=TM_EVEN, name="even_proj")
    qkv_c = _proj_call(xc, w_in, gains, segs, NA_WIDTH, mod=(gain_mix, modarr, 0, 1), rope=None,
                       mod_row=_ctx_row, tm=TM_EVEN, name="even_proj_ctx")

    oa = _na_call(qkv, qkv_c, rpb)
    ob = _diff_call(qkv, SEQ, [(SEQ, qkv), (CTX_LEN, qkv_c)], diff_lambda, subln[None, :],
                    lam_init, TQ_DIFF, "diff_attn")
    x = _outproj_call([oa, ob], w_out, x, modarr, 2, lat_row, name="even_out")
    x = _ffn_call(x, norm_ffn[None, :], modarr, *ffn_w, _lat_row(TM_FF), SEQ)
    if with_ctx:
        oa_c = _attn_call(
            [(qkv_c, lambda h: h)],
            [(CTX_LEN, [(qkv_c, lambda h: NA_HEADS + h)], (qkv_c, lambda h: 2 * NA_HEADS + h))],
            NA_HEADS, CTX_LEN, CTX_LEN, HEAD_DIM, "ctx_attn")
        ob_c = _diff_call(qkv_c, CTX_LEN, [(CTX_LEN, qkv_c)], diff_lambda, subln[None, :],
                          lam_init, CTX_LEN, "diff_attn_ctx")
        xc = _outproj_call([oa_c, ob_c], w_out, xc, modarr, 2, _ctx_row, name="even_out_ctx")
        xc = _ffn_call(xc, norm_ffn[None, :], modarr, *ffn_w, _ctx_row, CTX_LEN,
                       name="conv_ffn_ctx")
    return x, xc


def _odd_layer(x, xc, modarr, norm_mix, norm_ffn, ffn_w, w_down, q_a_norm, kv_a_norm, w_uq,
               w_ukv, qn_nope, qn_rope, kn_nope, kn_rope, w_out, with_ctx):
    scale = (MLA_NOPE + MLA_ROPE) ** -0.5 * LOG2E
    rope_pair = _rope_tables(MLA_ROPE)
    gain_mix = norm_mix[None, :]
    lat_row = _lat_row(TM)

    w_down_p = _bf(jnp.pad(w_down, ((0, 0), (0, MLA_DOWN_PAD - MLA_DOWN))))
    zeros_r = jnp.zeros((MLA_Q_RANK,), jnp.float32)
    kr_gain = jnp.concatenate([_pad_lanes(kn_rope), zeros_r[:MLA_Q_RANK - LANES]])
    gains_d = jnp.stack([q_a_norm, kv_a_norm, kr_gain] + [zeros_r] * 5)
    wq = w_uq.reshape(MLA_Q_RANK, MLA_HEADS, MLA_NOPE + MLA_ROPE)
    wq_rope = jnp.pad(wq[:, :, MLA_NOPE:], ((0, 0), (0, 0), (0, LANES - MLA_ROPE)))
    wq_p = _bf(jnp.concatenate([wq[:, :, :MLA_NOPE].reshape(MLA_Q_RANK, -1),
                                wq_rope.reshape(MLA_Q_RANK, -1)], -1))
    wkv = w_ukv.reshape(MLA_KV_RANK, MLA_HEADS, MLA_NOPE + MLA_V)
    wkv_p = _bf(jnp.concatenate([wkv[:, :, :MLA_NOPE].reshape(MLA_KV_RANK, -1),
                                 wkv[:, :, MLA_NOPE:].reshape(MLA_KV_RANK, -1)], -1))
    gains_p = jnp.stack([_pair(qn_nope * scale), _pair(_pad_lanes(qn_rope) * scale),
                         _pair(kn_nope)] + [jnp.zeros((PAIR,), jnp.float32)] * 5)
    mod_in = (gain_mix, modarr, 0, 1)
    q, kv, kr = _mla_proj_call(x, w_down_p, wq_p, wkv_p, gains_d, gains_p, mod_in, rope_pair,
                               _lat_row(TM_MLA_PROJ), True, "mla_proj")
    kv_c, kr_c = _mla_proj_call(xc, w_down_p, wq_p, wkv_p, gains_d, gains_p, mod_in, None,
                                _ctx_row, False, "mla_proj_ctx")

    def src(rows, kv_arr, kr_arr):
        return (rows, [(kv_arr, lambda h: h), (kr_arr, lambda h: 0)],
                (kv_arr, lambda h: MLA_HEADS + h))

    o = _attn_call([(q, lambda h: h), (q, lambda h: MLA_HEADS + h)],
                   [src(SEQ, kv, kr), src(CTX_LEN, kv_c, kr_c)],
                   MLA_HEADS, SEQ, TQ_MLA, MLA_V, "mla_attn")
    x = _outproj_call([o], _bf(w_out), x, modarr, 2, lat_row, name="mla_out")
    x = _ffn_call(x, norm_ffn[None, :], modarr, *ffn_w, _lat_row(TM_FF), SEQ)
    assert not with_ctx
    return x, xc


def _mla_proj_kernel(x_ref, g_ref, sh_ref, sc_ref, wd_ref, wq_ref, wkv_ref, gd_ref, gp_ref,
                     ones_ref, *rest, rope, rope_half, with_q):
    rope_refs = rest[:3] if rope else None
    outs = rest[3:] if rope else rest
    if with_q:
        q_ref, kv_ref, kr_ref = outs
    else:
        kv_ref, kr_ref = outs
    h = _bf(_modulate(x_ref[...], g_ref[...], sh_ref[0], sc_ref[0]))
    down = _dot(h, wd_ref[...])

    def latent_norm(xs, row, nvalid):
        ms = jnp.sum(xs * xs, axis=-1, keepdims=True) * (1.0 / nvalid)
        return xs * lax.rsqrt(ms + EPS) * gd_ref[row:row + 1, 0:xs.shape[1]]

    kva = _bf(latent_norm(down[:, MLA_Q_RANK:MLA_Q_RANK + MLA_KV_RANK], 1, MLA_KV_RANK))
    kr = latent_norm(down[:, MLA_Q_RANK + MLA_KV_RANK:MLA_DOWN_PAD], 2, MLA_ROPE)
    if rope:
        cos_ref, sa_ref, sb_ref = rope_refs
        kr = (kr * cos_ref[:, 0:LANES]
              + pltpu.roll(kr, LANES - rope_half, axis=1) * sa_ref[:, 0:LANES]
              + pltpu.roll(kr, rope_half, axis=1) * sb_ref[:, 0:LANES])
    kr_ref[...] = _bf(kr)

    def up(a, w_ref, o_ref, specs):
        width = w_ref.shape[1] // len(specs)
        for c in range(0, w_ref.shape[1], SUB_N):
            acc = _dot(a, w_ref[:, c:c + SUB_N])
            for p in range(0, SUB_N, PAIR):
                y = _pair_epilogue(acc[:, p:p + PAIR], specs[c // width], gp_ref, ones_ref,
                                   rope_refs, rope_half)
                o_ref[:, c + p:c + p + PAIR] = _bf(y)

    up(kva, wkv_ref, kv_ref, [(True, 2, MLA_NOPE, False), (False, 0, MLA_V, False)])
    if with_q:
        qa = _bf(latent_norm(down[:, 0:MLA_Q_RANK], 0, MLA_Q_RANK))
        up(qa, wq_ref, q_ref, [(True, 0, MLA_NOPE, False), (True, 1, MLA_ROPE, True)])


def _mla_proj_call(x, wd, wq, wkv, gains_d, gains_p, mod, rope, mod_row, with_q, name,
                   tm=TM_MLA_PROJ):
    m, k = x.shape
    tm = min(tm, m)
    bpb = SEQ // tm
    gain, modarr, sh_c, sc_c = mod
    ones = _pair_blockdiag(np.ones((LANES, LANES), np.float32))
    in_specs = [
        pl.BlockSpec((tm, k), lambda i: (i, 0)),
        pl.BlockSpec((1, k), lambda i: (0, 0)),
        pl.BlockSpec((1, 1, k), lambda i: (mod_row(i), 0, sh_c)),
        pl.BlockSpec((1, 1, k), lambda i: (mod_row(i), 0, sc_c)),
    ]
    args = [x, gain, modarr, modarr]
    for w in (wd, wq, wkv, gains_d, gains_p, ones):
        in_specs.append(pl.BlockSpec(w.shape, lambda i: (0, 0), pipeline_mode=pl.Buffered(1)))
        args.append(w)
    rope_half = 0
    if rope is not None:
        tables, rope_half = rope
        for t in tables:
            in_specs.append(pl.BlockSpec((tm, PAIR), lambda i: (i % bpb, 0)))
            args.append(t)
    widths = ([wq.shape[1]] if with_q else []) + [wkv.shape[1], LANES]
    return pl.pallas_call(
        functools.partial(_mla_proj_kernel, rope=rope is not None, rope_half=rope_half,
                          with_q=with_q),
        grid=(m // tm,),
        in_specs=in_specs,
        out_specs=[pl.BlockSpec((tm, n), lambda i: (i, 0)) for n in widths],
        out_shape=[jax.ShapeDtypeStruct((m, n), jnp.bfloat16) for n in widths],
        compiler_params=_params("parallel"),
        name=name,
    )(*args)


def _w_in_prep_kernel(x_ref, o_ref):
    pad = jnp.zeros((x_ref.shape[0], D_FF_PAD - D_FF), jnp.bfloat16)
    o_ref[:, 0:D_FF] = _bf(x_ref[:, 0:D_FF])
    o_ref[:, D_FF:D_FF_PAD] = pad
    o_ref[:, D_FF_PAD:D_FF_PAD + D_FF] = _bf(x_ref[:, D_FF:2 * D_FF])
    o_ref[:, D_FF_PAD + D_FF:2 * D_FF_PAD] = pad


def _w_out_prep_kernel(x_ref, o_ref):
    o_ref[0:D_FF, :] = _bf(x_ref[...])
    o_ref[D_FF:D_FF_PAD, :] = jnp.zeros((D_FF_PAD - D_FF, x_ref.shape[1]), jnp.bfloat16)


def _ffn_weights(w_in, conv_w, w_out):
    depth, d, _ = w_in.shape
    rows, cols = 256, 256
    w_in_p = pl.pallas_call(
        _w_in_prep_kernel,
        grid=(depth, d // rows),
        in_specs=[pl.BlockSpec((None, rows, 2 * D_FF), lambda l, i: (l, i, 0))],
        out_specs=pl.BlockSpec((None, rows, 2 * D_FF_PAD), lambda l, i: (l, i, 0)),
        out_shape=jax.ShapeDtypeStruct((depth, d, 2 * D_FF_PAD), jnp.bfloat16),
        compiler_params=_params("parallel", "parallel"),
        name="ffn_w_in_prep",
    )(w_in)
    w_out_p = pl.pallas_call(
        _w_out_prep_kernel,
        grid=(depth, d // cols),
        in_specs=[pl.BlockSpec((None, D_FF, cols), lambda l, j: (l, 0, j))],
        out_specs=pl.BlockSpec((None, D_FF_PAD, cols), lambda l, j: (l, 0, j)),
        out_shape=jax.ShapeDtypeStruct((depth, D_FF_PAD, d), jnp.bfloat16),
        compiler_params=_params("parallel", "parallel"),
        name="ffn_w_out_prep",
    )(w_out)
    pad = D_FF_PAD - D_FF
    zc = jnp.zeros(conv_w.shape[:2] + (pad,), conv_w.dtype)
    conv_p = jnp.concatenate([conv_w[..., :D_FF], zc, conv_w[..., D_FF:], zc], -1)
    return w_in_p, conv_p, w_out_p


def kernel(x, c, ctx, c_ctx, ada_w, ada_b, norm_mix, norm_ffn, ffn_w_in, ffn_conv, ffn_w_out, even_w_in, even_w_out, na_q_norm, na_k_norm, na_rpb, diff_q_norm, diff_k_norm, diff_lambda, diff_subln, mla_w_down, mla_q_a_norm, mla_kv_a_norm, mla_w_uq, mla_w_ukv, mla_q_nope_norm, mla_q_rope_norm, mla_k_nope_norm, mla_k_rope_norm, mla_w_out):
    cond = jnp.concatenate(
        [c, c_ctx[None, :], jnp.zeros((MOD_ROWS - BATCH - 1, D_MODEL), jnp.float32)], 0)
    mod = _ada_call(cond, ada_w, ada_b)
    xl = x.reshape(BATCH * SEQ, D_MODEL)
    xc = ctx.reshape(BATCH * CTX_LEN, D_MODEL)
    ffn_all = _ffn_weights(ffn_w_in, ffn_conv, ffn_w_out)
    for l in range(DEPTH):
        with_ctx = l < DEPTH - 1
        modarr = mod[l].reshape(MOD_ROWS, 1, 6 * D_MODEL)
        ffn_w = ffn_all + (l,)
        i = l // 2
        if l % 2 == 0:
            lam_init = 0.8 - 0.6 * math.exp(-0.3 * l)
            xl, xc = _even_layer(xl, xc, modarr, norm_mix[l], norm_ffn[l], ffn_w, even_w_in[i],
                                 even_w_out[i], na_q_norm[i], na_k_norm[i], na_rpb[i],
                                 diff_q_norm[i], diff_k_norm[i], diff_lambda[i], diff_subln[i],
                                 lam_init, with_ctx)
        else:
            xl, xc = _odd_layer(xl, xc, modarr, norm_mix[l], norm_ffn[l], ffn_w, mla_w_down[i],
                                mla_q_a_norm[i], mla_kv_a_norm[i], mla_w_uq[i], mla_w_ukv[i],
                                mla_q_nope_norm[i], mla_q_rope_norm[i], mla_k_nope_norm[i],
                                mla_k_rope_norm[i], mla_w_out[i], with_ctx)
    return xl.reshape(BATCH, SEQ, D_MODEL)
```

```python
import functools
import math

import numpy as np
import jax
import jax.numpy as jnp
from jax import lax
from jax.experimental import pallas as pl
from jax.experimental.pallas import tpu as pltpu

D_MODEL = 2048
BATCH = 4
SEQ = 4096
DEPTH = 2
GRID_W = 64
GRID_H = SEQ // GRID_W
CTX_LEN = 256
HEAD_DIM = 128
NA_HEADS = 8
NA_WIN_H = 8
NA_WIN_W = 16
DIFF_HEADS = 4
NA_WIDTH = 1024
DIFF_WIDTH = 1024
EVEN_PROJ = 6144
MLA_HEADS = 16
MLA_Q_RANK = 512
MLA_KV_RANK = 512
MLA_NOPE = 128
MLA_ROPE = 64
MLA_V = 128
MLA_DOWN = 1088
D_FF = 5504
ROPE_BASE = 10000.0
EPS = 1e-6

LANES = 128
SUBLANES = 8
MOD_ROWS = 8
CTX_MOD_ROW = BATCH
D_FF_PAD = 5632
MLA_DOWN_PAD = 1152
VMEM_LIMIT = 56 * 1024 * 1024

TM = 512
TN_FF = 512
TM_FF = 1024
TM_EVEN = 512
TM_MLA_PROJ = 512
HM_FF = 512
SUB_N = 512
PAIR = 2 * LANES
NA_GROUP_ROWS = 4
NA_KEY_ROWS = NA_GROUP_ROWS + NA_WIN_H
NA_UNROLL = 4
NA_Q = NA_GROUP_ROWS * GRID_W
NA_K = NA_KEY_ROWS * GRID_W
TQ_DIFF = 1024
TQ_MLA = 2048
TQ_SUB = 256
KEY_CHUNK_ATTN = 256
KEY_CHUNK_DIFF = 512
NEG = -1e30
LOG2E = math.log2(math.e)

_NT = (((1,), (1,)), ((), ()))


def _params(*sem):
    return pltpu.CompilerParams(dimension_semantics=sem, vmem_limit_bytes=VMEM_LIMIT)


def _bf(x):
    return x.astype(jnp.bfloat16)


def _dot(a, b):
    return jnp.dot(a, b, preferred_element_type=jnp.float32)


def _dot_nt(a, b):
    return lax.dot_general(a, b, _NT, preferred_element_type=jnp.float32)


def _modulate(x, gain, shift, scale, mxu_stats=False):
    d = x.shape[1]
    if mxu_stats:
        ss = _dot(_bf(x * x), jnp.ones((d, LANES), jnp.bfloat16))
        r = jnp.tile(lax.rsqrt(ss * (1.0 / d) + EPS), (1, d // LANES))
    else:
        r = lax.rsqrt(jnp.mean(x * x, axis=-1, keepdims=True) + EPS)
    return x * r * (gain * (1.0 + scale)) + shift


def _ada_kernel(c_ref, w_ref, b_ref, o_ref):
    c = c_ref[...]
    s = c * (1.0 / (1.0 + jnp.exp(-c)))
    o_ref[0] = _dot(_bf(s), _bf(w_ref[0])) + b_ref[0]


def _ada_call(cond, ada_w, ada_b):
    tn = 512
    n = ada_w.shape[-1]
    return pl.pallas_call(
        _ada_kernel,
        grid=(DEPTH, n // tn),
        in_specs=[
            pl.BlockSpec((MOD_ROWS, D_MODEL), lambda l, j: (0, 0)),
            pl.BlockSpec((1, D_MODEL, tn), lambda l, j: (l, 0, j)),
            pl.BlockSpec((1, 1, tn), lambda l, j: (l, 0, j)),
        ],
        out_specs=pl.BlockSpec((1, MOD_ROWS, tn), lambda l, j: (l, 0, j)),
        out_shape=jax.ShapeDtypeStruct((DEPTH, MOD_ROWS, n), jnp.float32),
        compiler_params=_params("parallel", "parallel"),
        name="ada_mod",
    )(cond, ada_w, ada_b.reshape(DEPTH, 1, n))


def _pair_epilogue(acc, spec, gains_ref, ones_ref, rope_refs, rope_half):
    norm, gain_row, nvalid, do_rope = spec
    y = acc
    if norm:
        ss = _dot(_bf(acc * acc), ones_ref[...])
        y = acc * lax.rsqrt(ss * (1.0 / nvalid) + EPS) * gains_ref[gain_row:gain_row + 1, :]
    if do_rope and rope_refs is not None:
        cos_ref, sa_ref, sb_ref = rope_refs
        y = (y * cos_ref[...]
             + pltpu.roll(y, PAIR - rope_half, axis=1) * sa_ref[...]
             + pltpu.roll(y, rope_half, axis=1) * sb_ref[...])
    return y


def _head_projection(a, w_ref, o_ref, segs, gains_ref, ones_ref, rope_refs, rope_half):
    width = w_ref.shape[1] // len(segs)
    for c in range(0, w_ref.shape[1], SUB_N):
        acc = _dot(a, w_ref[:, c:c + SUB_N])
        for p in range(0, SUB_N, PAIR):
            y = _pair_epilogue(acc[:, p:p + PAIR], segs[c // width], gains_ref, ones_ref,
                               rope_refs, rope_half)
            o_ref[:, c + p:c + p + PAIR] = _bf(y)


def _proj_kernel(x_ref, g_ref, sh_ref, sc_ref, w_ref, gains_ref, ones_ref, *rest, segs, rope,
                 rope_half):
    rope_refs = rest[:3] if rope else None
    o_ref = rest[-1]
    h = _bf(_modulate(x_ref[...], g_ref[...], sh_ref[0], sc_ref[0], mxu_stats=True))
    _head_projection(h, w_ref, o_ref, segs, gains_ref, ones_ref, rope_refs, rope_half)


def _proj_call(x, w, gains, segs, *, mod, rope=None, mod_row=None, tm=TM, name="proj"):
    m, k = x.shape
    n = w.shape[1]
    tm = min(tm, m)
    assert m % tm == 0 and n % (len(segs) * SUB_N) == 0
    bpb = SEQ // tm
    gain, modarr, sh_c, sc_c = mod
    ones = _pair_blockdiag(np.ones((LANES, LANES), np.float32))
    in_specs = [
        pl.BlockSpec((tm, k), lambda i: (i, 0)),
        pl.BlockSpec((1, k), lambda i: (0, 0)),
        pl.BlockSpec((1, 1, k), lambda i: (mod_row(i), 0, sh_c)),
        pl.BlockSpec((1, 1, k), lambda i: (mod_row(i), 0, sc_c)),
    ]
    args = [x, gain, modarr, modarr]
    for const in (w, gains, ones):
        in_specs.append(pl.BlockSpec(const.shape, lambda i: (0, 0), pipeline_mode=pl.Buffered(1)))
        args.append(const)
    rope_half = 0
    if rope is not None:
        tables, rope_half = rope
        for t in tables:
            in_specs.append(pl.BlockSpec((tm, PAIR), lambda i: (i % bpb, 0)))
            args.append(t)
    kern = functools.partial(_proj_kernel, segs=tuple(segs), rope=rope is not None,
                             rope_half=rope_half)
    return pl.pallas_call(
        kern,
        grid=(m // tm,),
        in_specs=in_specs,
        out_specs=pl.BlockSpec((tm, n), lambda i: (i, 0)),
        out_shape=jax.ShapeDtypeStruct((m, n), jnp.bfloat16),
        compiler_params=_params("parallel"),
        name=name,
    )(*args)


def _outproj_kernel(*refs, n_a):
    a_refs = refs[:n_a]
    w_ref, x_ref, g_ref, o_ref = refs[n_a:]
    n = o_ref.shape[1]
    a_vals = [a_ref[...] for a_ref in a_refs]
    for c in range(0, n, SUB_N):
        acc, row = None, 0
        for a in a_vals:
            part = _dot(a, w_ref[row:row + a.shape[1], c:c + SUB_N])
            acc = part if acc is None else acc + part
            row += a.shape[1]
        o_ref[:, c:c + SUB_N] = x_ref[:, c:c + SUB_N] + g_ref[0][:, c:c + SUB_N] * acc


def _outproj_call(a_list, w, x, modarr, gate_chunk, mod_row, *, tm=TM, name="outproj"):
    m, n = x.shape
    tm = min(tm, m)
    in_specs, args = [], []
    for a in a_list:
        in_specs.append(pl.BlockSpec((tm, a.shape[1]), lambda i: (i, 0)))
        args.append(a)
    in_specs += [
        pl.BlockSpec(w.shape, lambda i: (0, 0)),
        pl.BlockSpec((tm, n), lambda i: (i, 0)),
        pl.BlockSpec((1, 1, n), lambda i: (mod_row(i), 0, gate_chunk)),
    ]
    args += [w, x, modarr]
    return pl.pallas_call(
        functools.partial(_outproj_kernel, n_a=len(a_list)),
        grid=(m // tm,),
        in_specs=in_specs,
        out_specs=pl.BlockSpec((tm, n), lambda i: (i, 0)),
        out_shape=jax.ShapeDtypeStruct((m, n), jnp.float32),
        compiler_params=_params("parallel"),
        name=name,
    )(*args)


def _ffn_kernel(x_ref, xp_ref, xn_ref, g_ref, sh_ref, sc_ref, gt_ref, wa_ref, wb_ref,
                ca_ref, cb_ref, wo_ref, o_ref, h_ref, *, tm, hm, seq_len):
    i = pl.program_id(0)
    j = pl.program_id(1)
    nj = pl.num_programs(1)
    n_grp = tm // hm
    rows = hm + 2 * SUBLANES

    @pl.when(j == 0)
    def _():
        gain, shift, scale = g_ref[...], sh_ref[0], sc_ref[0]
        for r in range(n_grp):
            r0 = r * hm
            h_ref[r * rows:r * rows + hm, :] = _bf(
                _modulate(x_ref[r0:r0 + hm, :], gain, shift, scale, mxu_stats=True))
            nxt_src = x_ref[r0 + hm:r0 + hm + SUBLANES, :] if r + 1 < n_grp else xn_ref[...]
            prv_src = x_ref[r0 - SUBLANES:r0, :] if r > 0 else xp_ref[...]
            keep_next = jnp.where((i * tm + r0 + hm) % seq_len == 0, 0.0, 1.0)
            keep_prev = jnp.where((i * tm + r0) % seq_len == 0, 0.0, 1.0)
            nxt = keep_next * _modulate(nxt_src, gain, shift, scale)
            prv = keep_prev * _modulate(prv_src, gain, shift, scale)
            h_ref[r * rows + hm:(r + 1) * rows, :] = _bf(jnp.concatenate([nxt, prv], axis=0))
        o_ref[...] = jnp.zeros_like(o_ref)

    ca, cb = ca_ref[...], cb_ref[...]
    ups = []
    for r in range(n_grp):
        h = h_ref[r * rows:(r + 1) * rows, :]
        ups.append((_dot(h, wa_ref[...]), _dot(h, wb_ref[...])))

    def conv(u, cw):
        prev = pltpu.roll(u, 1, axis=0)[0:hm]
        nxt = pltpu.roll(u, rows - 1, axis=0)[0:hm]
        return prev * cw[0:1] + u[0:hm] * cw[1:2] + nxt * cw[2:3]

    for r, (ua, ub) in enumerate(ups):
        a = conv(ua, ca)
        b = conv(ub, cb)
        act = a * (1.0 / (1.0 + jnp.exp(-a))) * b
        o_ref[r * hm:(r + 1) * hm, :] += _dot(_bf(act), wo_ref[...])

    @pl.when(j == nj - 1)
    def _():
        o_ref[...] = x_ref[...] + gt_ref[0] * o_ref[...]


def _ffn_call(x, gain, modarr, w_in, conv_w, w_out, layer, mod_row, seq_len, *, tm=TM_FF,
              tn=TN_FF, name="conv_ffn"):
    m, d = x.shape
    tm = min(tm, m)
    hm = min(HM_FF, seq_len)
    assert m % tm == 0 and tm % hm == 0 and seq_len % hm == 0
    nj = D_FF_PAD // tn
    hb = tm // SUBLANES
    last_hb = m // SUBLANES - 1
    kern = functools.partial(_ffn_kernel, tm=tm, hm=hm, seq_len=seq_len)
    return pl.pallas_call(
        kern,
        grid=(m // tm, nj),
        in_specs=[
            pl.BlockSpec((tm, d), lambda i, j: (i, 0)),
            pl.BlockSpec((SUBLANES, d), lambda i, j: (jnp.maximum(i * hb - 1, 0), 0)),
            pl.BlockSpec((SUBLANES, d), lambda i, j: (jnp.minimum((i + 1) * hb, last_hb), 0)),
            pl.BlockSpec((1, d), lambda i, j: (0, 0)),
            pl.BlockSpec((1, 1, d), lambda i, j: (mod_row(i), 0, 3)),
            pl.BlockSpec((1, 1, d), lambda i, j: (mod_row(i), 0, 4)),
            pl.BlockSpec((1, 1, d), lambda i, j: (mod_row(i), 0, 5)),
            pl.BlockSpec((None, d, tn), lambda i, j: (layer, 0, j)),
            pl.BlockSpec((None, d, tn), lambda i, j: (layer, 0, nj + j)),
            pl.BlockSpec((None, 3, tn), lambda i, j: (layer, 0, j)),
            pl.BlockSpec((None, 3, tn), lambda i, j: (layer, 0, nj + j)),
            pl.BlockSpec((None, tn, d), lambda i, j: (layer, j, 0)),
        ],
        out_specs=pl.BlockSpec((tm, d), lambda i, j: (i, 0)),
        out_shape=jax.ShapeDtypeStruct((m, d), jnp.float32),
        scratch_shapes=[
            pltpu.VMEM(((tm // hm) * (hm + 2 * SUBLANES), d), jnp.bfloat16),
        ],
        compiler_params=_params("parallel", "arbitrary"),
        name=name,
    )(x, x, x, gain, modarr, modarr, modarr, w_in, w_in, conv_w, conv_w, w_out)


def _na_group_geometry(g):
    r0 = g * NA_GROUP_ROWS
    return r0, min(max(r0 - NA_WIN_H // 2, 0), GRID_H - NA_KEY_ROWS)


def _na_build_bias(rpb_ref, h, tc_scr, bias_scr):
    n_dr = 2 * NA_WIN_H - 1
    n_dc = 2 * NA_WIN_W - 1
    qc = lax.broadcasted_iota(jnp.int32, (GRID_W, GRID_W), 0)
    kc = lax.broadcasted_iota(jnp.int32, (GRID_W, GRID_W), 1)
    col0 = jnp.clip(qc - NA_WIN_W // 2, 0, GRID_W - NA_WIN_W)
    col_valid = (kc >= col0) & (kc < col0 + NA_WIN_W)
    delta = kc - qc + (NA_WIN_W - 1)
    base = h * (n_dr * n_dc)
    for dr in range(n_dr):
        acc = jnp.zeros((GRID_W, GRID_W), jnp.float32)
        for e in range(n_dc):
            acc = jnp.where(delta == e, rpb_ref[base + dr * n_dc + e] * LOG2E, acc)
        tc_scr[dr] = jnp.where(col_valid, acc, NEG)
    tc_scr[n_dr] = jnp.full((GRID_W, GRID_W), NEG, jnp.float32)
    n_groups = GRID_H // NA_GROUP_ROWS
    for t, g in enumerate((0, 1, n_groups - 1)):
        r0, ks = _na_group_geometry(g)
        for qr in range(NA_GROUP_ROWS):
            r = r0 + qr
            row0 = min(max(r - NA_WIN_H // 2, 0), GRID_H - NA_WIN_H)
            pieces = []
            for kr in range(NA_KEY_ROWS):
                kra = ks + kr
                inside = row0 <= kra < row0 + NA_WIN_H
                pieces.append(tc_scr[kra - r + (NA_WIN_H - 1)] if inside else tc_scr[n_dr])
            bias_scr[t, qr * GRID_W:(qr + 1) * GRID_W, :] = jnp.concatenate(pieces, axis=1)


def _na_kernel(rpb_ref, q_ref, k_ref, v_ref, kc_ref, vc_ref, o_ref, tc_scr, bias_scr):
    @pl.when(pl.program_id(1) == 0)
    def _():
        _na_build_bias(rpb_ref, pl.program_id(0), tc_scr, bias_scr)

    kc = kc_ref[...]
    vc = vc_ref[...]
    n_groups = GRID_H // NA_GROUP_ROWS

    def group(g):
        r0 = g * NA_GROUP_ROWS
        ks = jnp.clip(r0 - NA_WIN_H // 2, 0, GRID_H - NA_KEY_ROWS)
        q0 = pl.multiple_of(g * NA_Q, NA_Q)
        k0 = pl.multiple_of(ks * GRID_W, GRID_W)
        tb = jnp.where(g == 0, 0, jnp.where(g == n_groups - 1, 2, 1))
        q = q_ref[pl.ds(q0, NA_Q), :]
        kw = k_ref[pl.ds(k0, NA_K), :]
        vw = v_ref[pl.ds(k0, NA_K), :]
        sw = _dot_nt(q, kw) + bias_scr[tb]
        sc = _dot_nt(q, kc)
        mx = jnp.maximum(jnp.max(sw, axis=-1, keepdims=True), jnp.max(sc, axis=-1, keepdims=True))
        pw = jnp.exp2(sw - mx)
        pc = jnp.exp2(sc - mx)
        den = jnp.sum(pw, axis=-1, keepdims=True) + jnp.sum(pc, axis=-1, keepdims=True)
        o = _dot(_bf(pw), vw) + _dot(_bf(pc), vc)
        o_ref[pl.ds(q0, NA_Q), :] = _bf(o / den)

    def body(t, carry):
        for u in range(NA_UNROLL):
            group(t * NA_UNROLL + u)
        return carry

    lax.fori_loop(0, n_groups // NA_UNROLL, body, 0)


def _na_call(qkv, qkv_c, rpb):
    n_dr = 2 * NA_WIN_H - 1
    return pl.pallas_call(
        _na_kernel,
        grid=(NA_HEADS, BATCH),
        in_specs=[
            pl.BlockSpec(memory_space=pltpu.SMEM),
            pl.BlockSpec((SEQ, LANES), lambda h, b: (b, h)),
            pl.BlockSpec((SEQ, LANES), lambda h, b: (b, NA_HEADS + h)),
            pl.BlockSpec((SEQ, LANES), lambda h, b: (b, 2 * NA_HEADS + h)),
            pl.BlockSpec((CTX_LEN, LANES), lambda h, b: (b, NA_HEADS + h)),
            pl.BlockSpec((CTX_LEN, LANES), lambda h, b: (b, 2 * NA_HEADS + h)),
        ],
        out_specs=pl.BlockSpec((SEQ, LANES), lambda h, b: (b, h)),
        out_shape=jax.ShapeDtypeStruct((BATCH * SEQ, NA_WIDTH), jnp.bfloat16),
        scratch_shapes=[
            pltpu.VMEM((n_dr + 1, GRID_W, GRID_W), jnp.float32),
            pltpu.VMEM((3, NA_Q, NA_K), jnp.float32),
        ],
        compiler_params=_params("parallel", "arbitrary"),
        name="na_attn",
    )(rpb.reshape(-1), qkv, qkv, qkv, qkv_c, qkv_c)


def _key_chunks(src_rows, chunk):
    chunks, r = [], 0
    for rows in src_rows:
        size = min(chunk, rows)
        assert rows % size == 0
        chunks += [(r + c, size) for c in range(0, rows, size)]
        r += rows
    return chunks


def _pipelined_attention(streams, vt_scr, chunks):
    dv = vt_scr.shape[0]
    sub = streams[0][2].shape[1]
    outs, m_prev = [], None
    for u in range(len(streams) + 1):
        m = jnp.full((1, sub), NEG, jnp.float32)
        l = jnp.zeros((1, sub), jnp.float32)
        acc = jnp.zeros((dv, sub), jnp.float32)
        for c0, cs in chunks:
            if u < len(streams):
                q_t, k_scr, s_ref = streams[u]
                s = _dot(k_scr[c0:c0 + cs, :], q_t)
                s_ref[c0:c0 + cs, :] = s
                m = jnp.maximum(m, jnp.max(s, axis=0, keepdims=True))
            if u > 0:
                p = jnp.exp2(streams[u - 1][2][c0:c0 + cs, :] - m_prev)
                l = l + jnp.sum(p, axis=0, keepdims=True)
                acc = acc + _dot(vt_scr[:, c0:c0 + cs], _bf(p))
        if u > 0:
            outs.append(acc / l)
        m_prev = m
    return outs


def _transpose_bf16(x):
    return _bf(x.astype(jnp.float32).T)


def _attn_kernel(*refs, n_qparts, n_src, n_kparts, src_rows):
    it = iter(refs)
    q_refs = [next(it) for _ in range(n_qparts)]
    srcs = []
    for _ in range(n_src):
        k_refs = [next(it) for _ in range(n_kparts)]
        srcs.append((k_refs, next(it)))
    o_ref = next(it)
    k_scr, vt_scr = next(it), next(it)
    s_scrs = list(it)

    @pl.when(pl.program_id(2) == 0)
    def _():
        r = 0
        for (k_refs, v_ref), rows in zip(srcs, src_rows):
            for p, k_ref in enumerate(k_refs):
                k_scr[r:r + rows, p * LANES:(p + 1) * LANES] = k_ref[...]
            vt_scr[:, r:r + rows] = _transpose_bf16(v_ref[...])
            r += rows

    q = jnp.concatenate([qr[...] for qr in q_refs], axis=-1) if n_qparts > 1 else q_refs[0][...]
    sub = s_scrs[0].shape[1]
    streams = [(_transpose_bf16(q[u * sub:(u + 1) * sub]), k_scr, s_scrs[u % 2])
               for u in range(q.shape[0] // sub)]
    outs = _pipelined_attention(streams, vt_scr, _key_chunks(src_rows, KEY_CHUNK_ATTN))
    for u, o_t in enumerate(outs):
        o_ref[u * sub:(u + 1) * sub, :] = _bf(o_t.T)


def _attn_call(q_parts, sources, n_heads, lq, tq, dv, name):
    tq = min(tq, lq)
    sub = min(TQ_SUB, tq)
    nq = lq // tq
    in_specs, args = [], []
    for arr, cf in q_parts:
        in_specs.append(pl.BlockSpec((tq, LANES), lambda b, h, i, cf=cf: (b * nq + i, cf(h))))
        args.append(arr)
    src_rows = []
    n_kparts = len(sources[0][1])
    for rows, k_parts, (v_arr, vcf) in sources:
        src_rows.append(rows)
        for arr, cf in k_parts:
            in_specs.append(pl.BlockSpec((rows, LANES), lambda b, h, i, cf=cf: (b, cf(h))))
            args.append(arr)
        in_specs.append(pl.BlockSpec((rows, dv), lambda b, h, i, cf=vcf: (b, cf(h))))
        args.append(v_arr)
    nk = sum(src_rows)
    kern = functools.partial(_attn_kernel, n_qparts=len(q_parts), n_src=len(sources),
                             n_kparts=n_kparts, src_rows=tuple(src_rows))
    return pl.pallas_call(
        kern,
        grid=(BATCH, n_heads, nq),
        in_specs=in_specs,
        out_specs=pl.BlockSpec((tq, dv), lambda b, h, i: (b * nq + i, h)),
        out_shape=jax.ShapeDtypeStruct((BATCH * lq, n_heads * dv), jnp.bfloat16),
        scratch_shapes=[
            pltpu.VMEM((nk, n_kparts * LANES), jnp.bfloat16),
            pltpu.VMEM((dv, nk), jnp.bfloat16),
        ] + [pltpu.VMEM((nk, sub), jnp.float32)] * 2,
        compiler_params=_params("parallel", "parallel", "arbitrary"),
        name=name,
    )(*args)


def _diff_kernel(*refs, n_src, src_rows, lam_init):
    it = iter(refs)
    q1_ref, q2_ref = next(it), next(it)
    srcs = [(next(it), next(it), next(it)) for _ in range(n_src)]
    lam_ref, sub_ref = next(it), next(it)
    o_ref = next(it)
    k1_scr, k2_scr, vt_scr, s1_scr, s2_scr = [next(it) for _ in range(5)]

    @pl.when(pl.program_id(2) == 0)
    def _():
        r = 0
        for (k1_ref, k2_ref, v_ref), rows in zip(srcs, src_rows):
            k1_scr[r:r + rows, :] = k1_ref[...]
            k2_scr[r:r + rows, :] = k2_ref[...]
            vt_scr[:, r:r + rows] = _transpose_bf16(v_ref[...])
            r += rows

    lf = lam_ref[...]
    lam = (jnp.exp(jnp.sum(lf[0:1] * lf[1:2], axis=-1, keepdims=True))
           - jnp.exp(jnp.sum(lf[2:3] * lf[3:4], axis=-1, keepdims=True)) + lam_init)
    chunks = _key_chunks(src_rows, KEY_CHUNK_DIFF)
    sub = s1_scr.shape[1]
    n_sub = q1_ref.shape[0] // sub
    streams = []
    for u in range(n_sub):
        rows = slice(u * sub, (u + 1) * sub)
        streams.append((_transpose_bf16(q1_ref[rows, :]), k1_scr, s1_scr))
        streams.append((_transpose_bf16(q2_ref[rows, :]), k2_scr, s2_scr))
    outs = _pipelined_attention(streams, vt_scr, chunks)
    o_t = jnp.concatenate([outs[2 * u] - lam * outs[2 * u + 1] for u in range(n_sub)],
                          axis=1)
    o = o_t.T
    ms = jnp.mean(o * o, axis=-1, keepdims=True)
    o_ref[...] = _bf(o * lax.rsqrt(ms + EPS) * sub_ref[...] * (1.0 - lam_init))


def _diff_call(qkv_q, lq, sources, diff_lambda, subln, lam_init, tq, name):
    tq = min(tq, lq)
    sub = min(TQ_SUB, tq)
    nq = lq // tq
    qb0 = 3 * NA_WIDTH // LANES
    kb0 = (3 * NA_WIDTH + DIFF_WIDTH) // LANES
    vb0 = (3 * NA_WIDTH + 2 * DIFF_WIDTH) // (2 * LANES)
    in_specs = [
        pl.BlockSpec((tq, LANES), lambda b, h, i: (b * nq + i, qb0 + 2 * h)),
        pl.BlockSpec((tq, LANES), lambda b, h, i: (b * nq + i, qb0 + 2 * h + 1)),
    ]
    args = [qkv_q, qkv_q]
    src_rows = []
    for rows, arr in sources:
        src_rows.append(rows)
        in_specs += [
            pl.BlockSpec((rows, LANES), lambda b, h, i: (b, kb0 + 2 * h)),
            pl.BlockSpec((rows, LANES), lambda b, h, i: (b, kb0 + 2 * h + 1)),
            pl.BlockSpec((rows, 2 * LANES), lambda b, h, i: (b, vb0 + h)),
        ]
        args += [arr, arr, arr]
    in_specs += [
        pl.BlockSpec((4, LANES), lambda b, h, i: (0, 0)),
        pl.BlockSpec((1, 2 * LANES), lambda b, h, i: (0, 0)),
    ]
    args += [diff_lambda, subln]
    nk = sum(src_rows)
    kern = functools.partial(_diff_kernel, n_src=len(sources), src_rows=tuple(src_rows),
                             lam_init=lam_init)
    return pl.pallas_call(
        kern,
        grid=(BATCH, DIFF_HEADS, nq),
        in_specs=in_specs,
        out_specs=pl.BlockSpec((tq, 2 * LANES), lambda b, h, i: (b * nq + i, h)),
        out_shape=jax.ShapeDtypeStruct((BATCH * lq, DIFF_WIDTH), jnp.bfloat16),
        scratch_shapes=[
            pltpu.VMEM((nk, LANES), jnp.bfloat16),
            pltpu.VMEM((nk, LANES), jnp.bfloat16),
            pltpu.VMEM((2 * LANES, nk), jnp.bfloat16),
            pltpu.VMEM((nk, sub), jnp.float32),
            pltpu.VMEM((nk, sub), jnp.float32),
        ],
        compiler_params=_params("parallel", "parallel", "arbitrary"),
        name=name,
    )(*args)


def _rope_tables(d):
    h = d // 2
    half = h // 2
    lane = jnp.arange(PAIR, dtype=jnp.int32)[None, :] % LANES
    t = jnp.arange(SEQ, dtype=jnp.int32)[:, None]
    pos = jnp.where(lane < h, t // GRID_W, t % GRID_W).astype(jnp.float32)
    freqs = ROPE_BASE ** (-(lane % half).astype(jnp.float32) / half)
    ang = pos * freqs
    valid = lane < d
    first = (lane % h) < half
    cos = jnp.where(valid, jnp.cos(ang), 1.0)
    sin = jnp.where(valid, jnp.sin(ang), 0.0)
    sin_a = jnp.where(first, -sin, 0.0)
    sin_b = jnp.where(first, 0.0, sin)
    return (cos, sin_a, sin_b), half


def _pair_blockdiag(block):
    z = np.zeros_like(block)
    return jnp.asarray(np.block([[block, z], [z, block]]), dtype=jnp.bfloat16)


def _pad_lanes(v, fill=0.0):
    return jnp.pad(v, (0, LANES - v.shape[0]), constant_values=fill)


def _pair(v):
    return jnp.concatenate([v, v])


def _lat_row(tm):
    bpb = SEQ // tm
    return lambda i: i // bpb


def _ctx_row(i):
    return CTX_MOD_ROW


def _even_layer(x, xc, modarr, norm_mix, norm_ffn, ffn_w, w_in, w_out, qn_a, kn_a, rpb,
                qn_b, kn_b, diff_lambda, subln, lam_init, with_ctx):
    scale = HEAD_DIM ** -0.5 * LOG2E
    gains = jnp.stack([_pair(g) for g in (qn_a * scale, kn_a, qn_b * scale, kn_b)]
                      + [jnp.ones((PAIR,), jnp.float32)] * 4)
    rope = _rope_tables(HEAD_DIM)
    segs = [(True, 0, HEAD_DIM, False), (True, 1, HEAD_DIM, False), (False, 0, HEAD_DIM, False),
            (True, 2, HEAD_DIM, True), (True, 3, HEAD_DIM, True), (False, 0, HEAD_DIM, False)]
    w_in = _bf(w_in)
    w_out = _bf(w_out)
    gain_mix = norm_mix[None, :]
    lat_row = _lat_row(TM)

    qkv = _proj_call(x, w_in, gains, segs, mod=(gain_mix, modarr, 0, 1), rope=rope,
                     mod_row=_lat_row(TM_EVEN), tm=TM_EVEN, name="even_proj")
    qkv_c = _proj_call(xc, w_in, gains, segs, mod=(gain_mix, modarr, 0, 1), rope=None,
                       mod_row=_ctx_row, tm=TM_EVEN, name="even_proj_ctx")

    oa = _na_call(qkv, qkv_c, rpb)
    ob = _diff_call(qkv, SEQ, [(SEQ, qkv), (CTX_LEN, qkv_c)], diff_lambda, subln[None, :],
                    lam_init, TQ_DIFF, "diff_attn")
    x = _outproj_call([oa, ob], w_out, x, modarr, 2, lat_row, name="even_out")
    x = _ffn_call(x, norm_ffn[None, :], modarr, *ffn_w, _lat_row(TM_FF), SEQ)
    if with_ctx:
        oa_c = _attn_call(
            [(qkv_c, lambda h: h)],
            [(CTX_LEN, [(qkv_c, lambda h: NA_HEADS + h)], (qkv_c, lambda h: 2 * NA_HEADS + h))],
            NA_HEADS, CTX_LEN, CTX_LEN, HEAD_DIM, "ctx_attn")
        ob_c = _diff_call(qkv_c, CTX_LEN, [(CTX_LEN, qkv_c)], diff_lambda, subln[None, :],
                          lam_init, CTX_LEN, "diff_attn_ctx")
        xc = _outproj_call([oa_c, ob_c], w_out, xc, modarr, 2, _ctx_row, name="even_out_ctx")
        xc = _ffn_call(xc, norm_ffn[None, :], modarr, *ffn_w, _ctx_row, CTX_LEN,
                       name="conv_ffn_ctx")
    return x, xc


def _odd_layer(x, xc, modarr, norm_mix, norm_ffn, ffn_w, w_down, q_a_norm, kv_a_norm, w_uq,
               w_ukv, qn_nope, qn_rope, kn_nope, kn_rope, w_out, with_ctx):
    scale = (MLA_NOPE + MLA_ROPE) ** -0.5 * LOG2E
    rope_pair = _rope_tables(MLA_ROPE)
    gain_mix = norm_mix[None, :]
    lat_row = _lat_row(TM)

    w_down_p = _bf(jnp.pad(w_down, ((0, 0), (0, MLA_DOWN_PAD - MLA_DOWN))))
    zeros_r = jnp.zeros((MLA_Q_RANK,), jnp.float32)
    kr_gain = jnp.concatenate([_pad_lanes(kn_rope), zeros_r[:MLA_Q_RANK - LANES]])
    gains_d = jnp.stack([q_a_norm, kv_a_norm, kr_gain] + [zeros_r] * 5)
    wq = w_uq.reshape(MLA_Q_RANK, MLA_HEADS, MLA_NOPE + MLA_ROPE)
    wq_rope = jnp.pad(wq[:, :, MLA_NOPE:], ((0, 0), (0, 0), (0, LANES - MLA_ROPE)))
    wq_p = _bf(jnp.concatenate([wq[:, :, :MLA_NOPE].reshape(MLA_Q_RANK, -1),
                                wq_rope.reshape(MLA_Q_RANK, -1)], -1))
    wkv = w_ukv.reshape(MLA_KV_RANK, MLA_HEADS, MLA_NOPE + MLA_V)
    wkv_p = _bf(jnp.concatenate([wkv[:, :, :MLA_NOPE].reshape(MLA_KV_RANK, -1),
                                 wkv[:, :, MLA_NOPE:].reshape(MLA_KV_RANK, -1)], -1))
    gains_p = jnp.stack([_pair(qn_nope * scale), _pair(_pad_lanes(qn_rope) * scale),
                         _pair(kn_nope)] + [jnp.zeros((PAIR,), jnp.float32)] * 5)
    mod_in = (gain_mix, modarr, 0, 1)
    q, kv, kr = _mla_proj_call(x, w_down_p, wq_p, wkv_p, gains_d, gains_p, mod_in, rope_pair,
                               _lat_row(TM_MLA_PROJ), True, "mla_proj")
    kv_c, kr_c = _mla_proj_call(xc, w_down_p, wq_p, wkv_p, gains_d, gains_p, mod_in, None,
                                _ctx_row, False, "mla_proj_ctx")

    def src(rows, kv_arr, kr_arr):
        return (rows, [(kv_arr, lambda h: h), (kr_arr, lambda h: 0)],
                (kv_arr, lambda h: MLA_HEADS + h))

    o = _attn_call([(q, lambda h: h), (q, lambda h: MLA_HEADS + h)],
                   [src(SEQ, kv, kr), src(CTX_LEN, kv_c, kr_c)],
                   MLA_HEADS, SEQ, TQ_MLA, MLA_V, "mla_attn")
    x = _outproj_call([o], _bf(w_out), x, modarr, 2, lat_row, name="mla_out")
    x = _ffn_call(x, norm_ffn[None, :], modarr, *ffn_w, _lat_row(TM_FF), SEQ)
    assert not with_ctx
    return x, xc


def _mla_proj_kernel(x_ref, g_ref, sh_ref, sc_ref, wd_ref, wq_ref, wkv_ref, gd_ref, gp_ref,
                     ones_ref, *rest, rope, rope_half, with_q):
    rope_refs = rest[:3] if rope else None
    outs = rest[3:] if rope else rest
    if with_q:
        q_ref, kv_ref, kr_ref = outs
    else:
        kv_ref, kr_ref = outs
    h = _bf(_modulate(x_ref[...], g_ref[...], sh_ref[0], sc_ref[0]))
    down = _dot(h, wd_ref[...])

    def latent_norm(xs, row, nvalid):
        ms = jnp.sum(xs * xs, axis=-1, keepdims=True) * (1.0 / nvalid)
        return xs * lax.rsqrt(ms + EPS) * gd_ref[row:row + 1, 0:xs.shape[1]]

    kva = _bf(latent_norm(down[:, MLA_Q_RANK:MLA_Q_RANK + MLA_KV_RANK], 1, MLA_KV_RANK))
    kr = latent_norm(down[:, MLA_Q_RANK + MLA_KV_RANK:MLA_DOWN_PAD], 2, MLA_ROPE)
    if rope:
        cos_ref, sa_ref, sb_ref = rope_refs
        kr = (kr * cos_ref[:, 0:LANES]
              + pltpu.roll(kr, LANES - rope_half, axis=1) * sa_ref[:, 0:LANES]
              + pltpu.roll(kr, rope_half, axis=1) * sb_ref[:, 0:LANES])
    kr_ref[...] = _bf(kr)

    _head_projection(kva, wkv_ref, kv_ref,
                     [(True, 2, MLA_NOPE, False), (False, 0, MLA_V, False)],
                     gp_ref, ones_ref, rope_refs, rope_half)
    if with_q:
        qa = _bf(latent_norm(down[:, 0:MLA_Q_RANK], 0, MLA_Q_RANK))
        _head_projection(qa, wq_ref, q_ref,
                         [(True, 0, MLA_NOPE, False), (True, 1, MLA_ROPE, True)],
                         gp_ref, ones_ref, rope_refs, rope_half)


def _mla_proj_call(x, wd, wq, wkv, gains_d, gains_p, mod, rope, mod_row, with_q, name,
                   tm=TM_MLA_PROJ):
    m, k = x.shape
    tm = min(tm, m)
    bpb = SEQ // tm
    gain, modarr, sh_c, sc_c = mod
    ones = _pair_blockdiag(np.ones((LANES, LANES), np.float32))
    in_specs = [
        pl.BlockSpec((tm, k), lambda i: (i, 0)),
        pl.BlockSpec((1, k), lambda i: (0, 0)),
        pl.BlockSpec((1, 1, k), lambda i: (mod_row(i), 0, sh_c)),
        pl.BlockSpec((1, 1, k), lambda i: (mod_row(i), 0, sc_c)),
    ]
    args = [x, gain, modarr, modarr]
    for w in (wd, wq, wkv, gains_d, gains_p, ones):
        in_specs.append(pl.BlockSpec(w.shape, lambda i: (0, 0), pipeline_mode=pl.Buffered(1)))
        args.append(w)
    rope_half = 0
    if rope is not None:
        tables, rope_half = rope
        for t in tables:
            in_specs.append(pl.BlockSpec((tm, PAIR), lambda i: (i % bpb, 0)))
            args.append(t)
    widths = ([wq.shape[1]] if with_q else []) + [wkv.shape[1], LANES]
    return pl.pallas_call(
        functools.partial(_mla_proj_kernel, rope=rope is not None, rope_half=rope_half,
                          with_q=with_q),
        grid=(m // tm,),
        in_specs=in_specs,
        out_specs=[pl.BlockSpec((tm, n), lambda i: (i, 0)) for n in widths],
        out_shape=[jax.ShapeDtypeStruct((m, n), jnp.bfloat16) for n in widths],
        compiler_params=_params("parallel"),
        name=name,
    )(*args)


def _w_in_prep_kernel(x_ref, o_ref):
    pad = jnp.zeros((x_ref.shape[0], D_FF_PAD - D_FF), jnp.bfloat16)
    o_ref[:, 0:D_FF] = _bf(x_ref[:, 0:D_FF])
    o_ref[:, D_FF:D_FF_PAD] = pad
    o_ref[:, D_FF_PAD:D_FF_PAD + D_FF] = _bf(x_ref[:, D_FF:2 * D_FF])
    o_ref[:, D_FF_PAD + D_FF:2 * D_FF_PAD] = pad


def _w_out_prep_kernel(x_ref, o_ref):
    o_ref[0:D_FF, :] = _bf(x_ref[...])
    o_ref[D_FF:D_FF_PAD, :] = jnp.zeros((D_FF_PAD - D_FF, x_ref.shape[1]), jnp.bfloat16)


def _ffn_weights(w_in, conv_w, w_out):
    depth, d, _ = w_in.shape
    rows, cols = 256, 256
    w_in_p = pl.pallas_call(
        _w_in_prep_kernel,
        grid=(depth, d // rows),
        in_specs=[pl.BlockSpec((None, rows, 2 * D_FF), lambda l, i: (l, i, 0))],
        out_specs=pl.BlockSpec((None, rows, 2 * D_FF_PAD), lambda l, i: (l, i, 0)),
        out_shape=jax.ShapeDtypeStruct((depth, d, 2 * D_FF_PAD), jnp.bfloat16),
        compiler_params=_params("parallel", "parallel"),
        name="ffn_w_in_prep",
    )(w_in)
    w_out_p = pl.pallas_call(
        _w_out_prep_kernel,
        grid=(depth, d // cols),
        in_specs=[pl.BlockSpec((None, D_FF, cols), lambda l, j: (l, 0, j))],
        out_specs=pl.BlockSpec((None, D_FF_PAD, cols), lambda l, j: (l, 0, j)),
        out_shape=jax.ShapeDtypeStruct((depth, D_FF_PAD, d), jnp.bfloat16),
        compiler_params=_params("parallel", "parallel"),
        name="ffn_w_out_prep",
    )(w_out)
    pad = D_FF_PAD - D_FF
    zc = jnp.zeros(conv_w.shape[:2] + (pad,), conv_w.dtype)
    conv_p = jnp.concatenate([conv_w[..., :D_FF], zc, conv_w[..., D_FF:], zc], -1)
    return w_in_p, conv_p, w_out_p


def kernel(x, c, ctx, c_ctx, ada_w, ada_b, norm_mix, norm_ffn, ffn_w_in, ffn_conv, ffn_w_out, even_w_in, even_w_out, na_q_norm, na_k_norm, na_rpb, diff_q_norm, diff_k_norm, diff_lambda, diff_subln, mla_w_down, mla_q_a_norm, mla_kv_a_norm, mla_w_uq, mla_w_ukv, mla_q_nope_norm, mla_q_rope_norm, mla_k_nope_norm, mla_k_rope_norm, mla_w_out):
    cond = jnp.concatenate(
        [c, c_ctx[None, :], jnp.zeros((MOD_ROWS - BATCH - 1, D_MODEL), jnp.float32)], 0)
    mod = _ada_call(cond, ada_w, ada_b)
    xl = x.reshape(BATCH * SEQ, D_MODEL)
    xc = ctx.reshape(BATCH * CTX_LEN, D_MODEL)
    ffn_all = _ffn_weights(ffn_w_in, ffn_conv, ffn_w_out)
    for l in range(DEPTH):
        with_ctx = l < DEPTH - 1
        modarr = mod[l].reshape(MOD_ROWS, 1, 6 * D_MODEL)
        ffn_w = ffn_all + (l,)
        i = l // 2
        if l % 2 == 0:
            lam_init = 0.8 - 0.6 * math.exp(-0.3 * l)
            xl, xc = _even_layer(xl, xc, modarr, norm_mix[l], norm_ffn[l], ffn_w, even_w_in[i],
                                 even_w_out[i], na_q_norm[i], na_k_norm[i], na_rpb[i],
                                 diff_q_norm[i], diff_k_norm[i], diff_lambda[i], diff_subln[i],
                                 lam_init, with_ctx)
        else:
            xl, xc = _odd_layer(xl, xc, modarr, norm_mix[l], norm_ffn[l], ffn_w, mla_w_down[i],
                                mla_q_a_norm[i], mla_kv_a_norm[i], mla_w_uq[i], mla_w_ukv[i],
                                mla_q_nope_norm[i], mla_q_rope_norm[i], mla_k_nope_norm[i],
                                mla_k_rope_norm[i], mla_w_out[i], with_ctx)
    return xl.reshape(BATCH, SEQ, D_MODEL)
```

```python
import functools
import math

import numpy as np
import jax
import jax.numpy as jnp
from jax import lax
from jax.experimental import pallas as pl
from jax.experimental.pallas import tpu as pltpu

D_MODEL = 2048
BATCH = 4
SEQ = 4096
DEPTH = 2
GRID_W = 64
GRID_H = SEQ // GRID_W
CTX_LEN = 256
HEAD_DIM = 128
NA_HEADS = 8
NA_WIN_H = 8
NA_WIN_W = 16
DIFF_HEADS = 4
NA_WIDTH = 1024
DIFF_WIDTH = 1024
EVEN_PROJ = 6144
MLA_HEADS = 16
MLA_Q_RANK = 512
MLA_KV_RANK = 512
MLA_NOPE = 128
MLA_ROPE = 64
MLA_V = 128
MLA_DOWN = 1088
D_FF = 5504
ROPE_BASE = 10000.0
EPS = 1e-6

LANES = 128
SUBLANES = 8
MOD_ROWS = 8
CTX_MOD_ROW = BATCH
D_FF_PAD = 5632
MLA_DOWN_PAD = 1152
VMEM_LIMIT = 56 * 1024 * 1024

TM = 512
TN_FF = 512
TM_FF = 1024
TM_EVEN = 512
TM_MLA_PROJ = 512
HM_FF = 512
SUB_N = 512
PAIR = 2 * LANES
NA_GROUP_ROWS = 4
NA_KEY_ROWS = NA_GROUP_ROWS + NA_WIN_H
NA_UNROLL = 4
NA_Q = NA_GROUP_ROWS * GRID_W
NA_K = NA_KEY_ROWS * GRID_W
TQ_DIFF = 512
TQ_MLA = 1024
TQ_SUB = 256
KEY_CHUNK_ATTN = 256
KEY_CHUNK_DIFF = 512
NEG = -1e30
LOG2E = math.log2(math.e)

_NT = (((1,), (1,)), ((), ()))


def _params(*sem):
    return pltpu.CompilerParams(dimension_semantics=sem, vmem_limit_bytes=VMEM_LIMIT)


def _bf(x):
    return x.astype(jnp.bfloat16)


def _dot(a, b):
    return jnp.dot(a, b, preferred_element_type=jnp.float32)


def _dot_nt(a, b):
    return lax.dot_general(a, b, _NT, preferred_element_type=jnp.float32)


def _modulate(x, gain, shift, scale, mxu_stats=False):
    d = x.shape[1]
    if mxu_stats:
        ss = _dot(_bf(x * x), jnp.ones((d, LANES), jnp.bfloat16))
        r = jnp.tile(lax.rsqrt(ss * (1.0 / d) + EPS), (1, d // LANES))
    else:
        r = lax.rsqrt(jnp.mean(x * x, axis=-1, keepdims=True) + EPS)
    return x * r * (gain * (1.0 + scale)) + shift


def _ada_kernel(c_ref, w_ref, b_ref, o_ref):
    c = c_ref[...]
    s = c * (1.0 / (1.0 + jnp.exp(-c)))
    o_ref[0] = _dot(_bf(s), _bf(w_ref[0])) + b_ref[0]


def _ada_call(cond, ada_w, ada_b):
    tn = 512
    n = ada_w.shape[-1]
    return pl.pallas_call(
        _ada_kernel,
        grid=(DEPTH, n // tn),
        in_specs=[
            pl.BlockSpec((MOD_ROWS, D_MODEL), lambda l, j: (0, 0)),
            pl.BlockSpec((1, D_MODEL, tn), lambda l, j: (l, 0, j)),
            pl.BlockSpec((1, 1, tn), lambda l, j: (l, 0, j)),
        ],
        out_specs=pl.BlockSpec((1, MOD_ROWS, tn), lambda l, j: (l, 0, j)),
        out_shape=jax.ShapeDtypeStruct((DEPTH, MOD_ROWS, n), jnp.float32),
        compiler_params=_params("parallel", "parallel"),
        name="ada_mod",
    )(cond, ada_w, ada_b.reshape(DEPTH, 1, n))


def _pair_epilogue(acc, spec, gains_ref, ones_ref, rope_refs, rope_half):
    norm, gain_row, nvalid, do_rope = spec
    y = acc
    if norm:
        ss = _dot(_bf(acc * acc), ones_ref[...])
        y = acc * lax.rsqrt(ss * (1.0 / nvalid) + EPS) * gains_ref[gain_row:gain_row + 1, :]
    if do_rope and rope_refs is not None:
        cos_ref, sa_ref, sb_ref = rope_refs
        y = (y * cos_ref[...]
             + pltpu.roll(y, PAIR - rope_half, axis=1) * sa_ref[...]
             + pltpu.roll(y, rope_half, axis=1) * sb_ref[...])
    return y


def _head_projection(a, w_ref, o_ref, segs, gains_ref, ones_ref, rope_refs, rope_half):
    width = w_ref.shape[1] // len(segs)
    for c in range(0, w_ref.shape[1], SUB_N):
        acc = _dot(a, w_ref[:, c:c + SUB_N])
        for p in range(0, SUB_N, PAIR):
            y = _pair_epilogue(acc[:, p:p + PAIR], segs[c // width], gains_ref, ones_ref,
                               rope_refs, rope_half)
            o_ref[:, c + p:c + p + PAIR] = _bf(y)


def _proj_kernel(x_ref, g_ref, sh_ref, sc_ref, w_ref, gains_ref, ones_ref, *rest, segs, rope,
                 rope_half):
    rope_refs = rest[:3] if rope else None
    o_ref = rest[-1]
    h = _bf(_modulate(x_ref[...], g_ref[...], sh_ref[0], sc_ref[0], mxu_stats=True))
    _head_projection(h, w_ref, o_ref, segs, gains_ref, ones_ref, rope_refs, rope_half)


def _proj_call(x, w, gains, segs, *, mod, rope=None, mod_row=None, tm=TM, name="proj"):
    m, k = x.shape
    n = w.shape[1]
    tm = min(tm, m)
    assert m % tm == 0 and n % (len(segs) * SUB_N) == 0
    bpb = SEQ // tm
    gain, modarr, sh_c, sc_c = mod
    ones = _pair_blockdiag(np.ones((LANES, LANES), np.float32))
    in_specs = [
        pl.BlockSpec((tm, k), lambda i: (i, 0)),
        pl.BlockSpec((1, k), lambda i: (0, 0)),
        pl.BlockSpec((1, 1, k), lambda i: (mod_row(i), 0, sh_c)),
        pl.BlockSpec((1, 1, k), lambda i: (mod_row(i), 0, sc_c)),
    ]
    args = [x, gain, modarr, modarr]
    for const in (w, gains, ones):
        in_specs.append(pl.BlockSpec(const.shape, lambda i: (0, 0), pipeline_mode=pl.Buffered(1)))
        args.append(const)
    rope_half = 0
    if rope is not None:
        tables, rope_half = rope
        for t in tables:
            in_specs.append(pl.BlockSpec((tm, PAIR), lambda i: (i % bpb, 0)))
            args.append(t)
    kern = functools.partial(_proj_kernel, segs=tuple(segs), rope=rope is not None,
                             rope_half=rope_half)
    return pl.pallas_call(
        kern,
        grid=(m // tm,),
        in_specs=in_specs,
        out_specs=pl.BlockSpec((tm, n), lambda i: (i, 0)),
        out_shape=jax.ShapeDtypeStruct((m, n), jnp.bfloat16),
        compiler_params=_params("parallel"),
        name=name,
    )(*args)


def _outproj_kernel(*refs, n_a):
    a_refs = refs[:n_a]
    w_ref, x_ref, g_ref, o_ref = refs[n_a:]
    n = o_ref.shape[1]
    a_vals = [a_ref[...] for a_ref in a_refs]
    for c in range(0, n, SUB_N):
        acc, row = None, 0
        for a in a_vals:
            part = _dot(a, w_ref[row:row + a.shape[1], c:c + SUB_N])
            acc = part if acc is None else acc + part
            row += a.shape[1]
        o_ref[:, c:c + SUB_N] = x_ref[:, c:c + SUB_N] + g_ref[0][:, c:c + SUB_N] * acc


def _outproj_call(a_list, w, x, modarr, gate_chunk, mod_row, *, tm=TM, name="outproj"):
    m, n = x.shape
    tm = min(tm, m)
    in_specs, args = [], []
    for a in a_list:
        in_specs.append(pl.BlockSpec((tm, a.shape[1]), lambda i: (i, 0)))
        args.append(a)
    in_specs += [
        pl.BlockSpec(w.shape, lambda i: (0, 0)),
        pl.BlockSpec((tm, n), lambda i: (i, 0)),
        pl.BlockSpec((1, 1, n), lambda i: (mod_row(i), 0, gate_chunk)),
    ]
    args += [w, x, modarr]
    return pl.pallas_call(
        functools.partial(_outproj_kernel, n_a=len(a_list)),
        grid=(m // tm,),
        in_specs=in_specs,
        out_specs=pl.BlockSpec((tm, n), lambda i: (i, 0)),
        out_shape=jax.ShapeDtypeStruct((m, n), jnp.float32),
        compiler_params=_params("parallel"),
        name=name,
    )(*args)


def _ffn_kernel(x_ref, xp_ref, xn_ref, g_ref, sh_ref, sc_ref, gt_ref, wa_ref, wb_ref,
                ca_ref, cb_ref, wo_ref, o_ref, h_ref, *, tm, hm, seq_len):
    i = pl.program_id(0)
    j = pl.program_id(1)
    nj = pl.num_programs(1)
    n_grp = tm // hm
    rows = hm + 2 * SUBLANES

    @pl.when(j == 0)
    def _():
        gain, shift, scale = g_ref[...], sh_ref[0], sc_ref[0]
        for r in range(n_grp):
            r0 = r * hm
            h_ref[r * rows:r * rows + hm, :] = _bf(
                _modulate(x_ref[r0:r0 + hm, :], gain, shift, scale, mxu_stats=True))
            nxt_src = x_ref[r0 + hm:r0 + hm + SUBLANES, :] if r + 1 < n_grp else xn_ref[...]
            prv_src = x_ref[r0 - SUBLANES:r0, :] if r > 0 else xp_ref[...]
            keep_next = jnp.where((i * tm + r0 + hm) % seq_len == 0, 0.0, 1.0)
            keep_prev = jnp.where((i * tm + r0) % seq_len == 0, 0.0, 1.0)
            nxt = keep_next * _modulate(nxt_src, gain, shift, scale)
            prv = keep_prev * _modulate(prv_src, gain, shift, scale)
            h_ref[r * rows + hm:(r + 1) * rows, :] = _bf(jnp.concatenate([nxt, prv], axis=0))
        o_ref[...] = jnp.zeros_like(o_ref)

    ca, cb = ca_ref[...], cb_ref[...]
    ups = []
    for r in range(n_grp):
        h = h_ref[r * rows:(r + 1) * rows, :]
        ups.append((_dot(h, wa_ref[...]), _dot(h, wb_ref[...])))

    def conv(u, cw):
        prev = pltpu.roll(u, 1, axis=0)[0:hm]
        nxt = pltpu.roll(u, rows - 1, axis=0)[0:hm]
        return prev * cw[0:1] + u[0:hm] * cw[1:2] + nxt * cw[2:3]

    for r, (ua, ub) in enumerate(ups):
        a = conv(ua, ca)
        b = conv(ub, cb)
        act = a * (1.0 / (1.0 + jnp.exp(-a))) * b
        o_ref[r * hm:(r + 1) * hm, :] += _dot(_bf(act), wo_ref[...])

    @pl.when(j == nj - 1)
    def _():
        o_ref[...] = x_ref[...] + gt_ref[0] * o_ref[...]


def _ffn_call(x, gain, modarr, w_in, conv_w, w_out, layer, mod_row, seq_len, *, tm=TM_FF,
              tn=TN_FF, name="conv_ffn"):
    m, d = x.shape
    tm = min(tm, m)
    hm = min(HM_FF, seq_len)
    assert m % tm == 0 and tm % hm == 0 and seq_len % hm == 0
    nj = D_FF_PAD // tn
    hb = tm // SUBLANES
    last_hb = m // SUBLANES - 1
    kern = functools.partial(_ffn_kernel, tm=tm, hm=hm, seq_len=seq_len)
    return pl.pallas_call(
        kern,
        grid=(m // tm, nj),
        in_specs=[
            pl.BlockSpec((tm, d), lambda i, j: (i, 0)),
            pl.BlockSpec((SUBLANES, d), lambda i, j: (jnp.maximum(i * hb - 1, 0), 0)),
            pl.BlockSpec((SUBLANES, d), lambda i, j: (jnp.minimum((i + 1) * hb, last_hb), 0)),
            pl.BlockSpec((1, d), lambda i, j: (0, 0)),
            pl.BlockSpec((1, 1, d), lambda i, j: (mod_row(i), 0, 3)),
            pl.BlockSpec((1, 1, d), lambda i, j: (mod_row(i), 0, 4)),
            pl.BlockSpec((1, 1, d), lambda i, j: (mod_row(i), 0, 5)),
            pl.BlockSpec((None, d, tn), lambda i, j: (layer, 0, j)),
            pl.BlockSpec((None, d, tn), lambda i, j: (layer, 0, nj + j)),
            pl.BlockSpec((None, 3, tn), lambda i, j: (layer, 0, j)),
            pl.BlockSpec((None, 3, tn), lambda i, j: (layer, 0, nj + j)),
            pl.BlockSpec((None, tn, d), lambda i, j: (layer, j, 0)),
        ],
        out_specs=pl.BlockSpec((tm, d), lambda i, j: (i, 0)),
        out_shape=jax.ShapeDtypeStruct((m, d), jnp.float32),
        scratch_shapes=[
            pltpu.VMEM(((tm // hm) * (hm + 2 * SUBLANES), d), jnp.bfloat16),
        ],
        compiler_params=_params("parallel", "arbitrary"),
        name=name,
    )(x, x, x, gain, modarr, modarr, modarr, w_in, w_in, conv_w, conv_w, w_out)


def _na_group_geometry(g):
    r0 = g * NA_GROUP_ROWS
    return r0, min(max(r0 - NA_WIN_H // 2, 0), GRID_H - NA_KEY_ROWS)


def _na_build_bias(rpb_ref, h, tc_scr, bias_scr):
    n_dr = 2 * NA_WIN_H - 1
    n_dc = 2 * NA_WIN_W - 1
    qc = lax.broadcasted_iota(jnp.int32, (GRID_W, GRID_W), 0)
    kc = lax.broadcasted_iota(jnp.int32, (GRID_W, GRID_W), 1)
    col0 = jnp.clip(qc - NA_WIN_W // 2, 0, GRID_W - NA_WIN_W)
    col_valid = (kc >= col0) & (kc < col0 + NA_WIN_W)
    delta = kc - qc + (NA_WIN_W - 1)
    base = h * (n_dr * n_dc)
    for dr in range(n_dr):
        acc = jnp.zeros((GRID_W, GRID_W), jnp.float32)
        for e in range(n_dc):
            acc = jnp.where(delta == e, rpb_ref[base + dr * n_dc + e] * LOG2E, acc)
        tc_scr[dr] = jnp.where(col_valid, acc, NEG)
    tc_scr[n_dr] = jnp.full((GRID_W, GRID_W), NEG, jnp.float32)
    n_groups = GRID_H // NA_GROUP_ROWS
    for t, g in enumerate((0, 1, n_groups - 1)):
        r0, ks = _na_group_geometry(g)
        for qr in range(NA_GROUP_ROWS):
            r = r0 + qr
            row0 = min(max(r - NA_WIN_H // 2, 0), GRID_H - NA_WIN_H)
            pieces = []
            for kr in range(NA_KEY_ROWS):
                kra = ks + kr
                inside = row0 <= kra < row0 + NA_WIN_H
                pieces.append(tc_scr[kra - r + (NA_WIN_H - 1)] if inside else tc_scr[n_dr])
            bias_scr[t, qr * GRID_W:(qr + 1) * GRID_W, :] = jnp.concatenate(pieces, axis=1)


def _na_kernel(rpb_ref, q_ref, k_ref, v_ref, kc_ref, vc_ref, o_ref, tc_scr, bias_scr):
    @pl.when(pl.program_id(1) == 0)
    def _():
        _na_build_bias(rpb_ref, pl.program_id(0), tc_scr, bias_scr)

    kc = kc_ref[...]
    vc = vc_ref[...]
    n_groups = GRID_H // NA_GROUP_ROWS

    def group(g):
        r0 = g * NA_GROUP_ROWS
        ks = jnp.clip(r0 - NA_WIN_H // 2, 0, GRID_H - NA_KEY_ROWS)
        q0 = pl.multiple_of(g * NA_Q, NA_Q)
        k0 = pl.multiple_of(ks * GRID_W, GRID_W)
        tb = jnp.where(g == 0, 0, jnp.where(g == n_groups - 1, 2, 1))
        q = q_ref[pl.ds(q0, NA_Q), :]
        kw = k_ref[pl.ds(k0, NA_K), :]
        vw = v_ref[pl.ds(k0, NA_K), :]
        sw = _dot_nt(q, kw) + bias_scr[tb]
        sc = _dot_nt(q, kc)
        mx = jnp.maximum(jnp.max(sw, axis=-1, keepdims=True), jnp.max(sc, axis=-1, keepdims=True))
        pw = jnp.exp2(sw - mx)
        pc = jnp.exp2(sc - mx)
        den = jnp.sum(pw, axis=-1, keepdims=True) + jnp.sum(pc, axis=-1, keepdims=True)
        o = _dot(_bf(pw), vw) + _dot(_bf(pc), vc)
        o_ref[pl.ds(q0, NA_Q), :] = _bf(o / den)

    def body(t, carry):
        for u in range(NA_UNROLL):
            group(t * NA_UNROLL + u)
        return carry

    lax.fori_loop(0, n_groups // NA_UNROLL, body, 0)


def _na_call(qkv, qkv_c, rpb):
    n_dr = 2 * NA_WIN_H - 1
    return pl.pallas_call(
        _na_kernel,
        grid=(NA_HEADS, BATCH),
        in_specs=[
            pl.BlockSpec(memory_space=pltpu.SMEM),
            pl.BlockSpec((SEQ, LANES), lambda h, b: (b, h)),
            pl.BlockSpec((SEQ, LANES), lambda h, b: (b, NA_HEADS + h)),
            pl.BlockSpec((SEQ, LANES), lambda h, b: (b, 2 * NA_HEADS + h)),
            pl.BlockSpec((CTX_LEN, LANES), lambda h, b: (b, NA_HEADS + h)),
            pl.BlockSpec((CTX_LEN, LANES), lambda h, b: (b, 2 * NA_HEADS + h)),
        ],
        out_specs=pl.BlockSpec((SEQ, LANES), lambda h, b: (b, h)),
        out_shape=jax.ShapeDtypeStruct((BATCH * SEQ, NA_WIDTH), jnp.bfloat16),
        scratch_shapes=[
            pltpu.VMEM((n_dr + 1, GRID_W, GRID_W), jnp.float32),
            pltpu.VMEM((3, NA_Q, NA_K), jnp.float32),
        ],
        compiler_params=_params("parallel", "arbitrary"),
        name="na_attn",
    )(rpb.reshape(-1), qkv, qkv, qkv, qkv_c, qkv_c)


def _key_chunks(src_rows, chunk):
    chunks, r = [], 0
    for rows in src_rows:
        size = min(chunk, rows)
        assert rows % size == 0
        chunks += [(r + c, size) for c in range(0, rows, size)]
        r += rows
    return chunks


def _pipelined_attention(streams, vt_scr, chunks, carry_in=None, defer_last=False):
    dv = vt_scr.shape[0]
    sub = streams[0][2].shape[1]
    n = len(streams)
    outs, prev = [], carry_in
    for u in range(n if defer_last else n + 1):
        cur = streams[u] if u < n else None
        m = jnp.full((1, sub), NEG, jnp.float32)
        l = jnp.zeros((1, sub), jnp.float32)
        acc = jnp.zeros((dv, sub), jnp.float32)
        for c0, cs in chunks:
            if cur is not None:
                q_t, k_scr, s_ref = cur
                s = _dot(k_scr[c0:c0 + cs, :], q_t)
                s_ref[c0:c0 + cs, :] = s
                m = jnp.maximum(m, jnp.max(s, axis=0, keepdims=True))
            if prev is not None:
                p = jnp.exp2(prev[0][c0:c0 + cs, :] - prev[1])
                l = l + jnp.sum(p, axis=0, keepdims=True)
                acc = acc + _dot(vt_scr[:, c0:c0 + cs], _bf(p))
        if prev is not None:
            outs.append(acc / l)
        prev = (cur[2], m) if cur is not None else None
    return outs, prev


def _transpose_bf16(x):
    return _bf(x.astype(jnp.float32).T)


def _run_streams(streams, vt_scr, chunks, m_scr, nq, write):
    i = pl.program_id(2)
    n = len(streams)
    if nq == 1:
        outs, _ = _pipelined_attention(streams, vt_scr, chunks)
        for u, o_t in enumerate(outs):
            write(0, u, o_t)
        return
    carried = streams[n - 1][2]

    def step(carry, defer):
        outs, last = _pipelined_attention(
            streams, vt_scr, chunks,
            carry_in=(carried, m_scr[...]) if carry else None, defer_last=defer)
        if carry:
            write(-1, n - 1, outs[0])
            outs = outs[1:]
        for u, o_t in enumerate(outs):
            write(0, u, o_t)
        if defer:
            m_scr[...] = last[1]

    pl.when(i == 0)(lambda: step(False, True))
    if nq > 2:
        pl.when((i > 0) & (i < nq - 1))(lambda: step(True, True))
    pl.when(i == nq - 1)(lambda: step(True, False))


def _attn_kernel(*refs, n_qparts, n_src, n_kparts, src_rows, nq):
    it = iter(refs)
    q_refs = [next(it) for _ in range(n_qparts)]
    srcs = []
    for _ in range(n_src):
        k_refs = [next(it) for _ in range(n_kparts)]
        srcs.append((k_refs, next(it)))
    o_ref = next(it)
    k_scr, vt_scr, m_scr = next(it), next(it), next(it)
    s_scrs = list(it)

    @pl.when(pl.program_id(2) == 0)
    def _():
        r = 0
        for (k_refs, v_ref), rows in zip(srcs, src_rows):
            for p, k_ref in enumerate(k_refs):
                k_scr[r:r + rows, p * LANES:(p + 1) * LANES] = k_ref[...]
            vt_scr[:, r:r + rows] = _transpose_bf16(v_ref[...])
            r += rows

    q = jnp.concatenate([qr[...] for qr in q_refs], axis=-1) if n_qparts > 1 else q_refs[0][...]
    sub = s_scrs[0].shape[1]
    tq = q.shape[0]
    streams = [(_transpose_bf16(q[u * sub:(u + 1) * sub]), k_scr, s_scrs[u % 2])
               for u in range(tq // sub)]

    def write(step_offset, u, o_t):
        row = pl.multiple_of((pl.program_id(2) + step_offset) * tq + u * sub, sub)
        o_ref[pl.ds(row, sub), :] = _bf(o_t.T)

    _run_streams(streams, vt_scr, _key_chunks(src_rows, KEY_CHUNK_ATTN), m_scr, nq, write)


def _attn_call(q_parts, sources, n_heads, lq, tq, dv, name):
    tq = min(tq, lq)
    sub = min(TQ_SUB, tq)
    nq = lq // tq
    in_specs, args = [], []
    for arr, cf in q_parts:
        in_specs.append(pl.BlockSpec((tq, LANES), lambda b, h, i, cf=cf: (b * nq + i, cf(h))))
        args.append(arr)
    src_rows = []
    n_kparts = len(sources[0][1])
    for rows, k_parts, (v_arr, vcf) in sources:
        src_rows.append(rows)
        for arr, cf in k_parts:
            in_specs.append(pl.BlockSpec((rows, LANES), lambda b, h, i, cf=cf: (b, cf(h))))
            args.append(arr)
        in_specs.append(pl.BlockSpec((rows, dv), lambda b, h, i, cf=vcf: (b, cf(h))))
        args.append(v_arr)
    nk = sum(src_rows)
    assert nq == 1 or (tq // sub) % 2 == 0
    kern = functools.partial(_attn_kernel, n_qparts=len(q_parts), n_src=len(sources),
                             n_kparts=n_kparts, src_rows=tuple(src_rows), nq=nq)
    return pl.pallas_call(
        kern,
        grid=(BATCH, n_heads, nq),
        in_specs=in_specs,
        out_specs=pl.BlockSpec((lq, dv), lambda b, h, i: (b, h)),
        out_shape=jax.ShapeDtypeStruct((BATCH * lq, n_heads * dv), jnp.bfloat16),
        scratch_shapes=[
            pltpu.VMEM((nk, n_kparts * LANES), jnp.bfloat16),
            pltpu.VMEM((dv, nk), jnp.bfloat16),
            pltpu.VMEM((1, sub), jnp.float32),
        ] + [pltpu.VMEM((nk, sub), jnp.float32)] * 2,
        compiler_params=_params("parallel", "parallel", "arbitrary"),
        name=name,
    )(*args)


def _diff_kernel(*refs, n_src, src_rows, lam_init, nq):
    it = iter(refs)
    q1_ref, q2_ref = next(it), next(it)
    srcs = [(next(it), next(it), next(it)) for _ in range(n_src)]
    lam_ref, sub_ref = next(it), next(it)
    o_ref = next(it)
    k1_scr, k2_scr, vt_scr, m_scr, o1_scr, s1_scr, s2_scr = [next(it) for _ in range(7)]

    @pl.when(pl.program_id(2) == 0)
    def _():
        r = 0
        for (k1_ref, k2_ref, v_ref), rows in zip(srcs, src_rows):
            k1_scr[r:r + rows, :] = k1_ref[...]
            k2_scr[r:r + rows, :] = k2_ref[...]
            vt_scr[:, r:r + rows] = _transpose_bf16(v_ref[...])
            r += rows

    lf = lam_ref[...]
    lam = (jnp.exp(jnp.sum(lf[0:1] * lf[1:2], axis=-1, keepdims=True))
           - jnp.exp(jnp.sum(lf[2:3] * lf[3:4], axis=-1, keepdims=True)) + lam_init)
    chunks = _key_chunks(src_rows, KEY_CHUNK_DIFF)
    sub = s1_scr.shape[1]
    tq = q1_ref.shape[0]
    streams = []
    for u in range(tq // sub):
        rows = slice(u * sub, (u + 1) * sub)
        streams.append((_transpose_bf16(q1_ref[rows, :]), k1_scr, s1_scr))
        streams.append((_transpose_bf16(q2_ref[rows, :]), k2_scr, s2_scr))
    n = len(streams)
    branch1 = {}

    def write(step_offset, u, o_t):
        if u % 2 == 0:
            branch1[u // 2] = o_t
            if nq > 1 and u == n - 2:
                o1_scr[...] = o_t
            return
        o1 = o1_scr[...] if step_offset < 0 else branch1[u // 2]
        o = (o1 - lam * o_t).T
        ms = jnp.mean(o * o, axis=-1, keepdims=True)
        row = pl.multiple_of((pl.program_id(2) + step_offset) * tq + (u // 2) * sub, sub)
        o_ref[pl.ds(row, sub), :] = _bf(
            o * lax.rsqrt(ms + EPS) * sub_ref[...] * (1.0 - lam_init))

    _run_streams(streams, vt_scr, chunks, m_scr, nq, write)


def _diff_call(qkv_q, lq, sources, diff_lambda, subln, lam_init, tq, name):
    tq = min(tq, lq)
    sub = min(TQ_SUB, tq)
    nq = lq // tq
    qb0 = 3 * NA_WIDTH // LANES
    kb0 = (3 * NA_WIDTH + DIFF_WIDTH) // LANES
    vb0 = (3 * NA_WIDTH + 2 * DIFF_WIDTH) // (2 * LANES)
    in_specs = [
        pl.BlockSpec((tq, LANES), lambda b, h, i: (b * nq + i, qb0 + 2 * h)),
        pl.BlockSpec((tq, LANES), lambda b, h, i: (b * nq + i, qb0 + 2 * h + 1)),
    ]
    args = [qkv_q, qkv_q]
    src_rows = []
    for rows, arr in sources:
        src_rows.append(rows)
        in_specs += [
            pl.BlockSpec((rows, LANES), lambda b, h, i: (b, kb0 + 2 * h)),
            pl.BlockSpec((rows, LANES), lambda b, h, i: (b, kb0 + 2 * h + 1)),
            pl.BlockSpec((rows, 2 * LANES), lambda b, h, i: (b, vb0 + h)),
        ]
        args += [arr, arr, arr]
    in_specs += [
        pl.BlockSpec((4, LANES), lambda b, h, i: (0, 0)),
        pl.BlockSpec((1, 2 * LANES), lambda b, h, i: (0, 0)),
    ]
    args += [diff_lambda, subln]
    nk = sum(src_rows)
    kern = functools.partial(_diff_kernel, n_src=len(sources), src_rows=tuple(src_rows),
                             lam_init=lam_init, nq=nq)
    return pl.pallas_call(
        kern,
        grid=(BATCH, DIFF_HEADS, nq),
        in_specs=in_specs,
        out_specs=pl.BlockSpec((lq, 2 * LANES), lambda b, h, i: (b, h)),
        out_shape=jax.ShapeDtypeStruct((BATCH * lq, DIFF_WIDTH), jnp.bfloat16),
        scratch_shapes=[
            pltpu.VMEM((nk, LANES), jnp.bfloat16),
            pltpu.VMEM((nk, LANES), jnp.bfloat16),
            pltpu.VMEM((2 * LANES, nk), jnp.bfloat16),
            pltpu.VMEM((1, sub), jnp.float32),
            pltpu.VMEM((2 * LANES, sub), jnp.float32),
            pltpu.VMEM((nk, sub), jnp.float32),
            pltpu.VMEM((nk, sub), jnp.float32),
        ],
        compiler_params=_params("parallel", "parallel", "arbitrary"),
        name=name,
    )(*args)


def _rope_tables(d):
    h = d // 2
    half = h // 2
    lane = jnp.arange(PAIR, dtype=jnp.int32)[None, :] % LANES
    t = jnp.arange(SEQ, dtype=jnp.int32)[:, None]
    pos = jnp.where(lane < h, t // GRID_W, t % GRID_W).astype(jnp.float32)
    freqs = ROPE_BASE ** (-(lane % half).astype(jnp.float32) / half)
    ang = pos * freqs
    valid = lane < d
    first = (lane % h) < half
    cos = jnp.where(valid, jnp.cos(ang), 1.0)
    sin = jnp.where(valid, jnp.sin(ang), 0.0)
    sin_a = jnp.where(first, -sin, 0.0)
    sin_b = jnp.where(first, 0.0, sin)
    return (cos, sin_a, sin_b), half


def _pair_blockdiag(block):
    z = np.zeros_like(block)
    return jnp.asarray(np.block([[block, z], [z, block]]), dtype=jnp.bfloat16)


def _pad_lanes(v, fill=0.0):
    return jnp.pad(v, (0, LANES - v.shape[0]), constant_values=fill)


def _pair(v):
    return jnp.concatenate([v, v])


def _lat_row(tm):
    bpb = SEQ // tm
    return lambda i: i // bpb


def _ctx_row(i):
    return CTX_MOD_ROW


def _even_layer(x, xc, modarr, norm_mix, norm_ffn, ffn_w, w_in, w_out, qn_a, kn_a, rpb,
                qn_b, kn_b, diff_lambda, subln, lam_init, with_ctx):
    scale = HEAD_DIM ** -0.5 * LOG2E
    gains = jnp.stack([_pair(g) for g in (qn_a * scale, kn_a, qn_b * scale, kn_b)]
                      + [jnp.ones((PAIR,), jnp.float32)] * 4)
    rope = _rope_tables(HEAD_DIM)
    segs = [(True, 0, HEAD_DIM, False), (True, 1, HEAD_DIM, False), (False, 0, HEAD_DIM, False),
            (True, 2, HEAD_DIM, True), (True, 3, HEAD_DIM, True), (False, 0, HEAD_DIM, False)]
    w_in = _bf(w_in)
    w_out = _bf(w_out)
    gain_mix = norm_mix[None, :]
    lat_row = _lat_row(TM)

    qkv = _proj_call(x, w_in, gains, segs, mod=(gain_mix, modarr, 0, 1), rope=rope,
                     mod_row=_lat_row(TM_EVEN), tm=TM_EVEN, name="even_proj")
    qkv_c = _proj_call(xc, w_in, gains, segs, mod=(gain_mix, modarr, 0, 1), rope=None,
                       mod_row=_ctx_row, tm=TM_EVEN, name="even_proj_ctx")

    oa = _na_call(qkv, qkv_c, rpb)
    ob = _diff_call(qkv, SEQ, [(SEQ, qkv), (CTX_LEN, qkv_c)], diff_lambda, subln[None, :],
                    lam_init, TQ_DIFF, "diff_attn")
    x = _outproj_call([oa, ob], w_out, x, modarr, 2, lat_row, name="even_out")
    x = _ffn_call(x, norm_ffn[None, :], modarr, *ffn_w, _lat_row(TM_FF), SEQ)
    if with_ctx:
        oa_c = _attn_call(
            [(qkv_c, lambda h: h)],
            [(CTX_LEN, [(qkv_c, lambda h: NA_HEADS + h)], (qkv_c, lambda h: 2 * NA_HEADS + h))],
            NA_HEADS, CTX_LEN, CTX_LEN, HEAD_DIM, "ctx_attn")
        ob_c = _diff_call(qkv_c, CTX_LEN, [(CTX_LEN, qkv_c)], diff_lambda, subln[None, :],
                          lam_init, CTX_LEN, "diff_attn_ctx")
        xc = _outproj_call([oa_c, ob_c], w_out, xc, modarr, 2, _ctx_row, name="even_out_ctx")
        xc = _ffn_call(xc, norm_ffn[None, :], modarr, *ffn_w, _ctx_row, CTX_LEN,
                       name="conv_ffn_ctx")
    return x, xc


def _odd_layer(x, xc, modarr, norm_mix, norm_ffn, ffn_w, w_down, q_a_norm, kv_a_norm, w_uq,
               w_ukv, qn_nope, qn_rope, kn_nope, kn_rope, w_out, with_ctx):
    scale = (MLA_NOPE + MLA_ROPE) ** -0.5 * LOG2E
    rope_pair = _rope_tables(MLA_ROPE)
    gain_mix = norm_mix[None, :]
    lat_row = _lat_row(TM)

    w_down_p = _bf(jnp.pad(w_down, ((0, 0), (0, MLA_DOWN_PAD - MLA_DOWN))))
    zeros_r = jnp.zeros((MLA_Q_RANK,), jnp.float32)
    kr_gain = jnp.concatenate([_pad_lanes(kn_rope), zeros_r[:MLA_Q_RANK - LANES]])
    gains_d = jnp.stack([q_a_norm, kv_a_norm, kr_gain] + [zeros_r] * 5)
    wq = w_uq.reshape(MLA_Q_RANK, MLA_HEADS, MLA_NOPE + MLA_ROPE)
    wq_rope = jnp.pad(wq[:, :, MLA_NOPE:], ((0, 0), (0, 0), (0, LANES - MLA_ROPE)))
    wq_p = _bf(jnp.concatenate([wq[:, :, :MLA_NOPE].reshape(MLA_Q_RANK, -1),
                                wq_rope.reshape(MLA_Q_RANK, -1)], -1))
    wkv = w_ukv.reshape(MLA_KV_RANK, MLA_HEADS, MLA_NOPE + MLA_V)
    wkv_p = _bf(jnp.concatenate([wkv[:, :, :MLA_NOPE].reshape(MLA_KV_RANK, -1),
                                 wkv[:, :, MLA_NOPE:].reshape(MLA_KV_RANK, -1)], -1))
    gains_p = jnp.stack([_pair(qn_nope * scale), _pair(_pad_lanes(qn_rope) * scale),
                         _pair(kn_nope)] + [jnp.zeros((PAIR,), jnp.float32)] * 5)
    mod_in = (gain_mix, modarr, 0, 1)
    q, kv, kr = _mla_proj_call(x, w_down_p, wq_p, wkv_p, gains_d, gains_p, mod_in, rope_pair,
                               _lat_row(TM_MLA_PROJ), True, "mla_proj")
    kv_c, kr_c = _mla_proj_call(xc, w_down_p, wq_p, wkv_p, gains_d, gains_p, mod_in, None,
                                _ctx_row, False, "mla_proj_ctx")

    def src(rows, kv_arr, kr_arr):
        return (rows, [(kv_arr, lambda h: h), (kr_arr, lambda h: 0)],
                (kv_arr, lambda h: MLA_HEADS + h))

    o = _attn_call([(q, lambda h: h), (q, lambda h: MLA_HEADS + h)],
                   [src(SEQ, kv, kr), src(CTX_LEN, kv_c, kr_c)],
                   MLA_HEADS, SEQ, TQ_MLA, MLA_V, "mla_attn")
    x = _outproj_call([o], _bf(w_out), x, modarr, 2, lat_row, name="mla_out")
    x = _ffn_call(x, norm_ffn[None, :], modarr, *ffn_w, _lat_row(TM_FF), SEQ)
    assert not with_ctx
    return x, xc


def _mla_proj_kernel(x_ref, g_ref, sh_ref, sc_ref, wd_ref, wq_ref, wkv_ref, gd_ref, gp_ref,
                     ones_ref, *rest, rope, rope_half, with_q):
    rope_refs = rest[:3] if rope else None
    outs = rest[3:] if rope else rest
    if with_q:
        q_ref, kv_ref, kr_ref = outs
    else:
        kv_ref, kr_ref = outs
    h = _bf(_modulate(x_ref[...], g_ref[...], sh_ref[0], sc_ref[0]))
    down = _dot(h, wd_ref[...])

    def latent_norm(xs, row, nvalid):
        ms = jnp.sum(xs * xs, axis=-1, keepdims=True) * (1.0 / nvalid)
        return xs * lax.rsqrt(ms + EPS) * gd_ref[row:row + 1, 0:xs.shape[1]]

    kva = _bf(latent_norm(down[:, MLA_Q_RANK:MLA_Q_RANK + MLA_KV_RANK], 1, MLA_KV_RANK))
    kr = latent_norm(down[:, MLA_Q_RANK + MLA_KV_RANK:MLA_DOWN_PAD], 2, MLA_ROPE)
    if rope:
        cos_ref, sa_ref, sb_ref = rope_refs
        kr = (kr * cos_ref[:, 0:LANES]
              + pltpu.roll(kr, LANES - rope_half, axis=1) * sa_ref[:, 0:LANES]
              + pltpu.roll(kr, rope_half, axis=1) * sb_ref[:, 0:LANES])
    kr_ref[...] = _bf(kr)

    _head_projection(kva, wkv_ref, kv_ref,
                     [(True, 2, MLA_NOPE, False), (False, 0, MLA_V, False)],
                     gp_ref, ones_ref, rope_refs, rope_half)
    if with_q:
        qa = _bf(latent_norm(down[:, 0:MLA_Q_RANK], 0, MLA_Q_RANK))
        _head_projection(qa, wq_ref, q_ref,
                         [(True, 0, MLA_NOPE, False), (True, 1, MLA_ROPE, True)],
                         gp_ref, ones_ref, rope_refs, rope_half)


def _mla_proj_call(x, wd, wq, wkv, gains_d, gains_p, mod, rope, mod_row, with_q, name,
                   tm=TM_MLA_PROJ):
    m, k = x.shape
    tm = min(tm, m)
    bpb = SEQ // tm
    gain, modarr, sh_c, sc_c = mod
    ones = _pair_blockdiag(np.ones((LANES, LANES), np.float32))
    in_specs = [
        pl.BlockSpec((tm, k), lambda i: (i, 0)),
        pl.BlockSpec((1, k), lambda i: (0, 0)),
        pl.BlockSpec((1, 1, k), lambda i: (mod_row(i), 0, sh_c)),
        pl.BlockSpec((1, 1, k), lambda i: (mod_row(i), 0, sc_c)),
    ]
    args = [x, gain, modarr, modarr]
    for w in (wd, wq, wkv, gains_d, gains_p, ones):
        in_specs.append(pl.BlockSpec(w.shape, lambda i: (0, 0), pipeline_mode=pl.Buffered(1)))
        args.append(w)
    rope_half = 0
    if rope is not None:
        tables, rope_half = rope
        for t in tables:
            in_specs.append(pl.BlockSpec((tm, PAIR), lambda i: (i % bpb, 0)))
            args.append(t)
    widths = ([wq.shape[1]] if with_q else []) + [wkv.shape[1], LANES]
    return pl.pallas_call(
        functools.partial(_mla_proj_kernel, rope=rope is not None, rope_half=rope_half,
                          with_q=with_q),
        grid=(m // tm,),
        in_specs=in_specs,
        out_specs=[pl.BlockSpec((tm, n), lambda i: (i, 0)) for n in widths],
        out_shape=[jax.ShapeDtypeStruct((m, n), jnp.bfloat16) for n in widths],
        compiler_params=_params("parallel"),
        name=name,
    )(*args)


def _w_in_prep_kernel(x_ref, o_ref):
    pad = jnp.zeros((x_ref.shape[0], D_FF_PAD - D_FF), jnp.bfloat16)
    o_ref[:, 0:D_FF] = _bf(x_ref[:, 0:D_FF])
    o_ref[:, D_FF:D_FF_PAD] = pad
    o_ref[:, D_FF_PAD:D_FF_PAD + D_FF] = _bf(x_ref[:, D_FF:2 * D_FF])
    o_ref[:, D_FF_PAD + D_FF:2 * D_FF_PAD] = pad


def _w_out_prep_kernel(x_ref, o_ref):
    o_ref[0:D_FF, :] = _bf(x_ref[...])
    o_ref[D_FF:D_FF_PAD, :] = jnp.zeros((D_FF_PAD - D_FF, x_ref.shape[1]), jnp.bfloat16)


def _ffn_weights(w_in, conv_w, w_out):
    depth, d, _ = w_in.shape
    rows, cols = 256, 256
    w_in_p = pl.pallas_call(
        _w_in_prep_kernel,
        grid=(depth, d // rows),
        in_specs=[pl.BlockSpec((None, rows, 2 * D_FF), lambda l, i: (l, i, 0))],
        out_specs=pl.BlockSpec((None, rows, 2 * D_FF_PAD), lambda l, i: (l, i, 0)),
        out_shape=jax.ShapeDtypeStruct((depth, d, 2 * D_FF_PAD), jnp.bfloat16),
        compiler_params=_params("parallel", "parallel"),
        name="ffn_w_in_prep",
    )(w_in)
    w_out_p = pl.pallas_call(
        _w_out_prep_kernel,
        grid=(depth, d // cols),
        in_specs=[pl.BlockSpec((None, D_FF, cols), lambda l, j: (l, 0, j))],
        out_specs=pl.BlockSpec((None, D_FF_PAD, cols), lambda l, j: (l, 0, j)),
        out_shape=jax.ShapeDtypeStruct((depth, D_FF_PAD, d), jnp.bfloat16),
        compiler_params=_params("parallel", "parallel"),
        name="ffn_w_out_prep",
    )(w_out)
    pad = D_FF_PAD - D_FF
    zc = jnp.zeros(conv_w.shape[:2] + (pad,), conv_w.dtype)
    conv_p = jnp.concatenate([conv_w[..., :D_FF], zc, conv_w[..., D_FF:], zc], -1)
    return w_in_p, conv_p, w_out_p


def kernel(x, c, ctx, c_ctx, ada_w, ada_b, norm_mix, norm_ffn, ffn_w_in, ffn_conv, ffn_w_out, even_w_in, even_w_out, na_q_norm, na_k_norm, na_rpb, diff_q_norm, diff_k_norm, diff_lambda, diff_subln, mla_w_down, mla_q_a_norm, mla_kv_a_norm, mla_w_uq, mla_w_ukv, mla_q_nope_norm, mla_q_rope_norm, mla_k_nope_norm, mla_k_rope_norm, mla_w_out):
    cond = jnp.concatenate(
        [c, c_ctx[None, :], jnp.zeros((MOD_ROWS - BATCH - 1, D_MODEL), jnp.float32)], 0)
    mod = _ada_call(cond, ada_w, ada_b)
    xl = x.reshape(BATCH * SEQ, D_MODEL)
    xc = ctx.reshape(BATCH * CTX_LEN, D_MODEL)
    ffn_all = _ffn_weights(ffn_w_in, ffn_conv, ffn_w_out)
    for l in range(DEPTH):
        with_ctx = l < DEPTH - 1
        modarr = mod[l].reshape(MOD_ROWS, 1, 6 * D_MODEL)
        ffn_w = ffn_all + (l,)
        i = l // 2
        if l % 2 == 0:
            lam_init = 0.8 - 0.6 * math.exp(-0.3 * l)
            xl, xc = _even_layer(xl, xc, modarr, norm_mix[l], norm_ffn[l], ffn_w, even_w_in[i],
                                 even_w_out[i], na_q_norm[i], na_k_norm[i], na_rpb[i],
                                 diff_q_norm[i], diff_k_norm[i], diff_lambda[i], diff_subln[i],
                                 lam_init, with_ctx)
        else:
            xl, xc = _odd_layer(xl, xc, modarr, norm_mix[l], norm_ffn[l], ffn_w, mla_w_down[i],
                                mla_q_a_norm[i], mla_kv_a_norm[i], mla_w_uq[i], mla_w_ukv[i],
                                mla_q_nope_norm[i], mla_q_rope_norm[i], mla_k_nope_norm[i],
                                mla_k_rope_norm[i], mla_w_out[i], with_ctx)
    return xl.reshape(BATCH, SEQ, D_MODEL)
```

```python
import functools
import math

import numpy as np
import jax
import jax.numpy as jnp
from jax import lax
from jax.experimental import pallas as pl
from jax.experimental.pallas import tpu as pltpu

D_MODEL = 2048
BATCH = 4
SEQ = 4096
DEPTH = 2
GRID_W = 64
GRID_H = SEQ // GRID_W
CTX_LEN = 256
HEAD_DIM = 128
NA_HEADS = 8
NA_WIN_H = 8
NA_WIN_W = 16
DIFF_HEADS = 4
NA_WIDTH = 1024
DIFF_WIDTH = 1024
EVEN_PROJ = 6144
MLA_HEADS = 16
MLA_Q_RANK = 512
MLA_KV_RANK = 512
MLA_NOPE = 128
MLA_ROPE = 64
MLA_V = 128
MLA_DOWN = 1088
D_FF = 5504
ROPE_BASE = 10000.0
EPS = 1e-6

LANES = 128
SUBLANES = 8
PAIR = 2 * LANES
VMEM_LIMIT = 56 * 1024 * 1024
MOD_ROWS = 8
CTX_MOD_ROW = BATCH
NEG = -1e30
LOG2E = math.log2(math.e)

TM = 512
TM_EVEN = 512
TM_MLA_PROJ = 512
SUB_N = 512
TN_ADA = 1024
MLA_DOWN_PAD = 1152
TN_FF = 512
D_FF_PAD = -(-D_FF // TN_FF) * TN_FF
TM_FF = 1024
HM_FF = 512
PREP_ROWS = 256
PREP_COLS = 256
NA_GROUP_ROWS = 4
NA_KEY_ROWS = NA_GROUP_ROWS + NA_WIN_H
NA_UNROLL = 4
NA_Q = NA_GROUP_ROWS * GRID_W
NA_K = NA_KEY_ROWS * GRID_W
TQ_MLA = 1024
TQ_DIFF = 512
TQ_SUB = 256
KEY_CHUNK_ATTN = 256
KEY_CHUNK_DIFF = 512

_NT = (((1,), (1,)), ((), ()))


def _params(*sem):
    return pltpu.CompilerParams(dimension_semantics=sem, vmem_limit_bytes=VMEM_LIMIT)


def _bf(x):
    return x.astype(jnp.bfloat16)


def _dot(a, b):
    return jnp.dot(a, b, preferred_element_type=jnp.float32)


def _dot_nt(a, b):
    return lax.dot_general(a, b, _NT, preferred_element_type=jnp.float32)


def _modulate(x, gain, shift, scale, mxu_stats=False):
    d = x.shape[1]
    if mxu_stats:
        ss = _dot(_bf(x * x), jnp.ones((d, LANES), jnp.bfloat16))
        r = jnp.tile(lax.rsqrt(ss * (1.0 / d) + EPS), (1, d // LANES))
    else:
        r = lax.rsqrt(jnp.mean(x * x, axis=-1, keepdims=True) + EPS)
    return x * r * (gain * (1.0 + scale)) + shift


def _ada_kernel(c_ref, w_ref, b_ref, o_ref):
    c = c_ref[...]
    s = c * (1.0 / (1.0 + jnp.exp(-c)))
    o_ref[0] = _dot(_bf(s), _bf(w_ref[0])) + b_ref[0]


def _ada_call(cond, ada_w, ada_b):
    tn = TN_ADA
    n = ada_w.shape[-1]
    return pl.pallas_call(
        _ada_kernel,
        grid=(DEPTH, n // tn),
        in_specs=[
            pl.BlockSpec((MOD_ROWS, D_MODEL), lambda l, j: (0, 0)),
            pl.BlockSpec((1, D_MODEL, tn), lambda l, j: (l, 0, j)),
            pl.BlockSpec((1, 1, tn), lambda l, j: (l, 0, j)),
        ],
        out_specs=pl.BlockSpec((1, MOD_ROWS, tn), lambda l, j: (l, 0, j)),
        out_shape=jax.ShapeDtypeStruct((DEPTH, MOD_ROWS, n), jnp.float32),
        compiler_params=_params("parallel", "parallel"),
        name="ada_mod",
    )(cond, ada_w, ada_b.reshape(DEPTH, 1, n))


def _pair_epilogue(acc, spec, gains_ref, ones_ref, rope_refs, rope_half):
    norm, gain_row, nvalid, do_rope = spec
    y = acc
    if norm:
        ss = _dot(_bf(acc * acc), ones_ref[...])
        y = acc * lax.rsqrt(ss * (1.0 / nvalid) + EPS) * gains_ref[gain_row:gain_row + 1, :]
    if do_rope and rope_refs is not None:
        cos_ref, sa_ref, sb_ref = rope_refs
        y = (y * cos_ref[...]
             + pltpu.roll(y, PAIR - rope_half, axis=1) * sa_ref[...]
             + pltpu.roll(y, rope_half, axis=1) * sb_ref[...])
    return y


def _head_projection(a, w_ref, o_ref, segs, gains_ref, ones_ref, rope_refs, rope_half):
    width = w_ref.shape[1] // len(segs)
    for c in range(0, w_ref.shape[1], SUB_N):
        acc = _dot(a, w_ref[:, c:c + SUB_N])
        for p in range(0, SUB_N, PAIR):
            y = _pair_epilogue(acc[:, p:p + PAIR], segs[c // width], gains_ref, ones_ref,
                               rope_refs, rope_half)
            o_ref[:, c + p:c + p + PAIR] = _bf(y)


def _proj_kernel(x_ref, g_ref, sh_ref, sc_ref, w_ref, gains_ref, ones_ref, *rest, segs, rope,
                 rope_half):
    rope_refs = rest[:3] if rope else None
    o_ref = rest[-1]
    h = _bf(_modulate(x_ref[...], g_ref[...], sh_ref[0], sc_ref[0], mxu_stats=True))
    _head_projection(h, w_ref, o_ref, segs, gains_ref, ones_ref, rope_refs, rope_half)


def _proj_call(x, w, gains, segs, *, mod, rope=None, mod_row=None, tm=TM, name="proj"):
    m, k = x.shape
    n = w.shape[1]
    tm = min(tm, m)
    assert m % tm == 0 and n % (len(segs) * SUB_N) == 0
    bpb = SEQ // tm
    gain, modarr, sh_c, sc_c = mod
    ones = _pair_blockdiag(np.ones((LANES, LANES), np.float32))
    in_specs = [
        pl.BlockSpec((tm, k), lambda i: (i, 0)),
        pl.BlockSpec((1, k), lambda i: (0, 0)),
        pl.BlockSpec((1, 1, k), lambda i: (mod_row(i), 0, sh_c)),
        pl.BlockSpec((1, 1, k), lambda i: (mod_row(i), 0, sc_c)),
    ]
    args = [x, gain, modarr, modarr]
    for const in (w, gains, ones):
        in_specs.append(pl.BlockSpec(const.shape, lambda i: (0, 0), pipeline_mode=pl.Buffered(1)))
        args.append(const)
    rope_half = 0
    if rope is not None:
        tables, rope_half = rope
        for t in tables:
            in_specs.append(pl.BlockSpec((tm, PAIR), lambda i: (i % bpb, 0)))
            args.append(t)
    kern = functools.partial(_proj_kernel, segs=tuple(segs), rope=rope is not None,
                             rope_half=rope_half)
    return pl.pallas_call(
        kern,
        grid=(m // tm,),
        in_specs=in_specs,
        out_specs=pl.BlockSpec((tm, n), lambda i: (i, 0)),
        out_shape=jax.ShapeDtypeStruct((m, n), jnp.bfloat16),
        compiler_params=_params("parallel"),
        name=name,
    )(*args)


def _outproj_kernel(*refs, n_a):
    a_refs = refs[:n_a]
    w_ref, x_ref, g_ref, o_ref = refs[n_a:]
    n = o_ref.shape[1]
    a_vals = [a_ref[...] for a_ref in a_refs]
    for c in range(0, n, SUB_N):
        acc, row = None, 0
        for a in a_vals:
            part = _dot(a, w_ref[row:row + a.shape[1], c:c + SUB_N])
            acc = part if acc is None else acc + part
            row += a.shape[1]
        o_ref[:, c:c + SUB_N] = x_ref[:, c:c + SUB_N] + g_ref[0][:, c:c + SUB_N] * acc


def _outproj_call(a_list, w, x, modarr, gate_chunk, mod_row, *, tm=TM, name="outproj"):
    m, n = x.shape
    tm = min(tm, m)
    in_specs, args = [], []
    for a in a_list:
        in_specs.append(pl.BlockSpec((tm, a.shape[1]), lambda i: (i, 0)))
        args.append(a)
    in_specs += [
        pl.BlockSpec(w.shape, lambda i: (0, 0)),
        pl.BlockSpec((tm, n), lambda i: (i, 0)),
        pl.BlockSpec((1, 1, n), lambda i: (mod_row(i), 0, gate_chunk)),
    ]
    args += [w, x, modarr]
    return pl.pallas_call(
        functools.partial(_outproj_kernel, n_a=len(a_list)),
        grid=(m // tm,),
        in_specs=in_specs,
        out_specs=pl.BlockSpec((tm, n), lambda i: (i, 0)),
        out_shape=jax.ShapeDtypeStruct((m, n), jnp.float32),
        compiler_params=_params("parallel"),
        name=name,
    )(*args)


def _ffn_kernel(x_ref, xp_ref, xn_ref, g_ref, sh_ref, sc_ref, gt_ref, wa_ref, wb_ref,
                ca_ref, cb_ref, wo_ref, o_ref, h_ref, *, tm, hm, seq_len):
    i = pl.program_id(0)
    j = pl.program_id(1)
    nj = pl.num_programs(1)
    n_grp = tm // hm
    rows = hm + 2 * SUBLANES

    @pl.when(j == 0)
    def _():
        gain, shift, scale = g_ref[...], sh_ref[0], sc_ref[0]
        for r in range(n_grp):
            r0 = r * hm
            h_ref[r * rows:r * rows + hm, :] = _bf(
                _modulate(x_ref[r0:r0 + hm, :], gain, shift, scale, mxu_stats=True))
            nxt_src = x_ref[r0 + hm:r0 + hm + SUBLANES, :] if r + 1 < n_grp else xn_ref[...]
            prv_src = x_ref[r0 - SUBLANES:r0, :] if r > 0 else xp_ref[...]
            keep_next = jnp.where((i * tm + r0 + hm) % seq_len == 0, 0.0, 1.0)
            keep_prev = jnp.where((i * tm + r0) % seq_len == 0, 0.0, 1.0)
            nxt = keep_next * _modulate(nxt_src, gain, shift, scale)
            prv = keep_prev * _modulate(prv_src, gain, shift, scale)
            h_ref[r * rows + hm:(r + 1) * rows, :] = _bf(jnp.concatenate([nxt, prv], axis=0))
        o_ref[...] = jnp.zeros_like(o_ref)

    ca, cb = ca_ref[...], cb_ref[...]
    ups = []
    for r in range(n_grp):
        h = h_ref[r * rows:(r + 1) * rows, :]
        ups.append((_dot(h, wa_ref[...]), _dot(h, wb_ref[...])))

    def conv(u, cw):
        prev = pltpu.roll(u, 1, axis=0)[0:hm]
        nxt = pltpu.roll(u, rows - 1, axis=0)[0:hm]
        return prev * cw[0:1] + u[0:hm] * cw[1:2] + nxt * cw[2:3]

    for r, (ua, ub) in enumerate(ups):
        a = conv(ua, ca)
        b = conv(ub, cb)
        act = a * (1.0 / (1.0 + jnp.exp(-a))) * b
        o_ref[r * hm:(r + 1) * hm, :] += _dot(_bf(act), wo_ref[...])

    @pl.when(j == nj - 1)
    def _():
        o_ref[...] = x_ref[...] + gt_ref[0] * o_ref[...]


def _ffn_call(x, gain, modarr, w_in, conv_w, w_out, layer, mod_row, seq_len, *, tm=TM_FF,
              tn=TN_FF, name="conv_ffn"):
    m, d = x.shape
    tm = min(tm, m)
    hm = min(HM_FF, seq_len)
    assert m % tm == 0 and tm % hm == 0 and seq_len % hm == 0
    nj = D_FF_PAD // tn
    hb = tm // SUBLANES
    last_hb = m // SUBLANES - 1
    kern = functools.partial(_ffn_kernel, tm=tm, hm=hm, seq_len=seq_len)
    return pl.pallas_call(
        kern,
        grid=(m // tm, nj),
        in_specs=[
            pl.BlockSpec((tm, d), lambda i, j: (i, 0)),
            pl.BlockSpec((SUBLANES, d), lambda i, j: (jnp.maximum(i * hb - 1, 0), 0)),
            pl.BlockSpec((SUBLANES, d), lambda i, j: (jnp.minimum((i + 1) * hb, last_hb), 0)),
            pl.BlockSpec((1, d), lambda i, j: (0, 0)),
            pl.BlockSpec((1, 1, d), lambda i, j: (mod_row(i), 0, 3)),
            pl.BlockSpec((1, 1, d), lambda i, j: (mod_row(i), 0, 4)),
            pl.BlockSpec((1, 1, d), lambda i, j: (mod_row(i), 0, 5)),
            pl.BlockSpec((None, d, tn), lambda i, j: (layer, 0, j)),
            pl.BlockSpec((None, d, tn), lambda i, j: (layer, 0, nj + j)),
            pl.BlockSpec((None, 3, tn), lambda i, j: (layer, 0, j)),
            pl.BlockSpec((None, 3, tn), lambda i, j: (layer, 0, nj + j)),
            pl.BlockSpec((None, tn, d), lambda i, j: (layer, j, 0)),
        ],
        out_specs=pl.BlockSpec((tm, d), lambda i, j: (i, 0)),
        out_shape=jax.ShapeDtypeStruct((m, d), jnp.float32),
        scratch_shapes=[
            pltpu.VMEM(((tm // hm) * (hm + 2 * SUBLANES), d), jnp.bfloat16),
        ],
        compiler_params=_params("parallel", "arbitrary"),
        name=name,
    )(x, x, x, gain, modarr, modarr, modarr, w_in, w_in, conv_w, conv_w, w_out)


def _na_group_geometry(g):
    r0 = g * NA_GROUP_ROWS
    return r0, min(max(r0 - NA_WIN_H // 2, 0), GRID_H - NA_KEY_ROWS)


def _na_build_bias(rpb_ref, h, tc_scr, bias_scr):
    n_dr = 2 * NA_WIN_H - 1
    n_dc = 2 * NA_WIN_W - 1
    qc = lax.broadcasted_iota(jnp.int32, (GRID_W, GRID_W), 0)
    kc = lax.broadcasted_iota(jnp.int32, (GRID_W, GRID_W), 1)
    col0 = jnp.clip(qc - NA_WIN_W // 2, 0, GRID_W - NA_WIN_W)
    col_valid = (kc >= col0) & (kc < col0 + NA_WIN_W)
    delta = kc - qc + (NA_WIN_W - 1)
    base = h * (n_dr * n_dc)
    for dr in range(n_dr):
        acc = jnp.zeros((GRID_W, GRID_W), jnp.float32)
        for e in range(n_dc):
            acc = jnp.where(delta == e, rpb_ref[base + dr * n_dc + e] * LOG2E, acc)
        tc_scr[dr] = jnp.where(col_valid, acc, NEG)
    tc_scr[n_dr] = jnp.full((GRID_W, GRID_W), NEG, jnp.float32)
    n_groups = GRID_H // NA_GROUP_ROWS
    for t, g in enumerate((0, 1, n_groups - 1)):
        r0, ks = _na_group_geometry(g)
        for qr in range(NA_GROUP_ROWS):
            r = r0 + qr
            row0 = min(max(r - NA_WIN_H // 2, 0), GRID_H - NA_WIN_H)
            pieces = []
            for kr in range(NA_KEY_ROWS):
                kra = ks + kr
                inside = row0 <= kra < row0 + NA_WIN_H
                pieces.append(tc_scr[kra - r + (NA_WIN_H - 1)] if inside else tc_scr[n_dr])
            bias_scr[t, qr * GRID_W:(qr + 1) * GRID_W, :] = jnp.concatenate(pieces, axis=1)


def _na_kernel(rpb_ref, q_ref, k_ref, v_ref, kc_ref, vc_ref, o_ref, tc_scr, bias_scr):
    @pl.when(pl.program_id(1) == 0)
    def _():
        _na_build_bias(rpb_ref, pl.program_id(0), tc_scr, bias_scr)

    kc = kc_ref[...]
    vc = vc_ref[...]
    n_groups = GRID_H // NA_GROUP_ROWS

    def group(g):
        r0 = g * NA_GROUP_ROWS
        ks = jnp.clip(r0 - NA_WIN_H // 2, 0, GRID_H - NA_KEY_ROWS)
        q0 = pl.multiple_of(g * NA_Q, NA_Q)
        k0 = pl.multiple_of(ks * GRID_W, GRID_W)
        tb = jnp.where(g == 0, 0, jnp.where(g == n_groups - 1, 2, 1))
        q = q_ref[pl.ds(q0, NA_Q), :]
        kw = k_ref[pl.ds(k0, NA_K), :]
        vw = v_ref[pl.ds(k0, NA_K), :]
        sw = _dot_nt(q, kw) + bias_scr[tb]
        sc = _dot_nt(q, kc)
        mx = jnp.maximum(jnp.max(sw, axis=-1, keepdims=True), jnp.max(sc, axis=-1, keepdims=True))
        pw = jnp.exp2(sw - mx)
        pc = jnp.exp2(sc - mx)
        den = jnp.sum(pw, axis=-1, keepdims=True) + jnp.sum(pc, axis=-1, keepdims=True)
        o = _dot(_bf(pw), vw) + _dot(_bf(pc), vc)
        o_ref[pl.ds(q0, NA_Q), :] = _bf(o / den)

    def body(t, carry):
        for u in range(NA_UNROLL):
            group(t * NA_UNROLL + u)
        return carry

    lax.fori_loop(0, n_groups // NA_UNROLL, body, 0)


def _na_call(qkv, qkv_c, rpb):
    n_dr = 2 * NA_WIN_H - 1
    return pl.pallas_call(
        _na_kernel,
        grid=(NA_HEADS, BATCH),
        in_specs=[
            pl.BlockSpec(memory_space=pltpu.SMEM),
            pl.BlockSpec((SEQ, LANES), lambda h, b: (b, h)),
            pl.BlockSpec((SEQ, LANES), lambda h, b: (b, NA_HEADS + h)),
            pl.BlockSpec((SEQ, LANES), lambda h, b: (b, 2 * NA_HEADS + h)),
            pl.BlockSpec((CTX_LEN, LANES), lambda h, b: (b, NA_HEADS + h)),
            pl.BlockSpec((CTX_LEN, LANES), lambda h, b: (b, 2 * NA_HEADS + h)),
        ],
        out_specs=pl.BlockSpec((SEQ, LANES), lambda h, b: (b, h)),
        out_shape=jax.ShapeDtypeStruct((BATCH * SEQ, NA_WIDTH), jnp.bfloat16),
        scratch_shapes=[
            pltpu.VMEM((n_dr + 1, GRID_W, GRID_W), jnp.float32),
            pltpu.VMEM((3, NA_Q, NA_K), jnp.float32),
        ],
        compiler_params=_params("parallel", "arbitrary"),
        name="na_attn",
    )(rpb.reshape(-1), qkv, qkv, qkv, qkv_c, qkv_c)


def _key_chunks(src_rows, chunk):
    chunks, r = [], 0
    for rows in src_rows:
        size = min(chunk, rows)
        assert rows % size == 0
        chunks += [(r + c, size) for c in range(0, rows, size)]
        r += rows
    return chunks


def _pipelined_attention(streams, vt_scr, chunks, carry_in=None, defer_last=False):
    dv = vt_scr.shape[0]
    sub = streams[0][2].shape[1]
    n = len(streams)
    outs, prev = [], carry_in
    for u in range(n if defer_last else n + 1):
        cur = streams[u] if u < n else None
        m = jnp.full((1, sub), NEG, jnp.float32)
        l = jnp.zeros((1, sub), jnp.float32)
        acc = jnp.zeros((dv, sub), jnp.float32)
        for c0, cs in chunks:
            if cur is not None:
                q_t, k_scr, s_ref = cur
                s = _dot(k_scr[c0:c0 + cs, :], q_t)
                s_ref[c0:c0 + cs, :] = s
                m = jnp.maximum(m, jnp.max(s, axis=0, keepdims=True))
            if prev is not None:
                p = jnp.exp2(prev[0][c0:c0 + cs, :] - prev[1])
                l = l + jnp.sum(p, axis=0, keepdims=True)
                acc = acc + _dot(vt_scr[:, c0:c0 + cs], _bf(p))
        if prev is not None:
            outs.append(acc / l)
        prev = (cur[2], m) if cur is not None else None
    return outs, prev


def _transpose_bf16(x):
    return _bf(x.astype(jnp.float32).T)


def _run_streams(streams, vt_scr, chunks, m_scr, nq, write):
    i = pl.program_id(2)
    n = len(streams)
    if nq == 1:
        outs, _ = _pipelined_attention(streams, vt_scr, chunks)
        for u, o_t in enumerate(outs):
            write(0, u, o_t)
        return
    carried = streams[n - 1][2]

    def step(carry, defer):
        outs, last = _pipelined_attention(
            streams, vt_scr, chunks,
            carry_in=(carried, m_scr[...]) if carry else None, defer_last=defer)
        if carry:
            write(-1, n - 1, outs[0])
            outs = outs[1:]
        for u, o_t in enumerate(outs):
            write(0, u, o_t)
        if defer:
            m_scr[...] = last[1]

    pl.when(i == 0)(lambda: step(False, True))
    if nq > 2:
        pl.when((i > 0) & (i < nq - 1))(lambda: step(True, True))
    pl.when(i == nq - 1)(lambda: step(True, False))


def _attn_kernel(*refs, n_qparts, n_src, n_kparts, src_rows, nq):
    it = iter(refs)
    q_refs = [next(it) for _ in range(n_qparts)]
    srcs = []
    for _ in range(n_src):
        k_refs = [next(it) for _ in range(n_kparts)]
        srcs.append((k_refs, next(it)))
    o_ref = next(it)
    k_scr, vt_scr, m_scr = next(it), next(it), next(it)
    s_scrs = list(it)

    @pl.when(pl.program_id(2) == 0)
    def _():
        r = 0
        for (k_refs, v_ref), rows in zip(srcs, src_rows):
            for p, k_ref in enumerate(k_refs):
                k_scr[r:r + rows, p * LANES:(p + 1) * LANES] = k_ref[...]
            vt_scr[:, r:r + rows] = _transpose_bf16(v_ref[...])
            r += rows

    q = jnp.concatenate([qr[...] for qr in q_refs], axis=-1) if n_qparts > 1 else q_refs[0][...]
    sub = s_scrs[0].shape[1]
    tq = q.shape[0]
    streams = [(_transpose_bf16(q[u * sub:(u + 1) * sub]), k_scr, s_scrs[u % 2])
               for u in range(tq // sub)]

    def write(step_offset, u, o_t):
        row = pl.multiple_of((pl.program_id(2) + step_offset) * tq + u * sub, sub)
        o_ref[pl.ds(row, sub), :] = _bf(o_t.T)

    _run_streams(streams, vt_scr, _key_chunks(src_rows, KEY_CHUNK_ATTN), m_scr, nq, write)


def _attn_call(q_parts, sources, n_heads, lq, tq, dv, name):
    tq = min(tq, lq)
    sub = min(TQ_SUB, tq)
    nq = lq // tq
    in_specs, args = [], []
    for arr, cf in q_parts:
        in_specs.append(pl.BlockSpec((tq, LANES), lambda b, h, i, cf=cf: (b * nq + i, cf(h))))
        args.append(arr)
    src_rows = []
    n_kparts = len(sources[0][1])
    for rows, k_parts, (v_arr, vcf) in sources:
        src_rows.append(rows)
        for arr, cf in k_parts:
            in_specs.append(pl.BlockSpec((rows, LANES), lambda b, h, i, cf=cf: (b, cf(h))))
            args.append(arr)
        in_specs.append(pl.BlockSpec((rows, dv), lambda b, h, i, cf=vcf: (b, cf(h))))
        args.append(v_arr)
    nk = sum(src_rows)
    assert nq == 1 or (tq // sub) % 2 == 0
    kern = functools.partial(_attn_kernel, n_qparts=len(q_parts), n_src=len(sources),
                             n_kparts=n_kparts, src_rows=tuple(src_rows), nq=nq)
    return pl.pallas_call(
        kern,
        grid=(BATCH, n_heads, nq),
        in_specs=in_specs,
        out_specs=pl.BlockSpec((lq, dv), lambda b, h, i: (b, h)),
        out_shape=jax.ShapeDtypeStruct((BATCH * lq, n_heads * dv), jnp.bfloat16),
        scratch_shapes=[
            pltpu.VMEM((nk, n_kparts * LANES), jnp.bfloat16),
            pltpu.VMEM((dv, nk), jnp.bfloat16),
            pltpu.VMEM((1, sub), jnp.float32),
        ] + [pltpu.VMEM((nk, sub), jnp.float32)] * 2,
        compiler_params=_params("parallel", "parallel", "arbitrary"),
        name=name,
    )(*args)


def _diff_kernel(*refs, n_src, src_rows, lam_init, nq):
    it = iter(refs)
    q1_ref, q2_ref = next(it), next(it)
    srcs = [(next(it), next(it), next(it)) for _ in range(n_src)]
    lam_ref, sub_ref = next(it), next(it)
    o_ref = next(it)
    k1_scr, k2_scr, vt_scr, m_scr, o1_scr, s1_scr, s2_scr = [next(it) for _ in range(7)]

    @pl.when(pl.program_id(2) == 0)
    def _():
        r = 0
        for (k1_ref, k2_ref, v_ref), rows in zip(srcs, src_rows):
            k1_scr[r:r + rows, :] = k1_ref[...]
            k2_scr[r:r + rows, :] = k2_ref[...]
            vt_scr[:, r:r + rows] = _transpose_bf16(v_ref[...])
            r += rows

    lf = lam_ref[...]
    lam = (jnp.exp(jnp.sum(lf[0:1] * lf[1:2], axis=-1, keepdims=True))
           - jnp.exp(jnp.sum(lf[2:3] * lf[3:4], axis=-1, keepdims=True)) + lam_init)
    chunks = _key_chunks(src_rows, KEY_CHUNK_DIFF)
    sub = s1_scr.shape[1]
    tq = q1_ref.shape[0]
    streams = []
    for u in range(tq // sub):
        rows = slice(u * sub, (u + 1) * sub)
        streams.append((_transpose_bf16(q1_ref[rows, :]), k1_scr, s1_scr))
        streams.append((_transpose_bf16(q2_ref[rows, :]), k2_scr, s2_scr))
    n = len(streams)
    branch1 = {}

    def write(step_offset, u, o_t):
        if u % 2 == 0:
            branch1[u // 2] = o_t
            if nq > 1 and u == n - 2:
                o1_scr[...] = o_t
            return
        o1 = o1_scr[...] if step_offset < 0 else branch1[u // 2]
        o = (o1 - lam * o_t).T
        ms = jnp.mean(o * o, axis=-1, keepdims=True)
        row = pl.multiple_of((pl.program_id(2) + step_offset) * tq + (u // 2) * sub, sub)
        o_ref[pl.ds(row, sub), :] = _bf(
            o * lax.rsqrt(ms + EPS) * sub_ref[...] * (1.0 - lam_init))

    _run_streams(streams, vt_scr, chunks, m_scr, nq, write)


def _diff_call(qkv_q, lq, sources, diff_lambda, subln, lam_init, tq, name):
    tq = min(tq, lq)
    sub = min(TQ_SUB, tq)
    nq = lq // tq
    qb0 = 3 * NA_WIDTH // LANES
    kb0 = (3 * NA_WIDTH + DIFF_WIDTH) // LANES
    vb0 = (3 * NA_WIDTH + 2 * DIFF_WIDTH) // (2 * LANES)
    in_specs = [
        pl.BlockSpec((tq, LANES), lambda b, h, i: (b * nq + i, qb0 + 2 * h)),
        pl.BlockSpec((tq, LANES), lambda b, h, i: (b * nq + i, qb0 + 2 * h + 1)),
    ]
    args = [qkv_q, qkv_q]
    src_rows = []
    for rows, arr in sources:
        src_rows.append(rows)
        in_specs += [
            pl.BlockSpec((rows, LANES), lambda b, h, i: (b, kb0 + 2 * h)),
            pl.BlockSpec((rows, LANES), lambda b, h, i: (b, kb0 + 2 * h + 1)),
            pl.BlockSpec((rows, 2 * LANES), lambda b, h, i: (b, vb0 + h)),
        ]
        args += [arr, arr, arr]
    in_specs += [
        pl.BlockSpec((4, LANES), lambda b, h, i: (0, 0)),
        pl.BlockSpec((1, 2 * LANES), lambda b, h, i: (0, 0)),
    ]
    args += [diff_lambda, subln]
    nk = sum(src_rows)
    kern = functools.partial(_diff_kernel, n_src=len(sources), src_rows=tuple(src_rows),
                             lam_init=lam_init, nq=nq)
    return pl.pallas_call(
        kern,
        grid=(BATCH, DIFF_HEADS, nq),
        in_specs=in_specs,
        out_specs=pl.BlockSpec((lq, 2 * LANES), lambda b, h, i: (b, h)),
        out_shape=jax.ShapeDtypeStruct((BATCH * lq, DIFF_WIDTH), jnp.bfloat16),
        scratch_shapes=[
            pltpu.VMEM((nk, LANES), jnp.bfloat16),
            pltpu.VMEM((nk, LANES), jnp.bfloat16),
            pltpu.VMEM((2 * LANES, nk), jnp.bfloat16),
            pltpu.VMEM((1, sub), jnp.float32),
            pltpu.VMEM((2 * LANES, sub), jnp.float32),
            pltpu.VMEM((nk, sub), jnp.float32),
            pltpu.VMEM((nk, sub), jnp.float32),
        ],
        compiler_params=_params("parallel", "parallel", "arbitrary"),
        name=name,
    )(*args)


def _rope_tables(d):
    h = d // 2
    half = h // 2
    lane = jnp.arange(PAIR, dtype=jnp.int32)[None, :] % LANES
    t = jnp.arange(SEQ, dtype=jnp.int32)[:, None]
    pos = jnp.where(lane < h, t // GRID_W, t % GRID_W).astype(jnp.float32)
    freqs = ROPE_BASE ** (-(lane % half).astype(jnp.float32) / half)
    ang = pos * freqs
    valid = lane < d
    first = (lane % h) < half
    cos = jnp.where(valid, jnp.cos(ang), 1.0)
    sin = jnp.where(valid, jnp.sin(ang), 0.0)
    sin_a = jnp.where(first, -sin, 0.0)
    sin_b = jnp.where(first, 0.0, sin)
    return (cos, sin_a, sin_b), half


def _pair_blockdiag(block):
    z = np.zeros_like(block)
    return jnp.asarray(np.block([[block, z], [z, block]]), dtype=jnp.bfloat16)


def _pad_lanes(v, fill=0.0):
    return jnp.pad(v, (0, LANES - v.shape[0]), constant_values=fill)


def _pair(v):
    return jnp.concatenate([v, v])


def _lat_row(tm):
    bpb = SEQ // tm
    return lambda i: i // bpb


def _ctx_row(i):
    return CTX_MOD_ROW


def _even_layer(x, xc, modarr, norm_mix, norm_ffn, ffn_w, w_in, w_out, qn_a, kn_a, rpb,
                qn_b, kn_b, diff_lambda, subln, lam_init, with_ctx):
    scale = HEAD_DIM ** -0.5 * LOG2E
    gains = jnp.stack([_pair(g) for g in (qn_a * scale, kn_a, qn_b * scale, kn_b)]
                      + [jnp.ones((PAIR,), jnp.float32)] * 4)
    rope = _rope_tables(HEAD_DIM)
    segs = [(True, 0, HEAD_DIM, False), (True, 1, HEAD_DIM, False), (False, 0, HEAD_DIM, False),
            (True, 2, HEAD_DIM, True), (True, 3, HEAD_DIM, True), (False, 0, HEAD_DIM, False)]
    w_in = _bf(w_in)
    w_out = _bf(w_out)
    gain_mix = norm_mix[None, :]
    lat_row = _lat_row(TM)

    qkv = _proj_call(x, w_in, gains, segs, mod=(gain_mix, modarr, 0, 1), rope=rope,
                     mod_row=_lat_row(TM_EVEN), tm=TM_EVEN, name="even_proj")
    qkv_c = _proj_call(xc, w_in, gains, segs, mod=(gain_mix, modarr, 0, 1), rope=None,
                       mod_row=_ctx_row, tm=TM_EVEN, name="even_proj_ctx")

    oa = _na_call(qkv, qkv_c, rpb)
    ob = _diff_call(qkv, SEQ, [(SEQ, qkv), (CTX_LEN, qkv_c)], diff_lambda, subln[None, :],
                    lam_init, TQ_DIFF, "diff_attn")
    x = _outproj_call([oa, ob], w_out, x, modarr, 2, lat_row, name="even_out")
    x = _ffn_call(x, norm_ffn[None, :], modarr, *ffn_w, _lat_row(TM_FF), SEQ)
    if with_ctx:
        oa_c = _attn_call(
            [(qkv_c, lambda h: h)],
            [(CTX_LEN, [(qkv_c, lambda h: NA_HEADS + h)], (qkv_c, lambda h: 2 * NA_HEADS + h))],
            NA_HEADS, CTX_LEN, CTX_LEN, HEAD_DIM, "ctx_attn")
        ob_c = _diff_call(qkv_c, CTX_LEN, [(CTX_LEN, qkv_c)], diff_lambda, subln[None, :],
                          lam_init, CTX_LEN, "diff_attn_ctx")
        xc = _outproj_call([oa_c, ob_c], w_out, xc, modarr, 2, _ctx_row, name="even_out_ctx")
        xc = _ffn_call(xc, norm_ffn[None, :], modarr, *ffn_w, _ctx_row, CTX_LEN,
                       name="conv_ffn_ctx")
    return x, xc


def _odd_layer(x, xc, modarr, norm_mix, norm_ffn, ffn_w, w_down, q_a_norm, kv_a_norm, w_uq,
               w_ukv, qn_nope, qn_rope, kn_nope, kn_rope, w_out, with_ctx):
    scale = (MLA_NOPE + MLA_ROPE) ** -0.5 * LOG2E
    rope_pair = _rope_tables(MLA_ROPE)
    gain_mix = norm_mix[None, :]
    lat_row = _lat_row(TM)

    w_down_p = _bf(jnp.pad(w_down, ((0, 0), (0, MLA_DOWN_PAD - MLA_DOWN))))
    zeros_r = jnp.zeros((MLA_Q_RANK,), jnp.float32)
    kr_gain = jnp.concatenate([_pad_lanes(kn_rope), zeros_r[:MLA_Q_RANK - LANES]])
    gains_d = jnp.stack([q_a_norm, kv_a_norm, kr_gain] + [zeros_r] * 5)
    wq = w_uq.reshape(MLA_Q_RANK, MLA_HEADS, MLA_NOPE + MLA_ROPE)
    wq_rope = jnp.pad(wq[:, :, MLA_NOPE:], ((0, 0), (0, 0), (0, LANES - MLA_ROPE)))
    wq_p = _bf(jnp.concatenate([wq[:, :, :MLA_NOPE].reshape(MLA_Q_RANK, -1),
                                wq_rope.reshape(MLA_Q_RANK, -1)], -1))
    wkv = w_ukv.reshape(MLA_KV_RANK, MLA_HEADS, MLA_NOPE + MLA_V)
    wkv_p = _bf(jnp.concatenate([wkv[:, :, :MLA_NOPE].reshape(MLA_KV_RANK, -1),
                                 wkv[:, :, MLA_NOPE:].reshape(MLA_KV_RANK, -1)], -1))
    gains_p = jnp.stack([_pair(qn_nope * scale), _pair(_pad_lanes(qn_rope) * scale),
                         _pair(kn_nope)] + [jnp.zeros((PAIR,), jnp.float32)] * 5)
    mod_in = (gain_mix, modarr, 0, 1)
    q, kv, kr = _mla_proj_call(x, w_down_p, wq_p, wkv_p, gains_d, gains_p, mod_in, rope_pair,
                               _lat_row(TM_MLA_PROJ), True, "mla_proj")
    kv_c, kr_c = _mla_proj_call(xc, w_down_p, wq_p, wkv_p, gains_d, gains_p, mod_in, None,
                                _ctx_row, False, "mla_proj_ctx")

    def src(rows, kv_arr, kr_arr):
        return (rows, [(kv_arr, lambda h: h), (kr_arr, lambda h: 0)],
                (kv_arr, lambda h: MLA_HEADS + h))

    o = _attn_call([(q, lambda h: h), (q, lambda h: MLA_HEADS + h)],
                   [src(SEQ, kv, kr), src(CTX_LEN, kv_c, kr_c)],
                   MLA_HEADS, SEQ, TQ_MLA, MLA_V, "mla_attn")
    x = _outproj_call([o], _bf(w_out), x, modarr, 2, lat_row, name="mla_out")
    x = _ffn_call(x, norm_ffn[None, :], modarr, *ffn_w, _lat_row(TM_FF), SEQ)
    assert not with_ctx
    return x, xc


def _mla_proj_kernel(x_ref, g_ref, sh_ref, sc_ref, wd_ref, wq_ref, wkv_ref, gd_ref, gp_ref,
                     ones_ref, *rest, rope, rope_half, with_q):
    rope_refs = rest[:3] if rope else None
    outs = rest[3:] if rope else rest
    if with_q:
        q_ref, kv_ref, kr_ref = outs
    else:
        kv_ref, kr_ref = outs
    h = _bf(_modulate(x_ref[...], g_ref[...], sh_ref[0], sc_ref[0]))
    down = _dot(h, wd_ref[...])

    def latent_norm(xs, row, nvalid):
        ms = jnp.sum(xs * xs, axis=-1, keepdims=True) * (1.0 / nvalid)
        return xs * lax.rsqrt(ms + EPS) * gd_ref[row:row + 1, 0:xs.shape[1]]

    kva = _bf(latent_norm(down[:, MLA_Q_RANK:MLA_Q_RANK + MLA_KV_RANK], 1, MLA_KV_RANK))
    kr = latent_norm(down[:, MLA_Q_RANK + MLA_KV_RANK:MLA_DOWN_PAD], 2, MLA_ROPE)
    if rope:
        cos_ref, sa_ref, sb_ref = rope_refs
        kr = (kr * cos_ref[:, 0:LANES]
              + pltpu.roll(kr, LANES - rope_half, axis=1) * sa_ref[:, 0:LANES]
              + pltpu.roll(kr, rope_half, axis=1) * sb_ref[:, 0:LANES])
    kr_ref[...] = _bf(kr)

    _head_projection(kva, wkv_ref, kv_ref,
                     [(True, 2, MLA_NOPE, False), (False, 0, MLA_V, False)],
                     gp_ref, ones_ref, rope_refs, rope_half)
    if with_q:
        qa = _bf(latent_norm(down[:, 0:MLA_Q_RANK], 0, MLA_Q_RANK))
        _head_projection(qa, wq_ref, q_ref,
                         [(True, 0, MLA_NOPE, False), (True, 1, MLA_ROPE, True)],
                         gp_ref, ones_ref, rope_refs, rope_half)


def _mla_proj_call(x, wd, wq, wkv, gains_d, gains_p, mod, rope, mod_row, with_q, name,
                   tm=TM_MLA_PROJ):
    m, k = x.shape
    tm = min(tm, m)
    bpb = SEQ // tm
    gain, modarr, sh_c, sc_c = mod
    ones = _pair_blockdiag(np.ones((LANES, LANES), np.float32))
    in_specs = [
        pl.BlockSpec((tm, k), lambda i: (i, 0)),
        pl.BlockSpec((1, k), lambda i: (0, 0)),
        pl.BlockSpec((1, 1, k), lambda i: (mod_row(i), 0, sh_c)),
        pl.BlockSpec((1, 1, k), lambda i: (mod_row(i), 0, sc_c)),
    ]
    args = [x, gain, modarr, modarr]
    for w in (wd, wq, wkv, gains_d, gains_p, ones):
        in_specs.append(pl.BlockSpec(w.shape, lambda i: (0, 0), pipeline_mode=pl.Buffered(1)))
        args.append(w)
    rope_half = 0
    if rope is not None:
        tables, rope_half = rope
        for t in tables:
            in_specs.append(pl.BlockSpec((tm, PAIR), lambda i: (i % bpb, 0)))
            args.append(t)
    widths = ([wq.shape[1]] if with_q else []) + [wkv.shape[1], LANES]
    return pl.pallas_call(
        functools.partial(_mla_proj_kernel, rope=rope is not None, rope_half=rope_half,
                          with_q=with_q),
        grid=(m // tm,),
        in_specs=in_specs,
        out_specs=[pl.BlockSpec((tm, n), lambda i: (i, 0)) for n in widths],
        out_shape=[jax.ShapeDtypeStruct((m, n), jnp.bfloat16) for n in widths],
        compiler_params=_params("parallel"),
        name=name,
    )(*args)


def _w_in_prep_kernel(x_ref, o_ref):
    pad = jnp.zeros((x_ref.shape[0], D_FF_PAD - D_FF), jnp.bfloat16)
    o_ref[:, 0:D_FF] = _bf(x_ref[:, 0:D_FF])
    o_ref[:, D_FF:D_FF_PAD] = pad
    o_ref[:, D_FF_PAD:D_FF_PAD + D_FF] = _bf(x_ref[:, D_FF:2 * D_FF])
    o_ref[:, D_FF_PAD + D_FF:2 * D_FF_PAD] = pad


def _w_out_prep_kernel(x_ref, o_ref):
    o_ref[0:D_FF, :] = _bf(x_ref[...])
    o_ref[D_FF:D_FF_PAD, :] = jnp.zeros((D_FF_PAD - D_FF, x_ref.shape[1]), jnp.bfloat16)


def _ffn_weights(w_in, conv_w, w_out):
    depth, d, _ = w_in.shape
    rows, cols = PREP_ROWS, PREP_COLS
    w_in_p = pl.pallas_call(
        _w_in_prep_kernel,
        grid=(depth, d // rows),
        in_specs=[pl.BlockSpec((None, rows, 2 * D_FF), lambda l, i: (l, i, 0))],
        out_specs=pl.BlockSpec((None, rows, 2 * D_FF_PAD), lambda l, i: (l, i, 0)),
        out_shape=jax.ShapeDtypeStruct((depth, d, 2 * D_FF_PAD), jnp.bfloat16),
        compiler_params=_params("parallel", "parallel"),
        name="ffn_w_in_prep",
    )(w_in)
    w_out_p = pl.pallas_call(
        _w_out_prep_kernel,
        grid=(depth, d // cols),
        in_specs=[pl.BlockSpec((None, D_FF, cols), lambda l, j: (l, 0, j))],
        out_specs=pl.BlockSpec((None, D_FF_PAD, cols), lambda l, j: (l, 0, j)),
        out_shape=jax.ShapeDtypeStruct((depth, D_FF_PAD, d), jnp.bfloat16),
        compiler_params=_params("parallel", "parallel"),
        name="ffn_w_out_prep",
    )(w_out)
    pad = D_FF_PAD - D_FF
    zc = jnp.zeros(conv_w.shape[:2] + (pad,), conv_w.dtype)
    conv_p = jnp.concatenate([conv_w[..., :D_FF], zc, conv_w[..., D_FF:], zc], -1)
    return w_in_p, conv_p, w_out_p


def kernel(x, c, ctx, c_ctx, ada_w, ada_b, norm_mix, norm_ffn, ffn_w_in, ffn_conv, ffn_w_out, even_w_in, even_w_out, na_q_norm, na_k_norm, na_rpb, diff_q_norm, diff_k_norm, diff_lambda, diff_subln, mla_w_down, mla_q_a_norm, mla_kv_a_norm, mla_w_uq, mla_w_ukv, mla_q_nope_norm, mla_q_rope_norm, mla_k_nope_norm, mla_k_rope_norm, mla_w_out):
    cond = jnp.concatenate(
        [c, c_ctx[None, :], jnp.zeros((MOD_ROWS - BATCH - 1, D_MODEL), jnp.float32)], 0)
    mod = _ada_call(cond, ada_w, ada_b)
    xl = x.reshape(BATCH * SEQ, D_MODEL)
    xc = ctx.reshape(BATCH * CTX_LEN, D_MODEL)
    ffn_all = _ffn_weights(ffn_w_in, ffn_conv, ffn_w_out)
    for l in range(DEPTH):
        with_ctx = l < DEPTH - 1
        modarr = mod[l].reshape(MOD_ROWS, 1, 6 * D_MODEL)
        ffn_w = ffn_all + (l,)
        i = l // 2
        if l % 2 == 0:
            lam_init = 0.8 - 0.6 * math.exp(-0.3 * l)
            xl, xc = _even_layer(xl, xc, modarr, norm_mix[l], norm_ffn[l], ffn_w, even_w_in[i],
                                 even_w_out[i], na_q_norm[i], na_k_norm[i], na_rpb[i],
                                 diff_q_norm[i], diff_k_norm[i], diff_lambda[i], diff_subln[i],
                                 lam_init, with_ctx)
        else:
            xl, xc = _odd_layer(xl, xc, modarr, norm_mix[l], norm_ffn[l], ffn_w, mla_w_down[i],
                                mla_q_a_norm[i], mla_kv_a_norm[i], mla_w_uq[i], mla_w_ukv[i],
                                mla_q_nope_norm[i], mla_q_rope_norm[i], mla_k_nope_norm[i],
                                mla_k_rope_norm[i], mla_w_out[i], with_ctx)
    return xl.reshape(BATCH, SEQ, D_MODEL)
```

```python
import functools
import math

import numpy as np
import jax
import jax.numpy as jnp
from jax import lax
from jax.experimental import pallas as pl
from jax.experimental.pallas import tpu as pltpu

D_MODEL = 2048
BATCH = 4
SEQ = 4096
DEPTH = 2
GRID_W = 64
GRID_H = SEQ // GRID_W
CTX_LEN = 256
HEAD_DIM = 128
NA_HEADS = 8
NA_WIN_H = 8
NA_WIN_W = 16
DIFF_HEADS = 4
NA_WIDTH = 1024
DIFF_WIDTH = 1024
EVEN_PROJ = 6144
MLA_HEADS = 16
MLA_Q_RANK = 512
MLA_KV_RANK = 512
MLA_NOPE = 128
MLA_ROPE = 64
MLA_V = 128
MLA_DOWN = 1088
D_FF = 5504
ROPE_BASE = 10000.0
EPS = 1e-6

LANES = 128
SUBLANES = 8
PAIR = 2 * LANES
VMEM_LIMIT = 56 * 1024 * 1024
MOD_ROWS = 8
CTX_MOD_ROW = BATCH
NEG = -1e30
LOG2E = math.log2(math.e)

TM = 512
TM_EVEN = 512
TM_MLA_PROJ = 512
SUB_N = 512
TN_ADA = 1024
MLA_DOWN_PAD = 1152
TN_FF = 512
D_FF_PAD = -(-D_FF // TN_FF) * TN_FF
TM_FF = 1024
HM_FF = 512
PREP_ROWS = 256
PREP_COLS = 256
NA_GROUP_ROWS = 4
NA_KEY_ROWS = NA_GROUP_ROWS + NA_WIN_H
NA_UNROLL = 4
NA_Q = NA_GROUP_ROWS * GRID_W
NA_K = NA_KEY_ROWS * GRID_W
TQ_MLA = 1024
TQ_DIFF = 512
TQ_SUB = 256
KEY_CHUNK_ATTN = 512
KEY_CHUNK_DIFF = 512

_NT = (((1,), (1,)), ((), ()))


def _params(*sem):
    return pltpu.CompilerParams(dimension_semantics=sem, vmem_limit_bytes=VMEM_LIMIT)


def _bf(x):
    return x.astype(jnp.bfloat16)


def _dot(a, b):
    return jnp.dot(a, b, preferred_element_type=jnp.float32)


def _dot_nt(a, b):
    return lax.dot_general(a, b, _NT, preferred_element_type=jnp.float32)


def _modulate(x, gain, shift, scale, mxu_stats=False):
    d = x.shape[1]
    if mxu_stats:
        ss = _dot(_bf(x * x), jnp.ones((d, LANES), jnp.bfloat16))
        r = jnp.tile(lax.rsqrt(ss * (1.0 / d) + EPS), (1, d // LANES))
    else:
        r = lax.rsqrt(jnp.mean(x * x, axis=-1, keepdims=True) + EPS)
    return x * r * (gain * (1.0 + scale)) + shift


def _ada_kernel(c_ref, w_ref, b_ref, o_ref):
    c = c_ref[...]
    s = c * (1.0 / (1.0 + jnp.exp(-c)))
    o_ref[0] = _dot(_bf(s), _bf(w_ref[0])) + b_ref[0]


def _ada_call(cond, ada_w, ada_b):
    tn = TN_ADA
    n = ada_w.shape[-1]
    return pl.pallas_call(
        _ada_kernel,
        grid=(DEPTH, n // tn),
        in_specs=[
            pl.BlockSpec((MOD_ROWS, D_MODEL), lambda l, j: (0, 0)),
            pl.BlockSpec((1, D_MODEL, tn), lambda l, j: (l, 0, j)),
            pl.BlockSpec((1, 1, tn), lambda l, j: (l, 0, j)),
        ],
        out_specs=pl.BlockSpec((1, MOD_ROWS, tn), lambda l, j: (l, 0, j)),
        out_shape=jax.ShapeDtypeStruct((DEPTH, MOD_ROWS, n), jnp.float32),
        compiler_params=_params("parallel", "parallel"),
        name="ada_mod",
    )(cond, ada_w, ada_b.reshape(DEPTH, 1, n))


def _pair_epilogue(acc, spec, gains_ref, ones_ref, rope_refs, rope_half):
    norm, gain_row, nvalid, do_rope = spec
    y = acc
    if norm:
        ss = _dot(_bf(acc * acc), ones_ref[...])
        y = acc * lax.rsqrt(ss * (1.0 / nvalid) + EPS) * gains_ref[gain_row:gain_row + 1, :]
    if do_rope and rope_refs is not None:
        cos_ref, sa_ref, sb_ref = rope_refs
        y = (y * cos_ref[...]
             + pltpu.roll(y, PAIR - rope_half, axis=1) * sa_ref[...]
             + pltpu.roll(y, rope_half, axis=1) * sb_ref[...])
    return y


def _head_projection(a, w_ref, o_ref, segs, gains_ref, ones_ref, rope_refs, rope_half):
    width = w_ref.shape[1] // len(segs)
    for c in range(0, w_ref.shape[1], SUB_N):
        acc = _dot(a, w_ref[:, c:c + SUB_N])
        for p in range(0, SUB_N, PAIR):
            y = _pair_epilogue(acc[:, p:p + PAIR], segs[c // width], gains_ref, ones_ref,
                               rope_refs, rope_half)
            o_ref[:, c + p:c + p + PAIR] = _bf(y)


def _proj_kernel(x_ref, g_ref, sh_ref, sc_ref, w_ref, gains_ref, ones_ref, *rest, segs, rope,
                 rope_half):
    rope_refs = rest[:3] if rope else None
    o_ref = rest[-1]
    h = _bf(_modulate(x_ref[...], g_ref[...], sh_ref[0], sc_ref[0], mxu_stats=True))
    _head_projection(h, w_ref, o_ref, segs, gains_ref, ones_ref, rope_refs, rope_half)


def _proj_call(x, w, gains, segs, *, mod, rope=None, mod_row=None, tm=TM, name="proj"):
    m, k = x.shape
    n = w.shape[1]
    tm = min(tm, m)
    assert m % tm == 0 and n % (len(segs) * SUB_N) == 0
    bpb = SEQ // tm
    gain, modarr, sh_c, sc_c = mod
    ones = _pair_blockdiag(np.ones((LANES, LANES), np.float32))
    in_specs = [
        pl.BlockSpec((tm, k), lambda i: (i, 0)),
        pl.BlockSpec((1, k), lambda i: (0, 0)),
        pl.BlockSpec((1, 1, k), lambda i: (mod_row(i), 0, sh_c)),
        pl.BlockSpec((1, 1, k), lambda i: (mod_row(i), 0, sc_c)),
    ]
    args = [x, gain, modarr, modarr]
    for const in (w, gains, ones):
        in_specs.append(pl.BlockSpec(const.shape, lambda i: (0, 0), pipeline_mode=pl.Buffered(1)))
        args.append(const)
    rope_half = 0
    if rope is not None:
        tables, rope_half = rope
        for t in tables:
            in_specs.append(pl.BlockSpec((tm, PAIR), lambda i: (i % bpb, 0)))
            args.append(t)
    kern = functools.partial(_proj_kernel, segs=tuple(segs), rope=rope is not None,
                             rope_half=rope_half)
    return pl.pallas_call(
        kern,
        grid=(m // tm,),
        in_specs=in_specs,
        out_specs=pl.BlockSpec((tm, n), lambda i: (i, 0)),
        out_shape=jax.ShapeDtypeStruct((m, n), jnp.bfloat16),
        compiler_params=_params("parallel"),
        name=name,
    )(*args)


def _outproj_kernel(*refs, n_a):
    a_refs = refs[:n_a]
    w_ref, x_ref, g_ref, o_ref = refs[n_a:]
    n = o_ref.shape[1]
    a_vals = [a_ref[...] for a_ref in a_refs]
    for c in range(0, n, SUB_N):
        acc, row = None, 0
        for a in a_vals:
            part = _dot(a, w_ref[row:row + a.shape[1], c:c + SUB_N])
            acc = part if acc is None else acc + part
            row += a.shape[1]
        o_ref[:, c:c + SUB_N] = x_ref[:, c:c + SUB_N] + g_ref[0][:, c:c + SUB_N] * acc


def _outproj_call(a_list, w, x, modarr, gate_chunk, mod_row, *, tm=TM, name="outproj"):
    m, n = x.shape
    tm = min(tm, m)
    in_specs, args = [], []
    for a in a_list:
        in_specs.append(pl.BlockSpec((tm, a.shape[1]), lambda i: (i, 0)))
        args.append(a)
    in_specs += [
        pl.BlockSpec(w.shape, lambda i: (0, 0)),
        pl.BlockSpec((tm, n), lambda i: (i, 0)),
        pl.BlockSpec((1, 1, n), lambda i: (mod_row(i), 0, gate_chunk)),
    ]
    args += [w, x, modarr]
    return pl.pallas_call(
        functools.partial(_outproj_kernel, n_a=len(a_list)),
        grid=(m // tm,),
        in_specs=in_specs,
        out_specs=pl.BlockSpec((tm, n), lambda i: (i, 0)),
        out_shape=jax.ShapeDtypeStruct((m, n), jnp.float32),
        compiler_params=_params("parallel"),
        name=name,
    )(*args)


def _ffn_kernel(x_ref, xp_ref, xn_ref, g_ref, sh_ref, sc_ref, gt_ref, wa_ref, wb_ref,
                ca_ref, cb_ref, wo_ref, o_ref, h_ref, *, tm, hm, seq_len):
    i = pl.program_id(0)
    j = pl.program_id(1)
    nj = pl.num_programs(1)
    n_grp = tm // hm
    rows = hm + 2 * SUBLANES

    @pl.when(j == 0)
    def _():
        gain, shift, scale = g_ref[...], sh_ref[0], sc_ref[0]
        for r in range(n_grp):
            r0 = r * hm
            h_ref[r * rows:r * rows + hm, :] = _bf(
                _modulate(x_ref[r0:r0 + hm, :], gain, shift, scale, mxu_stats=True))
            nxt_src = x_ref[r0 + hm:r0 + hm + SUBLANES, :] if r + 1 < n_grp else xn_ref[...]
            prv_src = x_ref[r0 - SUBLANES:r0, :] if r > 0 else xp_ref[...]
            keep_next = jnp.where((i * tm + r0 + hm) % seq_len == 0, 0.0, 1.0)
            keep_prev = jnp.where((i * tm + r0) % seq_len == 0, 0.0, 1.0)
            nxt = keep_next * _modulate(nxt_src, gain, shift, scale)
            prv = keep_prev * _modulate(prv_src, gain, shift, scale)
            h_ref[r * rows + hm:(r + 1) * rows, :] = _bf(jnp.concatenate([nxt, prv], axis=0))
        o_ref[...] = jnp.zeros_like(o_ref)

    ca, cb = ca_ref[...], cb_ref[...]
    ups = []
    for r in range(n_grp):
        h = h_ref[r * rows:(r + 1) * rows, :]
        ups.append((_dot(h, wa_ref[...]), _dot(h, wb_ref[...])))

    def conv(u, cw):
        prev = pltpu.roll(u, 1, axis=0)[0:hm]
        nxt = pltpu.roll(u, rows - 1, axis=0)[0:hm]
        return prev * cw[0:1] + u[0:hm] * cw[1:2] + nxt * cw[2:3]

    for r, (ua, ub) in enumerate(ups):
        a = conv(ua, ca)
        b = conv(ub, cb)
        act = a * (1.0 / (1.0 + jnp.exp(-a))) * b
        o_ref[r * hm:(r + 1) * hm, :] += _dot(_bf(act), wo_ref[...])

    @pl.when(j == nj - 1)
    def _():
        o_ref[...] = x_ref[...] + gt_ref[0] * o_ref[...]


def _ffn_call(x, gain, modarr, w_in, conv_w, w_out, layer, mod_row, seq_len, *, tm=TM_FF,
              tn=TN_FF, name="conv_ffn"):
    m, d = x.shape
    tm = min(tm, m)
    hm = min(HM_FF, seq_len)
    assert m % tm == 0 and tm % hm == 0 and seq_len % hm == 0
    nj = D_FF_PAD // tn
    hb = tm // SUBLANES
    last_hb = m // SUBLANES - 1
    kern = functools.partial(_ffn_kernel, tm=tm, hm=hm, seq_len=seq_len)
    return pl.pallas_call(
        kern,
        grid=(m // tm, nj),
        in_specs=[
            pl.BlockSpec((tm, d), lambda i, j: (i, 0)),
            pl.BlockSpec((SUBLANES, d), lambda i, j: (jnp.maximum(i * hb - 1, 0), 0)),
            pl.BlockSpec((SUBLANES, d), lambda i, j: (jnp.minimum((i + 1) * hb, last_hb), 0)),
            pl.BlockSpec((1, d), lambda i, j: (0, 0)),
            pl.BlockSpec((1, 1, d), lambda i, j: (mod_row(i), 0, 3)),
            pl.BlockSpec((1, 1, d), lambda i, j: (mod_row(i), 0, 4)),
            pl.BlockSpec((1, 1, d), lambda i, j: (mod_row(i), 0, 5)),
            pl.BlockSpec((None, d, tn), lambda i, j: (layer, 0, j)),
            pl.BlockSpec((None, d, tn), lambda i, j: (layer, 0, nj + j)),
            pl.BlockSpec((None, 3, tn), lambda i, j: (layer, 0, j)),
            pl.BlockSpec((None, 3, tn), lambda i, j: (layer, 0, nj + j)),
            pl.BlockSpec((None, tn, d), lambda i, j: (layer, j, 0)),
        ],
        out_specs=pl.BlockSpec((tm, d), lambda i, j: (i, 0)),
        out_shape=jax.ShapeDtypeStruct((m, d), jnp.float32),
        scratch_shapes=[
            pltpu.VMEM(((tm // hm) * (hm + 2 * SUBLANES), d), jnp.bfloat16),
        ],
        compiler_params=_params("parallel", "arbitrary"),
        name=name,
    )(x, x, x, gain, modarr, modarr, modarr, w_in, w_in, conv_w, conv_w, w_out)


def _na_group_geometry(g):
    r0 = g * NA_GROUP_ROWS
    return r0, min(max(r0 - NA_WIN_H // 2, 0), GRID_H - NA_KEY_ROWS)


def _na_build_bias(rpb_ref, h, tc_scr, bias_scr):
    n_dr = 2 * NA_WIN_H - 1
    n_dc = 2 * NA_WIN_W - 1
    qc = lax.broadcasted_iota(jnp.int32, (GRID_W, GRID_W), 0)
    kc = lax.broadcasted_iota(jnp.int32, (GRID_W, GRID_W), 1)
    col0 = jnp.clip(qc - NA_WIN_W // 2, 0, GRID_W - NA_WIN_W)
    col_valid = (kc >= col0) & (kc < col0 + NA_WIN_W)
    delta = kc - qc + (NA_WIN_W - 1)
    base = h * (n_dr * n_dc)
    for dr in range(n_dr):
        acc = jnp.zeros((GRID_W, GRID_W), jnp.float32)
        for e in range(n_dc):
            acc = jnp.where(delta == e, rpb_ref[base + dr * n_dc + e] * LOG2E, acc)
        tc_scr[dr] = jnp.where(col_valid, acc, NEG)
    tc_scr[n_dr] = jnp.full((GRID_W, GRID_W), NEG, jnp.float32)
    n_groups = GRID_H // NA_GROUP_ROWS
    for t, g in enumerate((0, 1, n_groups - 1)):
        r0, ks = _na_group_geometry(g)
        for qr in range(NA_GROUP_ROWS):
            r = r0 + qr
            row0 = min(max(r - NA_WIN_H // 2, 0), GRID_H - NA_WIN_H)
            pieces = []
            for kr in range(NA_KEY_ROWS):
                kra = ks + kr
                inside = row0 <= kra < row0 + NA_WIN_H
                pieces.append(tc_scr[kra - r + (NA_WIN_H - 1)] if inside else tc_scr[n_dr])
            bias_scr[t, qr * GRID_W:(qr + 1) * GRID_W, :] = jnp.concatenate(pieces, axis=1)


def _na_kernel(rpb_ref, q_ref, k_ref, v_ref, kc_ref, vc_ref, o_ref, tc_scr, bias_scr):
    @pl.when(pl.program_id(1) == 0)
    def _():
        _na_build_bias(rpb_ref, pl.program_id(0), tc_scr, bias_scr)

    kc = kc_ref[...]
    vc = vc_ref[...]
    n_groups = GRID_H // NA_GROUP_ROWS

    def group(g):
        r0 = g * NA_GROUP_ROWS
        ks = jnp.clip(r0 - NA_WIN_H // 2, 0, GRID_H - NA_KEY_ROWS)
        q0 = pl.multiple_of(g * NA_Q, NA_Q)
        k0 = pl.multiple_of(ks * GRID_W, GRID_W)
        tb = jnp.where(g == 0, 0, jnp.where(g == n_groups - 1, 2, 1))
        q = q_ref[pl.ds(q0, NA_Q), :]
        kw = k_ref[pl.ds(k0, NA_K), :]
        vw = v_ref[pl.ds(k0, NA_K), :]
        sw = _dot_nt(q, kw) + bias_scr[tb]
        sc = _dot_nt(q, kc)
        mx = jnp.maximum(jnp.max(sw, axis=-1, keepdims=True), jnp.max(sc, axis=-1, keepdims=True))
        pw = jnp.exp2(sw - mx)
        pc = jnp.exp2(sc - mx)
        den = jnp.sum(pw, axis=-1, keepdims=True) + jnp.sum(pc, axis=-1, keepdims=True)
        o = _dot(_bf(pw), vw) + _dot(_bf(pc), vc)
        o_ref[pl.ds(q0, NA_Q), :] = _bf(o / den)

    def body(t, carry):
        for u in range(NA_UNROLL):
            group(t * NA_UNROLL + u)
        return carry

    lax.fori_loop(0, n_groups // NA_UNROLL, body, 0)


def _na_call(qkv, qkv_c, rpb):
    n_dr = 2 * NA_WIN_H - 1
    return pl.pallas_call(
        _na_kernel,
        grid=(NA_HEADS, BATCH),
        in_specs=[
            pl.BlockSpec(memory_space=pltpu.SMEM),
            pl.BlockSpec((SEQ, LANES), lambda h, b: (b, h)),
            pl.BlockSpec((SEQ, LANES), lambda h, b: (b, NA_HEADS + h)),
            pl.BlockSpec((SEQ, LANES), lambda h, b: (b, 2 * NA_HEADS + h)),
            pl.BlockSpec((CTX_LEN, LANES), lambda h, b: (b, NA_HEADS + h)),
            pl.BlockSpec((CTX_LEN, LANES), lambda h, b: (b, 2 * NA_HEADS + h)),
        ],
        out_specs=pl.BlockSpec((SEQ, LANES), lambda h, b: (b, h)),
        out_shape=jax.ShapeDtypeStruct((BATCH * SEQ, NA_WIDTH), jnp.bfloat16),
        scratch_shapes=[
            pltpu.VMEM((n_dr + 1, GRID_W, GRID_W), jnp.float32),
            pltpu.VMEM((3, NA_Q, NA_K), jnp.float32),
        ],
        compiler_params=_params("parallel", "arbitrary"),
        name="na_attn",
    )(rpb.reshape(-1), qkv, qkv, qkv, qkv_c, qkv_c)


def _key_chunks(src_rows, chunk):
    chunks, r = [], 0
    for rows in src_rows:
        size = min(chunk, rows)
        assert rows % size == 0
        chunks += [(r + c, size) for c in range(0, rows, size)]
        r += rows
    return chunks


def _pipelined_attention(streams, vt_scr, chunks, carry_in=None, defer_last=False):
    dv = vt_scr.shape[0]
    sub = streams[0][2].shape[1]
    n = len(streams)
    outs, prev = [], carry_in
    for u in range(n if defer_last else n + 1):
        cur = streams[u] if u < n else None
        m = jnp.full((1, sub), NEG, jnp.float32)
        l = jnp.zeros((1, sub), jnp.float32)
        acc = jnp.zeros((dv, sub), jnp.float32)
        for c0, cs in chunks:
            if cur is not None:
                q_t, k_scr, s_ref = cur
                s = _dot(k_scr[c0:c0 + cs, :], q_t)
                s_ref[c0:c0 + cs, :] = s
                m = jnp.maximum(m, jnp.max(s, axis=0, keepdims=True))
            if prev is not None:
                p = jnp.exp2(prev[0][c0:c0 + cs, :] - prev[1])
                l = l + jnp.sum(p, axis=0, keepdims=True)
                acc = acc + _dot(vt_scr[:, c0:c0 + cs], _bf(p))
        if prev is not None:
            outs.append(acc / l)
        prev = (cur[2], m) if cur is not None else None
    return outs, prev


def _transpose_bf16(x):
    return _bf(x.astype(jnp.float32).T)


def _run_streams(streams, vt_scr, chunks, m_scr, nq, write):
    i = pl.program_id(2)
    n = len(streams)
    if nq == 1:
        outs, _ = _pipelined_attention(streams, vt_scr, chunks)
        for u, o_t in enumerate(outs):
            write(0, u, o_t)
        return
    carried = streams[n - 1][2]

    def step(carry, defer):
        outs, last = _pipelined_attention(
            streams, vt_scr, chunks,
            carry_in=(carried, m_scr[...]) if carry else None, defer_last=defer)
        if carry:
            write(-1, n - 1, outs[0])
            outs = outs[1:]
        for u, o_t in enumerate(outs):
            write(0, u, o_t)
        if defer:
            m_scr[...] = last[1]

    pl.when(i == 0)(lambda: step(False, True))
    if nq > 2:
        pl.when((i > 0) & (i < nq - 1))(lambda: step(True, True))
    pl.when(i == nq - 1)(lambda: step(True, False))


def _attn_kernel(*refs, n_qparts, n_src, n_kparts, src_rows, nq):
    it = iter(refs)
    q_refs = [next(it) for _ in range(n_qparts)]
    srcs = []
    for _ in range(n_src):
        k_refs = [next(it) for _ in range(n_kparts)]
        srcs.append((k_refs, next(it)))
    o_ref = next(it)
    k_scr, vt_scr, m_scr = next(it), next(it), next(it)
    s_scrs = list(it)

    @pl.when(pl.program_id(2) == 0)
    def _():
        r = 0
        for (k_refs, v_ref), rows in zip(srcs, src_rows):
            for p, k_ref in enumerate(k_refs):
                k_scr[r:r + rows, p * LANES:(p + 1) * LANES] = k_ref[...]
            vt_scr[:, r:r + rows] = _transpose_bf16(v_ref[...])
            r += rows

    q = jnp.concatenate([qr[...] for qr in q_refs], axis=-1) if n_qparts > 1 else q_refs[0][...]
    sub = s_scrs[0].shape[1]
    tq = q.shape[0]
    streams = [(_transpose_bf16(q[u * sub:(u + 1) * sub]), k_scr, s_scrs[u % 2])
               for u in range(tq // sub)]

    def write(step_offset, u, o_t):
        row = pl.multiple_of((pl.program_id(2) + step_offset) * tq + u * sub, sub)
        o_ref[pl.ds(row, sub), :] = _bf(o_t.T)

    _run_streams(streams, vt_scr, _key_chunks(src_rows, KEY_CHUNK_ATTN), m_scr, nq, write)


def _attn_call(q_parts, sources, n_heads, lq, tq, dv, name):
    tq = min(tq, lq)
    sub = min(TQ_SUB, tq)
    nq = lq // tq
    in_specs, args = [], []
    for arr, cf in q_parts:
        in_specs.append(pl.BlockSpec((tq, LANES), lambda b, h, i, cf=cf: (b * nq + i, cf(h))))
        args.append(arr)
    src_rows = []
    n_kparts = len(sources[0][1])
    for rows, k_parts, (v_arr, vcf) in sources:
        src_rows.append(rows)
        for arr, cf in k_parts:
            in_specs.append(pl.BlockSpec((rows, LANES), lambda b, h, i, cf=cf: (b, cf(h))))
            args.append(arr)
        in_specs.append(pl.BlockSpec((rows, dv), lambda b, h, i, cf=vcf: (b, cf(h))))
        args.append(v_arr)
    nk = sum(src_rows)
    assert nq == 1 or (tq // sub) % 2 == 0
    kern = functools.partial(_attn_kernel, n_qparts=len(q_parts), n_src=len(sources),
                             n_kparts=n_kparts, src_rows=tuple(src_rows), nq=nq)
    return pl.pallas_call(
        kern,
        grid=(BATCH, n_heads, nq),
        in_specs=in_specs,
        out_specs=pl.BlockSpec((lq, dv), lambda b, h, i: (b, h)),
        out_shape=jax.ShapeDtypeStruct((BATCH * lq, n_heads * dv), jnp.bfloat16),
        scratch_shapes=[
            pltpu.VMEM((nk, n_kparts * LANES), jnp.bfloat16),
            pltpu.VMEM((dv, nk), jnp.bfloat16),
            pltpu.VMEM((1, sub), jnp.float32),
        ] + [pltpu.VMEM((nk, sub), jnp.float32)] * 2,
        compiler_params=_params("parallel", "parallel", "arbitrary"),
        name=name,
    )(*args)


def _diff_kernel(*refs, n_src, src_rows, lam_init, nq):
    it = iter(refs)
    q1_ref, q2_ref = next(it), next(it)
    srcs = [(next(it), next(it), next(it)) for _ in range(n_src)]
    lam_ref, sub_ref = next(it), next(it)
    o_ref = next(it)
    k1_scr, k2_scr, vt_scr, m_scr, o1_scr, s1_scr, s2_scr = [next(it) for _ in range(7)]

    @pl.when(pl.program_id(2) == 0)
    def _():
        r = 0
        for (k1_ref, k2_ref, v_ref), rows in zip(srcs, src_rows):
            k1_scr[r:r + rows, :] = k1_ref[...]
            k2_scr[r:r + rows, :] = k2_ref[...]
            vt_scr[:, r:r + rows] = _transpose_bf16(v_ref[...])
            r += rows

    lf = lam_ref[...]
    lam = (jnp.exp(jnp.sum(lf[0:1] * lf[1:2], axis=-1, keepdims=True))
           - jnp.exp(jnp.sum(lf[2:3] * lf[3:4], axis=-1, keepdims=True)) + lam_init)
    chunks = _key_chunks(src_rows, KEY_CHUNK_DIFF)
    sub = s1_scr.shape[1]
    tq = q1_ref.shape[0]
    streams = []
    for u in range(tq // sub):
        rows = slice(u * sub, (u + 1) * sub)
        streams.append((_transpose_bf16(q1_ref[rows, :]), k1_scr, s1_scr))
        streams.append((_transpose_bf16(q2_ref[rows, :]), k2_scr, s2_scr))
    n = len(streams)
    branch1 = {}

    def write(step_offset, u, o_t):
        if u % 2 == 0:
            branch1[u // 2] = o_t
            if nq > 1 and u == n - 2:
                o1_scr[...] = o_t
            return
        o1 = o1_scr[...] if step_offset < 0 else branch1[u // 2]
        o = (o1 - lam * o_t).T
        ms = jnp.mean(o * o, axis=-1, keepdims=True)
        row = pl.multiple_of((pl.program_id(2) + step_offset) * tq + (u // 2) * sub, sub)
        o_ref[pl.ds(row, sub), :] = _bf(
            o * lax.rsqrt(ms + EPS) * sub_ref[...] * (1.0 - lam_init))

    _run_streams(streams, vt_scr, chunks, m_scr, nq, write)


def _diff_call(qkv_q, lq, sources, diff_lambda, subln, lam_init, tq, name):
    tq = min(tq, lq)
    sub = min(TQ_SUB, tq)
    nq = lq // tq
    qb0 = 3 * NA_WIDTH // LANES
    kb0 = (3 * NA_WIDTH + DIFF_WIDTH) // LANES
    vb0 = (3 * NA_WIDTH + 2 * DIFF_WIDTH) // (2 * LANES)
    in_specs = [
        pl.BlockSpec((tq, LANES), lambda b, h, i: (b * nq + i, qb0 + 2 * h)),
        pl.BlockSpec((tq, LANES), lambda b, h, i: (b * nq + i, qb0 + 2 * h + 1)),
    ]
    args = [qkv_q, qkv_q]
    src_rows = []
    for rows, arr in sources:
        src_rows.append(rows)
        in_specs += [
            pl.BlockSpec((rows, LANES), lambda b, h, i: (b, kb0 + 2 * h)),
            pl.BlockSpec((rows, LANES), lambda b, h, i: (b, kb0 + 2 * h + 1)),
            pl.BlockSpec((rows, 2 * LANES), lambda b, h, i: (b, vb0 + h)),
        ]
        args += [arr, arr, arr]
    in_specs += [
        pl.BlockSpec((4, LANES), lambda b, h, i: (0, 0)),
        pl.BlockSpec((1, 2 * LANES), lambda b, h, i: (0, 0)),
    ]
    args += [diff_lambda, subln]
    nk = sum(src_rows)
    kern = functools.partial(_diff_kernel, n_src=len(sources), src_rows=tuple(src_rows),
                             lam_init=lam_init, nq=nq)
    return pl.pallas_call(
        kern,
        grid=(BATCH, DIFF_HEADS, nq),
        in_specs=in_specs,
        out_specs=pl.BlockSpec((lq, 2 * LANES), lambda b, h, i: (b, h)),
        out_shape=jax.ShapeDtypeStruct((BATCH * lq, DIFF_WIDTH), jnp.bfloat16),
        scratch_shapes=[
            pltpu.VMEM((nk, LANES), jnp.bfloat16),
            pltpu.VMEM((nk, LANES), jnp.bfloat16),
            pltpu.VMEM((2 * LANES, nk), jnp.bfloat16),
            pltpu.VMEM((1, sub), jnp.float32),
            pltpu.VMEM((2 * LANES, sub), jnp.float32),
            pltpu.VMEM((nk, sub), jnp.float32),
            pltpu.VMEM((nk, sub), jnp.float32),
        ],
        compiler_params=_params("parallel", "parallel", "arbitrary"),
        name=name,
    )(*args)


def _rope_tables(d):
    h = d // 2
    half = h // 2
    lane = jnp.arange(PAIR, dtype=jnp.int32)[None, :] % LANES
    t = jnp.arange(SEQ, dtype=jnp.int32)[:, None]
    pos = jnp.where(lane < h, t // GRID_W, t % GRID_W).astype(jnp.float32)
    freqs = ROPE_BASE ** (-(lane % half).astype(jnp.float32) / half)
    ang = pos * freqs
    valid = lane < d
    first = (lane % h) < half
    cos = jnp.where(valid, jnp.cos(ang), 1.0)
    sin = jnp.where(valid, jnp.sin(ang), 0.0)
    sin_a = jnp.where(first, -sin, 0.0)
    sin_b = jnp.where(first, 0.0, sin)
    return (cos, sin_a, sin_b), half


def _pair_blockdiag(block):
    z = np.zeros_like(block)
    return jnp.asarray(np.block([[block, z], [z, block]]), dtype=jnp.bfloat16)


def _pad_lanes(v, fill=0.0):
    return jnp.pad(v, (0, LANES - v.shape[0]), constant_values=fill)


def _pair(v):
    return jnp.concatenate([v, v])


def _lat_row(tm):
    bpb = SEQ // tm
    return lambda i: i // bpb


def _ctx_row(i):
    return CTX_MOD_ROW


def _even_layer(x, xc, modarr, norm_mix, norm_ffn, ffn_w, w_in, w_out, qn_a, kn_a, rpb,
                qn_b, kn_b, diff_lambda, subln, lam_init, with_ctx):
    scale = HEAD_DIM ** -0.5 * LOG2E
    gains = jnp.stack([_pair(g) for g in (qn_a * scale, kn_a, qn_b * scale, kn_b)]
                      + [jnp.ones((PAIR,), jnp.float32)] * 4)
    rope = _rope_tables(HEAD_DIM)
    segs = [(True, 0, HEAD_DIM, False), (True, 1, HEAD_DIM, False), (False, 0, HEAD_DIM, False),
            (True, 2, HEAD_DIM, True), (True, 3, HEAD_DIM, True), (False, 0, HEAD_DIM, False)]
    w_in = _bf(w_in)
    w_out = _bf(w_out)
    gain_mix = norm_mix[None, :]
    lat_row = _lat_row(TM)

    qkv = _proj_call(x, w_in, gains, segs, mod=(gain_mix, modarr, 0, 1), rope=rope,
                     mod_row=_lat_row(TM_EVEN), tm=TM_EVEN, name="even_proj")
    qkv_c = _proj_call(xc, w_in, gains, segs, mod=(gain_mix, modarr, 0, 1), rope=None,
                       mod_row=_ctx_row, tm=TM_EVEN, name="even_proj_ctx")

    oa = _na_call(qkv, qkv_c, rpb)
    ob = _diff_call(qkv, SEQ, [(SEQ, qkv), (CTX_LEN, qkv_c)], diff_lambda, subln[None, :],
                    lam_init, TQ_DIFF, "diff_attn")
    x = _outproj_call([oa, ob], w_out, x, modarr, 2, lat_row, name="even_out")
    x = _ffn_call(x, norm_ffn[None, :], modarr, *ffn_w, _lat_row(TM_FF), SEQ)
    if with_ctx:
        oa_c = _attn_call(
            [(qkv_c, lambda h: h)],
            [(CTX_LEN, [(qkv_c, lambda h: NA_HEADS + h)], (qkv_c, lambda h: 2 * NA_HEADS + h))],
            NA_HEADS, CTX_LEN, CTX_LEN, HEAD_DIM, "ctx_attn")
        ob_c = _diff_call(qkv_c, CTX_LEN, [(CTX_LEN, qkv_c)], diff_lambda, subln[None, :],
                          lam_init, CTX_LEN, "diff_attn_ctx")
        xc = _outproj_call([oa_c, ob_c], w_out, xc, modarr, 2, _ctx_row, name="even_out_ctx")
        xc = _ffn_call(xc, norm_ffn[None, :], modarr, *ffn_w, _ctx_row, CTX_LEN,
                       name="conv_ffn_ctx")
    return x, xc


def _odd_layer(x, xc, modarr, norm_mix, norm_ffn, ffn_w, w_down, q_a_norm, kv_a_norm, w_uq,
               w_ukv, qn_nope, qn_rope, kn_nope, kn_rope, w_out, with_ctx):
    scale = (MLA_NOPE + MLA_ROPE) ** -0.5 * LOG2E
    rope_pair = _rope_tables(MLA_ROPE)
    gain_mix = norm_mix[None, :]
    lat_row = _lat_row(TM)

    w_down_p = _bf(jnp.pad(w_down, ((0, 0), (0, MLA_DOWN_PAD - MLA_DOWN))))
    zeros_r = jnp.zeros((MLA_Q_RANK,), jnp.float32)
    kr_gain = jnp.concatenate([_pad_lanes(kn_rope), zeros_r[:MLA_Q_RANK - LANES]])
    gains_d = jnp.stack([q_a_norm, kv_a_norm, kr_gain] + [zeros_r] * 5)
    wq = w_uq.reshape(MLA_Q_RANK, MLA_HEADS, MLA_NOPE + MLA_ROPE)
    wq_rope = jnp.pad(wq[:, :, MLA_NOPE:], ((0, 0), (0, 0), (0, LANES - MLA_ROPE)))
    wq_p = _bf(jnp.concatenate([wq[:, :, :MLA_NOPE].reshape(MLA_Q_RANK, -1),
                                wq_rope.reshape(MLA_Q_RANK, -1)], -1))
    wkv = w_ukv.reshape(MLA_KV_RANK, MLA_HEADS, MLA_NOPE + MLA_V)
    wkv_p = _bf(jnp.concatenate([wkv[:, :, :MLA_NOPE].reshape(MLA_KV_RANK, -1),
                                 wkv[:, :, MLA_NOPE:].reshape(MLA_KV_RANK, -1)], -1))
    gains_p = jnp.stack([_pair(qn_nope * scale), _pair(_pad_lanes(qn_rope) * scale),
                         _pair(kn_nope)] + [jnp.zeros((PAIR,), jnp.float32)] * 5)
    mod_in = (gain_mix, modarr, 0, 1)
    q, kv, kr = _mla_proj_call(x, w_down_p, wq_p, wkv_p, gains_d, gains_p, mod_in, rope_pair,
                               _lat_row(TM_MLA_PROJ), True, "mla_proj")
    kv_c, kr_c = _mla_proj_call(xc, w_down_p, wq_p, wkv_p, gains_d, gains_p, mod_in, None,
                                _ctx_row, False, "mla_proj_ctx")

    def src(rows, kv_arr, kr_arr):
        return (rows, [(kv_arr, lambda h: h), (kr_arr, lambda h: 0)],
                (kv_arr, lambda h: MLA_HEADS + h))

    o = _attn_call([(q, lambda h: h), (q, lambda h: MLA_HEADS + h)],
                   [src(SEQ, kv, kr), src(CTX_LEN, kv_c, kr_c)],
                   MLA_HEADS, SEQ, TQ_MLA, MLA_V, "mla_attn")
    x = _outproj_call([o], _bf(w_out), x, modarr, 2, lat_row, name="mla_out")
    x = _ffn_call(x, norm_ffn[None, :], modarr, *ffn_w, _lat_row(TM_FF), SEQ)
    assert not with_ctx
    return x, xc


def _mla_proj_kernel(x_ref, g_ref, sh_ref, sc_ref, wd_ref, wq_ref, wkv_ref, gd_ref, gp_ref,
                     ones_ref, *rest, rope, rope_half, with_q):
    rope_refs = rest[:3] if rope else None
    outs = rest[3:] if rope else rest
    if with_q:
        q_ref, kv_ref, kr_ref = outs
    else:
        kv_ref, kr_ref = outs
    h = _bf(_modulate(x_ref[...], g_ref[...], sh_ref[0], sc_ref[0]))
    down = _dot(h, wd_ref[...])

    def latent_norm(xs, row, nvalid):
        ms = jnp.sum(xs * xs, axis=-1, keepdims=True) * (1.0 / nvalid)
        return xs * lax.rsqrt(ms + EPS) * gd_ref[row:row + 1, 0:xs.shape[1]]

    kva = _bf(latent_norm(down[:, MLA_Q_RANK:MLA_Q_RANK + MLA_KV_RANK], 1, MLA_KV_RANK))
    kr = latent_norm(down[:, MLA_Q_RANK + MLA_KV_RANK:MLA_DOWN_PAD], 2, MLA_ROPE)
    if rope:
        cos_ref, sa_ref, sb_ref = rope_refs
        kr = (kr * cos_ref[:, 0:LANES]
              + pltpu.roll(kr, LANES - rope_half, axis=1) * sa_ref[:, 0:LANES]
              + pltpu.roll(kr, rope_half, axis=1) * sb_ref[:, 0:LANES])
    kr_ref[...] = _bf(kr)

    _head_projection(kva, wkv_ref, kv_ref,
                     [(True, 2, MLA_NOPE, False), (False, 0, MLA_V, False)],
                     gp_ref, ones_ref, rope_refs, rope_half)
    if with_q:
        qa = _bf(latent_norm(down[:, 0:MLA_Q_RANK], 0, MLA_Q_RANK))
        _head_projection(qa, wq_ref, q_ref,
                         [(True, 0, MLA_NOPE, False), (True, 1, MLA_ROPE, True)],
                         gp_ref, ones_ref, rope_refs, rope_half)


def _mla_proj_call(x, wd, wq, wkv, gains_d, gains_p, mod, rope, mod_row, with_q, name,
                   tm=TM_MLA_PROJ):
    m, k = x.shape
    tm = min(tm, m)
    bpb = SEQ // tm
    gain, modarr, sh_c, sc_c = mod
    ones = _pair_blockdiag(np.ones((LANES, LANES), np.float32))
    in_specs = [
        pl.BlockSpec((tm, k), lambda i: (i, 0)),
        pl.BlockSpec((1, k), lambda i: (0, 0)),
        pl.BlockSpec((1, 1, k), lambda i: (mod_row(i), 0, sh_c)),
        pl.BlockSpec((1, 1, k), lambda i: (mod_row(i), 0, sc_c)),
    ]
    args = [x, gain, modarr, modarr]
    for w in (wd, wq, wkv, gains_d, gains_p, ones):
        in_specs.append(pl.BlockSpec(w.shape, lambda i: (0, 0), pipeline_mode=pl.Buffered(1)))
        args.append(w)
    rope_half = 0
    if rope is not None:
        tables, rope_half = rope
        for t in tables:
            in_specs.append(pl.BlockSpec((tm, PAIR), lambda i: (i % bpb, 0)))
            args.append(t)
    widths = ([wq.shape[1]] if with_q else []) + [wkv.shape[1], LANES]
    return pl.pallas_call(
        functools.partial(_mla_proj_kernel, rope=rope is not None, rope_half=rope_half,
                          with_q=with_q),
        grid=(m // tm,),
        in_specs=in_specs,
        out_specs=[pl.BlockSpec((tm, n), lambda i: (i, 0)) for n in widths],
        out_shape=[jax.ShapeDtypeStruct((m, n), jnp.bfloat16) for n in widths],
        compiler_params=_params("parallel"),
        name=name,
    )(*args)


def _w_in_prep_kernel(x_ref, o_ref):
    pad = jnp.zeros((x_ref.shape[0], D_FF_PAD - D_FF), jnp.bfloat16)
    o_ref[:, 0:D_FF] = _bf(x_ref[:, 0:D_FF])
    o_ref[:, D_FF:D_FF_PAD] = pad
    o_ref[:, D_FF_PAD:D_FF_PAD + D_FF] = _bf(x_ref[:, D_FF:2 * D_FF])
    o_ref[:, D_FF_PAD + D_FF:2 * D_FF_PAD] = pad


def _w_out_prep_kernel(x_ref, o_ref):
    o_ref[0:D_FF, :] = _bf(x_ref[...])
    o_ref[D_FF:D_FF_PAD, :] = jnp.zeros((D_FF_PAD - D_FF, x_ref.shape[1]), jnp.bfloat16)


def _ffn_weights(w_in, conv_w, w_out):
    depth, d, _ = w_in.shape
    rows, cols = PREP_ROWS, PREP_COLS
    w_in_p = pl.pallas_call(
        _w_in_prep_kernel,
        grid=(depth, d // rows),
        in_specs=[pl.BlockSpec((None, rows, 2 * D_FF), lambda l, i: (l, i, 0))],
        out_specs=pl.BlockSpec((None, rows, 2 * D_FF_PAD), lambda l, i: (l, i, 0)),
        out_shape=jax.ShapeDtypeStruct((depth, d, 2 * D_FF_PAD), jnp.bfloat16),
        compiler_params=_params("parallel", "parallel"),
        name="ffn_w_in_prep",
    )(w_in)
    w_out_p = pl.pallas_call(
        _w_out_prep_kernel,
        grid=(depth, d // cols),
        in_specs=[pl.BlockSpec((None, D_FF, cols), lambda l, j: (l, 0, j))],
        out_specs=pl.BlockSpec((None, D_FF_PAD, cols), lambda l, j: (l, 0, j)),
        out_shape=jax.ShapeDtypeStruct((depth, D_FF_PAD, d), jnp.bfloat16),
        compiler_params=_params("parallel", "parallel"),
        name="ffn_w_out_prep",
    )(w_out)
    pad = D_FF_PAD - D_FF
    zc = jnp.zeros(conv_w.shape[:2] + (pad,), conv_w.dtype)
    conv_p = jnp.concatenate([conv_w[..., :D_FF], zc, conv_w[..., D_FF:], zc], -1)
    return w_in_p, conv_p, w_out_p


def kernel(x, c, ctx, c_ctx, ada_w, ada_b, norm_mix, norm_ffn, ffn_w_in, ffn_conv, ffn_w_out, even_w_in, even_w_out, na_q_norm, na_k_norm, na_rpb, diff_q_norm, diff_k_norm, diff_lambda, diff_subln, mla_w_down, mla_q_a_norm, mla_kv_a_norm, mla_w_uq, mla_w_ukv, mla_q_nope_norm, mla_q_rope_norm, mla_k_nope_norm, mla_k_rope_norm, mla_w_out):
    cond = jnp.concatenate(
        [c, c_ctx[None, :], jnp.zeros((MOD_ROWS - BATCH - 1, D_MODEL), jnp.float32)], 0)
    mod = _ada_call(cond, ada_w, ada_b)
    xl = x.reshape(BATCH * SEQ, D_MODEL)
    xc = ctx.reshape(BATCH * CTX_LEN, D_MODEL)
    ffn_all = _ffn_weights(ffn_w_in, ffn_conv, ffn_w_out)
    for l in range(DEPTH):
        with_ctx = l < DEPTH - 1
        modarr = mod[l].reshape(MOD_ROWS, 1, 6 * D_MODEL)
        ffn_w = ffn_all + (l,)
        i = l // 2
        if l % 2 == 0:
            lam_init = 0.8 - 0.6 * math.exp(-0.3 * l)
            xl, xc = _even_layer(xl, xc, modarr, norm_mix[l], norm_ffn[l], ffn_w, even_w_in[i],
                                 even_w_out[i], na_q_norm[i], na_k_norm[i], na_rpb[i],
                                 diff_q_norm[i], diff_k_norm[i], diff_lambda[i], diff_subln[i],
                                 lam_init, with_ctx)
        else:
            xl, xc = _odd_layer(xl, xc, modarr, norm_mix[l], norm_ffn[l], ffn_w, mla_w_down[i],
                                mla_q_a_norm[i], mla_kv_a_norm[i], mla_w_uq[i], mla_w_ukv[i],
                                mla_q_nope_norm[i], mla_q_rope_norm[i], mla_k_nope_norm[i],
                                mla_k_rope_norm[i], mla_w_out[i], with_ctx)
    return xl.reshape(BATCH, SEQ, D_MODEL)
```

```python
import functools
import math

import numpy as np
import jax
import jax.numpy as jnp
from jax import lax
from jax.experimental import pallas as pl
from jax.experimental.pallas import tpu as pltpu

D_MODEL = 2048
BATCH = 4
SEQ = 4096
DEPTH = 2
GRID_W = 64
GRID_H = SEQ // GRID_W
CTX_LEN = 256
HEAD_DIM = 128
NA_HEADS = 8
NA_WIN_H = 8
NA_WIN_W = 16
DIFF_HEADS = 4
NA_WIDTH = 1024
DIFF_WIDTH = 1024
EVEN_PROJ = 6144
MLA_HEADS = 16
MLA_Q_RANK = 512
MLA_KV_RANK = 512
MLA_NOPE = 128
MLA_ROPE = 64
MLA_V = 128
MLA_DOWN = 1088
D_FF = 5504
ROPE_BASE = 10000.0
EPS = 1e-6

LANES = 128
SUBLANES = 8
PAIR = 2 * LANES
VMEM_LIMIT = 56 * 1024 * 1024
MOD_ROWS = 8
CTX_MOD_ROW = BATCH
NEG = -1e30
LOG2E = math.log2(math.e)

TM = 512
TM_EVEN = 512
TM_MLA_PROJ = 512
SUB_N = 512
TN_ADA = 1024
MLA_DOWN_PAD = 1152
TN_FF = 512
D_FF_PAD = -(-D_FF // TN_FF) * TN_FF
TM_FF = 1024
HM_FF = 512
PREP_ROWS = 256
PREP_COLS = 256
NA_GROUP_ROWS = 4
NA_KEY_ROWS = NA_GROUP_ROWS + NA_WIN_H
NA_UNROLL = 8
NA_Q = NA_GROUP_ROWS * GRID_W
NA_K = NA_KEY_ROWS * GRID_W
TQ_MLA = 1024
TQ_DIFF = 512
TQ_SUB = 256
KEY_CHUNK_ATTN = 512
KEY_CHUNK_DIFF = 512

_NT = (((1,), (1,)), ((), ()))


def _params(*sem):
    return pltpu.CompilerParams(dimension_semantics=sem, vmem_limit_bytes=VMEM_LIMIT)


def _bf(x):
    return x.astype(jnp.bfloat16)


def _dot(a, b):
    return jnp.dot(a, b, preferred_element_type=jnp.float32)


def _dot_nt(a, b):
    return lax.dot_general(a, b, _NT, preferred_element_type=jnp.float32)


def _modulate(x, gain, shift, scale, mxu_stats=False):
    d = x.shape[1]
    if mxu_stats:
        ss = _dot(_bf(x * x), jnp.ones((d, LANES), jnp.bfloat16))
        r = jnp.tile(lax.rsqrt(ss * (1.0 / d) + EPS), (1, d // LANES))
    else:
        r = lax.rsqrt(jnp.mean(x * x, axis=-1, keepdims=True) + EPS)
    return x * r * (gain * (1.0 + scale)) + shift


def _ada_kernel(c_ref, w_ref, b_ref, o_ref):
    c = c_ref[...]
    s = c * (1.0 / (1.0 + jnp.exp(-c)))
    o_ref[0] = _dot(_bf(s), _bf(w_ref[0])) + b_ref[0]


def _ada_call(cond, ada_w, ada_b):
    tn = TN_ADA
    n = ada_w.shape[-1]
    return pl.pallas_call(
        _ada_kernel,
        grid=(DEPTH, n // tn),
        in_specs=[
            pl.BlockSpec((MOD_ROWS, D_MODEL), lambda l, j: (0, 0)),
            pl.BlockSpec((1, D_MODEL, tn), lambda l, j: (l, 0, j)),
            pl.BlockSpec((1, 1, tn), lambda l, j: (l, 0, j)),
        ],
        out_specs=pl.BlockSpec((1, MOD_ROWS, tn), lambda l, j: (l, 0, j)),
        out_shape=jax.ShapeDtypeStruct((DEPTH, MOD_ROWS, n), jnp.float32),
        compiler_params=_params("parallel", "parallel"),
        name="ada_mod",
    )(cond, ada_w, ada_b.reshape(DEPTH, 1, n))


def _pair_epilogue(acc, spec, gains_ref, ones_ref, rope_refs, rope_half):
    norm, gain_row, nvalid, do_rope = spec
    y = acc
    if norm:
        ss = _dot(_bf(acc * acc), ones_ref[...])
        y = acc * lax.rsqrt(ss * (1.0 / nvalid) + EPS) * gains_ref[gain_row:gain_row + 1, :]
    if do_rope and rope_refs is not None:
        cos_ref, sa_ref, sb_ref = rope_refs
        y = (y * cos_ref[...]
             + pltpu.roll(y, PAIR - rope_half, axis=1) * sa_ref[...]
             + pltpu.roll(y, rope_half, axis=1) * sb_ref[...])
    return y


def _head_projection(a, w_ref, o_ref, segs, gains_ref, ones_ref, rope_refs, rope_half):
    width = w_ref.shape[1] // len(segs)
    for c in range(0, w_ref.shape[1], SUB_N):
        acc = _dot(a, w_ref[:, c:c + SUB_N])
        for p in range(0, SUB_N, PAIR):
            y = _pair_epilogue(acc[:, p:p + PAIR], segs[c // width], gains_ref, ones_ref,
                               rope_refs, rope_half)
            o_ref[:, c + p:c + p + PAIR] = _bf(y)


def _proj_kernel(x_ref, g_ref, sh_ref, sc_ref, w_ref, gains_ref, ones_ref, *rest, segs, rope,
                 rope_half):
    rope_refs = rest[:3] if rope else None
    o_ref = rest[-1]
    h = _bf(_modulate(x_ref[...], g_ref[...], sh_ref[0], sc_ref[0], mxu_stats=True))
    _head_projection(h, w_ref, o_ref, segs, gains_ref, ones_ref, rope_refs, rope_half)


def _proj_call(x, w, gains, segs, *, mod, rope=None, mod_row=None, tm=TM, name="proj"):
    m, k = x.shape
    n = w.shape[1]
    tm = min(tm, m)
    assert m % tm == 0 and n % (len(segs) * SUB_N) == 0
    bpb = SEQ // tm
    gain, modarr, sh_c, sc_c = mod
    ones = _pair_blockdiag(np.ones((LANES, LANES), np.float32))
    in_specs = [
        pl.BlockSpec((tm, k), lambda i: (i, 0)),
        pl.BlockSpec((1, k), lambda i: (0, 0)),
        pl.BlockSpec((1, 1, k), lambda i: (mod_row(i), 0, sh_c)),
        pl.BlockSpec((1, 1, k), lambda i: (mod_row(i), 0, sc_c)),
    ]
    args = [x, gain, modarr, modarr]
    for const in (w, gains, ones):
        in_specs.append(pl.BlockSpec(const.shape, lambda i: (0, 0), pipeline_mode=pl.Buffered(1)))
        args.append(const)
    rope_half = 0
    if rope is not None:
        tables, rope_half = rope
        for t in tables:
            in_specs.append(pl.BlockSpec((tm, PAIR), lambda i: (i % bpb, 0)))
            args.append(t)
    kern = functools.partial(_proj_kernel, segs=tuple(segs), rope=rope is not None,
                             rope_half=rope_half)
    return pl.pallas_call(
        kern,
        grid=(m // tm,),
        in_specs=in_specs,
        out_specs=pl.BlockSpec((tm, n), lambda i: (i, 0)),
        out_shape=jax.ShapeDtypeStruct((m, n), jnp.bfloat16),
        compiler_params=_params("parallel"),
        name=name,
    )(*args)


def _outproj_kernel(*refs, n_a):
    a_refs = refs[:n_a]
    w_ref, x_ref, g_ref, o_ref = refs[n_a:]
    n = o_ref.shape[1]
    a_vals = [a_ref[...] for a_ref in a_refs]
    for c in range(0, n, SUB_N):
        acc, row = None, 0
        for a in a_vals:
            part = _dot(a, w_ref[row:row + a.shape[1], c:c + SUB_N])
            acc = part if acc is None else acc + part
            row += a.shape[1]
        o_ref[:, c:c + SUB_N] = x_ref[:, c:c + SUB_N] + g_ref[0][:, c:c + SUB_N] * acc


def _outproj_call(a_list, w, x, modarr, gate_chunk, mod_row, *, tm=TM, name="outproj"):
    m, n = x.shape
    tm = min(tm, m)
    in_specs, args = [], []
    for a in a_list:
        in_specs.append(pl.BlockSpec((tm, a.shape[1]), lambda i: (i, 0)))
        args.append(a)
    in_specs += [
        pl.BlockSpec(w.shape, lambda i: (0, 0)),
        pl.BlockSpec((tm, n), lambda i: (i, 0)),
        pl.BlockSpec((1, 1, n), lambda i: (mod_row(i), 0, gate_chunk)),
    ]
    args += [w, x, modarr]
    return pl.pallas_call(
        functools.partial(_outproj_kernel, n_a=len(a_list)),
        grid=(m // tm,),
        in_specs=in_specs,
        out_specs=pl.BlockSpec((tm, n), lambda i: (i, 0)),
        out_shape=jax.ShapeDtypeStruct((m, n), jnp.float32),
        compiler_params=_params("parallel"),
        name=name,
    )(*args)


def _ffn_kernel(x_ref, xp_ref, xn_ref, g_ref, sh_ref, sc_ref, gt_ref, wa_ref, wb_ref,
                ca_ref, cb_ref, wo_ref, o_ref, h_ref, *, tm, hm, seq_len):
    i = pl.program_id(0)
    j = pl.program_id(1)
    nj = pl.num_programs(1)
    n_grp = tm // hm
    rows = hm + 2 * SUBLANES

    @pl.when(j == 0)
    def _():
        gain, shift, scale = g_ref[...], sh_ref[0], sc_ref[0]
        for r in range(n_grp):
            r0 = r * hm
            h_ref[r * rows:r * rows + hm, :] = _bf(
                _modulate(x_ref[r0:r0 + hm, :], gain, shift, scale, mxu_stats=True))
            nxt_src = x_ref[r0 + hm:r0 + hm + SUBLANES, :] if r + 1 < n_grp else xn_ref[...]
            prv_src = x_ref[r0 - SUBLANES:r0, :] if r > 0 else xp_ref[...]
            keep_next = jnp.where((i * tm + r0 + hm) % seq_len == 0, 0.0, 1.0)
            keep_prev = jnp.where((i * tm + r0) % seq_len == 0, 0.0, 1.0)
            nxt = keep_next * _modulate(nxt_src, gain, shift, scale)
            prv = keep_prev * _modulate(prv_src, gain, shift, scale)
            h_ref[r * rows + hm:(r + 1) * rows, :] = _bf(jnp.concatenate([nxt, prv], axis=0))
        o_ref[...] = jnp.zeros_like(o_ref)

    ca, cb = ca_ref[...], cb_ref[...]
    ups = []
    for r in range(n_grp):
        h = h_ref[r * rows:(r + 1) * rows, :]
        ups.append((_dot(h, wa_ref[...]), _dot(h, wb_ref[...])))

    def conv(u, cw):
        prev = pltpu.roll(u, 1, axis=0)[0:hm]
        nxt = pltpu.roll(u, rows - 1, axis=0)[0:hm]
        return prev * cw[0:1] + u[0:hm] * cw[1:2] + nxt * cw[2:3]

    for r, (ua, ub) in enumerate(ups):
        a = conv(ua, ca)
        b = conv(ub, cb)
        act = a * (1.0 / (1.0 + jnp.exp(-a))) * b
        o_ref[r * hm:(r + 1) * hm, :] += _dot(_bf(act), wo_ref[...])

    @pl.when(j == nj - 1)
    def _():
        o_ref[...] = x_ref[...] + gt_ref[0] * o_ref[...]


def _ffn_call(x, gain, modarr, w_in, conv_w, w_out, layer, mod_row, seq_len, *, tm=TM_FF,
              tn=TN_FF, name="conv_ffn"):
    m, d = x.shape
    tm = min(tm, m)
    hm = min(HM_FF, seq_len)
    assert m % tm == 0 and tm % hm == 0 and seq_len % hm == 0
    nj = D_FF_PAD // tn
    hb = tm // SUBLANES
    last_hb = m // SUBLANES - 1
    kern = functools.partial(_ffn_kernel, tm=tm, hm=hm, seq_len=seq_len)
    return pl.pallas_call(
        kern,
        grid=(m // tm, nj),
        in_specs=[
            pl.BlockSpec((tm, d), lambda i, j: (i, 0)),
            pl.BlockSpec((SUBLANES, d), lambda i, j: (jnp.maximum(i * hb - 1, 0), 0)),
            pl.BlockSpec((SUBLANES, d), lambda i, j: (jnp.minimum((i + 1) * hb, last_hb), 0)),
            pl.BlockSpec((1, d), lambda i, j: (0, 0)),
            pl.BlockSpec((1, 1, d), lambda i, j: (mod_row(i), 0, 3)),
            pl.BlockSpec((1, 1, d), lambda i, j: (mod_row(i), 0, 4)),
            pl.BlockSpec((1, 1, d), lambda i, j: (mod_row(i), 0, 5)),
            pl.BlockSpec((None, d, tn), lambda i, j: (layer, 0, j)),
            pl.BlockSpec((None, d, tn), lambda i, j: (layer, 0, nj + j)),
            pl.BlockSpec((None, 3, tn), lambda i, j: (layer, 0, j)),
            pl.BlockSpec((None, 3, tn), lambda i, j: (layer, 0, nj + j)),
            pl.BlockSpec((None, tn, d), lambda i, j: (layer, j, 0)),
        ],
        out_specs=pl.BlockSpec((tm, d), lambda i, j: (i, 0)),
        out_shape=jax.ShapeDtypeStruct((m, d), jnp.float32),
        scratch_shapes=[
            pltpu.VMEM(((tm // hm) * (hm + 2 * SUBLANES), d), jnp.bfloat16),
        ],
        compiler_params=_params("parallel", "arbitrary"),
        name=name,
    )(x, x, x, gain, modarr, modarr, modarr, w_in, w_in, conv_w, conv_w, w_out)


def _na_group_geometry(g):
    r0 = g * NA_GROUP_ROWS
    return r0, min(max(r0 - NA_WIN_H // 2, 0), GRID_H - NA_KEY_ROWS)


def _na_build_bias(rpb_ref, h, tc_scr, bias_scr):
    n_dr = 2 * NA_WIN_H - 1
    n_dc = 2 * NA_WIN_W - 1
    qc = lax.broadcasted_iota(jnp.int32, (GRID_W, GRID_W), 0)
    kc = lax.broadcasted_iota(jnp.int32, (GRID_W, GRID_W), 1)
    col0 = jnp.clip(qc - NA_WIN_W // 2, 0, GRID_W - NA_WIN_W)
    col_valid = (kc >= col0) & (kc < col0 + NA_WIN_W)
    delta = kc - qc + (NA_WIN_W - 1)
    base = h * (n_dr * n_dc)
    for dr in range(n_dr):
        acc = jnp.zeros((GRID_W, GRID_W), jnp.float32)
        for e in range(n_dc):
            acc = jnp.where(delta == e, rpb_ref[base + dr * n_dc + e] * LOG2E, acc)
        tc_scr[dr] = jnp.where(col_valid, acc, NEG)
    tc_scr[n_dr] = jnp.full((GRID_W, GRID_W), NEG, jnp.float32)
    n_groups = GRID_H // NA_GROUP_ROWS
    for t, g in enumerate((0, 1, n_groups - 1)):
        r0, ks = _na_group_geometry(g)
        for qr in range(NA_GROUP_ROWS):
            r = r0 + qr
            row0 = min(max(r - NA_WIN_H // 2, 0), GRID_H - NA_WIN_H)
            pieces = []
            for kr in range(NA_KEY_ROWS):
                kra = ks + kr
                inside = row0 <= kra < row0 + NA_WIN_H
                pieces.append(tc_scr[kra - r + (NA_WIN_H - 1)] if inside else tc_scr[n_dr])
            bias_scr[t, qr * GRID_W:(qr + 1) * GRID_W, :] = jnp.concatenate(pieces, axis=1)


def _na_kernel(rpb_ref, q_ref, k_ref, v_ref, kc_ref, vc_ref, o_ref, tc_scr, bias_scr):
    @pl.when(pl.program_id(1) == 0)
    def _():
        _na_build_bias(rpb_ref, pl.program_id(0), tc_scr, bias_scr)

    kc = kc_ref[...]
    vc = vc_ref[...]
    n_groups = GRID_H // NA_GROUP_ROWS

    def group(g):
        r0 = g * NA_GROUP_ROWS
        ks = jnp.clip(r0 - NA_WIN_H // 2, 0, GRID_H - NA_KEY_ROWS)
        q0 = pl.multiple_of(g * NA_Q, NA_Q)
        k0 = pl.multiple_of(ks * GRID_W, GRID_W)
        tb = jnp.where(g == 0, 0, jnp.where(g == n_groups - 1, 2, 1))
        q = q_ref[pl.ds(q0, NA_Q), :]
        kw = k_ref[pl.ds(k0, NA_K), :]
        vw = v_ref[pl.ds(k0, NA_K), :]
        sw = _dot_nt(q, kw) + bias_scr[tb]
        sc = _dot_nt(q, kc)
        mx = jnp.maximum(jnp.max(sw, axis=-1, keepdims=True), jnp.max(sc, axis=-1, keepdims=True))
        pw = jnp.exp2(sw - mx)
        pc = jnp.exp2(sc - mx)
        den = jnp.sum(pw, axis=-1, keepdims=True) + jnp.sum(pc, axis=-1, keepdims=True)
        o = _dot(_bf(pw), vw) + _dot(_bf(pc), vc)
        o_ref[pl.ds(q0, NA_Q), :] = _bf(o / den)

    def body(t, carry):
        for u in range(NA_UNROLL):
            group(t * NA_UNROLL + u)
        return carry

    lax.fori_loop(0, n_groups // NA_UNROLL, body, 0)


def _na_call(qkv, qkv_c, rpb):
    n_dr = 2 * NA_WIN_H - 1
    return pl.pallas_call(
        _na_kernel,
        grid=(NA_HEADS, BATCH),
        in_specs=[
            pl.BlockSpec(memory_space=pltpu.SMEM),
            pl.BlockSpec((SEQ, LANES), lambda h, b: (b, h)),
            pl.BlockSpec((SEQ, LANES), lambda h, b: (b, NA_HEADS + h)),
            pl.BlockSpec((SEQ, LANES), lambda h, b: (b, 2 * NA_HEADS + h)),
            pl.BlockSpec((CTX_LEN, LANES), lambda h, b: (b, NA_HEADS + h)),
            pl.BlockSpec((CTX_LEN, LANES), lambda h, b: (b, 2 * NA_HEADS + h)),
        ],
        out_specs=pl.BlockSpec((SEQ, LANES), lambda h, b: (b, h)),
        out_shape=jax.ShapeDtypeStruct((BATCH * SEQ, NA_WIDTH), jnp.bfloat16),
        scratch_shapes=[
            pltpu.VMEM((n_dr + 1, GRID_W, GRID_W), jnp.float32),
            pltpu.VMEM((3, NA_Q, NA_K), jnp.float32),
        ],
        compiler_params=_params("parallel", "arbitrary"),
        name="na_attn",
    )(rpb.reshape(-1), qkv, qkv, qkv, qkv_c, qkv_c)


def _key_chunks(src_rows, chunk):
    chunks, r = [], 0
    for rows in src_rows:
        size = min(chunk, rows)
        assert rows % size == 0
        chunks += [(r + c, size) for c in range(0, rows, size)]
        r += rows
    return chunks


def _pipelined_attention(streams, vt_scr, chunks, carry_in=None, defer_last=False):
    dv = vt_scr.shape[0]
    sub = streams[0][2].shape[1]
    n = len(streams)
    outs, prev = [], carry_in
    for u in range(n if defer_last else n + 1):
        cur = streams[u] if u < n else None
        m = jnp.full((1, sub), NEG, jnp.float32)
        l = jnp.zeros((1, sub), jnp.float32)
        acc = jnp.zeros((dv, sub), jnp.float32)
        for c0, cs in chunks:
            if cur is not None:
                q_t, k_scr, s_ref = cur
                s = _dot(k_scr[c0:c0 + cs, :], q_t)
                s_ref[c0:c0 + cs, :] = s
                m = jnp.maximum(m, jnp.max(s, axis=0, keepdims=True))
            if prev is not None:
                p = jnp.exp2(prev[0][c0:c0 + cs, :] - prev[1])
                l = l + jnp.sum(p, axis=0, keepdims=True)
                acc = acc + _dot(vt_scr[:, c0:c0 + cs], _bf(p))
        if prev is not None:
            outs.append(acc / l)
        prev = (cur[2], m) if cur is not None else None
    return outs, prev


def _transpose_bf16(x):
    return _bf(x.astype(jnp.float32).T)


def _run_streams(streams, vt_scr, chunks, m_scr, nq, write):
    i = pl.program_id(2)
    n = len(streams)
    if nq == 1:
        outs, _ = _pipelined_attention(streams, vt_scr, chunks)
        for u, o_t in enumerate(outs):
            write(0, u, o_t)
        return
    carried = streams[n - 1][2]

    def step(carry, defer):
        outs, last = _pipelined_attention(
            streams, vt_scr, chunks,
            carry_in=(carried, m_scr[...]) if carry else None, defer_last=defer)
        if carry:
            write(-1, n - 1, outs[0])
            outs = outs[1:]
        for u, o_t in enumerate(outs):
            write(0, u, o_t)
        if defer:
            m_scr[...] = last[1]

    pl.when(i == 0)(lambda: step(False, True))
    if nq > 2:
        pl.when((i > 0) & (i < nq - 1))(lambda: step(True, True))
    pl.when(i == nq - 1)(lambda: step(True, False))


def _attn_kernel(*refs, n_qparts, n_src, n_kparts, src_rows, nq):
    it = iter(refs)
    q_refs = [next(it) for _ in range(n_qparts)]
    srcs = []
    for _ in range(n_src):
        k_refs = [next(it) for _ in range(n_kparts)]
        srcs.append((k_refs, next(it)))
    o_ref = next(it)
    k_scr, vt_scr, m_scr = next(it), next(it), next(it)
    s_scrs = list(it)

    @pl.when(pl.program_id(2) == 0)
    def _():
        r = 0
        for (k_refs, v_ref), rows in zip(srcs, src_rows):
            for p, k_ref in enumerate(k_refs):
                k_scr[r:r + rows, p * LANES:(p + 1) * LANES] = k_ref[...]
            vt_scr[:, r:r + rows] = _transpose_bf16(v_ref[...])
            r += rows

    q = jnp.concatenate([qr[...] for qr in q_refs], axis=-1) if n_qparts > 1 else q_refs[0][...]
    sub = s_scrs[0].shape[1]
    tq = q.shape[0]
    streams = [(_transpose_bf16(q[u * sub:(u + 1) * sub]), k_scr, s_scrs[u % 2])
               for u in range(tq // sub)]

    def write(step_offset, u, o_t):
        row = pl.multiple_of((pl.program_id(2) + step_offset) * tq + u * sub, sub)
        o_ref[pl.ds(row, sub), :] = _bf(o_t.T)

    _run_streams(streams, vt_scr, _key_chunks(src_rows, KEY_CHUNK_ATTN), m_scr, nq, write)


def _attn_call(q_parts, sources, n_heads, lq, tq, dv, name):
    tq = min(tq, lq)
    sub = min(TQ_SUB, tq)
    nq = lq // tq
    in_specs, args = [], []
    for arr, cf in q_parts:
        in_specs.append(pl.BlockSpec((tq, LANES), lambda b, h, i, cf=cf: (b * nq + i, cf(h))))
        args.append(arr)
    src_rows = []
    n_kparts = len(sources[0][1])
    for rows, k_parts, (v_arr, vcf) in sources:
        src_rows.append(rows)
        for arr, cf in k_parts:
            in_specs.append(pl.BlockSpec((rows, LANES), lambda b, h, i, cf=cf: (b, cf(h))))
            args.append(arr)
        in_specs.append(pl.BlockSpec((rows, dv), lambda b, h, i, cf=vcf: (b, cf(h))))
        args.append(v_arr)
    nk = sum(src_rows)
    assert nq == 1 or (tq // sub) % 2 == 0
    kern = functools.partial(_attn_kernel, n_qparts=len(q_parts), n_src=len(sources),
                             n_kparts=n_kparts, src_rows=tuple(src_rows), nq=nq)
    return pl.pallas_call(
        kern,
        grid=(BATCH, n_heads, nq),
        in_specs=in_specs,
        out_specs=pl.BlockSpec((lq, dv), lambda b, h, i: (b, h)),
        out_shape=jax.ShapeDtypeStruct((BATCH * lq, n_heads * dv), jnp.bfloat16),
        scratch_shapes=[
            pltpu.VMEM((nk, n_kparts * LANES), jnp.bfloat16),
            pltpu.VMEM((dv, nk), jnp.bfloat16),
            pltpu.VMEM((1, sub), jnp.float32),
        ] + [pltpu.VMEM((nk, sub), jnp.float32)] * 2,
        compiler_params=_params("parallel", "parallel", "arbitrary"),
        name=name,
    )(*args)


def _diff_kernel(*refs, n_src, src_rows, lam_init, nq):
    it = iter(refs)
    q1_ref, q2_ref = next(it), next(it)
    srcs = [(next(it), next(it), next(it)) for _ in range(n_src)]
    lam_ref, sub_ref = next(it), next(it)
    o_ref = next(it)
    k1_scr, k2_scr, vt_scr, m_scr, o1_scr, s1_scr, s2_scr = [next(it) for _ in range(7)]

    @pl.when(pl.program_id(2) == 0)
    def _():
        r = 0
        for (k1_ref, k2_ref, v_ref), rows in zip(srcs, src_rows):
            k1_scr[r:r + rows, :] = k1_ref[...]
            k2_scr[r:r + rows, :] = k2_ref[...]
            vt_scr[:, r:r + rows] = _transpose_bf16(v_ref[...])
            r += rows

    lf = lam_ref[...]
    lam = (jnp.exp(jnp.sum(lf[0:1] * lf[1:2], axis=-1, keepdims=True))
           - jnp.exp(jnp.sum(lf[2:3] * lf[3:4], axis=-1, keepdims=True)) + lam_init)
    chunks = _key_chunks(src_rows, KEY_CHUNK_DIFF)
    sub = s1_scr.shape[1]
    tq = q1_ref.shape[0]
    streams = []
    for u in range(tq // sub):
        rows = slice(u * sub, (u + 1) * sub)
        streams.append((_transpose_bf16(q1_ref[rows, :]), k1_scr, s1_scr))
        streams.append((_transpose_bf16(q2_ref[rows, :]), k2_scr, s2_scr))
    n = len(streams)
    branch1 = {}

    def write(step_offset, u, o_t):
        if u % 2 == 0:
            branch1[u // 2] = o_t
            if nq > 1 and u == n - 2:
                o1_scr[...] = o_t
            return
        o1 = o1_scr[...] if step_offset < 0 else branch1[u // 2]
        o = (o1 - lam * o_t).T
        ms = jnp.mean(o * o, axis=-1, keepdims=True)
        row = pl.multiple_of((pl.program_id(2) + step_offset) * tq + (u // 2) * sub, sub)
        o_ref[pl.ds(row, sub), :] = _bf(
            o * lax.rsqrt(ms + EPS) * sub_ref[...] * (1.0 - lam_init))

    _run_streams(streams, vt_scr, chunks, m_scr, nq, write)


def _diff_call(qkv_q, lq, sources, diff_lambda, subln, lam_init, tq, name):
    tq = min(tq, lq)
    sub = min(TQ_SUB, tq)
    nq = lq // tq
    qb0 = 3 * NA_WIDTH // LANES
    kb0 = (3 * NA_WIDTH + DIFF_WIDTH) // LANES
    vb0 = (3 * NA_WIDTH + 2 * DIFF_WIDTH) // (2 * LANES)
    in_specs = [
        pl.BlockSpec((tq, LANES), lambda b, h, i: (b * nq + i, qb0 + 2 * h)),
        pl.BlockSpec((tq, LANES), lambda b, h, i: (b * nq + i, qb0 + 2 * h + 1)),
    ]
    args = [qkv_q, qkv_q]
    src_rows = []
    for rows, arr in sources:
        src_rows.append(rows)
        in_specs += [
            pl.BlockSpec((rows, LANES), lambda b, h, i: (b, kb0 + 2 * h)),
            pl.BlockSpec((rows, LANES), lambda b, h, i: (b, kb0 + 2 * h + 1)),
            pl.BlockSpec((rows, 2 * LANES), lambda b, h, i: (b, vb0 + h)),
        ]
        args += [arr, arr, arr]
    in_specs += [
        pl.BlockSpec((4, LANES), lambda b, h, i: (0, 0)),
        pl.BlockSpec((1, 2 * LANES), lambda b, h, i: (0, 0)),
    ]
    args += [diff_lambda, subln]
    nk = sum(src_rows)
    kern = functools.partial(_diff_kernel, n_src=len(sources), src_rows=tuple(src_rows),
                             lam_init=lam_init, nq=nq)
    return pl.pallas_call(
        kern,
        grid=(BATCH, DIFF_HEADS, nq),
        in_specs=in_specs,
        out_specs=pl.BlockSpec((lq, 2 * LANES), lambda b, h, i: (b, h)),
        out_shape=jax.ShapeDtypeStruct((BATCH * lq, DIFF_WIDTH), jnp.bfloat16),
        scratch_shapes=[
            pltpu.VMEM((nk, LANES), jnp.bfloat16),
            pltpu.VMEM((nk, LANES), jnp.bfloat16),
            pltpu.VMEM((2 * LANES, nk), jnp.bfloat16),
            pltpu.VMEM((1, sub), jnp.float32),
            pltpu.VMEM((2 * LANES, sub), jnp.float32),
            pltpu.VMEM((nk, sub), jnp.float32),
            pltpu.VMEM((nk, sub), jnp.float32),
        ],
        compiler_params=_params("parallel", "parallel", "arbitrary"),
        name=name,
    )(*args)


def _rope_tables(d):
    h = d // 2
    half = h // 2
    assert GRID_H == GRID_W
    lane = jnp.arange(PAIR, dtype=jnp.int32)[None, :] % LANES
    freqs = ROPE_BASE ** (-(lane % half).astype(jnp.float32) / half)
    ang = jnp.arange(GRID_W, dtype=jnp.float32)[:, None] * freqs
    by_row = lane < h

    def table(f, pad):
        one_axis = f(ang)
        full = jnp.where(by_row, one_axis[:, None, :], one_axis[None, :, :])
        return jnp.where(lane < d, full, pad).reshape(SEQ, PAIR)

    cos, sin = table(jnp.cos, 1.0), table(jnp.sin, 0.0)
    first = (lane % h) < half
    sin_a = jnp.where(first, -sin, 0.0)
    sin_b = jnp.where(first, 0.0, sin)
    return (cos, sin_a, sin_b), half


def _pair_blockdiag(block):
    z = np.zeros_like(block)
    return jnp.asarray(np.block([[block, z], [z, block]]), dtype=jnp.bfloat16)


def _pad_lanes(v, fill=0.0):
    return jnp.pad(v, (0, LANES - v.shape[0]), constant_values=fill)


def _pair(v):
    return jnp.concatenate([v, v])


def _lat_row(tm):
    bpb = SEQ // tm
    return lambda i: i // bpb


def _ctx_row(i):
    return CTX_MOD_ROW


def _even_layer(x, xc, modarr, norm_mix, norm_ffn, ffn_w, w_in, w_out, qn_a, kn_a, rpb,
                qn_b, kn_b, diff_lambda, subln, lam_init, with_ctx):
    scale = HEAD_DIM ** -0.5 * LOG2E
    gains = jnp.stack([_pair(g) for g in (qn_a * scale, kn_a, qn_b * scale, kn_b)]
                      + [jnp.ones((PAIR,), jnp.float32)] * 4)
    rope = _rope_tables(HEAD_DIM)
    segs = [(True, 0, HEAD_DIM, False), (True, 1, HEAD_DIM, False), (False, 0, HEAD_DIM, False),
            (True, 2, HEAD_DIM, True), (True, 3, HEAD_DIM, True), (False, 0, HEAD_DIM, False)]
    w_in = _bf(w_in)
    w_out = _bf(w_out)
    gain_mix = norm_mix[None, :]
    lat_row = _lat_row(TM)

    qkv = _proj_call(x, w_in, gains, segs, mod=(gain_mix, modarr, 0, 1), rope=rope,
                     mod_row=_lat_row(TM_EVEN), tm=TM_EVEN, name="even_proj")
    qkv_c = _proj_call(xc, w_in, gains, segs, mod=(gain_mix, modarr, 0, 1), rope=None,
                       mod_row=_ctx_row, tm=TM_EVEN, name="even_proj_ctx")

    oa = _na_call(qkv, qkv_c, rpb)
    ob = _diff_call(qkv, SEQ, [(SEQ, qkv), (CTX_LEN, qkv_c)], diff_lambda, subln[None, :],
                    lam_init, TQ_DIFF, "diff_attn")
    x = _outproj_call([oa, ob], w_out, x, modarr, 2, lat_row, name="even_out")
    x = _ffn_call(x, norm_ffn[None, :], modarr, *ffn_w, _lat_row(TM_FF), SEQ)
    if with_ctx:
        oa_c = _attn_call(
            [(qkv_c, lambda h: h)],
            [(CTX_LEN, [(qkv_c, lambda h: NA_HEADS + h)], (qkv_c, lambda h: 2 * NA_HEADS + h))],
            NA_HEADS, CTX_LEN, CTX_LEN, HEAD_DIM, "ctx_attn")
        ob_c = _diff_call(qkv_c, CTX_LEN, [(CTX_LEN, qkv_c)], diff_lambda, subln[None, :],
                          lam_init, CTX_LEN, "diff_attn_ctx")
        xc = _outproj_call([oa_c, ob_c], w_out, xc, modarr, 2, _ctx_row, name="even_out_ctx")
        xc = _ffn_call(xc, norm_ffn[None, :], modarr, *ffn_w, _ctx_row, CTX_LEN,
                       name="conv_ffn_ctx")
    return x, xc


def _odd_layer(x, xc, modarr, norm_mix, norm_ffn, ffn_w, w_down, q_a_norm, kv_a_norm, w_uq,
               w_ukv, qn_nope, qn_rope, kn_nope, kn_rope, w_out, with_ctx):
    scale = (MLA_NOPE + MLA_ROPE) ** -0.5 * LOG2E
    rope_pair = _rope_tables(MLA_ROPE)
    gain_mix = norm_mix[None, :]
    lat_row = _lat_row(TM)

    w_down_p = _bf(jnp.pad(w_down, ((0, 0), (0, MLA_DOWN_PAD - MLA_DOWN))))
    zeros_r = jnp.zeros((MLA_Q_RANK,), jnp.float32)
    kr_gain = jnp.concatenate([_pad_lanes(kn_rope), zeros_r[:MLA_Q_RANK - LANES]])
    gains_d = jnp.stack([q_a_norm, kv_a_norm, kr_gain] + [zeros_r] * 5)
    wq = w_uq.reshape(MLA_Q_RANK, MLA_HEADS, MLA_NOPE + MLA_ROPE)
    wq_rope = jnp.pad(wq[:, :, MLA_NOPE:], ((0, 0), (0, 0), (0, LANES - MLA_ROPE)))
    wq_p = _bf(jnp.concatenate([wq[:, :, :MLA_NOPE].reshape(MLA_Q_RANK, -1),
                                wq_rope.reshape(MLA_Q_RANK, -1)], -1))
    wkv = w_ukv.reshape(MLA_KV_RANK, MLA_HEADS, MLA_NOPE + MLA_V)
    wkv_p = _bf(jnp.concatenate([wkv[:, :, :MLA_NOPE].reshape(MLA_KV_RANK, -1),
                                 wkv[:, :, MLA_NOPE:].reshape(MLA_KV_RANK, -1)], -1))
    gains_p = jnp.stack([_pair(qn_nope * scale), _pair(_pad_lanes(qn_rope) * scale),
                         _pair(kn_nope)] + [jnp.zeros((PAIR,), jnp.float32)] * 5)
    mod_in = (gain_mix, modarr, 0, 1)
    q, kv, kr = _mla_proj_call(x, w_down_p, wq_p, wkv_p, gains_d, gains_p, mod_in, rope_pair,
                               _lat_row(TM_MLA_PROJ), True, "mla_proj")
    kv_c, kr_c = _mla_proj_call(xc, w_down_p, wq_p, wkv_p, gains_d, gains_p, mod_in, None,
                                _ctx_row, False, "mla_proj_ctx")

    def src(rows, kv_arr, kr_arr):
        return (rows, [(kv_arr, lambda h: h), (kr_arr, lambda h: 0)],
                (kv_arr, lambda h: MLA_HEADS + h))

    o = _attn_call([(q, lambda h: h), (q, lambda h: MLA_HEADS + h)],
                   [src(SEQ, kv, kr), src(CTX_LEN, kv_c, kr_c)],
                   MLA_HEADS, SEQ, TQ_MLA, MLA_V, "mla_attn")
    x = _outproj_call([o], _bf(w_out), x, modarr, 2, lat_row, name="mla_out")
    x = _ffn_call(x, norm_ffn[None, :], modarr, *ffn_w, _lat_row(TM_FF), SEQ)
    assert not with_ctx
    return x, xc


def _mla_proj_kernel(x_ref, g_ref, sh_ref, sc_ref, wd_ref, wq_ref, wkv_ref, gd_ref, gp_ref,
                     ones_ref, *rest, rope, rope_half, with_q):
    rope_refs = rest[:3] if rope else None
    outs = rest[3:] if rope else rest
    if with_q:
        q_ref, kv_ref, kr_ref = outs
    else:
        kv_ref, kr_ref = outs
    h = _bf(_modulate(x_ref[...], g_ref[...], sh_ref[0], sc_ref[0]))
    down = _dot(h, wd_ref[...])

    def latent_norm(xs, row, nvalid):
        ms = jnp.sum(xs * xs, axis=-1, keepdims=True) * (1.0 / nvalid)
        return xs * lax.rsqrt(ms + EPS) * gd_ref[row:row + 1, 0:xs.shape[1]]

    kva = _bf(latent_norm(down[:, MLA_Q_RANK:MLA_Q_RANK + MLA_KV_RANK], 1, MLA_KV_RANK))
    kr = latent_norm(down[:, MLA_Q_RANK + MLA_KV_RANK:MLA_DOWN_PAD], 2, MLA_ROPE)
    if rope:
        cos_ref, sa_ref, sb_ref = rope_refs
        kr = (kr * cos_ref[:, 0:LANES]
              + pltpu.roll(kr, LANES - rope_half, axis=1) * sa_ref[:, 0:LANES]
              + pltpu.roll(kr, rope_half, axis=1) * sb_ref[:, 0:LANES])
    kr_ref[...] = _bf(kr)

    _head_projection(kva, wkv_ref, kv_ref,
                     [(True, 2, MLA_NOPE, False), (False, 0, MLA_V, False)],
                     gp_ref, ones_ref, rope_refs, rope_half)
    if with_q:
        qa = _bf(latent_norm(down[:, 0:MLA_Q_RANK], 0, MLA_Q_RANK))
        _head_projection(qa, wq_ref, q_ref,
                         [(True, 0, MLA_NOPE, False), (True, 1, MLA_ROPE, True)],
                         gp_ref, ones_ref, rope_refs, rope_half)


def _mla_proj_call(x, wd, wq, wkv, gains_d, gains_p, mod, rope, mod_row, with_q, name,
                   tm=TM_MLA_PROJ):
    m, k = x.shape
    tm = min(tm, m)
    bpb = SEQ // tm
    gain, modarr, sh_c, sc_c = mod
    ones = _pair_blockdiag(np.ones((LANES, LANES), np.float32))
    in_specs = [
        pl.BlockSpec((tm, k), lambda i: (i, 0)),
        pl.BlockSpec((1, k), lambda i: (0, 0)),
        pl.BlockSpec((1, 1, k), lambda i: (mod_row(i), 0, sh_c)),
        pl.BlockSpec((1, 1, k), lambda i: (mod_row(i), 0, sc_c)),
    ]
    args = [x, gain, modarr, modarr]
    for w in (wd, wq, wkv, gains_d, gains_p, ones):
        in_specs.append(pl.BlockSpec(w.shape, lambda i: (0, 0), pipeline_mode=pl.Buffered(1)))
        args.append(w)
    rope_half = 0
    if rope is not None:
        tables, rope_half = rope
        for t in tables:
            in_specs.append(pl.BlockSpec((tm, PAIR), lambda i: (i % bpb, 0)))
            args.append(t)
    widths = ([wq.shape[1]] if with_q else []) + [wkv.shape[1], LANES]
    return pl.pallas_call(
        functools.partial(_mla_proj_kernel, rope=rope is not None, rope_half=rope_half,
                          with_q=with_q),
        grid=(m // tm,),
        in_specs=in_specs,
        out_specs=[pl.BlockSpec((tm, n), lambda i: (i, 0)) for n in widths],
        out_shape=[jax.ShapeDtypeStruct((m, n), jnp.bfloat16) for n in widths],
        compiler_params=_params("parallel"),
        name=name,
    )(*args)


def _w_in_prep_kernel(x_ref, o_ref):
    pad = jnp.zeros((x_ref.shape[0], D_FF_PAD - D_FF), jnp.bfloat16)
    o_ref[:, 0:D_FF] = _bf(x_ref[:, 0:D_FF])
    o_ref[:, D_FF:D_FF_PAD] = pad
    o_ref[:, D_FF_PAD:D_FF_PAD + D_FF] = _bf(x_ref[:, D_FF:2 * D_FF])
    o_ref[:, D_FF_PAD + D_FF:2 * D_FF_PAD] = pad


def _w_out_prep_kernel(x_ref, o_ref):
    o_ref[0:D_FF, :] = _bf(x_ref[...])
    o_ref[D_FF:D_FF_PAD, :] = jnp.zeros((D_FF_PAD - D_FF, x_ref.shape[1]), jnp.bfloat16)


def _ffn_weights(w_in, conv_w, w_out):
    depth, d, _ = w_in.shape
    rows, cols = PREP_ROWS, PREP_COLS
    w_in_p = pl.pallas_call(
        _w_in_prep_kernel,
        grid=(depth, d // rows),
        in_specs=[pl.BlockSpec((None, rows, 2 * D_FF), lambda l, i: (l, i, 0))],
        out_specs=pl.BlockSpec((None, rows, 2 * D_FF_PAD), lambda l, i: (l, i, 0)),
        out_shape=jax.ShapeDtypeStruct((depth, d, 2 * D_FF_PAD), jnp.bfloat16),
        compiler_params=_params("parallel", "parallel"),
        name="ffn_w_in_prep",
    )(w_in)
    w_out_p = pl.pallas_call(
        _w_out_prep_kernel,
        grid=(depth, d // cols),
        in_specs=[pl.BlockSpec((None, D_FF, cols), lambda l, j: (l, 0, j))],
        out_specs=pl.BlockSpec((None, D_FF_PAD, cols), lambda l, j: (l, 0, j)),
        out_shape=jax.ShapeDtypeStruct((depth, D_FF_PAD, d), jnp.bfloat16),
        compiler_params=_params("parallel", "parallel"),
        name="ffn_w_out_prep",
    )(w_out)
    pad = D_FF_PAD - D_FF
    zc = jnp.zeros(conv_w.shape[:2] + (pad,), conv_w.dtype)
    conv_p = jnp.concatenate([conv_w[..., :D_FF], zc, conv_w[..., D_FF:], zc], -1)
    return w_in_p, conv_p, w_out_p


def kernel(x, c, ctx, c_ctx, ada_w, ada_b, norm_mix, norm_ffn, ffn_w_in, ffn_conv, ffn_w_out, even_w_in, even_w_out, na_q_norm, na_k_norm, na_rpb, diff_q_norm, diff_k_norm, diff_lambda, diff_subln, mla_w_down, mla_q_a_norm, mla_kv_a_norm, mla_w_uq, mla_w_ukv, mla_q_nope_norm, mla_q_rope_norm, mla_k_nope_norm, mla_k_rope_norm, mla_w_out):
    cond = jnp.concatenate(
        [c, c_ctx[None, :], jnp.zeros((MOD_ROWS - BATCH - 1, D_MODEL), jnp.float32)], 0)
    mod = _ada_call(cond, ada_w, ada_b)
    xl = x.reshape(BATCH * SEQ, D_MODEL)
    xc = ctx.reshape(BATCH * CTX_LEN, D_MODEL)
    ffn_all = _ffn_weights(ffn_w_in, ffn_conv, ffn_w_out)
    for l in range(DEPTH):
        with_ctx = l < DEPTH - 1
        modarr = mod[l].reshape(MOD_ROWS, 1, 6 * D_MODEL)
        ffn_w = ffn_all + (l,)
        i = l // 2
        if l % 2 == 0:
            lam_init = 0.8 - 0.6 * math.exp(-0.3 * l)
            xl, xc = _even_layer(xl, xc, modarr, norm_mix[l], norm_ffn[l], ffn_w, even_w_in[i],
                                 even_w_out[i], na_q_norm[i], na_k_norm[i], na_rpb[i],
                                 diff_q_norm[i], diff_k_norm[i], diff_lambda[i], diff_subln[i],
                                 lam_init, with_ctx)
        else:
            xl, xc = _odd_layer(xl, xc, modarr, norm_mix[l], norm_ffn[l], ffn_w, mla_w_down[i],
                                mla_q_a_norm[i], mla_kv_a_norm[i], mla_w_uq[i], mla_w_ukv[i],
                                mla_q_nope_norm[i], mla_q_rope_norm[i], mla_k_nope_norm[i],
                                mla_k_rope_norm[i], mla_w_out[i], with_ctx)
    return xl.reshape(BATCH, SEQ, D_MODEL)
```

```python
import functools
import math

import numpy as np
import jax
import jax.numpy as jnp
from jax import lax
from jax.experimental import pallas as pl
from jax.experimental.pallas import tpu as pltpu

D_MODEL = 2048
BATCH = 4
SEQ = 4096
DEPTH = 2
GRID_W = 64
GRID_H = SEQ // GRID_W
CTX_LEN = 256
HEAD_DIM = 128
NA_HEADS = 8
NA_WIN_H = 8
NA_WIN_W = 16
DIFF_HEADS = 4
NA_WIDTH = 1024
DIFF_WIDTH = 1024
EVEN_PROJ = 6144
MLA_HEADS = 16
MLA_Q_RANK = 512
MLA_KV_RANK = 512
MLA_NOPE = 128
MLA_ROPE = 64
MLA_V = 128
MLA_DOWN = 1088
D_FF = 5504
ROPE_BASE = 10000.0
EPS = 1e-6

LANES = 128
SUBLANES = 8
PAIR = 2 * LANES
VMEM_LIMIT = 56 * 1024 * 1024
MOD_ROWS = 8
CTX_MOD_ROW = BATCH
NEG = -1e30
LOG2E = math.log2(math.e)

TM = 512
TM_EVEN = 512
TM_MLA_PROJ = 512
SUB_N = 512
TN_ADA = 1024
MLA_DOWN_PAD = 1152
TN_FF = 512
D_FF_PAD = -(-D_FF // TN_FF) * TN_FF
TM_FF = 1024
HM_FF = 512
PREP_ROWS = 256
PREP_COLS = 256
NA_GROUP_ROWS = 4
NA_KEY_ROWS = NA_GROUP_ROWS + NA_WIN_H
NA_UNROLL = 4
NA_Q = NA_GROUP_ROWS * GRID_W
NA_K = NA_KEY_ROWS * GRID_W
TQ_MLA = 1024
TQ_DIFF = 512
TQ_SUB = 256
KEY_CHUNK_ATTN = 512
KEY_CHUNK_DIFF = 512

_NT = (((1,), (1,)), ((), ()))


def _params(*sem):
    return pltpu.CompilerParams(dimension_semantics=sem, vmem_limit_bytes=VMEM_LIMIT)


def _bf(x):
    return x.astype(jnp.bfloat16)


def _dot(a, b):
    return jnp.dot(a, b, preferred_element_type=jnp.float32)


def _dot_nt(a, b):
    return lax.dot_general(a, b, _NT, preferred_element_type=jnp.float32)


def _modulate(x, gain, shift, scale, mxu_stats=False):
    d = x.shape[1]
    if mxu_stats:
        ss = _dot(_bf(x * x), jnp.ones((d, LANES), jnp.bfloat16))
        r = jnp.tile(lax.rsqrt(ss * (1.0 / d) + EPS), (1, d // LANES))
    else:
        r = lax.rsqrt(jnp.mean(x * x, axis=-1, keepdims=True) + EPS)
    return x * r * (gain * (1.0 + scale)) + shift


def _ada_kernel(c_ref, w_ref, b_ref, o_ref):
    c = c_ref[...]
    s = c * (1.0 / (1.0 + jnp.exp(-c)))
    o_ref[0] = _dot(_bf(s), _bf(w_ref[0])) + b_ref[0]


def _ada_call(cond, ada_w, ada_b):
    tn = TN_ADA
    n = ada_w.shape[-1]
    return pl.pallas_call(
        _ada_kernel,
        grid=(DEPTH, n // tn),
        in_specs=[
            pl.BlockSpec((MOD_ROWS, D_MODEL), lambda l, j: (0, 0)),
            pl.BlockSpec((1, D_MODEL, tn), lambda l, j: (l, 0, j)),
            pl.BlockSpec((1, 1, tn), lambda l, j: (l, 0, j)),
        ],
        out_specs=pl.BlockSpec((1, MOD_ROWS, tn), lambda l, j: (l, 0, j)),
        out_shape=jax.ShapeDtypeStruct((DEPTH, MOD_ROWS, n), jnp.float32),
        compiler_params=_params("parallel", "parallel"),
        name="ada_mod",
    )(cond, ada_w, ada_b.reshape(DEPTH, 1, n))


def _pair_epilogue(acc, spec, gains_ref, ones_ref, rope_refs, rope_half):
    norm, gain_row, nvalid, do_rope = spec
    y = acc
    if norm:
        ss = _dot(_bf(acc * acc), ones_ref[...])
        y = acc * lax.rsqrt(ss * (1.0 / nvalid) + EPS) * gains_ref[gain_row:gain_row + 1, :]
    if do_rope and rope_refs is not None:
        cos_ref, sa_ref, sb_ref = rope_refs
        y = (y * cos_ref[...]
             + pltpu.roll(y, PAIR - rope_half, axis=1) * sa_ref[...]
             + pltpu.roll(y, rope_half, axis=1) * sb_ref[...])
    return y


def _head_projection(a, w_ref, o_ref, segs, gains_ref, ones_ref, rope_refs, rope_half):
    width = w_ref.shape[1] // len(segs)
    for c in range(0, w_ref.shape[1], SUB_N):
        acc = _dot(a, w_ref[:, c:c + SUB_N])
        for p in range(0, SUB_N, PAIR):
            y = _pair_epilogue(acc[:, p:p + PAIR], segs[c // width], gains_ref, ones_ref,
                               rope_refs, rope_half)
            o_ref[:, c + p:c + p + PAIR] = _bf(y)


def _proj_kernel(x_ref, g_ref, sh_ref, sc_ref, w_ref, gains_ref, ones_ref, *rest, segs, rope,
                 rope_half):
    rope_refs = rest[:3] if rope else None
    o_ref = rest[-1]
    h = _bf(_modulate(x_ref[...], g_ref[...], sh_ref[0], sc_ref[0], mxu_stats=True))
    _head_projection(h, w_ref, o_ref, segs, gains_ref, ones_ref, rope_refs, rope_half)


def _proj_call(x, w, gains, segs, *, mod, rope=None, mod_row=None, tm=TM, name="proj"):
    m, k = x.shape
    n = w.shape[1]
    tm = min(tm, m)
    assert m % tm == 0 and n % (len(segs) * SUB_N) == 0
    bpb = SEQ // tm
    gain, modarr, sh_c, sc_c = mod
    ones = _pair_blockdiag(np.ones((LANES, LANES), np.float32))
    in_specs = [
        pl.BlockSpec((tm, k), lambda i: (i, 0)),
        pl.BlockSpec((1, k), lambda i: (0, 0)),
        pl.BlockSpec((1, 1, k), lambda i: (mod_row(i), 0, sh_c)),
        pl.BlockSpec((1, 1, k), lambda i: (mod_row(i), 0, sc_c)),
    ]
    args = [x, gain, modarr, modarr]
    for const in (w, gains, ones):
        in_specs.append(pl.BlockSpec(const.shape, lambda i: (0, 0), pipeline_mode=pl.Buffered(1)))
        args.append(const)
    rope_half = 0
    if rope is not None:
        tables, rope_half = rope
        for t in tables:
            in_specs.append(pl.BlockSpec((tm, PAIR), lambda i: (i % bpb, 0)))
            args.append(t)
    kern = functools.partial(_proj_kernel, segs=tuple(segs), rope=rope is not None,
                             rope_half=rope_half)
    return pl.pallas_call(
        kern,
        grid=(m // tm,),
        in_specs=in_specs,
        out_specs=pl.BlockSpec((tm, n), lambda i: (i, 0)),
        out_shape=jax.ShapeDtypeStruct((m, n), jnp.bfloat16),
        compiler_params=_params("parallel"),
        name=name,
    )(*args)


def _outproj_kernel(*refs, n_a):
    a_refs = refs[:n_a]
    w_ref, x_ref, g_ref, o_ref = refs[n_a:]
    n = o_ref.shape[1]
    a_vals = [a_ref[...] for a_ref in a_refs]
    for c in range(0, n, SUB_N):
        acc, row = None, 0
        for a in a_vals:
            part = _dot(a, w_ref[row:row + a.shape[1], c:c + SUB_N])
            acc = part if acc is None else acc + part
            row += a.shape[1]
        o_ref[:, c:c + SUB_N] = x_ref[:, c:c + SUB_N] + g_ref[0][:, c:c + SUB_N] * acc


def _outproj_call(a_list, w, x, modarr, gate_chunk, mod_row, *, tm=TM, name="outproj"):
    m, n = x.shape
    tm = min(tm, m)
    in_specs, args = [], []
    for a in a_list:
        in_specs.append(pl.BlockSpec((tm, a.shape[1]), lambda i: (i, 0)))
        args.append(a)
    in_specs += [
        pl.BlockSpec(w.shape, lambda i: (0, 0)),
        pl.BlockSpec((tm, n), lambda i: (i, 0)),
        pl.BlockSpec((1, 1, n), lambda i: (mod_row(i), 0, gate_chunk)),
    ]
    args += [w, x, modarr]
    return pl.pallas_call(
        functools.partial(_outproj_kernel, n_a=len(a_list)),
        grid=(m // tm,),
        in_specs=in_specs,
        out_specs=pl.BlockSpec((tm, n), lambda i: (i, 0)),
        out_shape=jax.ShapeDtypeStruct((m, n), jnp.float32),
        compiler_params=_params("parallel"),
        name=name,
    )(*args)


def _ffn_kernel(x_ref, xp_ref, xn_ref, g_ref, sh_ref, sc_ref, gt_ref, wa_ref, wb_ref,
                ca_ref, cb_ref, wo_ref, o_ref, h_ref, *, tm, hm, seq_len):
    i = pl.program_id(0)
    j = pl.program_id(1)
    nj = pl.num_programs(1)
    n_grp = tm // hm
    rows = hm + 2 * SUBLANES

    @pl.when(j == 0)
    def _():
        gain, shift, scale = g_ref[...], sh_ref[0], sc_ref[0]
        for r in range(n_grp):
            r0 = r * hm
            h_ref[r * rows:r * rows + hm, :] = _bf(
                _modulate(x_ref[r0:r0 + hm, :], gain, shift, scale, mxu_stats=True))
            nxt_src = x_ref[r0 + hm:r0 + hm + SUBLANES, :] if r + 1 < n_grp else xn_ref[...]
            prv_src = x_ref[r0 - SUBLANES:r0, :] if r > 0 else xp_ref[...]
            keep_next = jnp.where((i * tm + r0 + hm) % seq_len == 0, 0.0, 1.0)
            keep_prev = jnp.where((i * tm + r0) % seq_len == 0, 0.0, 1.0)
            nxt = keep_next * _modulate(nxt_src, gain, shift, scale)
            prv = keep_prev * _modulate(prv_src, gain, shift, scale)
            h_ref[r * rows + hm:(r + 1) * rows, :] = _bf(jnp.concatenate([nxt, prv], axis=0))
        o_ref[...] = jnp.zeros_like(o_ref)

    ca, cb = ca_ref[...], cb_ref[...]
    ups = []
    for r in range(n_grp):
        h = h_ref[r * rows:(r + 1) * rows, :]
        ups.append((_dot(h, wa_ref[...]), _dot(h, wb_ref[...])))

    def conv(u, cw):
        prev = pltpu.roll(u, 1, axis=0)[0:hm]
        nxt = pltpu.roll(u, rows - 1, axis=0)[0:hm]
        return prev * cw[0:1] + u[0:hm] * cw[1:2] + nxt * cw[2:3]

    for r, (ua, ub) in enumerate(ups):
        a = conv(ua, ca)
        b = conv(ub, cb)
        act = a * (1.0 / (1.0 + jnp.exp(-a))) * b
        o_ref[r * hm:(r + 1) * hm, :] += _dot(_bf(act), wo_ref[...])

    @pl.when(j == nj - 1)
    def _():
        o_ref[...] = x_ref[...] + gt_ref[0] * o_ref[...]


def _ffn_call(x, gain, modarr, w_in, conv_w, w_out, layer, mod_row, seq_len, *, tm=TM_FF,
              tn=TN_FF, name="conv_ffn"):
    m, d = x.shape
    tm = min(tm, m)
    hm = min(HM_FF, seq_len)
    assert m % tm == 0 and tm % hm == 0 and seq_len % hm == 0
    nj = D_FF_PAD // tn
    hb = tm // SUBLANES
    last_hb = m // SUBLANES - 1
    kern = functools.partial(_ffn_kernel, tm=tm, hm=hm, seq_len=seq_len)
    return pl.pallas_call(
        kern,
        grid=(m // tm, nj),
        in_specs=[
            pl.BlockSpec((tm, d), lambda i, j: (i, 0)),
            pl.BlockSpec((SUBLANES, d), lambda i, j: (jnp.maximum(i * hb - 1, 0), 0)),
            pl.BlockSpec((SUBLANES, d), lambda i, j: (jnp.minimum((i + 1) * hb, last_hb), 0)),
            pl.BlockSpec((1, d), lambda i, j: (0, 0)),
            pl.BlockSpec((1, 1, d), lambda i, j: (mod_row(i), 0, 3)),
            pl.BlockSpec((1, 1, d), lambda i, j: (mod_row(i), 0, 4)),
            pl.BlockSpec((1, 1, d), lambda i, j: (mod_row(i), 0, 5)),
            pl.BlockSpec((None, d, tn), lambda i, j: (layer, 0, j)),
            pl.BlockSpec((None, d, tn), lambda i, j: (layer, 0, nj + j)),
            pl.BlockSpec((None, 3, tn), lambda i, j: (layer, 0, j)),
            pl.BlockSpec((None, 3, tn), lambda i, j: (layer, 0, nj + j)),
            pl.BlockSpec((None, tn, d), lambda i, j: (layer, j, 0)),
        ],
        out_specs=pl.BlockSpec((tm, d), lambda i, j: (i, 0)),
        out_shape=jax.ShapeDtypeStruct((m, d), jnp.float32),
        scratch_shapes=[
            pltpu.VMEM(((tm // hm) * (hm + 2 * SUBLANES), d), jnp.bfloat16),
        ],
        compiler_params=_params("parallel", "arbitrary"),
        name=name,
    )(x, x, x, gain, modarr, modarr, modarr, w_in, w_in, conv_w, conv_w, w_out)


def _na_group_geometry(g):
    r0 = g * NA_GROUP_ROWS
    return r0, min(max(r0 - NA_WIN_H // 2, 0), GRID_H - NA_KEY_ROWS)


def _na_build_bias(rpb_ref, h, tc_scr, bias_scr):
    n_dr = 2 * NA_WIN_H - 1
    n_dc = 2 * NA_WIN_W - 1
    qc = lax.broadcasted_iota(jnp.int32, (GRID_W, GRID_W), 0)
    kc = lax.broadcasted_iota(jnp.int32, (GRID_W, GRID_W), 1)
    col0 = jnp.clip(qc - NA_WIN_W // 2, 0, GRID_W - NA_WIN_W)
    col_valid = (kc >= col0) & (kc < col0 + NA_WIN_W)
    delta = kc - qc + (NA_WIN_W - 1)
    base = h * (n_dr * n_dc)
    for dr in range(n_dr):
        acc = jnp.zeros((GRID_W, GRID_W), jnp.float32)
        for e in range(n_dc):
            acc = jnp.where(delta == e, rpb_ref[base + dr * n_dc + e] * LOG2E, acc)
        tc_scr[dr] = jnp.where(col_valid, acc, NEG)
    tc_scr[n_dr] = jnp.full((GRID_W, GRID_W), NEG, jnp.float32)
    n_groups = GRID_H // NA_GROUP_ROWS
    for t, g in enumerate((0, 1, n_groups - 1)):
        r0, ks = _na_group_geometry(g)
        for qr in range(NA_GROUP_ROWS):
            r = r0 + qr
            row0 = min(max(r - NA_WIN_H // 2, 0), GRID_H - NA_WIN_H)
            pieces = []
            for kr in range(NA_KEY_ROWS):
                kra = ks + kr
                inside = row0 <= kra < row0 + NA_WIN_H
                pieces.append(tc_scr[kra - r + (NA_WIN_H - 1)] if inside else tc_scr[n_dr])
            bias_scr[t, qr * GRID_W:(qr + 1) * GRID_W, :] = jnp.concatenate(pieces, axis=1)


def _na_kernel(rpb_ref, q_ref, k_ref, v_ref, kc_ref, vc_ref, o_ref, tc_scr, bias_scr):
    @pl.when(pl.program_id(1) == 0)
    def _():
        _na_build_bias(rpb_ref, pl.program_id(0), tc_scr, bias_scr)

    kc = kc_ref[...]
    vc = vc_ref[...]
    n_groups = GRID_H // NA_GROUP_ROWS

    def offsets(g):
        ks = jnp.clip(g * NA_GROUP_ROWS - NA_WIN_H // 2, 0, GRID_H - NA_KEY_ROWS)
        return pl.multiple_of(g * NA_Q, NA_Q), pl.multiple_of(ks * GRID_W, GRID_W)

    def scores(g):
        q0, k0 = offsets(g)
        tb = jnp.where(g == 0, 0, jnp.where(g == n_groups - 1, 2, 1))
        q = q_ref[pl.ds(q0, NA_Q), :]
        return _dot_nt(q, k_ref[pl.ds(k0, NA_K), :]) + bias_scr[tb], _dot_nt(q, kc)

    def finish(g, sw, sc):
        q0, k0 = offsets(g)
        mx = jnp.maximum(jnp.max(sw, axis=-1, keepdims=True), jnp.max(sc, axis=-1, keepdims=True))
        pw = jnp.exp2(sw - mx)
        pc = jnp.exp2(sc - mx)
        den = jnp.sum(pw, axis=-1, keepdims=True) + jnp.sum(pc, axis=-1, keepdims=True)
        o = _dot(_bf(pw), v_ref[pl.ds(k0, NA_K), :]) + _dot(_bf(pc), vc)
        o_ref[pl.ds(q0, NA_Q), :] = _bf(o / den)

    def body(t, carry):
        groups = [t * NA_UNROLL + u for u in range(NA_UNROLL)]
        all_scores = [scores(g) for g in groups]
        for g, (sw, sc) in zip(groups, all_scores):
            finish(g, sw, sc)
        return carry

    lax.fori_loop(0, n_groups // NA_UNROLL, body, 0)


def _na_call(qkv, qkv_c, rpb):
    n_dr = 2 * NA_WIN_H - 1
    return pl.pallas_call(
        _na_kernel,
        grid=(NA_HEADS, BATCH),
        in_specs=[
            pl.BlockSpec(memory_space=pltpu.SMEM),
            pl.BlockSpec((SEQ, LANES), lambda h, b: (b, h)),
            pl.BlockSpec((SEQ, LANES), lambda h, b: (b, NA_HEADS + h)),
            pl.BlockSpec((SEQ, LANES), lambda h, b: (b, 2 * NA_HEADS + h)),
            pl.BlockSpec((CTX_LEN, LANES), lambda h, b: (b, NA_HEADS + h)),
            pl.BlockSpec((CTX_LEN, LANES), lambda h, b: (b, 2 * NA_HEADS + h)),
        ],
        out_specs=pl.BlockSpec((SEQ, LANES), lambda h, b: (b, h)),
        out_shape=jax.ShapeDtypeStruct((BATCH * SEQ, NA_WIDTH), jnp.bfloat16),
        scratch_shapes=[
            pltpu.VMEM((n_dr + 1, GRID_W, GRID_W), jnp.float32),
            pltpu.VMEM((3, NA_Q, NA_K), jnp.float32),
        ],
        compiler_params=_params("parallel", "arbitrary"),
        name="na_attn",
    )(rpb.reshape(-1), qkv, qkv, qkv, qkv_c, qkv_c)


def _key_chunks(src_rows, chunk):
    chunks, r = [], 0
    for rows in src_rows:
        size = min(chunk, rows)
        assert rows % size == 0
        chunks += [(r + c, size) for c in range(0, rows, size)]
        r += rows
    return chunks


def _pipelined_attention(streams, vt_scr, chunks, carry_in=None, defer_last=False):
    dv = vt_scr.shape[0]
    sub = streams[0][2].shape[1]
    n = len(streams)
    outs, prev = [], carry_in
    for u in range(n if defer_last else n + 1):
        cur = streams[u] if u < n else None
        m = jnp.full((1, sub), NEG, jnp.float32)
        l = jnp.zeros((1, sub), jnp.float32)
        acc = jnp.zeros((dv, sub), jnp.float32)
        for c0, cs in chunks:
            if cur is not None:
                q_t, k_scr, s_ref = cur
                s = _dot(k_scr[c0:c0 + cs, :], q_t)
                s_ref[c0:c0 + cs, :] = s
                m = jnp.maximum(m, jnp.max(s, axis=0, keepdims=True))
            if prev is not None:
                p = jnp.exp2(prev[0][c0:c0 + cs, :] - prev[1])
                l = l + jnp.sum(p, axis=0, keepdims=True)
                acc = acc + _dot(vt_scr[:, c0:c0 + cs], _bf(p))
        if prev is not None:
            outs.append(acc / l)
        prev = (cur[2], m) if cur is not None else None
    return outs, prev


def _transpose_bf16(x):
    return _bf(x.astype(jnp.float32).T)


def _run_streams(streams, vt_scr, chunks, m_scr, nq, write):
    i = pl.program_id(2)
    n = len(streams)
    if nq == 1:
        outs, _ = _pipelined_attention(streams, vt_scr, chunks)
        for u, o_t in enumerate(outs):
            write(0, u, o_t)
        return
    carried = streams[n - 1][2]

    def step(carry, defer):
        outs, last = _pipelined_attention(
            streams, vt_scr, chunks,
            carry_in=(carried, m_scr[...]) if carry else None, defer_last=defer)
        if carry:
            write(-1, n - 1, outs[0])
            outs = outs[1:]
        for u, o_t in enumerate(outs):
            write(0, u, o_t)
        if defer:
            m_scr[...] = last[1]

    pl.when(i == 0)(lambda: step(False, True))
    if nq > 2:
        pl.when((i > 0) & (i < nq - 1))(lambda: step(True, True))
    pl.when(i == nq - 1)(lambda: step(True, False))


def _attn_kernel(*refs, n_qparts, n_src, n_kparts, src_rows, nq):
    it = iter(refs)
    q_refs = [next(it) for _ in range(n_qparts)]
    srcs = []
    for _ in range(n_src):
        k_refs = [next(it) for _ in range(n_kparts)]
        srcs.append((k_refs, next(it)))
    o_ref = next(it)
    k_scr, vt_scr, m_scr = next(it), next(it), next(it)
    s_scrs = list(it)

    @pl.when(pl.program_id(2) == 0)
    def _():
        r = 0
        for (k_refs, v_ref), rows in zip(srcs, src_rows):
            for p, k_ref in enumerate(k_refs):
                k_scr[r:r + rows, p * LANES:(p + 1) * LANES] = k_ref[...]
            vt_scr[:, r:r + rows] = _transpose_bf16(v_ref[...])
            r += rows

    q = jnp.concatenate([qr[...] for qr in q_refs], axis=-1) if n_qparts > 1 else q_refs[0][...]
    sub = s_scrs[0].shape[1]
    tq = q.shape[0]
    streams = [(_transpose_bf16(q[u * sub:(u + 1) * sub]), k_scr, s_scrs[u % 2])
               for u in range(tq // sub)]

    def write(step_offset, u, o_t):
        row = pl.multiple_of((pl.program_id(2) + step_offset) * tq + u * sub, sub)
        o_ref[pl.ds(row, sub), :] = _bf(o_t.T)

    _run_streams(streams, vt_scr, _key_chunks(src_rows, KEY_CHUNK_ATTN), m_scr, nq, write)


def _attn_call(q_parts, sources, n_heads, lq, tq, dv, name):
    tq = min(tq, lq)
    sub = min(TQ_SUB, tq)
    nq = lq // tq
    in_specs, args = [], []
    for arr, cf in q_parts:
        in_specs.append(pl.BlockSpec((tq, LANES), lambda b, h, i, cf=cf: (b * nq + i, cf(h))))
        args.append(arr)
    src_rows = []
    n_kparts = len(sources[0][1])
    for rows, k_parts, (v_arr, vcf) in sources:
        src_rows.append(rows)
        for arr, cf in k_parts:
            in_specs.append(pl.BlockSpec((rows, LANES), lambda b, h, i, cf=cf: (b, cf(h))))
            args.append(arr)
        in_specs.append(pl.BlockSpec((rows, dv), lambda b, h, i, cf=vcf: (b, cf(h))))
        args.append(v_arr)
    nk = sum(src_rows)
    assert nq == 1 or (tq // sub) % 2 == 0
    kern = functools.partial(_attn_kernel, n_qparts=len(q_parts), n_src=len(sources),
                             n_kparts=n_kparts, src_rows=tuple(src_rows), nq=nq)
    return pl.pallas_call(
        kern,
        grid=(BATCH, n_heads, nq),
        in_specs=in_specs,
        out_specs=pl.BlockSpec((lq, dv), lambda b, h, i: (b, h)),
        out_shape=jax.ShapeDtypeStruct((BATCH * lq, n_heads * dv), jnp.bfloat16),
        scratch_shapes=[
            pltpu.VMEM((nk, n_kparts * LANES), jnp.bfloat16),
            pltpu.VMEM((dv, nk), jnp.bfloat16),
            pltpu.VMEM((1, sub), jnp.float32),
        ] + [pltpu.VMEM((nk, sub), jnp.float32)] * 2,
        compiler_params=_params("parallel", "parallel", "arbitrary"),
        name=name,
    )(*args)


def _diff_kernel(*refs, n_src, src_rows, lam_init, nq):
    it = iter(refs)
    q1_ref, q2_ref = next(it), next(it)
    srcs = [(next(it), next(it), next(it)) for _ in range(n_src)]
    lam_ref, sub_ref = next(it), next(it)
    o_ref = next(it)
    k1_scr, k2_scr, vt_scr, m_scr, o1_scr, s1_scr, s2_scr = [next(it) for _ in range(7)]

    @pl.when(pl.program_id(2) == 0)
    def _():
        r = 0
        for (k1_ref, k2_ref, v_ref), rows in zip(srcs, src_rows):
            k1_scr[r:r + rows, :] = k1_ref[...]
            k2_scr[r:r + rows, :] = k2_ref[...]
            vt_scr[:, r:r + rows] = _transpose_bf16(v_ref[...])
            r += rows

    lf = lam_ref[...]
    lam = (jnp.exp(jnp.sum(lf[0:1] * lf[1:2], axis=-1, keepdims=True))
           - jnp.exp(jnp.sum(lf[2:3] * lf[3:4], axis=-1, keepdims=True)) + lam_init)
    chunks = _key_chunks(src_rows, KEY_CHUNK_DIFF)
    sub = s1_scr.shape[1]
    tq = q1_ref.shape[0]
    streams = []
    for u in range(tq // sub):
        rows = slice(u * sub, (u + 1) * sub)
        streams.append((_transpose_bf16(q1_ref[rows, :]), k1_scr, s1_scr))
        streams.append((_transpose_bf16(q2_ref[rows, :]), k2_scr, s2_scr))
    n = len(streams)
    branch1 = {}

    def write(step_offset, u, o_t):
        if u % 2 == 0:
            branch1[u // 2] = o_t
            if nq > 1 and u == n - 2:
                o1_scr[...] = o_t
            return
        o1 = o1_scr[...] if step_offset < 0 else branch1[u // 2]
        o = (o1 - lam * o_t).T
        ms = jnp.mean(o * o, axis=-1, keepdims=True)
        row = pl.multiple_of((pl.program_id(2) + step_offset) * tq + (u // 2) * sub, sub)
        o_ref[pl.ds(row, sub), :] = _bf(
            o * lax.rsqrt(ms + EPS) * sub_ref[...] * (1.0 - lam_init))

    _run_streams(streams, vt_scr, chunks, m_scr, nq, write)


def _diff_call(qkv_q, lq, sources, diff_lambda, subln, lam_init, tq, name):
    tq = min(tq, lq)
    sub = min(TQ_SUB, tq)
    nq = lq // tq
    qb0 = 3 * NA_WIDTH // LANES
    kb0 = (3 * NA_WIDTH + DIFF_WIDTH) // LANES
    vb0 = (3 * NA_WIDTH + 2 * DIFF_WIDTH) // (2 * LANES)
    in_specs = [
        pl.BlockSpec((tq, LANES), lambda b, h, i: (b * nq + i, qb0 + 2 * h)),
        pl.BlockSpec((tq, LANES), lambda b, h, i: (b * nq + i, qb0 + 2 * h + 1)),
    ]
    args = [qkv_q, qkv_q]
    src_rows = []
    for rows, arr in sources:
        src_rows.append(rows)
        in_specs += [
            pl.BlockSpec((rows, LANES), lambda b, h, i: (b, kb0 + 2 * h)),
            pl.BlockSpec((rows, LANES), lambda b, h, i: (b, kb0 + 2 * h + 1)),
            pl.BlockSpec((rows, 2 * LANES), lambda b, h, i: (b, vb0 + h)),
        ]
        args += [arr, arr, arr]
    in_specs += [
        pl.BlockSpec((4, LANES), lambda b, h, i: (0, 0)),
        pl.BlockSpec((1, 2 * LANES), lambda b, h, i: (0, 0)),
    ]
    args += [diff_lambda, subln]
    nk = sum(src_rows)
    kern = functools.partial(_diff_kernel, n_src=len(sources), src_rows=tuple(src_rows),
                             lam_init=lam_init, nq=nq)
    return pl.pallas_call(
        kern,
        grid=(BATCH, DIFF_HEADS, nq),
        in_specs=in_specs,
        out_specs=pl.BlockSpec((lq, 2 * LANES), lambda b, h, i: (b, h)),
        out_shape=jax.ShapeDtypeStruct((BATCH * lq, DIFF_WIDTH), jnp.bfloat16),
        scratch_shapes=[
            pltpu.VMEM((nk, LANES), jnp.bfloat16),
            pltpu.VMEM((nk, LANES), jnp.bfloat16),
            pltpu.VMEM((2 * LANES, nk), jnp.bfloat16),
            pltpu.VMEM((1, sub), jnp.float32),
            pltpu.VMEM((2 * LANES, sub), jnp.float32),
            pltpu.VMEM((nk, sub), jnp.float32),
            pltpu.VMEM((nk, sub), jnp.float32),
        ],
        compiler_params=_params("parallel", "parallel", "arbitrary"),
        name=name,
    )(*args)


def _rope_tables(d):
    h = d // 2
    half = h // 2
    lane = jnp.arange(PAIR, dtype=jnp.int32)[None, :] % LANES
    t = jnp.arange(SEQ, dtype=jnp.int32)[:, None]
    pos = jnp.where(lane < h, t // GRID_W, t % GRID_W).astype(jnp.float32)
    freqs = ROPE_BASE ** (-(lane % half).astype(jnp.float32) / half)
    ang = pos * freqs
    valid = lane < d
    first = (lane % h) < half
    cos = jnp.where(valid, jnp.cos(ang), 1.0)
    sin = jnp.where(valid, jnp.sin(ang), 0.0)
    sin_a = jnp.where(first, -sin, 0.0)
    sin_b = jnp.where(first, 0.0, sin)
    return (cos, sin_a, sin_b), half


def _pair_blockdiag(block):
    z = np.zeros_like(block)
    return jnp.asarray(np.block([[block, z], [z, block]]), dtype=jnp.bfloat16)


def _pad_lanes(v, fill=0.0):
    return jnp.pad(v, (0, LANES - v.shape[0]), constant_values=fill)


def _pair(v):
    return jnp.concatenate([v, v])


def _lat_row(tm):
    bpb = SEQ // tm
    return lambda i: i // bpb


def _ctx_row(i):
    return CTX_MOD_ROW


def _even_layer(x, xc, modarr, norm_mix, norm_ffn, ffn_w, w_in, w_out, qn_a, kn_a, rpb,
                qn_b, kn_b, diff_lambda, subln, lam_init, with_ctx):
    scale = HEAD_DIM ** -0.5 * LOG2E
    gains = jnp.stack([_pair(g) for g in (qn_a * scale, kn_a, qn_b * scale, kn_b)]
                      + [jnp.ones((PAIR,), jnp.float32)] * 4)
    rope = _rope_tables(HEAD_DIM)
    segs = [(True, 0, HEAD_DIM, False), (True, 1, HEAD_DIM, False), (False, 0, HEAD_DIM, False),
            (True, 2, HEAD_DIM, True), (True, 3, HEAD_DIM, True), (False, 0, HEAD_DIM, False)]
    w_in = _bf(w_in)
    w_out = _bf(w_out)
    gain_mix = norm_mix[None, :]
    lat_row = _lat_row(TM)

    qkv = _proj_call(x, w_in, gains, segs, mod=(gain_mix, modarr, 0, 1), rope=rope,
                     mod_row=_lat_row(TM_EVEN), tm=TM_EVEN, name="even_proj")
    qkv_c = _proj_call(xc, w_in, gains, segs, mod=(gain_mix, modarr, 0, 1), rope=None,
                       mod_row=_ctx_row, tm=TM_EVEN, name="even_proj_ctx")

    oa = _na_call(qkv, qkv_c, rpb)
    ob = _diff_call(qkv, SEQ, [(SEQ, qkv), (CTX_LEN, qkv_c)], diff_lambda, subln[None, :],
                    lam_init, TQ_DIFF, "diff_attn")
    x = _outproj_call([oa, ob], w_out, x, modarr, 2, lat_row, name="even_out")
    x = _ffn_call(x, norm_ffn[None, :], modarr, *ffn_w, _lat_row(TM_FF), SEQ)
    if with_ctx:
        oa_c = _attn_call(
            [(qkv_c, lambda h: h)],
            [(CTX_LEN, [(qkv_c, lambda h: NA_HEADS + h)], (qkv_c, lambda h: 2 * NA_HEADS + h))],
            NA_HEADS, CTX_LEN, CTX_LEN, HEAD_DIM, "ctx_attn")
        ob_c = _diff_call(qkv_c, CTX_LEN, [(CTX_LEN, qkv_c)], diff_lambda, subln[None, :],
                          lam_init, CTX_LEN, "diff_attn_ctx")
        xc = _outproj_call([oa_c, ob_c], w_out, xc, modarr, 2, _ctx_row, name="even_out_ctx")
        xc = _ffn_call(xc, norm_ffn[None, :], modarr, *ffn_w, _ctx_row, CTX_LEN,
                       name="conv_ffn_ctx")
    return x, xc


def _odd_layer(x, xc, modarr, norm_mix, norm_ffn, ffn_w, w_down, q_a_norm, kv_a_norm, w_uq,
               w_ukv, qn_nope, qn_rope, kn_nope, kn_rope, w_out, with_ctx):
    scale = (MLA_NOPE + MLA_ROPE) ** -0.5 * LOG2E
    rope_pair = _rope_tables(MLA_ROPE)
    gain_mix = norm_mix[None, :]
    lat_row = _lat_row(TM)

    w_down_p = _bf(jnp.pad(w_down, ((0, 0), (0, MLA_DOWN_PAD - MLA_DOWN))))
    zeros_r = jnp.zeros((MLA_Q_RANK,), jnp.float32)
    kr_gain = jnp.concatenate([_pad_lanes(kn_rope), zeros_r[:MLA_Q_RANK - LANES]])
    gains_d = jnp.stack([q_a_norm, kv_a_norm, kr_gain] + [zeros_r] * 5)
    wq = w_uq.reshape(MLA_Q_RANK, MLA_HEADS, MLA_NOPE + MLA_ROPE)
    wq_rope = jnp.pad(wq[:, :, MLA_NOPE:], ((0, 0), (0, 0), (0, LANES - MLA_ROPE)))
    wq_p = _bf(jnp.concatenate([wq[:, :, :MLA_NOPE].reshape(MLA_Q_RANK, -1),
                                wq_rope.reshape(MLA_Q_RANK, -1)], -1))
    wkv = w_ukv.reshape(MLA_KV_RANK, MLA_HEADS, MLA_NOPE + MLA_V)
    wkv_p = _bf(jnp.concatenate([wkv[:, :, :MLA_NOPE].reshape(MLA_KV_RANK, -1),
                                 wkv[:, :, MLA_NOPE:].reshape(MLA_KV_RANK, -1)], -1))
    gains_p = jnp.stack([_pair(qn_nope * scale), _pair(_pad_lanes(qn_rope) * scale),
                         _pair(kn_nope)] + [jnp.zeros((PAIR,), jnp.float32)] * 5)
    mod_in = (gain_mix, modarr, 0, 1)
    q, kv, kr = _mla_proj_call(x, w_down_p, wq_p, wkv_p, gains_d, gains_p, mod_in, rope_pair,
                               _lat_row(TM_MLA_PROJ), True, "mla_proj")
    kv_c, kr_c = _mla_proj_call(xc, w_down_p, wq_p, wkv_p, gains_d, gains_p, mod_in, None,
                                _ctx_row, False, "mla_proj_ctx")

    def src(rows, kv_arr, kr_arr):
        return (rows, [(kv_arr, lambda h: h), (kr_arr, lambda h: 0)],
                (kv_arr, lambda h: MLA_HEADS + h))

    o = _attn_call([(q, lambda h: h), (q, lambda h: MLA_HEADS + h)],
                   [src(SEQ, kv, kr), src(CTX_LEN, kv_c, kr_c)],
                   MLA_HEADS, SEQ, TQ_MLA, MLA_V, "mla_attn")
    x = _outproj_call([o], _bf(w_out), x, modarr, 2, lat_row, name="mla_out")
    x = _ffn_call(x, norm_ffn[None, :], modarr, *ffn_w, _lat_row(TM_FF), SEQ)
    assert not with_ctx
    return x, xc


def _mla_proj_kernel(x_ref, g_ref, sh_ref, sc_ref, wd_ref, wq_ref, wkv_ref, gd_ref, gp_ref,
                     ones_ref, *rest, rope, rope_half, with_q):
    rope_refs = rest[:3] if rope else None
    outs = rest[3:] if rope else rest
    if with_q:
        q_ref, kv_ref, kr_ref = outs
    else:
        kv_ref, kr_ref = outs
    h = _bf(_modulate(x_ref[...], g_ref[...], sh_ref[0], sc_ref[0]))
    down = _dot(h, wd_ref[...])

    def latent_norm(xs, row, nvalid):
        ms = jnp.sum(xs * xs, axis=-1, keepdims=True) * (1.0 / nvalid)
        return xs * lax.rsqrt(ms + EPS) * gd_ref[row:row + 1, 0:xs.shape[1]]

    kva = _bf(latent_norm(down[:, MLA_Q_RANK:MLA_Q_RANK + MLA_KV_RANK], 1, MLA_KV_RANK))
    kr = latent_norm(down[:, MLA_Q_RANK + MLA_KV_RANK:MLA_DOWN_PAD], 2, MLA_ROPE)
    if rope:
        cos_ref, sa_ref, sb_ref = rope_refs
        kr = (kr * cos_ref[:, 0:LANES]
              + pltpu.roll(kr, LANES - rope_half, axis=1) * sa_ref[:, 0:LANES]
              + pltpu.roll(kr, rope_half, axis=1) * sb_ref[:, 0:LANES])
    kr_ref[...] = _bf(kr)

    _head_projection(kva, wkv_ref, kv_ref,
                     [(True, 2, MLA_NOPE, False), (False, 0, MLA_V, False)],
                     gp_ref, ones_ref, rope_refs, rope_half)
    if with_q:
        qa = _bf(latent_norm(down[:, 0:MLA_Q_RANK], 0, MLA_Q_RANK))
        _head_projection(qa, wq_ref, q_ref,
                         [(True, 0, MLA_NOPE, False), (True, 1, MLA_ROPE, True)],
                         gp_ref, ones_ref, rope_refs, rope_half)


def _mla_proj_call(x, wd, wq, wkv, gains_d, gains_p, mod, rope, mod_row, with_q, name,
                   tm=TM_MLA_PROJ):
    m, k = x.shape
    tm = min(tm, m)
    bpb = SEQ // tm
    gain, modarr, sh_c, sc_c = mod
    ones = _pair_blockdiag(np.ones((LANES, LANES), np.float32))
    in_specs = [
        pl.BlockSpec((tm, k), lambda i: (i, 0)),
        pl.BlockSpec((1, k), lambda i: (0, 0)),
        pl.BlockSpec((1, 1, k), lambda i: (mod_row(i), 0, sh_c)),
        pl.BlockSpec((1, 1, k), lambda i: (mod_row(i), 0, sc_c)),
    ]
    args = [x, gain, modarr, modarr]
    for w in (wd, wq, wkv, gains_d, gains_p, ones):
        in_specs.append(pl.BlockSpec(w.shape, lambda i: (0, 0), pipeline_mode=pl.Buffered(1)))
        args.append(w)
    rope_half = 0
    if rope is not None:
        tables, rope_half = rope
        for t in tables:
            in_specs.append(pl.BlockSpec((tm, PAIR), lambda i: (i % bpb, 0)))
            args.append(t)
    widths = ([wq.shape[1]] if with_q else []) + [wkv.shape[1], LANES]
    return pl.pallas_call(
        functools.partial(_mla_proj_kernel, rope=rope is not None, rope_half=rope_half,
                          with_q=with_q),
        grid=(m // tm,),
        in_specs=in_specs,
        out_specs=[pl.BlockSpec((tm, n), lambda i: (i, 0)) for n in widths],
        out_shape=[jax.ShapeDtypeStruct((m, n), jnp.bfloat16) for n in widths],
        compiler_params=_params("parallel"),
        name=name,
    )(*args)


def _w_in_prep_kernel(x_ref, o_ref):
    pad = jnp.zeros((x_ref.shape[0], D_FF_PAD - D_FF), jnp.bfloat16)
    o_ref[:, 0:D_FF] = _bf(x_ref[:, 0:D_FF])
    o_ref[:, D_FF:D_FF_PAD] = pad
    o_ref[:, D_FF_PAD:D_FF_PAD + D_FF] = _bf(x_ref[:, D_FF:2 * D_FF])
    o_ref[:, D_FF_PAD + D_FF:2 * D_FF_PAD] = pad


def _w_out_prep_kernel(x_ref, o_ref):
    o_ref[0:D_FF, :] = _bf(x_ref[...])
    o_ref[D_FF:D_FF_PAD, :] = jnp.zeros((D_FF_PAD - D_FF, x_ref.shape[1]), jnp.bfloat16)


def _ffn_weights(w_in, conv_w, w_out):
    depth, d, _ = w_in.shape
    rows, cols = PREP_ROWS, PREP_COLS
    w_in_p = pl.pallas_call(
        _w_in_prep_kernel,
        grid=(depth, d // rows),
        in_specs=[pl.BlockSpec((None, rows, 2 * D_FF), lambda l, i: (l, i, 0))],
        out_specs=pl.BlockSpec((None, rows, 2 * D_FF_PAD), lambda l, i: (l, i, 0)),
        out_shape=jax.ShapeDtypeStruct((depth, d, 2 * D_FF_PAD), jnp.bfloat16),
        compiler_params=_params("parallel", "parallel"),
        name="ffn_w_in_prep",
    )(w_in)
    w_out_p = pl.pallas_call(
        _w_out_prep_kernel,
        grid=(depth, d // cols),
        in_specs=[pl.BlockSpec((None, D_FF, cols), lambda l, j: (l, 0, j))],
        out_specs=pl.BlockSpec((None, D_FF_PAD, cols), lambda l, j: (l, 0, j)),
        out_shape=jax.ShapeDtypeStruct((depth, D_FF_PAD, d), jnp.bfloat16),
        compiler_params=_params("parallel", "parallel"),
        name="ffn_w_out_prep",
    )(w_out)
    pad = D_FF_PAD - D_FF
    zc = jnp.zeros(conv_w.shape[:2] + (pad,), conv_w.dtype)
    conv_p = jnp.concatenate([conv_w[..., :D_FF], zc, conv_w[..., D_FF:], zc], -1)
    return w_in_p, conv_p, w_out_p


def kernel(x, c, ctx, c_ctx, ada_w, ada_b, norm_mix, norm_ffn, ffn_w_in, ffn_conv, ffn_w_out, even_w_in, even_w_out, na_q_norm, na_k_norm, na_rpb, diff_q_norm, diff_k_norm, diff_lambda, diff_subln, mla_w_down, mla_q_a_norm, mla_kv_a_norm, mla_w_uq, mla_w_ukv, mla_q_nope_norm, mla_q_rope_norm, mla_k_nope_norm, mla_k_rope_norm, mla_w_out):
    cond = jnp.concatenate(
        [c, c_ctx[None, :], jnp.zeros((MOD_ROWS - BATCH - 1, D_MODEL), jnp.float32)], 0)
    mod = _ada_call(cond, ada_w, ada_b)
    xl = x.reshape(BATCH * SEQ, D_MODEL)
    xc = ctx.reshape(BATCH * CTX_LEN, D_MODEL)
    ffn_all = _ffn_weights(ffn_w_in, ffn_conv, ffn_w_out)
    for l in range(DEPTH):
        with_ctx = l < DEPTH - 1
        modarr = mod[l].reshape(MOD_ROWS, 1, 6 * D_MODEL)
        ffn_w = ffn_all + (l,)
        i = l // 2
        if l % 2 == 0:
            lam_init = 0.8 - 0.6 * math.exp(-0.3 * l)
            xl, xc = _even_layer(xl, xc, modarr, norm_mix[l], norm_ffn[l], ffn_w, even_w_in[i],
                                 even_w_out[i], na_q_norm[i], na_k_norm[i], na_rpb[i],
                                 diff_q_norm[i], diff_k_norm[i], diff_lambda[i], diff_subln[i],
                                 lam_init, with_ctx)
        else:
            xl, xc = _odd_layer(xl, xc, modarr, norm_mix[l], norm_ffn[l], ffn_w, mla_w_down[i],
                                mla_q_a_norm[i], mla_kv_a_norm[i], mla_w_uq[i], mla_w_ukv[i],
                                mla_q_nope_norm[i], mla_q_rope_norm[i], mla_k_nope_norm[i],
                                mla_k_rope_norm[i], mla_w_out[i], with_ctx)
    return xl.reshape(BATCH, SEQ, D_MODEL)
```

```python
import functools
import math

import numpy as np
import jax
import jax.numpy as jnp
from jax import lax
from jax.experimental import pallas as pl
from jax.experimental.pallas import tpu as pltpu

D_MODEL = 2048
BATCH = 4
SEQ = 4096
DEPTH = 2
GRID_W = 64
GRID_H = SEQ // GRID_W
CTX_LEN = 256
HEAD_DIM = 128
NA_HEADS = 8
NA_WIN_H = 8
NA_WIN_W = 16
DIFF_HEADS = 4
NA_WIDTH = 1024
DIFF_WIDTH = 1024
EVEN_PROJ = 6144
MLA_HEADS = 16
MLA_Q_RANK = 512
MLA_KV_RANK = 512
MLA_NOPE = 128
MLA_ROPE = 64
MLA_V = 128
MLA_DOWN = 1088
D_FF = 5504
ROPE_BASE = 10000.0
EPS = 1e-6

LANES = 128
SUBLANES = 8
PAIR = 2 * LANES
VMEM_LIMIT = 56 * 1024 * 1024
MOD_ROWS = 8
CTX_MOD_ROW = BATCH
NEG = -1e30
LOG2E = math.log2(math.e)

TM = 512
TM_EVEN = 512
TM_MLA_PROJ = 512
SUB_N = 512
TN_ADA = 1024
MLA_DOWN_PAD = 1152
TN_FF = 512
D_FF_PAD = -(-D_FF // TN_FF) * TN_FF
TM_FF = 1024
HM_FF = 512
PREP_ROWS = 256
PREP_COLS = 256
NA_GROUP_ROWS = 4
NA_KEY_ROWS = NA_GROUP_ROWS + NA_WIN_H
NA_UNROLL = 4
NA_Q = NA_GROUP_ROWS * GRID_W
NA_K = NA_KEY_ROWS * GRID_W
TQ_MLA = 1024
TQ_DIFF = 512
TQ_SUB = 512
KEY_CHUNK_ATTN = 512
KEY_CHUNK_DIFF = 512

_NT = (((1,), (1,)), ((), ()))


def _params(*sem):
    return pltpu.CompilerParams(dimension_semantics=sem, vmem_limit_bytes=VMEM_LIMIT)


def _bf(x):
    return x.astype(jnp.bfloat16)


def _dot(a, b):
    return jnp.dot(a, b, preferred_element_type=jnp.float32)


def _dot_nt(a, b):
    return lax.dot_general(a, b, _NT, preferred_element_type=jnp.float32)


def _modulate(x, gain, shift, scale, mxu_stats=False):
    d = x.shape[1]
    if mxu_stats:
        ss = _dot(_bf(x * x), jnp.ones((d, LANES), jnp.bfloat16))
        r = jnp.tile(lax.rsqrt(ss * (1.0 / d) + EPS), (1, d // LANES))
    else:
        r = lax.rsqrt(jnp.mean(x * x, axis=-1, keepdims=True) + EPS)
    return x * r * (gain * (1.0 + scale)) + shift


def _ada_kernel(c_ref, w_ref, b_ref, o_ref):
    c = c_ref[...]
    s = c * (1.0 / (1.0 + jnp.exp(-c)))
    o_ref[0] = _dot(_bf(s), _bf(w_ref[0])) + b_ref[0]


def _ada_call(cond, ada_w, ada_b):
    tn = TN_ADA
    n = ada_w.shape[-1]
    return pl.pallas_call(
        _ada_kernel,
        grid=(DEPTH, n // tn),
        in_specs=[
            pl.BlockSpec((MOD_ROWS, D_MODEL), lambda l, j: (0, 0)),
            pl.BlockSpec((1, D_MODEL, tn), lambda l, j: (l, 0, j)),
            pl.BlockSpec((1, 1, tn), lambda l, j: (l, 0, j)),
        ],
        out_specs=pl.BlockSpec((1, MOD_ROWS, tn), lambda l, j: (l, 0, j)),
        out_shape=jax.ShapeDtypeStruct((DEPTH, MOD_ROWS, n), jnp.float32),
        compiler_params=_params("parallel", "parallel"),
        name="ada_mod",
    )(cond, ada_w, ada_b.reshape(DEPTH, 1, n))


def _pair_epilogue(acc, spec, gains_ref, ones_ref, rope_refs, rope_half):
    norm, gain_row, nvalid, do_rope = spec
    y = acc
    if norm:
        ss = _dot(_bf(acc * acc), ones_ref[...])
        y = acc * lax.rsqrt(ss * (1.0 / nvalid) + EPS) * gains_ref[gain_row:gain_row + 1, :]
    if do_rope and rope_refs is not None:
        cos_ref, sa_ref, sb_ref = rope_refs
        y = (y * cos_ref[...]
             + pltpu.roll(y, PAIR - rope_half, axis=1) * sa_ref[...]
             + pltpu.roll(y, rope_half, axis=1) * sb_ref[...])
    return y


def _head_projection(a, w_ref, o_ref, segs, gains_ref, ones_ref, rope_refs, rope_half):
    width = w_ref.shape[1] // len(segs)
    for c in range(0, w_ref.shape[1], SUB_N):
        acc = _dot(a, w_ref[:, c:c + SUB_N])
        for p in range(0, SUB_N, PAIR):
            y = _pair_epilogue(acc[:, p:p + PAIR], segs[c // width], gains_ref, ones_ref,
                               rope_refs, rope_half)
            o_ref[:, c + p:c + p + PAIR] = _bf(y)


def _proj_kernel(x_ref, g_ref, sh_ref, sc_ref, w_ref, gains_ref, ones_ref, *rest, segs, rope,
                 rope_half):
    rope_refs = rest[:3] if rope else None
    o_ref = rest[-1]
    h = _bf(_modulate(x_ref[...], g_ref[...], sh_ref[0], sc_ref[0], mxu_stats=True))
    _head_projection(h, w_ref, o_ref, segs, gains_ref, ones_ref, rope_refs, rope_half)


def _proj_call(x, w, gains, segs, *, mod, rope=None, mod_row=None, tm=TM, name="proj"):
    m, k = x.shape
    n = w.shape[1]
    tm = min(tm, m)
    assert m % tm == 0 and n % (len(segs) * SUB_N) == 0
    bpb = SEQ // tm
    gain, modarr, sh_c, sc_c = mod
    ones = _pair_blockdiag(np.ones((LANES, LANES), np.float32))
    in_specs = [
        pl.BlockSpec((tm, k), lambda i: (i, 0)),
        pl.BlockSpec((1, k), lambda i: (0, 0)),
        pl.BlockSpec((1, 1, k), lambda i: (mod_row(i), 0, sh_c)),
        pl.BlockSpec((1, 1, k), lambda i: (mod_row(i), 0, sc_c)),
    ]
    args = [x, gain, modarr, modarr]
    for const in (w, gains, ones):
        in_specs.append(pl.BlockSpec(const.shape, lambda i: (0, 0), pipeline_mode=pl.Buffered(1)))
        args.append(const)
    rope_half = 0
    if rope is not None:
        tables, rope_half = rope
        for t in tables:
            in_specs.append(pl.BlockSpec((tm, PAIR), lambda i: (i % bpb, 0)))
            args.append(t)
    kern = functools.partial(_proj_kernel, segs=tuple(segs), rope=rope is not None,
                             rope_half=rope_half)
    return pl.pallas_call(
        kern,
        grid=(m // tm,),
        in_specs=in_specs,
        out_specs=pl.BlockSpec((tm, n), lambda i: (i, 0)),
        out_shape=jax.ShapeDtypeStruct((m, n), jnp.bfloat16),
        compiler_params=_params("parallel"),
        name=name,
    )(*args)


def _outproj_kernel(*refs, n_a):
    a_refs = refs[:n_a]
    w_ref, x_ref, g_ref, o_ref = refs[n_a:]
    n = o_ref.shape[1]
    a_vals = [a_ref[...] for a_ref in a_refs]
    for c in range(0, n, SUB_N):
        acc, row = None, 0
        for a in a_vals:
            part = _dot(a, w_ref[row:row + a.shape[1], c:c + SUB_N])
            acc = part if acc is None else acc + part
            row += a.shape[1]
        o_ref[:, c:c + SUB_N] = x_ref[:, c:c + SUB_N] + g_ref[0][:, c:c + SUB_N] * acc


def _outproj_call(a_list, w, x, modarr, gate_chunk, mod_row, *, tm=TM, name="outproj"):
    m, n = x.shape
    tm = min(tm, m)
    in_specs, args = [], []
    for a in a_list:
        in_specs.append(pl.BlockSpec((tm, a.shape[1]), lambda i: (i, 0)))
        args.append(a)
    in_specs += [
        pl.BlockSpec(w.shape, lambda i: (0, 0)),
        pl.BlockSpec((tm, n), lambda i: (i, 0)),
        pl.BlockSpec((1, 1, n), lambda i: (mod_row(i), 0, gate_chunk)),
    ]
    args += [w, x, modarr]
    return pl.pallas_call(
        functools.partial(_outproj_kernel, n_a=len(a_list)),
        grid=(m // tm,),
        in_specs=in_specs,
        out_specs=pl.BlockSpec((tm, n), lambda i: (i, 0)),
        out_shape=jax.ShapeDtypeStruct((m, n), jnp.float32),
        compiler_params=_params("parallel"),
        name=name,
    )(*args)


def _ffn_kernel(x_ref, xp_ref, xn_ref, g_ref, sh_ref, sc_ref, gt_ref, wa_ref, wb_ref,
                ca_ref, cb_ref, wo_ref, o_ref, h_ref, *, tm, hm, seq_len):
    i = pl.program_id(0)
    j = pl.program_id(1)
    nj = pl.num_programs(1)
    n_grp = tm // hm
    rows = hm + 2 * SUBLANES

    @pl.when(j == 0)
    def _():
        gain, shift, scale = g_ref[...], sh_ref[0], sc_ref[0]
        for r in range(n_grp):
            r0 = r * hm
            h_ref[r * rows:r * rows + hm, :] = _bf(
                _modulate(x_ref[r0:r0 + hm, :], gain, shift, scale, mxu_stats=True))
            nxt_src = x_ref[r0 + hm:r0 + hm + SUBLANES, :] if r + 1 < n_grp else xn_ref[...]
            prv_src = x_ref[r0 - SUBLANES:r0, :] if r > 0 else xp_ref[...]
            keep_next = jnp.where((i * tm + r0 + hm) % seq_len == 0, 0.0, 1.0)
            keep_prev = jnp.where((i * tm + r0) % seq_len == 0, 0.0, 1.0)
            nxt = keep_next * _modulate(nxt_src, gain, shift, scale)
            prv = keep_prev * _modulate(prv_src, gain, shift, scale)
            h_ref[r * rows + hm:(r + 1) * rows, :] = _bf(jnp.concatenate([nxt, prv], axis=0))
        o_ref[...] = jnp.zeros_like(o_ref)

    ca, cb = ca_ref[...], cb_ref[...]
    ups = []
    for r in range(n_grp):
        h = h_ref[r * rows:(r + 1) * rows, :]
        ups.append((_dot(h, wa_ref[...]), _dot(h, wb_ref[...])))

    def conv(u, cw):
        prev = pltpu.roll(u, 1, axis=0)[0:hm]
        nxt = pltpu.roll(u, rows - 1, axis=0)[0:hm]
        return prev * cw[0:1] + u[0:hm] * cw[1:2] + nxt * cw[2:3]

    for r, (ua, ub) in enumerate(ups):
        a = conv(ua, ca)
        b = conv(ub, cb)
        act = a * (1.0 / (1.0 + jnp.exp(-a))) * b
        o_ref[r * hm:(r + 1) * hm, :] += _dot(_bf(act), wo_ref[...])

    @pl.when(j == nj - 1)
    def _():
        o_ref[...] = x_ref[...] + gt_ref[0] * o_ref[...]


def _ffn_call(x, gain, modarr, w_in, conv_w, w_out, layer, mod_row, seq_len, *, tm=TM_FF,
              tn=TN_FF, name="conv_ffn"):
    m, d = x.shape
    tm = min(tm, m)
    hm = min(HM_FF, seq_len)
    assert m % tm == 0 and tm % hm == 0 and seq_len % hm == 0
    nj = D_FF_PAD // tn
    hb = tm // SUBLANES
    last_hb = m // SUBLANES - 1
    kern = functools.partial(_ffn_kernel, tm=tm, hm=hm, seq_len=seq_len)
    return pl.pallas_call(
        kern,
        grid=(m // tm, nj),
        in_specs=[
            pl.BlockSpec((tm, d), lambda i, j: (i, 0)),
            pl.BlockSpec((SUBLANES, d), lambda i, j: (jnp.maximum(i * hb - 1, 0), 0)),
            pl.BlockSpec((SUBLANES, d), lambda i, j: (jnp.minimum((i + 1) * hb, last_hb), 0)),
            pl.BlockSpec((1, d), lambda i, j: (0, 0)),
            pl.BlockSpec((1, 1, d), lambda i, j: (mod_row(i), 0, 3)),
            pl.BlockSpec((1, 1, d), lambda i, j: (mod_row(i), 0, 4)),
            pl.BlockSpec((1, 1, d), lambda i, j: (mod_row(i), 0, 5)),
            pl.BlockSpec((None, d, tn), lambda i, j: (layer, 0, j)),
            pl.BlockSpec((None, d, tn), lambda i, j: (layer, 0, nj + j)),
            pl.BlockSpec((None, 3, tn), lambda i, j: (layer, 0, j)),
            pl.BlockSpec((None, 3, tn), lambda i, j: (layer, 0, nj + j)),
            pl.BlockSpec((None, tn, d), lambda i, j: (layer, j, 0)),
        ],
        out_specs=pl.BlockSpec((tm, d), lambda i, j: (i, 0)),
        out_shape=jax.ShapeDtypeStruct((m, d), jnp.float32),
        scratch_shapes=[
            pltpu.VMEM(((tm // hm) * (hm + 2 * SUBLANES), d), jnp.bfloat16),
        ],
        compiler_params=_params("parallel", "arbitrary"),
        name=name,
    )(x, x, x, gain, modarr, modarr, modarr, w_in, w_in, conv_w, conv_w, w_out)


def _na_group_geometry(g):
    r0 = g * NA_GROUP_ROWS
    return r0, min(max(r0 - NA_WIN_H // 2, 0), GRID_H - NA_KEY_ROWS)


def _na_build_bias(rpb_ref, h, tc_scr, bias_scr):
    n_dr = 2 * NA_WIN_H - 1
    n_dc = 2 * NA_WIN_W - 1
    qc = lax.broadcasted_iota(jnp.int32, (GRID_W, GRID_W), 0)
    kc = lax.broadcasted_iota(jnp.int32, (GRID_W, GRID_W), 1)
    col0 = jnp.clip(qc - NA_WIN_W // 2, 0, GRID_W - NA_WIN_W)
    col_valid = (kc >= col0) & (kc < col0 + NA_WIN_W)
    delta = kc - qc + (NA_WIN_W - 1)
    base = h * (n_dr * n_dc)
    for dr in range(n_dr):
        acc = jnp.zeros((GRID_W, GRID_W), jnp.float32)
        for e in range(n_dc):
            acc = jnp.where(delta == e, rpb_ref[base + dr * n_dc + e] * LOG2E, acc)
        tc_scr[dr] = jnp.where(col_valid, acc, NEG)
    tc_scr[n_dr] = jnp.full((GRID_W, GRID_W), NEG, jnp.float32)
    n_groups = GRID_H // NA_GROUP_ROWS
    for t, g in enumerate((0, 1, n_groups - 1)):
        r0, ks = _na_group_geometry(g)
        for qr in range(NA_GROUP_ROWS):
            r = r0 + qr
            row0 = min(max(r - NA_WIN_H // 2, 0), GRID_H - NA_WIN_H)
            pieces = []
            for kr in range(NA_KEY_ROWS):
                kra = ks + kr
                inside = row0 <= kra < row0 + NA_WIN_H
                pieces.append(tc_scr[kra - r + (NA_WIN_H - 1)] if inside else tc_scr[n_dr])
            bias_scr[t, qr * GRID_W:(qr + 1) * GRID_W, :] = jnp.concatenate(pieces, axis=1)


def _na_kernel(rpb_ref, q_ref, k_ref, v_ref, kc_ref, vc_ref, o_ref, tc_scr, bias_scr):
    @pl.when(pl.program_id(1) == 0)
    def _():
        _na_build_bias(rpb_ref, pl.program_id(0), tc_scr, bias_scr)

    kc = kc_ref[...]
    vc = vc_ref[...]
    n_groups = GRID_H // NA_GROUP_ROWS

    def offsets(g):
        ks = jnp.clip(g * NA_GROUP_ROWS - NA_WIN_H // 2, 0, GRID_H - NA_KEY_ROWS)
        return pl.multiple_of(g * NA_Q, NA_Q), pl.multiple_of(ks * GRID_W, GRID_W)

    def scores(g):
        q0, k0 = offsets(g)
        tb = jnp.where(g == 0, 0, jnp.where(g == n_groups - 1, 2, 1))
        q = q_ref[pl.ds(q0, NA_Q), :]
        return _dot_nt(q, k_ref[pl.ds(k0, NA_K), :]) + bias_scr[tb], _dot_nt(q, kc)

    def finish(g, sw, sc):
        q0, k0 = offsets(g)
        mx = jnp.maximum(jnp.max(sw, axis=-1, keepdims=True), jnp.max(sc, axis=-1, keepdims=True))
        pw = jnp.exp2(sw - mx)
        pc = jnp.exp2(sc - mx)
        den = jnp.sum(pw, axis=-1, keepdims=True) + jnp.sum(pc, axis=-1, keepdims=True)
        o = _dot(_bf(pw), v_ref[pl.ds(k0, NA_K), :]) + _dot(_bf(pc), vc)
        o_ref[pl.ds(q0, NA_Q), :] = _bf(o / den)

    def body(t, carry):
        groups = [t * NA_UNROLL + u for u in range(NA_UNROLL)]
        all_scores = [scores(g) for g in groups]
        for g, (sw, sc) in zip(groups, all_scores):
            finish(g, sw, sc)
        return carry

    lax.fori_loop(0, n_groups // NA_UNROLL, body, 0)


def _na_call(qkv, qkv_c, rpb):
    n_dr = 2 * NA_WIN_H - 1
    return pl.pallas_call(
        _na_kernel,
        grid=(NA_HEADS, BATCH),
        in_specs=[
            pl.BlockSpec(memory_space=pltpu.SMEM),
            pl.BlockSpec((SEQ, LANES), lambda h, b: (b, h)),
            pl.BlockSpec((SEQ, LANES), lambda h, b: (b, NA_HEADS + h)),
            pl.BlockSpec((SEQ, LANES), lambda h, b: (b, 2 * NA_HEADS + h)),
            pl.BlockSpec((CTX_LEN, LANES), lambda h, b: (b, NA_HEADS + h)),
            pl.BlockSpec((CTX_LEN, LANES), lambda h, b: (b, 2 * NA_HEADS + h)),
        ],
        out_specs=pl.BlockSpec((SEQ, LANES), lambda h, b: (b, h)),
        out_shape=jax.ShapeDtypeStruct((BATCH * SEQ, NA_WIDTH), jnp.bfloat16),
        scratch_shapes=[
            pltpu.VMEM((n_dr + 1, GRID_W, GRID_W), jnp.float32),
            pltpu.VMEM((3, NA_Q, NA_K), jnp.float32),
        ],
        compiler_params=_params("parallel", "arbitrary"),
        name="na_attn",
    )(rpb.reshape(-1), qkv, qkv, qkv, qkv_c, qkv_c)


def _key_chunks(src_rows, chunk):
    chunks, r = [], 0
    for rows in src_rows:
        size = min(chunk, rows)
        assert rows % size == 0
        chunks += [(r + c, size) for c in range(0, rows, size)]
        r += rows
    return chunks


def _pipelined_attention(streams, vt_scr, chunks, carry_in=None, defer_last=False):
    dv = vt_scr.shape[0]
    sub = streams[0][2].shape[1]
    n = len(streams)
    outs, prev = [], carry_in
    for u in range(n if defer_last else n + 1):
        cur = streams[u] if u < n else None
        m = jnp.full((1, sub), NEG, jnp.float32)
        l = jnp.zeros((1, sub), jnp.float32)
        acc = jnp.zeros((dv, sub), jnp.float32)
        for c0, cs in chunks:
            if cur is not None:
                q_t, k_scr, s_ref = cur
                s = _dot(k_scr[c0:c0 + cs, :], q_t)
                s_ref[c0:c0 + cs, :] = s
                m = jnp.maximum(m, jnp.max(s, axis=0, keepdims=True))
            if prev is not None:
                p = jnp.exp2(prev[0][c0:c0 + cs, :] - prev[1])
                l = l + jnp.sum(p, axis=0, keepdims=True)
                acc = acc + _dot(vt_scr[:, c0:c0 + cs], _bf(p))
        if prev is not None:
            outs.append(acc / l)
        prev = (cur[2], m) if cur is not None else None
    return outs, prev


def _transpose_bf16(x):
    return _bf(x.astype(jnp.float32).T)


def _run_streams(streams, vt_scr, chunks, m_scr, nq, write):
    i = pl.program_id(2)
    n = len(streams)
    if nq == 1:
        outs, _ = _pipelined_attention(streams, vt_scr, chunks)
        for u, o_t in enumerate(outs):
            write(0, u, o_t)
        return
    carried = streams[n - 1][2]

    def step(carry, defer):
        outs, last = _pipelined_attention(
            streams, vt_scr, chunks,
            carry_in=(carried, m_scr[...]) if carry else None, defer_last=defer)
        if carry:
            write(-1, n - 1, outs[0])
            outs = outs[1:]
        for u, o_t in enumerate(outs):
            write(0, u, o_t)
        if defer:
            m_scr[...] = last[1]

    pl.when(i == 0)(lambda: step(False, True))
    if nq > 2:
        pl.when((i > 0) & (i < nq - 1))(lambda: step(True, True))
    pl.when(i == nq - 1)(lambda: step(True, False))


def _attn_kernel(*refs, n_qparts, n_src, n_kparts, src_rows, nq):
    it = iter(refs)
    q_refs = [next(it) for _ in range(n_qparts)]
    srcs = []
    for _ in range(n_src):
        k_refs = [next(it) for _ in range(n_kparts)]
        srcs.append((k_refs, next(it)))
    o_ref = next(it)
    k_scr, vt_scr, m_scr = next(it), next(it), next(it)
    s_scrs = list(it)

    @pl.when(pl.program_id(2) == 0)
    def _():
        r = 0
        for (k_refs, v_ref), rows in zip(srcs, src_rows):
            for p, k_ref in enumerate(k_refs):
                k_scr[r:r + rows, p * LANES:(p + 1) * LANES] = k_ref[...]
            vt_scr[:, r:r + rows] = _transpose_bf16(v_ref[...])
            r += rows

    q = jnp.concatenate([qr[...] for qr in q_refs], axis=-1) if n_qparts > 1 else q_refs[0][...]
    sub = s_scrs[0].shape[1]
    tq = q.shape[0]
    streams = [(_transpose_bf16(q[u * sub:(u + 1) * sub]), k_scr, s_scrs[u % 2])
               for u in range(tq // sub)]

    def write(step_offset, u, o_t):
        row = pl.multiple_of((pl.program_id(2) + step_offset) * tq + u * sub, sub)
        o_ref[pl.ds(row, sub), :] = _bf(o_t.T)

    _run_streams(streams, vt_scr, _key_chunks(src_rows, KEY_CHUNK_ATTN), m_scr, nq, write)


def _attn_call(q_parts, sources, n_heads, lq, tq, dv, name):
    tq = min(tq, lq)
    sub = min(TQ_SUB, tq)
    nq = lq // tq
    in_specs, args = [], []
    for arr, cf in q_parts:
        in_specs.append(pl.BlockSpec((tq, LANES), lambda b, h, i, cf=cf: (b * nq + i, cf(h))))
        args.append(arr)
    src_rows = []
    n_kparts = len(sources[0][1])
    for rows, k_parts, (v_arr, vcf) in sources:
        src_rows.append(rows)
        for arr, cf in k_parts:
            in_specs.append(pl.BlockSpec((rows, LANES), lambda b, h, i, cf=cf: (b, cf(h))))
            args.append(arr)
        in_specs.append(pl.BlockSpec((rows, dv), lambda b, h, i, cf=vcf: (b, cf(h))))
        args.append(v_arr)
    nk = sum(src_rows)
    assert nq == 1 or (tq // sub) % 2 == 0
    kern = functools.partial(_attn_kernel, n_qparts=len(q_parts), n_src=len(sources),
                             n_kparts=n_kparts, src_rows=tuple(src_rows), nq=nq)
    return pl.pallas_call(
        kern,
        grid=(BATCH, n_heads, nq),
        in_specs=in_specs,
        out_specs=pl.BlockSpec((lq, dv), lambda b, h, i: (b, h)),
        out_shape=jax.ShapeDtypeStruct((BATCH * lq, n_heads * dv), jnp.bfloat16),
        scratch_shapes=[
            pltpu.VMEM((nk, n_kparts * LANES), jnp.bfloat16),
            pltpu.VMEM((dv, nk), jnp.bfloat16),
            pltpu.VMEM((1, sub), jnp.float32),
        ] + [pltpu.VMEM((nk, sub), jnp.float32)] * 2,
        compiler_params=_params("parallel", "parallel", "arbitrary"),
        name=name,
    )(*args)


def _diff_kernel(*refs, n_src, src_rows, lam_init, nq):
    it = iter(refs)
    q1_ref, q2_ref = next(it), next(it)
    srcs = [(next(it), next(it), next(it)) for _ in range(n_src)]
    lam_ref, sub_ref = next(it), next(it)
    o_ref = next(it)
    k1_scr, k2_scr, vt_scr, m_scr, o1_scr, s1_scr, s2_scr = [next(it) for _ in range(7)]

    @pl.when(pl.program_id(2) == 0)
    def _():
        r = 0
        for (k1_ref, k2_ref, v_ref), rows in zip(srcs, src_rows):
            k1_scr[r:r + rows, :] = k1_ref[...]
            k2_scr[r:r + rows, :] = k2_ref[...]
            vt_scr[:, r:r + rows] = _transpose_bf16(v_ref[...])
            r += rows

    lf = lam_ref[...]
    lam = (jnp.exp(jnp.sum(lf[0:1] * lf[1:2], axis=-1, keepdims=True))
           - jnp.exp(jnp.sum(lf[2:3] * lf[3:4], axis=-1, keepdims=True)) + lam_init)
    chunks = _key_chunks(src_rows, KEY_CHUNK_DIFF)
    sub = s1_scr.shape[1]
    tq = q1_ref.shape[0]
    streams = []
    for u in range(tq // sub):
        rows = slice(u * sub, (u + 1) * sub)
        streams.append((_transpose_bf16(q1_ref[rows, :]), k1_scr, s1_scr))
        streams.append((_transpose_bf16(q2_ref[rows, :]), k2_scr, s2_scr))
    n = len(streams)
    branch1 = {}

    def write(step_offset, u, o_t):
        if u % 2 == 0:
            branch1[u // 2] = o_t
            if nq > 1 and u == n - 2:
                o1_scr[...] = o_t
            return
        o1 = o1_scr[...] if step_offset < 0 else branch1[u // 2]
        o = (o1 - lam * o_t).T
        ms = jnp.mean(o * o, axis=-1, keepdims=True)
        row = pl.multiple_of((pl.program_id(2) + step_offset) * tq + (u // 2) * sub, sub)
        o_ref[pl.ds(row, sub), :] = _bf(
            o * lax.rsqrt(ms + EPS) * sub_ref[...] * (1.0 - lam_init))

    _run_streams(streams, vt_scr, chunks, m_scr, nq, write)


def _diff_call(qkv_q, lq, sources, diff_lambda, subln, lam_init, tq, name):
    tq = min(tq, lq)
    sub = min(TQ_SUB, tq)
    nq = lq // tq
    qb0 = 3 * NA_WIDTH // LANES
    kb0 = (3 * NA_WIDTH + DIFF_WIDTH) // LANES
    vb0 = (3 * NA_WIDTH + 2 * DIFF_WIDTH) // (2 * LANES)
    in_specs = [
        pl.BlockSpec((tq, LANES), lambda b, h, i: (b * nq + i, qb0 + 2 * h)),
        pl.BlockSpec((tq, LANES), lambda b, h, i: (b * nq + i, qb0 + 2 * h + 1)),
    ]
    args = [qkv_q, qkv_q]
    src_rows = []
    for rows, arr in sources:
        src_rows.append(rows)
        in_specs += [
            pl.BlockSpec((rows, LANES), lambda b, h, i: (b, kb0 + 2 * h)),
            pl.BlockSpec((rows, LANES), lambda b, h, i: (b, kb0 + 2 * h + 1)),
            pl.BlockSpec((rows, 2 * LANES), lambda b, h, i: (b, vb0 + h)),
        ]
        args += [arr, arr, arr]
    in_specs += [
        pl.BlockSpec((4, LANES), lambda b, h, i: (0, 0)),
        pl.BlockSpec((1, 2 * LANES), lambda b, h, i: (0, 0)),
    ]
    args += [diff_lambda, subln]
    nk = sum(src_rows)
    kern = functools.partial(_diff_kernel, n_src=len(sources), src_rows=tuple(src_rows),
                             lam_init=lam_init, nq=nq)
    return pl.pallas_call(
        kern,
        grid=(BATCH, DIFF_HEADS, nq),
        in_specs=in_specs,
        out_specs=pl.BlockSpec((lq, 2 * LANES), lambda b, h, i: (b, h)),
        out_shape=jax.ShapeDtypeStruct((BATCH * lq, DIFF_WIDTH), jnp.bfloat16),
        scratch_shapes=[
            pltpu.VMEM((nk, LANES), jnp.bfloat16),
            pltpu.VMEM((nk, LANES), jnp.bfloat16),
            pltpu.VMEM((2 * LANES, nk), jnp.bfloat16),
            pltpu.VMEM((1, sub), jnp.float32),
            pltpu.VMEM((2 * LANES, sub), jnp.float32),
            pltpu.VMEM((nk, sub), jnp.float32),
            pltpu.VMEM((nk, sub), jnp.float32),
        ],
        compiler_params=_params("parallel", "parallel", "arbitrary"),
        name=name,
    )(*args)


def _rope_tables(d):
    h = d // 2
    half = h // 2
    lane = jnp.arange(PAIR, dtype=jnp.int32)[None, :] % LANES
    t = jnp.arange(SEQ, dtype=jnp.int32)[:, None]
    pos = jnp.where(lane < h, t // GRID_W, t % GRID_W).astype(jnp.float32)
    freqs = ROPE_BASE ** (-(lane % half).astype(jnp.float32) / half)
    ang = pos * freqs
    valid = lane < d
    first = (lane % h) < half
    cos = jnp.where(valid, jnp.cos(ang), 1.0)
    sin = jnp.where(valid, jnp.sin(ang), 0.0)
    sin_a = jnp.where(first, -sin, 0.0)
    sin_b = jnp.where(first, 0.0, sin)
    return (cos, sin_a, sin_b), half


def _pair_blockdiag(block):
    z = np.zeros_like(block)
    return jnp.asarray(np.block([[block, z], [z, block]]), dtype=jnp.bfloat16)


def _pad_lanes(v, fill=0.0):
    return jnp.pad(v, (0, LANES - v.shape[0]), constant_values=fill)


def _pair(v):
    return jnp.concatenate([v, v])


def _lat_row(tm):
    bpb = SEQ // tm
    return lambda i: i // bpb


def _ctx_row(i):
    return CTX_MOD_ROW


def _even_layer(x, xc, modarr, norm_mix, norm_ffn, ffn_w, w_in, w_out, qn_a, kn_a, rpb,
                qn_b, kn_b, diff_lambda, subln, lam_init, with_ctx):
    scale = HEAD_DIM ** -0.5 * LOG2E
    gains = jnp.stack([_pair(g) for g in (qn_a * scale, kn_a, qn_b * scale, kn_b)]
                      + [jnp.ones((PAIR,), jnp.float32)] * 4)
    rope = _rope_tables(HEAD_DIM)
    segs = [(True, 0, HEAD_DIM, False), (True, 1, HEAD_DIM, False), (False, 0, HEAD_DIM, False),
            (True, 2, HEAD_DIM, True), (True, 3, HEAD_DIM, True), (False, 0, HEAD_DIM, False)]
    w_in = _bf(w_in)
    w_out = _bf(w_out)
    gain_mix = norm_mix[None, :]
    lat_row = _lat_row(TM)

    qkv = _proj_call(x, w_in, gains, segs, mod=(gain_mix, modarr, 0, 1), rope=rope,
                     mod_row=_lat_row(TM_EVEN), tm=TM_EVEN, name="even_proj")
    qkv_c = _proj_call(xc, w_in, gains, segs, mod=(gain_mix, modarr, 0, 1), rope=None,
                       mod_row=_ctx_row, tm=TM_EVEN, name="even_proj_ctx")

    oa = _na_call(qkv, qkv_c, rpb)
    ob = _diff_call(qkv, SEQ, [(SEQ, qkv), (CTX_LEN, qkv_c)], diff_lambda, subln[None, :],
                    lam_init, TQ_DIFF, "diff_attn")
    x = _outproj_call([oa, ob], w_out, x, modarr, 2, lat_row, name="even_out")
    x = _ffn_call(x, norm_ffn[None, :], modarr, *ffn_w, _lat_row(TM_FF), SEQ)
    if with_ctx:
        oa_c = _attn_call(
            [(qkv_c, lambda h: h)],
            [(CTX_LEN, [(qkv_c, lambda h: NA_HEADS + h)], (qkv_c, lambda h: 2 * NA_HEADS + h))],
            NA_HEADS, CTX_LEN, CTX_LEN, HEAD_DIM, "ctx_attn")
        ob_c = _diff_call(qkv_c, CTX_LEN, [(CTX_LEN, qkv_c)], diff_lambda, subln[None, :],
                          lam_init, CTX_LEN, "diff_attn_ctx")
        xc = _outproj_call([oa_c, ob_c], w_out, xc, modarr, 2, _ctx_row, name="even_out_ctx")
        xc = _ffn_call(xc, norm_ffn[None, :], modarr, *ffn_w, _ctx_row, CTX_LEN,
                       name="conv_ffn_ctx")
    return x, xc


def _odd_layer(x, xc, modarr, norm_mix, norm_ffn, ffn_w, w_down, q_a_norm, kv_a_norm, w_uq,
               w_ukv, qn_nope, qn_rope, kn_nope, kn_rope, w_out, with_ctx):
    scale = (MLA_NOPE + MLA_ROPE) ** -0.5 * LOG2E
    rope_pair = _rope_tables(MLA_ROPE)
    gain_mix = norm_mix[None, :]
    lat_row = _lat_row(TM)

    w_down_p = _bf(jnp.pad(w_down, ((0, 0), (0, MLA_DOWN_PAD - MLA_DOWN))))
    zeros_r = jnp.zeros((MLA_Q_RANK,), jnp.float32)
    kr_gain = jnp.concatenate([_pad_lanes(kn_rope), zeros_r[:MLA_Q_RANK - LANES]])
    gains_d = jnp.stack([q_a_norm, kv_a_norm, kr_gain] + [zeros_r] * 5)
    wq = w_uq.reshape(MLA_Q_RANK, MLA_HEADS, MLA_NOPE + MLA_ROPE)
    wq_rope = jnp.pad(wq[:, :, MLA_NOPE:], ((0, 0), (0, 0), (0, LANES - MLA_ROPE)))
    wq_p = _bf(jnp.concatenate([wq[:, :, :MLA_NOPE].reshape(MLA_Q_RANK, -1),
                                wq_rope.reshape(MLA_Q_RANK, -1)], -1))
    wkv = w_ukv.reshape(MLA_KV_RANK, MLA_HEADS, MLA_NOPE + MLA_V)
    wkv_p = _bf(jnp.concatenate([wkv[:, :, :MLA_NOPE].reshape(MLA_KV_RANK, -1),
                                 wkv[:, :, MLA_NOPE:].reshape(MLA_KV_RANK, -1)], -1))
    gains_p = jnp.stack([_pair(qn_nope * scale), _pair(_pad_lanes(qn_rope) * scale),
                         _pair(kn_nope)] + [jnp.zeros((PAIR,), jnp.float32)] * 5)
    mod_in = (gain_mix, modarr, 0, 1)
    q, kv, kr = _mla_proj_call(x, w_down_p, wq_p, wkv_p, gains_d, gains_p, mod_in, rope_pair,
                               _lat_row(TM_MLA_PROJ), True, "mla_proj")
    kv_c, kr_c = _mla_proj_call(xc, w_down_p, wq_p, wkv_p, gains_d, gains_p, mod_in, None,
                                _ctx_row, False, "mla_proj_ctx")

    def src(rows, kv_arr, kr_arr):
        return (rows, [(kv_arr, lambda h: h), (kr_arr, lambda h: 0)],
                (kv_arr, lambda h: MLA_HEADS + h))

    o = _attn_call([(q, lambda h: h), (q, lambda h: MLA_HEADS + h)],
                   [src(SEQ, kv, kr), src(CTX_LEN, kv_c, kr_c)],
                   MLA_HEADS, SEQ, TQ_MLA, MLA_V, "mla_attn")
    x = _outproj_call([o], _bf(w_out), x, modarr, 2, lat_row, name="mla_out")
    x = _ffn_call(x, norm_ffn[None, :], modarr, *ffn_w, _lat_row(TM_FF), SEQ)
    assert not with_ctx
    return x, xc


def _mla_proj_kernel(x_ref, g_ref, sh_ref, sc_ref, wd_ref, wq_ref, wkv_ref, gd_ref, gp_ref,
                     ones_ref, *rest, rope, rope_half, with_q):
    rope_refs = rest[:3] if rope else None
    outs = rest[3:] if rope else rest
    if with_q:
        q_ref, kv_ref, kr_ref = outs
    else:
        kv_ref, kr_ref = outs
    h = _bf(_modulate(x_ref[...], g_ref[...], sh_ref[0], sc_ref[0]))
    down = _dot(h, wd_ref[...])

    def latent_norm(xs, row, nvalid):
        ms = jnp.sum(xs * xs, axis=-1, keepdims=True) * (1.0 / nvalid)
        return xs * lax.rsqrt(ms + EPS) * gd_ref[row:row + 1, 0:xs.shape[1]]

    kva = _bf(latent_norm(down[:, MLA_Q_RANK:MLA_Q_RANK + MLA_KV_RANK], 1, MLA_KV_RANK))
    kr = latent_norm(down[:, MLA_Q_RANK + MLA_KV_RANK:MLA_DOWN_PAD], 2, MLA_ROPE)
    if rope:
        cos_ref, sa_ref, sb_ref = rope_refs
        kr = (kr * cos_ref[:, 0:LANES]
              + pltpu.roll(kr, LANES - rope_half, axis=1) * sa_ref[:, 0:LANES]
              + pltpu.roll(kr, rope_half, axis=1) * sb_ref[:, 0:LANES])
    kr_ref[...] = _bf(kr)

    _head_projection(kva, wkv_ref, kv_ref,
                     [(True, 2, MLA_NOPE, False), (False, 0, MLA_V, False)],
                     gp_ref, ones_ref, rope_refs, rope_half)
    if with_q:
        qa = _bf(latent_norm(down[:, 0:MLA_Q_RANK], 0, MLA_Q_RANK))
        _head_projection(qa, wq_ref, q_ref,
                         [(True, 0, MLA_NOPE, False), (True, 1, MLA_ROPE, True)],
                         gp_ref, ones_ref, rope_refs, rope_half)


def _mla_proj_call(x, wd, wq, wkv, gains_d, gains_p, mod, rope, mod_row, with_q, name,
                   tm=TM_MLA_PROJ):
    m, k = x.shape
    tm = min(tm, m)
    bpb = SEQ // tm
    gain, modarr, sh_c, sc_c = mod
    ones = _pair_blockdiag(np.ones((LANES, LANES), np.float32))
    in_specs = [
        pl.BlockSpec((tm, k), lambda i: (i, 0)),
        pl.BlockSpec((1, k), lambda i: (0, 0)),
        pl.BlockSpec((1, 1, k), lambda i: (mod_row(i), 0, sh_c)),
        pl.BlockSpec((1, 1, k), lambda i: (mod_row(i), 0, sc_c)),
    ]
    args = [x, gain, modarr, modarr]
    for w in (wd, wq, wkv, gains_d, gains_p, ones):
        in_specs.append(pl.BlockSpec(w.shape, lambda i: (0, 0), pipeline_mode=pl.Buffered(1)))
        args.append(w)
    rope_half = 0
    if rope is not None:
        tables, rope_half = rope
        for t in tables:
            in_specs.append(pl.BlockSpec((tm, PAIR), lambda i: (i % bpb, 0)))
            args.append(t)
    widths = ([wq.shape[1]] if with_q else []) + [wkv.shape[1], LANES]
    return pl.pallas_call(
        functools.partial(_mla_proj_kernel, rope=rope is not None, rope_half=rope_half,
                          with_q=with_q),
        grid=(m // tm,),
        in_specs=in_specs,
        out_specs=[pl.BlockSpec((tm, n), lambda i: (i, 0)) for n in widths],
        out_shape=[jax.ShapeDtypeStruct((m, n), jnp.bfloat16) for n in widths],
        compiler_params=_params("parallel"),
        name=name,
    )(*args)


def _w_in_prep_kernel(x_ref, o_ref):
    pad = jnp.zeros((x_ref.shape[0], D_FF_PAD - D_FF), jnp.bfloat16)
    o_ref[:, 0:D_FF] = _bf(x_ref[:, 0:D_FF])
    o_ref[:, D_FF:D_FF_PAD] = pad
    o_ref[:, D_FF_PAD:D_FF_PAD + D_FF] = _bf(x_ref[:, D_FF:2 * D_FF])
    o_ref[:, D_FF_PAD + D_FF:2 * D_FF_PAD] = pad


def _w_out_prep_kernel(x_ref, o_ref):
    o_ref[0:D_FF, :] = _bf(x_ref[...])
    o_ref[D_FF:D_FF_PAD, :] = jnp.zeros((D_FF_PAD - D_FF, x_ref.shape[1]), jnp.bfloat16)


def _ffn_weights(w_in, conv_w, w_out):
    depth, d, _ = w_in.shape
    rows, cols = PREP_ROWS, PREP_COLS
    w_in_p = pl.pallas_call(
        _w_in_prep_kernel,
        grid=(depth, d // rows),
        in_specs=[pl.BlockSpec((None, rows, 2 * D_FF), lambda l, i: (l, i, 0))],
        out_specs=pl.BlockSpec((None, rows, 2 * D_FF_PAD), lambda l, i: (l, i, 0)),
        out_shape=jax.ShapeDtypeStruct((depth, d, 2 * D_FF_PAD), jnp.bfloat16),
        compiler_params=_params("parallel", "parallel"),
        name="ffn_w_in_prep",
    )(w_in)
    w_out_p = pl.pallas_call(
        _w_out_prep_kernel,
        grid=(depth, d // cols),
        in_specs=[pl.BlockSpec((None, D_FF, cols), lambda l, j: (l, 0, j))],
        out_specs=pl.BlockSpec((None, D_FF_PAD, cols), lambda l, j: (l, 0, j)),
        out_shape=jax.ShapeDtypeStruct((depth, D_FF_PAD, d), jnp.bfloat16),
        compiler_params=_params("parallel", "parallel"),
        name="ffn_w_out_prep",
    )(w_out)
    pad = D_FF_PAD - D_FF
    zc = jnp.zeros(conv_w.shape[:2] + (pad,), conv_w.dtype)
    conv_p = jnp.concatenate([conv_w[..., :D_FF], zc, conv_w[..., D_FF:], zc], -1)
    return w_in_p, conv_p, w_out_p


def kernel(x, c, ctx, c_ctx, ada_w, ada_b, norm_mix, norm_ffn, ffn_w_in, ffn_conv, ffn_w_out, even_w_in, even_w_out, na_q_norm, na_k_norm, na_rpb, diff_q_norm, diff_k_norm, diff_lambda, diff_subln, mla_w_down, mla_q_a_norm, mla_kv_a_norm, mla_w_uq, mla_w_ukv, mla_q_nope_norm, mla_q_rope_norm, mla_k_nope_norm, mla_k_rope_norm, mla_w_out):
    cond = jnp.concatenate(
        [c, c_ctx[None, :], jnp.zeros((MOD_ROWS - BATCH - 1, D_MODEL), jnp.float32)], 0)
    mod = _ada_call(cond, ada_w, ada_b)
    xl = x.reshape(BATCH * SEQ, D_MODEL)
    xc = ctx.reshape(BATCH * CTX_LEN, D_MODEL)
    ffn_all = _ffn_weights(ffn_w_in, ffn_conv, ffn_w_out)
    for l in range(DEPTH):
        with_ctx = l < DEPTH - 1
        modarr = mod[l].reshape(MOD_ROWS, 1, 6 * D_MODEL)
        ffn_w = ffn_all + (l,)
        i = l // 2
        if l % 2 == 0:
            lam_init = 0.8 - 0.6 * math.exp(-0.3 * l)
            xl, xc = _even_layer(xl, xc, modarr, norm_mix[l], norm_ffn[l], ffn_w, even_w_in[i],
                                 even_w_out[i], na_q_norm[i], na_k_norm[i], na_rpb[i],
                                 diff_q_norm[i], diff_k_norm[i], diff_lambda[i], diff_subln[i],
                                 lam_init, with_ctx)
        else:
            xl, xc = _odd_layer(xl, xc, modarr, norm_mix[l], norm_ffn[l], ffn_w, mla_w_down[i],
                                mla_q_a_norm[i], mla_kv_a_norm[i], mla_w_uq[i], mla_w_ukv[i],
                                mla_q_nope_norm[i], mla_q_rope_norm[i], mla_k_nope_norm[i],
                                mla_k_rope_norm[i], mla_w_out[i], with_ctx)
    return xl.reshape(BATCH, SEQ, D_MODEL)
```
